```python
import math
import jax
import jax.numpy as jnp
from jax import lax
import numpy as np

D_MODEL = 4096
BATCH = 8
SEQ = 2048
DEPTH = 2
DEC_BATCH = 1
DEC_SEQ = 8192
PAST_LEN = 128

GRID_W = 64
NA_HEADS = 8
NA_HEAD_DIM = D_MODEL // 32
NA_WIDTH = NA_HEADS * NA_HEAD_DIM
WIN_R_MAX = 8
WIN_C = 16
Q_BLK_C = 16
KEY_BLK_C = 2 * WIN_C
HY_WIDTH = D_MODEL // 4
HY_ORDER = 2
FILTER_EMB = 33
FILTER_HIDDEN = 64
HY_TARGET = 1e-2
HY_FAST_DECAY = 0.3
HY_SLOW_DECAY = 1.5
HY_MIN_DECAY = math.log(HY_TARGET) / HY_SLOW_DECAY
HY_MAX_DECAY = math.log(HY_TARGET) / HY_FAST_DECAY
RET_HEADS = 8
RET_WIDTH = D_MODEL // 2
RET_HEAD_DIM = RET_WIDTH // RET_HEADS
RET_CHUNK = 128
ROPE_BASE = 10000.0
MIX_WIDTH = NA_WIDTH + HY_WIDTH + RET_WIDTH
IN_COLS = 3 * NA_WIDTH + (HY_ORDER + 1) * HY_WIDTH + 4 * RET_WIDTH
D_FF = ((8 * D_MODEL + 3 * 256 - 1) // (3 * 256)) * 256
ALPHA = (2.0 * DEPTH) ** 0.25
BETA = (8.0 * DEPTH) ** -0.25
LN_EPS = 1e-5
RMS_EPS = 1e-6
NEG_INF = -1e30

kernel_name = 'hybrid_na_hyena_retention_encoder'


def _layernorm(x, g, b):
    xf = x.astype(jnp.float32)
    mu = jnp.mean(xf, -1, keepdims=True)
    var = jnp.mean(jnp.square(xf - mu), -1, keepdims=True)
    return ((xf - mu) * lax.rsqrt(var + LN_EPS) * g + b).astype(x.dtype)


def _rms_norm(x, gain):
    xf = x.astype(jnp.float32)
    y = xf * lax.rsqrt(jnp.mean(xf * xf, -1, keepdims=True) + RMS_EPS)
    return (y * gain).astype(x.dtype)


def _na_tables(rows):
    wr = min(WIN_R_MAX, rows)
    r = np.arange(rows)
    row_start = np.clip(r - wr // 2, 0, rows - wr)
    key_rows = row_start[:, None] + np.arange(wr)[None, :]
    dr_idx = key_rows - r[:, None] + (WIN_R_MAX - 1)
    n_cb = GRID_W // Q_BLK_C
    qcols = np.arange(GRID_W).reshape(n_cb, Q_BLK_C)
    blk_start = np.clip(qcols[:, 0] - WIN_C // 2, 0, GRID_W - KEY_BLK_C)
    key_cols = blk_start[:, None] + np.arange(KEY_BLK_C)[None, :]
    win_start = np.clip(qcols - WIN_C // 2, 0, GRID_W - WIN_C)
    kc = key_cols[:, None, :]
    in_win = (kc >= win_start[:, :, None]) & (kc < win_start[:, :, None] + WIN_C)
    dc_idx = np.clip(kc - qcols[:, :, None] + (WIN_C - 1), 0, 2 * WIN_C - 2)
    return wr, key_rows, dr_idx, key_cols, in_win, dc_idx


def _neighbourhood_attention(q, k, v, rpb):
    bsz, seq_len = q.shape[0], q.shape[1]
    rows = seq_len // GRID_W
    n_cb = GRID_W // Q_BLK_C
    wr, key_rows, dr_idx, key_cols, in_win, dc_idx = _na_tables(rows)
    grid = (bsz, rows, GRID_W, NA_HEADS, NA_HEAD_DIM)
    ri = key_rows[:, None, :, None]
    ci = key_cols[None, :, None, :]
    k_blk = k.reshape(grid)[:, ri, ci]
    v_blk = v.reshape(grid)[:, ri, ci]
    q_blk = q.reshape(bsz, rows, n_cb, Q_BLK_C, NA_HEADS, NA_HEAD_DIM) * (NA_HEAD_DIM ** -0.5)
    s = jnp.einsum('brcqhd,brcwkhd->bhrcqwk', q_blk, k_blk).astype(jnp.float32)
    bias = rpb[:, dr_idx[:, None, None, :, None], dc_idx[None, :, :, None, :]].astype(jnp.float32)
    s = jnp.where(in_win[:, :, None, :], s + bias[None], NEG_INF)
    p = jax.nn.softmax(s.reshape(s.shape[:5] + (wr * KEY_BLK_C,)), axis=-1)
    p = p.reshape(s.shape).astype(v.dtype)
    o = jnp.einsum('bhrcqwk,brcwkhd->brcqhd', p, v_blk)
    return o.reshape(bsz, seq_len, NA_WIDTH)


def _short_conv3(x, w, b):
    xp = jnp.pad(x, ((0, 0), (1, 1), (0, 0)))
    return xp[:, :-2] * w[0] + xp[:, 1:-1] * w[1] + xp[:, 2:] * w[2] + b


def _hyena_filters(seq_len, w1, b1, freq, w2, b2, w3, b3):
    f32 = jnp.float32
    t = jnp.linspace(0.0, 1.0, seq_len, dtype=f32)[:, None]
    bands = (FILTER_EMB - 1) // 2
    fr = jnp.linspace(1e-4, bands - 1, bands, dtype=f32)[None, :]
    wpos = 2.0 * math.pi * jnp.arange(seq_len, dtype=f32)[:, None] / seq_len
    z = jnp.concatenate([t, jnp.cos(fr * wpos), -jnp.sin(fr * wpos)], axis=-1)
    h = jnp.sin(freq[0].astype(f32) * (z @ w1.astype(f32) + b1.astype(f32)))
    h = jnp.sin(freq[1].astype(f32) * (h @ w2.astype(f32) + b2.astype(f32)))
    h = h @ w3.astype(f32) + b3.astype(f32)
    deltas = jnp.abs(jnp.linspace(HY_MIN_DECAY, HY_MAX_DECAY, HY_WIDTH, dtype=f32))
    decay = jnp.exp(-t * deltas[None, :])
    return h.reshape(seq_len, HY_ORDER, 2, HY_WIDTH) * decay[:, None, None, :]


def _bidir_long_conv(u, h_fwd, h_bwd, skip):
    seq_len, ch = h_fwd.shape
    filt = jnp.concatenate([h_fwd, jnp.zeros((1, ch), jnp.float32), h_bwd[1:][::-1]], axis=0)
    uf32 = u.astype(jnp.float32)
    kf = jnp.fft.rfft(filt, n=2 * seq_len, axis=0)
    uf = jnp.fft.rfft(uf32, n=2 * seq_len, axis=1)
    y = jnp.fft.irfft(uf * kf[None], n=2 * seq_len, axis=1)[:, :seq_len]
    return (y + uf32 * skip.astype(jnp.float32)).astype(u.dtype)


def _rotate(x, pos):
    half = RET_HEAD_DIM // 2
    inv_freq = 1.0 / (ROPE_BASE ** jnp.linspace(0.0, 1.0, half, dtype=jnp.float32))
    ang = pos[:, None] * inv_freq[None, :]
    cos = jnp.cos(ang)[None, :, None, :]
    sin = jnp.sin(ang)[None, :, None, :]
    x1, x2 = x[..., :half], x[..., half:]
    return jnp.concatenate([x1 * cos - x2 * sin, x1 * sin + x2 * cos], axis=-1)


def _retention_chunkwise(q, k, v, log_gamma, strict):
    bsz, seq_len, nh, dh = q.shape
    n_chunks = seq_len // RET_CHUNK
    cs = (bsz, n_chunks, RET_CHUNK, nh, dh)
    qc, kc, vc = q.reshape(cs), k.reshape(cs), v.reshape(cs)
    idx = jnp.arange(RET_CHUNK, dtype=jnp.float32)
    diff = idx[:, None] - idx[None, :]
    mask = (diff > 0) if strict else (diff >= 0)
    inner_decay = jnp.where(mask[None], jnp.exp(log_gamma[:, None, None] * jnp.maximum(diff, 0.0)[None]), 0.0)
    s = jnp.einsum('bnihd,bnjhd->bnhij', qc, kc) * inner_decay
    inner = jnp.einsum('bnhij,bnjhv->bnihv', s, vc)
    zeta = jnp.exp(log_gamma[:, None] * (RET_CHUNK - 1 - idx)[None, :])
    kv = jnp.einsum('bnjhd,bnjhv,hj->bnhdv', kc, vc, zeta)
    chunk_decay = jnp.exp(log_gamma * RET_CHUNK)[None, :, None, None]

    def step(state, kv_i):
        return state * chunk_decay + kv_i, state

    init = jnp.zeros((bsz, nh, dh, dh), jnp.float32)
    _, prev = lax.scan(step, init, jnp.moveaxis(kv, 1, 0))
    prev = jnp.moveaxis(prev, 0, 1)
    xi = jnp.exp(log_gamma[:, None] * (idx + 1.0)[None, :])
    cross = jnp.einsum('bnihd,bnhdv,hi->bnihv', qc, prev, xi)
    return (inner + cross).reshape(bsz, seq_len, nh, dh)


def _token_mixers(x, w_in, na_rpb, hy_conv_w, hy_conv_b, hy_f_w1, hy_f_b1, hy_f_freq, hy_f_w2, hy_f_b2,
                  hy_f_w3, hy_f_b3, hy_skip, ret_decay_exp, grp_gain_a, grp_gain_b, w_out):
    bsz, seq_len, _ = x.shape
    f32 = jnp.float32
    proj = jnp.einsum('bld,dc->blc', x, w_in)
    s1 = 3 * NA_WIDTH
    s2 = s1 + (HY_ORDER + 1) * HY_WIDTH
    s3 = s2 + RET_WIDTH
    s4 = s3 + RET_WIDTH
    s5 = s4 + RET_WIDTH
    a_qkv, hy_in, r_q, r_k, r_v, r_g = jnp.split(proj, [s1, s2, s3, s4, s5], axis=-1)

    a_q, a_k, a_v = jnp.split(a_qkv, 3, axis=-1)
    a_out = _rms_norm(_neighbourhood_attention(a_q, a_k, a_v, na_rpb), grp_gain_a)

    hy = _short_conv3(hy_in, hy_conv_w, hy_conv_b)
    hv, hx1, hx2 = jnp.split(hy, 3, axis=-1)
    filt = _hyena_filters(seq_len, hy_f_w1, hy_f_b1, hy_f_freq, hy_f_w2, hy_f_b2, hy_f_w3, hy_f_b3)
    gates = (hx1, hx2)
    z = hv
    for o in range(HY_ORDER):
        z = gates[o] * _bidir_long_conv(z, filt[:, o, 0], filt[:, o, 1], hy_skip[o])
    b_out = _rms_norm(z, grp_gain_b)

    pos = jnp.arange(seq_len, dtype=f32)
    shp = (bsz, seq_len, RET_HEADS, RET_HEAD_DIM)
    rq = _rotate(r_q.reshape(shp).astype(f32), pos)
    rk = _rotate(r_k.reshape(shp).astype(f32), pos) * (RET_HEAD_DIM ** -0.5)
    rv = r_v.reshape(shp).astype(f32)
    log_gamma = jnp.log1p(-jnp.exp2(-ret_decay_exp.astype(f32)))
    fwd = _retention_chunkwise(rq, rk, rv, log_gamma[0], False)
    bwd = jnp.flip(_retention_chunkwise(jnp.flip(rq, 1), jnp.flip(rk, 1), jnp.flip(rv, 1), log_gamma[1], True), 1)
    ret = fwd + bwd
    ret = ret * lax.rsqrt(jnp.mean(ret * ret, -1, keepdims=True) + RMS_EPS)
    c_out = (ret.reshape(bsz, seq_len, RET_WIDTH) * jax.nn.silu(r_g.astype(f32))).astype(x.dtype)

    mixed = jnp.concatenate([a_out, b_out, c_out], axis=-1)
    return jnp.einsum('blm,md->bld', mixed, w_out)


def _swiglu(x, w_ffn_in, w_ffn_out):
    gu = jnp.einsum('bld,df->blf', x, w_ffn_in)
    g, u = jnp.split(gu, 2, axis=-1)
    return jnp.einsum('blf,fd->bld', jax.nn.silu(g) * u, w_ffn_out)


def _trunk(x, ln_in_g, ln_in_b, w_in, na_rpb, hy_conv_w, hy_conv_b, hy_f_w1, hy_f_b1, hy_f_freq, hy_f_w2,
           hy_f_b2, hy_f_w3, hy_f_b3, hy_skip, ret_decay_exp, grp_gain_a, grp_gain_b, w_out, ln1_g, ln1_b,
           w_ffn_in, w_ffn_out, ln2_g, ln2_b):
    x = _layernorm(x, ln_in_g, ln_in_b)
    for l in range(DEPTH):
        h = _token_mixers(x, w_in[l], na_rpb[l], hy_conv_w[l], hy_conv_b[l], hy_f_w1[l], hy_f_b1[l],
                          hy_f_freq[l], hy_f_w2[l], hy_f_b2[l], hy_f_w3[l], hy_f_b3[l], hy_skip[l],
                          ret_decay_exp[l], grp_gain_a[l], grp_gain_b[l], w_out[l])
        x = _layernorm(ALPHA * x + h, ln1_g[l], ln1_b[l])
        x = _layernorm(ALPHA * x + _swiglu(x, w_ffn_in[l], w_ffn_out[l]), ln2_g[l], ln2_b[l])
    return x


def setup_inputs(seed: int = 0) -> dict:
    key = jax.random.key(seed)
    ks = jax.random.split(key, 26)
    f32 = jnp.float32

    def nrm(k, shape, scale):
        return jax.random.normal(k, shape, f32) * scale

    col_scale = np.ones((IN_COLS,), np.float32)
    col_scale[2 * NA_WIDTH:3 * NA_WIDTH] = BETA
    hy0 = 3 * NA_WIDTH
    col_scale[hy0:hy0 + HY_WIDTH] = BETA
    rv0 = hy0 + (HY_ORDER + 1) * HY_WIDTH + 2 * RET_WIDTH
    col_scale[rv0:rv0 + RET_WIDTH] = BETA
    hy_cols = (HY_ORDER + 1) * HY_WIDTH
    return {
        'x_prompt': nrm(ks[0], (BATCH, SEQ, D_MODEL), 1.0),
        'x_sample': nrm(ks[1], (DEC_BATCH, DEC_SEQ, D_MODEL), 1.0),
        'ln_in_g': 1.0 + nrm(ks[2], (D_MODEL,), 0.02),
        'ln_in_b': nrm(ks[3], (D_MODEL,), 0.02),
        'w_in': nrm(ks[4], (DEPTH, D_MODEL, IN_COLS), D_MODEL ** -0.5) * jnp.asarray(col_scale),
        'na_rpb': nrm(ks[5], (DEPTH, NA_HEADS, 2 * WIN_R_MAX - 1, 2 * WIN_C - 1), 0.02),
        'hy_conv_w': nrm(ks[6], (DEPTH, 3, hy_cols), 3 ** -0.5),
        'hy_conv_b': nrm(ks[7], (DEPTH, hy_cols), 0.02),
        'hy_f_w1': nrm(ks[8], (DEPTH, FILTER_EMB, FILTER_HIDDEN), FILTER_EMB ** -0.5),
        'hy_f_b1': nrm(ks[9], (DEPTH, FILTER_HIDDEN), 0.02),
        'hy_f_freq': 1.0 + nrm(ks[10], (DEPTH, 2, FILTER_HIDDEN), 0.02),
        'hy_f_w2': nrm(ks[11], (DEPTH, FILTER_HIDDEN, FILTER_HIDDEN), FILTER_HIDDEN ** -0.5),
        'hy_f_b2': nrm(ks[12], (DEPTH, FILTER_HIDDEN), 0.02),
        'hy_f_w3': nrm(ks[13], (DEPTH, FILTER_HIDDEN, HY_ORDER * 2 * HY_WIDTH), FILTER_HIDDEN ** -0.5),
        'hy_f_b3': nrm(ks[14], (DEPTH, HY_ORDER * 2 * HY_WIDTH), 0.02),
        'hy_skip': nrm(ks[15], (DEPTH, HY_ORDER, HY_WIDTH), 1.0),
        'ret_decay_exp': 5.0 + jnp.arange(RET_HEADS, dtype=f32) + nrm(ks[16], (DEPTH, 2, RET_HEADS), 0.1),
        'grp_gain_a': 1.0 + nrm(ks[17], (DEPTH, NA_WIDTH), 0.02),
        'grp_gain_b': 1.0 + nrm(ks[18], (DEPTH, HY_WIDTH), 0.02),
        'w_out': nrm(ks[19], (DEPTH, MIX_WIDTH, D_MODEL), MIX_WIDTH ** -0.5 * BETA),
        'ln1_g': 1.0 + nrm(ks[20], (DEPTH, D_MODEL), 0.02),
        'ln1_b': nrm(ks[21], (DEPTH, D_MODEL), 0.02),
        'w_ffn_in': nrm(ks[22], (DEPTH, D_MODEL, 2 * D_FF), D_MODEL ** -0.5 * BETA),
        'w_ffn_out': nrm(ks[23], (DEPTH, D_FF, D_MODEL), D_FF ** -0.5 * BETA),
        'ln2_g': 1.0 + nrm(ks[24], (DEPTH, D_MODEL), 0.02),
        'ln2_b': nrm(ks[25], (DEPTH, D_MODEL), 0.02),
    }


def reference(x_prompt, x_sample, ln_in_g, ln_in_b, w_in, na_rpb, hy_conv_w, hy_conv_b, hy_f_w1, hy_f_b1,
              hy_f_freq, hy_f_w2, hy_f_b2, hy_f_w3, hy_f_b3, hy_skip, ret_decay_exp, grp_gain_a, grp_gain_b,
              w_out, ln1_g, ln1_b, w_ffn_in, w_ffn_out, ln2_g, ln2_b):
    params = (ln_in_g, ln_in_b, w_in, na_rpb, hy_conv_w, hy_conv_b, hy_f_w1, hy_f_b1, hy_f_freq, hy_f_w2,
              hy_f_b2, hy_f_w3, hy_f_b3, hy_skip, ret_decay_exp, grp_gain_a, grp_gain_b, w_out, ln1_g, ln1_b,
              w_ffn_in, w_ffn_out, ln2_g, ln2_b)
    y_prompt = _trunk(x_prompt, *params)
    y_sample = _trunk(x_sample, *params)
    return (y_prompt, y_sample)
```

```python
import functools
import math

import numpy as np
import jax
import jax.numpy as jnp
from jax import lax
from jax.experimental import pallas as pl
from jax.experimental.pallas import tpu as pltpu

F32 = jnp.float32
BF16 = jnp.bfloat16

GRID_W = 64
WIN_R = 8
WIN_C = 16
ROPE_BASE = 10000.0
LN_EPS = 1e-5
RMS_EPS = 1e-6
NEG_INF = -1e30
HY_TARGET = 1e-2
HY_FAST_DECAY = 0.3
HY_SLOW_DECAY = 1.5
HY_MIN_DECAY = math.log(HY_TARGET) / HY_SLOW_DECAY
HY_MAX_DECAY = math.log(HY_TARGET) / HY_FAST_DECAY

V7X_VMEM_LIMIT = 56 * 1024 * 1024
LANES = 128
DFT_N2 = 128
NA_ROWS_PER_STEP = 8
RET_CHUNK = 256
RET_SUPER = 2048


def _cparams(sem, vmem=V7X_VMEM_LIMIT):
    return pltpu.CompilerParams(dimension_semantics=sem, vmem_limit_bytes=vmem)


def _ln_kernel(x_ref, g_ref, b_ref, of_ref, ob_ref):
    x = x_ref[...]
    mu = jnp.mean(x, -1, keepdims=True)
    xc = x - mu
    var = jnp.mean(xc * xc, -1, keepdims=True)
    y = xc * lax.rsqrt(var + LN_EPS) * g_ref[...] + b_ref[...]
    of_ref[...] = y
    ob_ref[...] = y.astype(BF16)


def _layernorm(x, g, b, bm=256):
    m, d = x.shape
    return pl.pallas_call(
        _ln_kernel,
        grid=(m // bm,),
        in_specs=[pl.BlockSpec((bm, d), lambda i: (i, 0)),
                  pl.BlockSpec((1, d), lambda i: (0, 0)),
                  pl.BlockSpec((1, d), lambda i: (0, 0))],
        out_specs=[pl.BlockSpec((bm, d), lambda i: (i, 0)),
                   pl.BlockSpec((bm, d), lambda i: (i, 0))],
        out_shape=[jax.ShapeDtypeStruct((m, d), F32), jax.ShapeDtypeStruct((m, d), BF16)],
        compiler_params=_cparams(("parallel",)),
        name="layernorm",
    )(x, g.reshape(1, d), b.reshape(1, d))


def _mm_kernel(x_ref, w_ref, o_ref):
    o_ref[...] = jnp.dot(x_ref[...], w_ref[...], preferred_element_type=F32).astype(o_ref.dtype)


def _mm_res_kernel(x_ref, w_ref, r_ref, o_ref, *, alpha):
    o_ref[...] = alpha * r_ref[...] + jnp.dot(x_ref[...], w_ref[...], preferred_element_type=F32)


def _mm_swiglu_kernel(x_ref, wg_ref, wu_ref, o_ref):
    x = x_ref[...]
    g = jnp.dot(x, wg_ref[...], preferred_element_type=F32)
    u = jnp.dot(x, wu_ref[...], preferred_element_type=F32)
    o_ref[...] = (g / (1.0 + jnp.exp(-g)) * u).astype(o_ref.dtype)


def _matmul(x, w, bm, bn, out_dtype=BF16):
    m, k = x.shape
    n = w.shape[1]
    return pl.pallas_call(
        _mm_kernel,
        grid=(m // bm, n // bn),
        in_specs=[pl.BlockSpec((bm, k), lambda i, j: (i, 0)),
                  pl.BlockSpec((k, bn), lambda i, j: (0, j))],
        out_specs=pl.BlockSpec((bm, bn), lambda i, j: (i, j)),
        out_shape=jax.ShapeDtypeStruct((m, n), out_dtype),
        compiler_params=_cparams(("parallel", "arbitrary")),
        name="matmul",
    )(x, w)


def _matmul_residual(x, w, r, alpha, bm, bn):
    m, k = x.shape
    n = w.shape[1]
    return pl.pallas_call(
        functools.partial(_mm_res_kernel, alpha=alpha),
        grid=(m // bm, n // bn),
        in_specs=[pl.BlockSpec((bm, k), lambda i, j: (i, 0)),
                  pl.BlockSpec((k, bn), lambda i, j: (0, j)),
                  pl.BlockSpec((bm, bn), lambda i, j: (i, j))],
        out_specs=pl.BlockSpec((bm, bn), lambda i, j: (i, j)),
        out_shape=jax.ShapeDtypeStruct((m, n), F32),
        compiler_params=_cparams(("parallel", "arbitrary")),
        name="matmul_residual",
    )(x, w, r)


def _matmul_swiglu(x, w, bm, bn):
    m, k = x.shape
    f = w.shape[1] // 2
    nb = f // bn
    return pl.pallas_call(
        _mm_swiglu_kernel,
        grid=(m // bm, nb),
        in_specs=[pl.BlockSpec((bm, k), lambda i, j: (i, 0)),
                  pl.BlockSpec((k, bn), lambda i, j: (0, j)),
                  pl.BlockSpec((k, bn), lambda i, j: (0, j + nb))],
        out_specs=pl.BlockSpec((bm, bn), lambda i, j: (i, j)),
        out_shape=jax.ShapeDtypeStruct((m, f), BF16),
        compiler_params=_cparams(("parallel", "arbitrary")),
        name="matmul_swiglu",
    )(x, w, w)


def _na_bias_tables(rpb, rows):
    nh = rpb.shape[0]
    r8 = NA_ROWS_PER_STEP
    i = np.arange(r8)[:, None]
    j = np.arange(2 * r8)[None, :]
    rel_start = [np.maximum(i - 4, 0), i, np.minimum(i + 4, 8)]
    q_rel = [i, i + 4, i + 8]
    qc = np.arange(GRID_W)[:, None]
    kc = np.arange(GRID_W)[None, :]
    win_start = np.clip(qc - WIN_C // 2, 0, GRID_W - WIN_C)
    col_ok = (kc >= win_start) & (kc < win_start + WIN_C)
    dc_idx = np.clip(kc - qc + (WIN_C - 1), 0, 2 * WIN_C - 2)
    onehot = (dc_idx.reshape(-1)[None, :] == np.arange(2 * WIN_C - 1)[:, None]).astype(np.float32)
    tiles = jnp.einsum("hdc,cq->hdq", rpb.astype(F32), jnp.asarray(onehot),
                       precision=lax.Precision.HIGHEST).reshape(nh, 2 * WIN_R - 1, GRID_W, GRID_W)
    tiles = jnp.where(jnp.asarray(col_ok)[None, None], tiles, NEG_INF)
    masked_tile = jnp.full((nh, 1, GRID_W, GRID_W), NEG_INF, F32)
    tiles = jnp.concatenate([tiles, masked_tile], axis=1)
    out = []
    for v in range(3):
        row_ok = (j >= rel_start[v]) & (j < rel_start[v] + WIN_R)
        dr = np.where(row_ok, j - q_rel[v] + (WIN_R - 1), 2 * WIN_R - 1)
        assert dr.min() >= 0 and dr.max() <= 2 * WIN_R - 1
        t = tiles[:, dr]
        out.append(t.transpose(0, 1, 3, 2, 4).reshape(nh, r8 * GRID_W, 2 * r8 * GRID_W))
    return jnp.stack(out)


def _na_kernel(q_ref, k_ref, v_ref, bias_ref, o_ref, *, rows, scale):
    g = pl.program_id(2)
    n_groups = rows // NA_ROWS_PER_STEP
    kb = jnp.clip(NA_ROWS_PER_STEP * g - 4, 0, rows - 2 * NA_ROWS_PER_STEP) * GRID_W
    kb = pl.multiple_of(kb, GRID_W)
    variant = jnp.where(g == 0, 0, jnp.where(g == n_groups - 1, 2, 1))
    nk = 2 * NA_ROWS_PER_STEP * GRID_W
    q = q_ref[0]
    k = k_ref[0, pl.ds(kb, nk), :]
    v = v_ref[0, pl.ds(kb, nk), :]
    s = lax.dot_general(q, k, (((1,), (1,)), ((), ())), preferred_element_type=F32)
    s = s * scale + bias_ref[variant]
    m = jnp.max(s, -1, keepdims=True)
    p = jnp.exp(s - m)
    l = jnp.sum(p, -1, keepdims=True)
    o = jnp.dot(p.astype(BF16), v, preferred_element_type=F32) / l
    o_ref[0] = o.astype(o_ref.dtype)


def _neighbourhood_attention(proj3, b0, bsz, seq_len, bias, nh, hd):
    rows = seq_len // GRID_W
    assert rows % NA_ROWS_PER_STEP == 0 and rows >= 2 * NA_ROWS_PER_STEP
    nq = NA_ROWS_PER_STEP * GRID_W
    return pl.pallas_call(
        functools.partial(_na_kernel, rows=rows, scale=hd ** -0.5),
        grid=(nh, bsz, rows // NA_ROWS_PER_STEP),
        in_specs=[pl.BlockSpec((1, nq, hd), lambda h, b, g: (b0 + b, g, h)),
                  pl.BlockSpec((1, seq_len, hd), lambda h, b, g: (b0 + b, 0, nh + h)),
                  pl.BlockSpec((1, seq_len, hd), lambda h, b, g: (b0 + b, 0, 2 * nh + h)),
                  pl.BlockSpec((3, None, nq, 2 * nq), lambda h, b, g: (0, h, 0, 0))],
        out_specs=pl.BlockSpec((1, nq, hd), lambda h, b, g: (b, g, h)),
        out_shape=jax.ShapeDtypeStruct((bsz, seq_len, nh * hd), BF16),
        compiler_params=_cparams(("parallel", "parallel", "arbitrary")),
        name="neighbourhood_attention",
    )(proj3, proj3, proj3, bias)


def _conv3_kernel(x_ref, w_ref, b_ref, o_ref, *, seq_len, rb):
    w = w_ref[...]
    bias = b_ref[...]
    n_chunks = seq_len // rb

    def body(c, carry):
        r0 = pl.multiple_of(c * rb, rb)
        x = x_ref[0, pl.ds(r0, rb), :].astype(F32)
        row = lax.broadcasted_iota(jnp.int32, x.shape, 0)
        p0 = pl.multiple_of(jnp.maximum(r0 - 16, 0), 16)
        n0 = pl.multiple_of(jnp.minimum(r0 + rb, seq_len - 16), 16)
        prev_row = x_ref[0, pl.ds(p0, 16), :].astype(F32)[15:16]
        next_row = x_ref[0, pl.ds(n0, 16), :].astype(F32)[0:1]
        prev_row = jnp.where(c == 0, 0.0, prev_row)
        next_row = jnp.where(c == n_chunks - 1, 0.0, next_row)
        up = jnp.where(row == 0, prev_row, pltpu.roll(x, 1, 0))
        down = jnp.where(row == rb - 1, next_row, pltpu.roll(x, rb - 1, 0))
        y = up * w[0:1] + x * w[1:2] + down * w[2:3] + bias
        o_ref[0, pl.ds(r0, rb), :] = y.astype(o_ref.dtype)
        return carry

    lax.fori_loop(0, n_chunks, body, 0)


def _short_conv3(proj3, b0, bsz, seq_len, col0, w, b, cb=256, rb=512):
    width = w.shape[1]
    rb = min(rb, seq_len)
    c0 = col0 // cb
    return pl.pallas_call(
        functools.partial(_conv3_kernel, seq_len=seq_len, rb=rb),
        grid=(bsz, width // cb),
        in_specs=[pl.BlockSpec((1, seq_len, cb), lambda bi, c: (b0 + bi, 0, c0 + c)),
                  pl.BlockSpec((3, cb), lambda bi, c: (0, c)),
                  pl.BlockSpec((1, cb), lambda bi, c: (0, c))],
        out_specs=pl.BlockSpec((1, seq_len, cb), lambda bi, c: (bi, 0, c)),
        out_shape=jax.ShapeDtypeStruct((bsz, seq_len, width), BF16),
        compiler_params=_cparams(("parallel", "parallel")),
        name="hyena_short_conv",
    )(proj3, w, b.reshape(1, width))


def _filter_kernel(z_ref, w1_ref, b1_ref, fr_ref, w2_ref, b2_ref, w3_ref, b3_ref, dl_ref, o_ref, *,
                   seq_len, rb):
    i = pl.program_id(0)
    hi = lax.Precision.HIGHEST
    z = z_ref[...]
    h = jnp.sin(fr_ref[0:1] * (jnp.dot(z, w1_ref[...], precision=hi, preferred_element_type=F32)
                               + b1_ref[...]))
    h = jnp.sin(fr_ref[1:2] * (jnp.dot(h, w2_ref[...], precision=hi, preferred_element_type=F32)
                               + b2_ref[...]))
    h = jnp.dot(h, w3_ref[...], precision=hi, preferred_element_type=F32) + b3_ref[...]
    decay = jnp.exp(-z[:, 0:1] * dl_ref[...])
    n = i * rb + lax.broadcasted_iota(jnp.int32, h.shape, 0)
    sign = jnp.where(n < seq_len, 1.0, jnp.where(n == seq_len, 0.0, -1.0))
    o_ref[...] = (h * decay * sign).astype(o_ref.dtype)


def _hyena_filter_signal(seq_len, w1, b1, freq, w2, b2, w3, b3, order, ch, rb=512):
    n = 2 * seq_len
    emb, hid = w1.shape
    pad = LANES
    t = jnp.linspace(0.0, 1.0, seq_len, dtype=F32)[:, None]
    bands = (emb - 1) // 2
    fr = jnp.linspace(1e-4, bands - 1, bands, dtype=F32)[None, :]
    wpos = 2.0 * math.pi * jnp.arange(seq_len, dtype=F32)[:, None] / seq_len
    z = jnp.concatenate([t, jnp.cos(fr * wpos), -jnp.sin(fr * wpos)], axis=-1)
    pos = np.arange(n)
    src = np.clip(np.where(pos < seq_len, pos, n - pos), 0, seq_len - 1)
    z2 = jnp.pad(z[src], ((0, 0), (0, pad - emb)))
    w1p = jnp.pad(w1.astype(F32), ((0, pad - emb), (0, pad - hid)))
    b1p = jnp.pad(b1.astype(F32), (0, pad - hid)).reshape(1, pad)
    frp = jnp.pad(freq.astype(F32), ((0, 0), (0, pad - hid)), constant_values=1.0)
    w2p = jnp.pad(w2.astype(F32), ((0, pad - hid), (0, pad - hid)))
    b2p = jnp.pad(b2.astype(F32), (0, pad - hid)).reshape(1, pad)
    w3p = jnp.pad(w3.astype(F32), ((0, pad - hid), (0, 0))).reshape(pad, order * 2, ch).transpose(1, 0, 2)
    b3p = b3.astype(F32).reshape(order * 2, 1, ch)
    deltas = jnp.abs(jnp.linspace(HY_MIN_DECAY, HY_MAX_DECAY, ch, dtype=F32)).reshape(1, ch)
    nb = n // rb
    half = nb // 2
    full = lambda i, o: (0, 0)
    return pl.pallas_call(
        functools.partial(_filter_kernel, seq_len=seq_len, rb=rb),
        grid=(nb, order),
        in_specs=[pl.BlockSpec((rb, pad), lambda i, o: (i, 0)),
                  pl.BlockSpec((pad, pad), full), pl.BlockSpec((1, pad), full),
                  pl.BlockSpec((2, pad), full),
                  pl.BlockSpec((pad, pad), full), pl.BlockSpec((1, pad), full),
                  pl.BlockSpec((None, pad, ch), lambda i, o: (2 * o + i // half, 0, 0)),
                  pl.BlockSpec((None, 1, ch), lambda i, o: (2 * o + i // half, 0, 0)),
                  pl.BlockSpec((1, ch), full)],
        out_specs=pl.BlockSpec((rb, ch), lambda i, o: (i, o)),
        out_shape=jax.ShapeDtypeStruct((n, order * ch), BF16),
        compiler_params=_cparams(("parallel", "arbitrary")),
        name="hyena_filter_mlp",
    )(z2, w1p, b1p, frp, w2p, b2p, w3p, b3p, deltas)


def _dft_tables(seq_len):
    n = 2 * seq_len
    n2 = DFT_N2
    n1 = n // n2
    hk = n2 // 2
    pi = math.pi
    k1 = jnp.arange(n1, dtype=jnp.int32)
    ph1 = ((2 * k1[:, None] + 1) * k1[None, :]) % (2 * n1)
    th1 = ph1.astype(F32) * (pi / n1)
    c1, s1 = jnp.cos(th1), jnp.sin(th1)
    f1 = jnp.stack([c1, -s1], axis=1).reshape(2 * n1, n1)
    fb = jnp.stack([c1, -s1], axis=1).reshape(2 * n1, n1).T[: n1 // 2] * (2.0 / n)
    k = k1[:, None, None] + n1 * jnp.arange(hk, dtype=jnp.int32)[None, :, None]
    ph2 = ((2 * k + 1) * jnp.arange(n2, dtype=jnp.int32)[None, None, :]) % (2 * n)
    th2 = ph2.astype(F32) * (pi / n)
    c2, s2 = jnp.cos(th2), jnp.sin(th2)
    g = jnp.concatenate([jnp.concatenate([c2, s2], axis=2),
                         jnp.concatenate([-s2, c2], axis=2)], axis=1)
    c2t, s2t = c2.transpose(0, 2, 1), s2.transpose(0, 2, 1)
    h = jnp.concatenate([jnp.concatenate([c2t, -s2t], axis=2),
                         jnp.concatenate([s2t, c2t], axis=2)], axis=1)
    return dict(n1=n1, f1_full=f1.astype(BF16), f1_half=f1[:, : n1 // 2].astype(BF16),
                fb=fb.astype(BF16), g=g.astype(BF16), h=h.astype(BF16))


def _lmm_kernel(w_ref, x_ref, o_ref):
    w = w_ref[...]
    for b in range(x_ref.shape[0]):
        o_ref[b] = jnp.dot(w, x_ref[b], preferred_element_type=F32).astype(o_ref.dtype)


def _lmm_gate_kernel(w_ref, x_ref, u_ref, g_ref, s_ref, o_ref):
    w = w_ref[...]
    skip = s_ref[...]
    for b in range(x_ref.shape[0]):
        y = jnp.dot(w, x_ref[b], preferred_element_type=F32)
        y = (y + u_ref[b].astype(F32) * skip) * g_ref[b].astype(F32)
        o_ref[b] = y.astype(o_ref.dtype)


def _slab_dft(w, x3, col_stride, col_off, n_cols, ch):
    bsz, k, _ = x3.shape
    m = w.shape[0]
    return pl.pallas_call(
        _lmm_kernel,
        grid=(n_cols,),
        in_specs=[pl.BlockSpec((m, k), lambda j: (0, 0)),
                  pl.BlockSpec((bsz, k, ch), lambda j: (0, 0, col_stride * j + col_off))],
        out_specs=pl.BlockSpec((bsz, m, ch), lambda j: (0, 0, j)),
        out_shape=jax.ShapeDtypeStruct((bsz, m, n_cols * ch), BF16),
        compiler_params=_cparams(("parallel",)),
        name="hyena_slab_dft",
    )(w, x3)


def _slab_idft_gate(w, z3, u3, u_stride, u_off, g3, g_stride, g_off, skip, ch):
    bsz, k, cols = z3.shape
    m = w.shape[0]
    n_cols = cols // ch
    return pl.pallas_call(
        _lmm_gate_kernel,
        grid=(n_cols,),
        in_specs=[pl.BlockSpec((m, k), lambda j: (0, 0)),
                  pl.BlockSpec((bsz, k, ch), lambda j: (0, 0, j)),
                  pl.BlockSpec((bsz, m, ch), lambda j: (0, 0, u_stride * j + u_off)),
                  pl.BlockSpec((bsz, m, ch), lambda j: (0, 0, g_stride * j + g_off)),
                  pl.BlockSpec((1, ch), lambda j: (0, 0))],
        out_specs=pl.BlockSpec((bsz, m, ch), lambda j: (0, 0, j)),
        out_shape=jax.ShapeDtypeStruct((bsz, m, cols), BF16),
        compiler_params=_cparams(("parallel",)),
        name="hyena_slab_idft_gate",
    )(w, z3, u3, g3, skip.astype(F32).reshape(1, ch))


def _spec_filter_kernel(a_ref, g_ref, o_ref):
    for i in range(a_ref.shape[0]):
        a = a_ref[i].reshape(2 * DFT_N2, a_ref.shape[-1])
        o_ref[i] = jnp.dot(g_ref[i], a, preferred_element_type=F32)


def _filter_spectrum(a5, g, kb=4):
    n1, _, n2, cols = a5.shape
    cb = min(cols, 1024)
    return pl.pallas_call(
        _spec_filter_kernel,
        grid=(n1 // kb, cols // cb),
        in_specs=[pl.BlockSpec((kb, 2, n2, cb), lambda j, c: (j, 0, 0, c)),
                  pl.BlockSpec((kb, n2, 2 * n2), lambda j, c: (j, 0, 0))],
        out_specs=pl.BlockSpec((kb, n2, cb), lambda j, c: (j, 0, c)),
        out_shape=jax.ShapeDtypeStruct((n1, n2, cols), F32),
        compiler_params=_cparams(("parallel", "parallel")),
        name="hyena_filter_spectrum",
    )(a5, g)


def _spec_kernel(a_ref, g_ref, h_ref, kf_ref, o_ref):
    hk = DFT_N2 // 2
    for i in range(a_ref.shape[1]):
        a = a_ref[0, i].reshape(2 * DFT_N2, a_ref.shape[-1])
        x = jnp.dot(g_ref[i], a, preferred_element_type=F32)
        kf = kf_ref[i]
        xr, xi = x[:hk], x[hk:]
        kr, ki = kf[:hk], kf[hk:]
        y = jnp.concatenate([xr * kr - xi * ki, xr * ki + xi * kr], axis=0).astype(BF16)
        z = jnp.dot(h_ref[i], y, preferred_element_type=F32)
        o_ref[0, i] = z.reshape(2, DFT_N2, z.shape[-1]).astype(o_ref.dtype)


def _spectral_product(a5, g, h, kf, order_idx, ch, kb=4):
    bsz, n1, _, n2, _ = a5.shape
    return pl.pallas_call(
        _spec_kernel,
        grid=(n1 // kb, bsz),
        in_specs=[pl.BlockSpec((1, kb, 2, n2, ch), lambda j, b: (b, j, 0, 0, 0)),
                  pl.BlockSpec((kb, n2, 2 * n2), lambda j, b: (j, 0, 0)),
                  pl.BlockSpec((kb, 2 * n2, n2), lambda j, b: (j, 0, 0)),
                  pl.BlockSpec((kb, n2, ch), lambda j, b: (j, 0, order_idx))],
        out_specs=pl.BlockSpec((1, kb, 2, n2, ch), lambda j, b: (b, j, 0, 0, 0)),
        out_shape=jax.ShapeDtypeStruct(a5.shape, BF16),
        compiler_params=_cparams(("parallel", "arbitrary")),
        name="hyena_spectral_product",
    )(a5, g, h, kf)


def _hyena(hy, tables, kf, skip, order, ch):
    bsz, seq_len, width = hy.shape
    n1 = tables["n1"]
    nblk = width // ch
    hy3 = hy.reshape(bsz, n1 // 2, DFT_N2 * width)
    z3, z_stride, z_off = hy3, nblk, 0
    for o in range(order):
        a = _slab_dft(tables["f1_half"], z3, z_stride, z_off, DFT_N2, ch)
        a5 = a.reshape(bsz, n1, 2, DFT_N2, ch)
        zz = _spectral_product(a5, tables["g"], tables["h"], kf, o, ch)
        zz3 = zz.reshape(bsz, 2 * n1, DFT_N2 * ch)
        z3 = _slab_idft_gate(tables["fb"], zz3, z3, z_stride, z_off, hy3, nblk, o + 1, skip[o], ch)
        z_stride, z_off = 1, 0
    return z3.reshape(bsz, seq_len, ch)


def _rope(x, cos, sin):
    half = x.shape[-1] // 2
    x1, x2 = x[:, :half], x[:, half:]
    return jnp.concatenate([x1 * cos - x2 * sin, x1 * sin + x2 * cos], axis=-1)


def _ret_kernel(lg_ref, q_ref, k_ref, v_ref, g_ref, cos_ref, sin_ref, o_ref, sb_all, sf_ref, sb_ref, *,
                n_super, chunks_per_super, k_scale):
    h = pl.program_id(1)
    s = pl.program_id(2)
    cc = RET_CHUNK
    dh = q_ref.shape[-1]
    lgf = lg_ref[0, h]
    lgb = lg_ref[1, h]
    row = lax.broadcasted_iota(jnp.int32, (cc, dh), 0).astype(F32)

    def load_k(c0):
        cos = cos_ref[pl.ds(c0, cc), :]
        sin = sin_ref[pl.ds(c0, cc), :]
        return _rope(k_ref[0, pl.ds(c0, cc), :].astype(F32), cos, sin) * k_scale

    @pl.when(s < n_super)
    def _backward_sweep():
        @pl.when(s == 0)
        def _():
            sb_ref[...] = jnp.zeros_like(sb_ref)

        sup = n_super - 1 - s
        zeta_b = jnp.exp(lgb * row)
        chunk_decay = jnp.exp(lgb * jnp.full((1, dh), float(cc), F32))

        def body(t, carry):
            c = chunks_per_super - 1 - t
            c0 = pl.multiple_of(c * cc, cc)
            sb_all[sup * chunks_per_super + c] = sb_ref[...].astype(BF16)
            kz = (load_k(c0) * zeta_b).astype(BF16)
            v = v_ref[0, pl.ds(c0, cc), :]
            upd = lax.dot_general(kz, v, (((0,), (0,)), ((), ())), preferred_element_type=F32)
            sb_ref[...] = sb_ref[...] * chunk_decay + upd
            return carry

        lax.fori_loop(0, chunks_per_super, body, 0)

    @pl.when(s >= n_super)
    def _forward_sweep():
        @pl.when(s == n_super)
        def _():
            sf_ref[...] = jnp.zeros_like(sf_ref)

        sup = s - n_super
        col = lax.broadcasted_iota(jnp.int32, (cc, cc), 1).astype(F32)
        rowc = lax.broadcasted_iota(jnp.int32, (cc, cc), 0).astype(F32)
        diff = rowc - col
        inner_decay = jnp.where(diff >= 0, jnp.exp(lgf * jnp.maximum(diff, 0.0)),
                                jnp.exp(lgb * jnp.maximum(-diff, 0.0)))
        xi_f = jnp.exp(lgf * (row + 1.0))
        xi_b = jnp.exp(lgb * (cc - row))
        zeta_f = jnp.exp(lgf * (cc - 1.0 - row))
        chunk_decay = jnp.exp(lgf * jnp.full((1, dh), float(cc), F32))

        def body(c, carry):
            c0 = pl.multiple_of(c * cc, cc)
            cos = cos_ref[pl.ds(c0, cc), :]
            sin = sin_ref[pl.ds(c0, cc), :]
            q = _rope(q_ref[0, pl.ds(c0, cc), :].astype(F32), cos, sin)
            k = load_k(c0)
            v = v_ref[0, pl.ds(c0, cc), :]
            sc = lax.dot_general(q.astype(BF16), k.astype(BF16), (((1,), (1,)), ((), ())),
                                 preferred_element_type=F32) * inner_decay
            ret = jnp.dot(sc.astype(BF16), v, preferred_element_type=F32)
            ret = ret + jnp.dot((q * xi_f).astype(BF16), sf_ref[...].astype(BF16),
                                preferred_element_type=F32)
            ret = ret + jnp.dot((q * xi_b).astype(BF16), sb_all[sup * chunks_per_super + c],
                                preferred_element_type=F32)
            upd = lax.dot_general((k * zeta_f).astype(BF16), v, (((0,), (0,)), ((), ())),
                                  preferred_element_type=F32)
            sf_ref[...] = sf_ref[...] * chunk_decay + upd
            ret = ret * lax.rsqrt(jnp.mean(ret * ret, -1, keepdims=True) + RMS_EPS)
            gate = g_ref[0, pl.ds(c0, cc), :].astype(F32)
            o_ref[0, pl.ds(c0, cc), :] = (ret * (gate / (1.0 + jnp.exp(-gate)))).astype(o_ref.dtype)
            return carry

        lax.fori_loop(0, chunks_per_super, body, 0)


def _retention(proj3, b0, bsz, seq_len, col0, nh, dh, log_gamma, cos, sin):
    sup_len = min(RET_SUPER, seq_len)
    n_super = seq_len // sup_len
    cps = sup_len // RET_CHUNK
    cb0 = col0 // dh

    def fwd_idx(s):
        return jnp.maximum(s - n_super, 0)

    def kv_idx(s):
        return jnp.where(s < n_super, n_super - 1 - s, s - n_super)

    return pl.pallas_call(
        functools.partial(_ret_kernel, n_super=n_super, chunks_per_super=cps, k_scale=dh ** -0.5),
        grid=(bsz, nh, 2 * n_super),
        in_specs=[pl.BlockSpec(memory_space=pltpu.SMEM),
                  pl.BlockSpec((1, sup_len, dh), lambda b, h, s: (b0 + b, fwd_idx(s), cb0 + h)),
                  pl.BlockSpec((1, sup_len, dh), lambda b, h, s: (b0 + b, kv_idx(s), cb0 + nh + h)),
                  pl.BlockSpec((1, sup_len, dh), lambda b, h, s: (b0 + b, kv_idx(s), cb0 + 2 * nh + h)),
                  pl.BlockSpec((1, sup_len, dh), lambda b, h, s: (b0 + b, fwd_idx(s), cb0 + 3 * nh + h)),
                  pl.BlockSpec((sup_len, dh // 2), lambda b, h, s: (kv_idx(s), 0)),
                  pl.BlockSpec((sup_len, dh // 2), lambda b, h, s: (kv_idx(s), 0))],
        out_specs=pl.BlockSpec((1, sup_len, dh), lambda b, h, s: (b, fwd_idx(s), h)),
        out_shape=jax.ShapeDtypeStruct((bsz, seq_len, nh * dh), BF16),
        scratch_shapes=[pltpu.VMEM((seq_len // RET_CHUNK, dh, dh), BF16),
                        pltpu.VMEM((dh, dh), F32),
                        pltpu.VMEM((dh, dh), F32)],
        compiler_params=_cparams(("parallel", "parallel", "arbitrary")),
        name="retention",
    )(log_gamma, proj3, proj3, proj3, proj3, cos, sin)


def _assemble_kernel(a_ref, z_ref, c_ref, ga_ref, gb_ref, o_ref):
    wa = a_ref.shape[-1]
    wb = z_ref.shape[-1]
    a = a_ref[...].astype(F32)
    a = a * lax.rsqrt(jnp.mean(a * a, -1, keepdims=True) + RMS_EPS) * ga_ref[...]
    z = z_ref[...].astype(F32)
    z = z * lax.rsqrt(jnp.mean(z * z, -1, keepdims=True) + RMS_EPS) * gb_ref[...]
    o_ref[:, :wa] = a.astype(o_ref.dtype)
    o_ref[:, wa:wa + wb] = z.astype(o_ref.dtype)
    o_ref[:, wa + wb:] = c_ref[...]


def _assemble(a, z, c, gain_a, gain_b, bm=512):
    m, wa = a.shape
    wb, wc = z.shape[1], c.shape[1]
    return pl.pallas_call(
        _assemble_kernel,
        grid=(m // bm,),
        in_specs=[pl.BlockSpec((bm, wa), lambda i: (i, 0)),
                  pl.BlockSpec((bm, wb), lambda i: (i, 0)),
                  pl.BlockSpec((bm, wc), lambda i: (i, 0)),
                  pl.BlockSpec((1, wa), lambda i: (0, 0)),
                  pl.BlockSpec((1, wb), lambda i: (0, 0))],
        out_specs=pl.BlockSpec((bm, wa + wb + wc), lambda i: (i, 0)),
        out_shape=jax.ShapeDtypeStruct((m, wa + wb + wc), BF16),
        compiler_params=_cparams(("parallel",)),
        name="assemble_mixers",
    )(a, z, c, gain_a.astype(F32).reshape(1, wa), gain_b.astype(F32).reshape(1, wb))


def _token_mixers(proj, requests, p, l, dims):
    nh_a, hd_a, ch, order, nh_r, dh_r = dims
    in_cols = proj.shape[1]
    total = proj.shape[0]
    wa = nh_a * hd_a
    hy0 = 3 * wa
    ret0 = hy0 + (order + 1) * ch
    log_gamma = jnp.log1p(-jnp.exp2(-p["ret_decay_exp"][l].astype(F32)))
    a_parts, z_parts, c_parts = [], [], []
    for (bsz, seq_len, tok0) in requests:
        assert tok0 % seq_len == 0 and total % seq_len == 0
        proj3 = proj.reshape(total // seq_len, seq_len, in_cols)
        b0 = tok0 // seq_len
        bias = _na_bias_tables(p["na_rpb"][l], seq_len // GRID_W)
        a = _neighbourhood_attention(proj3, b0, bsz, seq_len, bias, nh_a, hd_a)
        a_parts.append(a.reshape(bsz * seq_len, wa))
        tables = _dft_tables(seq_len)
        n1 = tables["n1"]
        f = _hyena_filter_signal(seq_len, p["hy_f_w1"][l], p["hy_f_b1"][l], p["hy_f_freq"][l],
                                 p["hy_f_w2"][l], p["hy_f_b2"][l], p["hy_f_w3"][l], p["hy_f_b3"][l],
                                 order, ch)
        fa = _slab_dft(tables["f1_full"], f.reshape(1, n1, DFT_N2 * order * ch), 1, 0,
                       DFT_N2 * order, ch)
        kf = _filter_spectrum(fa.reshape(n1, 2, DFT_N2, order * ch), tables["g"])
        hy = _short_conv3(proj3, b0, bsz, seq_len, hy0, p["hy_conv_w"][l].astype(F32),
                          p["hy_conv_b"][l].astype(F32))
        z = _hyena(hy, tables, kf, p["hy_skip"][l], order, ch)
        z_parts.append(z.reshape(bsz * seq_len, ch))
        half = dh_r // 2
        inv_freq = 1.0 / (ROPE_BASE ** jnp.linspace(0.0, 1.0, half, dtype=F32))
        ang = jnp.arange(seq_len, dtype=F32)[:, None] * inv_freq[None, :]
        c = _retention(proj3, b0, bsz, seq_len, ret0, nh_r, dh_r, log_gamma, jnp.cos(ang), jnp.sin(ang))
        c_parts.append(c.reshape(bsz * seq_len, nh_r * dh_r))
    cat = lambda parts: parts[0] if len(parts) == 1 else jnp.concatenate(parts, axis=0)
    return _assemble(cat(a_parts), cat(z_parts), cat(c_parts), p["grp_gain_a"][l], p["grp_gain_b"][l])


def _pick(n, candidates):
    for c in candidates:
        if n % c == 0:
            return c
    raise ValueError(f"no block size for {n}")


def kernel(x_prompt, x_sample, ln_in_g, ln_in_b, w_in, na_rpb, hy_conv_w, hy_conv_b, hy_f_w1, hy_f_b1,
           hy_f_freq, hy_f_w2, hy_f_b2, hy_f_w3, hy_f_b3, hy_skip, ret_decay_exp, grp_gain_a, grp_gain_b,
           w_out, ln1_g, ln1_b, w_ffn_in, w_ffn_out, ln2_g, ln2_b):
    p = dict(na_rpb=na_rpb, hy_conv_w=hy_conv_w, hy_conv_b=hy_conv_b, hy_f_w1=hy_f_w1, hy_f_b1=hy_f_b1,
             hy_f_freq=hy_f_freq, hy_f_w2=hy_f_w2, hy_f_b2=hy_f_b2, hy_f_w3=hy_f_w3, hy_f_b3=hy_f_b3,
             hy_skip=hy_skip, ret_decay_exp=ret_decay_exp, grp_gain_a=grp_gain_a, grp_gain_b=grp_gain_b)
    depth, d_model, in_cols = w_in.shape
    nh_a = na_rpb.shape[1]
    wa = grp_gain_a.shape[1]
    ch = grp_gain_b.shape[1]
    order = hy_skip.shape[1]
    nh_r = ret_decay_exp.shape[2]
    wr = (in_cols - 3 * wa - (order + 1) * ch) // 4
    dims = (nh_a, wa // nh_a, ch, order, nh_r, wr // nh_r)
    d_ff = w_ffn_out.shape[1]
    alpha = (2.0 * depth) ** 0.25

    bp, lp, _ = x_prompt.shape
    bs, ls, _ = x_sample.shape
    requests = [(bp, lp, 0), (bs, ls, bp * lp)]
    x = jnp.concatenate([x_prompt.reshape(bp * lp, d_model), x_sample.reshape(bs * ls, d_model)], axis=0)
    m = x.shape[0]
    bm = _pick(m, (1024, 512, 256))
    bm_ln = _pick(m, (256,))

    xf, xb = _layernorm(x, ln_in_g.astype(F32), ln_in_b.astype(F32), bm_ln)
    for l in range(depth):
        proj = _matmul(xb, w_in[l].astype(BF16), bm, _pick(in_cols, (1024, 512, 256)))
        mixed = _token_mixers(proj, requests, p, l, dims)
        y = _matmul_residual(mixed, w_out[l].astype(BF16), xf, alpha, bm, _pick(d_model, (1024, 512, 256)))
        xf, xb = _layernorm(y, ln1_g[l].astype(F32), ln1_b[l].astype(F32), bm_ln)
        hmid = _matmul_swiglu(xb, w_ffn_in[l].astype(BF16), bm, _pick(d_ff, (512, 256, 128)))
        y = _matmul_residual(hmid, w_ffn_out[l].astype(BF16), xf, alpha, _pick(m, (512, 256)),
                             _pick(d_model, (512, 256)))
        xf, xb = _layernorm(y, ln2_g[l].astype(F32), ln2_b[l].astype(F32), bm_ln)
    return (xf[: bp * lp].reshape(bp, lp, d_model), xf[bp * lp:].reshape(bs, ls, d_model))
```

```python
import functools
import math

import numpy as np
import jax
import jax.numpy as jnp
from jax import lax
from jax.experimental import pallas as pl
from jax.experimental.pallas import tpu as pltpu

F32 = jnp.float32
BF16 = jnp.bfloat16

GRID_W = 64
WIN_R = 8
WIN_C = 16
ROPE_BASE = 10000.0
LN_EPS = 1e-5
RMS_EPS = 1e-6
NEG_INF = -1e30
HY_TARGET = 1e-2
HY_FAST_DECAY = 0.3
HY_SLOW_DECAY = 1.5
HY_MIN_DECAY = math.log(HY_TARGET) / HY_SLOW_DECAY
HY_MAX_DECAY = math.log(HY_TARGET) / HY_FAST_DECAY

V7X_VMEM_LIMIT = 56 * 1024 * 1024
LANES = 128
SUBLANES = 8
DFT_N2 = 128
DFT_GROUP = 32
NA_ROWS_PER_STEP = 8
RET_CHUNK = 256
RET_SUPER = 2048


def _cparams(sem, vmem=V7X_VMEM_LIMIT):
    return pltpu.CompilerParams(dimension_semantics=sem, vmem_limit_bytes=vmem)


def _pitch(rows):
    p = -(-rows // SUBLANES)
    return SUBLANES * (p + 1 - p % 2)


def _ln(x, g, b):
    mu = jnp.mean(x, -1, keepdims=True)
    xc = x - mu
    var = jnp.mean(xc * xc, -1, keepdims=True)
    return xc * lax.rsqrt(var + LN_EPS) * g + b


def _ln_kernel(x_ref, g_ref, b_ref, of_ref, ob_ref):
    y = _ln(x_ref[...], g_ref[...], b_ref[...])
    of_ref[...] = y
    ob_ref[...] = y.astype(BF16)


def _ln_join_kernel(x1_ref, x2_ref, g_ref, b_ref, of_ref, ob_ref, *, nb1):
    x = jnp.where(pl.program_id(0) < nb1, x1_ref[...], x2_ref[...])
    y = _ln(x, g_ref[...], b_ref[...])
    of_ref[...] = y
    ob_ref[...] = y.astype(BF16)


def _ln_split_kernel(x_ref, g_ref, b_ref, o1_ref, o2_ref, *, nb1):
    y = _ln(x_ref[...], g_ref[...], b_ref[...])

    @pl.when(pl.program_id(0) < nb1)
    def _():
        o1_ref[...] = y

    @pl.when(pl.program_id(0) >= nb1)
    def _():
        o2_ref[...] = y


def _first_rows(nb1):
    return lambda i: (jnp.minimum(i, nb1 - 1), 0)


def _second_rows(nb1):
    return lambda i: (jnp.maximum(i - nb1, 0), 0)


def _layernorm_join(x1, x2, g, b, bm=256):
    m1, d = x1.shape
    m = m1 + x2.shape[0]
    nb1 = m1 // bm
    return pl.pallas_call(
        functools.partial(_ln_join_kernel, nb1=nb1),
        grid=(m // bm,),
        in_specs=[pl.BlockSpec((bm, d), _first_rows(nb1)),
                  pl.BlockSpec((bm, d), _second_rows(nb1)),
                  pl.BlockSpec((1, d), lambda i: (0, 0)),
                  pl.BlockSpec((1, d), lambda i: (0, 0))],
        out_specs=[pl.BlockSpec((bm, d), lambda i: (i, 0)),
                   pl.BlockSpec((bm, d), lambda i: (i, 0))],
        out_shape=[jax.ShapeDtypeStruct((m, d), F32), jax.ShapeDtypeStruct((m, d), BF16)],
        compiler_params=_cparams(("arbitrary",)),
        name="layernorm_join",
    )(x1, x2, g.reshape(1, d), b.reshape(1, d))


def _layernorm_split(x, g, b, m1, bm=256):
    m, d = x.shape
    nb1 = m1 // bm
    return pl.pallas_call(
        functools.partial(_ln_split_kernel, nb1=nb1),
        grid=(m // bm,),
        in_specs=[pl.BlockSpec((bm, d), lambda i: (i, 0)),
                  pl.BlockSpec((1, d), lambda i: (0, 0)),
                  pl.BlockSpec((1, d), lambda i: (0, 0))],
        out_specs=[pl.BlockSpec((bm, d), _first_rows(nb1)),
                   pl.BlockSpec((bm, d), _second_rows(nb1))],
        out_shape=[jax.ShapeDtypeStruct((m1, d), F32), jax.ShapeDtypeStruct((m - m1, d), F32)],
        compiler_params=_cparams(("arbitrary",)),
        name="layernorm_split",
    )(x, g.reshape(1, d), b.reshape(1, d))


def _layernorm(x, g, b, bm=256):
    m, d = x.shape
    return pl.pallas_call(
        _ln_kernel,
        grid=(m // bm,),
        in_specs=[pl.BlockSpec((bm, d), lambda i: (i, 0)),
                  pl.BlockSpec((1, d), lambda i: (0, 0)),
                  pl.BlockSpec((1, d), lambda i: (0, 0))],
        out_specs=[pl.BlockSpec((bm, d), lambda i: (i, 0)),
                   pl.BlockSpec((bm, d), lambda i: (i, 0))],
        out_shape=[jax.ShapeDtypeStruct((m, d), F32), jax.ShapeDtypeStruct((m, d), BF16)],
        compiler_params=_cparams(("parallel",)),
        name="layernorm",
    )(x, g.reshape(1, d), b.reshape(1, d))


def _mm_kernel(x_ref, w_ref, o_ref):
    o_ref[...] = jnp.dot(x_ref[...], w_ref[...], preferred_element_type=F32).astype(o_ref.dtype)


def _mm_res_kernel(x_ref, w_ref, r_ref, o_ref, *, alpha):
    o_ref[...] = alpha * r_ref[...] + jnp.dot(x_ref[...], w_ref[...], preferred_element_type=F32)


def _mm_swiglu_kernel(x_ref, wg_ref, wu_ref, o_ref):
    x = x_ref[...]
    g = jnp.dot(x, wg_ref[...], preferred_element_type=F32)
    u = jnp.dot(x, wu_ref[...], preferred_element_type=F32)
    o_ref[...] = (g / (1.0 + jnp.exp(-g)) * u).astype(o_ref.dtype)


def _matmul(x, w, bm, bn, out_dtype=BF16):
    m, k = x.shape
    n = w.shape[1]
    return pl.pallas_call(
        _mm_kernel,
        grid=(m // bm, n // bn),
        in_specs=[pl.BlockSpec((bm, k), lambda i, j: (i, 0)),
                  pl.BlockSpec((k, bn), lambda i, j: (0, j))],
        out_specs=pl.BlockSpec((bm, bn), lambda i, j: (i, j)),
        out_shape=jax.ShapeDtypeStruct((m, n), out_dtype),
        compiler_params=_cparams(("parallel", "arbitrary")),
        name="matmul",
    )(x, w)


def _matmul_residual(x, w, r, alpha, bm, bn):
    m, k = x.shape
    n = w.shape[1]
    return pl.pallas_call(
        functools.partial(_mm_res_kernel, alpha=alpha),
        grid=(m // bm, n // bn),
        in_specs=[pl.BlockSpec((bm, k), lambda i, j: (i, 0)),
                  pl.BlockSpec((k, bn), lambda i, j: (0, j)),
                  pl.BlockSpec((bm, bn), lambda i, j: (i, j))],
        out_specs=pl.BlockSpec((bm, bn), lambda i, j: (i, j)),
        out_shape=jax.ShapeDtypeStruct((m, n), F32),
        compiler_params=_cparams(("parallel", "arbitrary")),
        name="matmul_residual",
    )(x, w, r)


def _matmul_swiglu(x, w, bm, bn):
    m, k = x.shape
    f = w.shape[1] // 2
    nb = f // bn
    return pl.pallas_call(
        _mm_swiglu_kernel,
        grid=(m // bm, nb),
        in_specs=[pl.BlockSpec((bm, k), lambda i, j: (i, 0)),
                  pl.BlockSpec((k, bn), lambda i, j: (0, j)),
                  pl.BlockSpec((k, bn), lambda i, j: (0, j + nb))],
        out_specs=pl.BlockSpec((bm, bn), lambda i, j: (i, j)),
        out_shape=jax.ShapeDtypeStruct((m, f), BF16),
        compiler_params=_cparams(("parallel", "arbitrary")),
        name="matmul_swiglu",
    )(x, w, w)


def _na_bias_tables(rpb):
    nh = rpb.shape[0]
    r8 = NA_ROWS_PER_STEP
    i = np.arange(r8)[:, None]
    j = np.arange(2 * r8)[None, :]
    rel_start = [np.maximum(i - 4, 0), i, np.minimum(i + 4, 8)]
    q_rel = [i, i + 4, i + 8]
    qc = np.arange(GRID_W)[:, None]
    kc = np.arange(GRID_W)[None, :]
    win_start = np.clip(qc - WIN_C // 2, 0, GRID_W - WIN_C)
    col_ok = (kc >= win_start) & (kc < win_start + WIN_C)
    dc_idx = np.clip(kc - qc + (WIN_C - 1), 0, 2 * WIN_C - 2)
    onehot = (dc_idx.reshape(-1)[None, :] == np.arange(2 * WIN_C - 1)[:, None]).astype(np.float32)
    tiles = jnp.einsum("hdc,cq->hdq", rpb.astype(F32), jnp.asarray(onehot),
                       precision=lax.Precision.HIGHEST).reshape(nh, 2 * WIN_R - 1, GRID_W, GRID_W)
    tiles = jnp.where(jnp.asarray(col_ok)[None, None], tiles, NEG_INF)
    masked_tile = jnp.full((nh, 1, GRID_W, GRID_W), NEG_INF, F32)
    tiles = jnp.concatenate([tiles, masked_tile], axis=1)
    dr = []
    for v in range(3):
        row_ok = (j >= rel_start[v]) & (j < rel_start[v] + WIN_R)
        dr.append(np.where(row_ok, j - q_rel[v] + (WIN_R - 1), 2 * WIN_R - 1))
    dr = np.stack(dr)
    assert dr.min() >= 0 and dr.max() <= 2 * WIN_R - 1
    t = tiles[:, dr]
    return t.transpose(1, 0, 2, 4, 3, 5).reshape(3, nh, r8 * GRID_W, 2 * r8 * GRID_W)


def _na_kernel(q_ref, k_ref, v_ref, bias_ref, o_ref, *, rows, scale):
    g = pl.program_id(2)
    n_groups = rows // NA_ROWS_PER_STEP
    kb = jnp.clip(NA_ROWS_PER_STEP * g - 4, 0, rows - 2 * NA_ROWS_PER_STEP) * GRID_W
    kb = pl.multiple_of(kb, GRID_W)
    variant = jnp.where(g == 0, 0, jnp.where(g == n_groups - 1, 2, 1))
    nk = 2 * NA_ROWS_PER_STEP * GRID_W
    q = q_ref[0]
    k = k_ref[0, pl.ds(kb, nk), :]
    v = v_ref[0, pl.ds(kb, nk), :]
    s = lax.dot_general(q, k, (((1,), (1,)), ((), ())), preferred_element_type=F32)
    s = s * scale + bias_ref[variant]
    m = jnp.max(s, -1, keepdims=True)
    p = jnp.exp(s - m)
    l = jnp.sum(p, -1, keepdims=True)
    o = jnp.dot(p.astype(BF16), v, preferred_element_type=F32) / l
    o_ref[0] = o.astype(o_ref.dtype)


def _neighbourhood_attention(proj3, b0, bsz, seq_len, bias, nh, hd):
    rows = seq_len // GRID_W
    assert rows % NA_ROWS_PER_STEP == 0 and rows >= 2 * NA_ROWS_PER_STEP
    nq = NA_ROWS_PER_STEP * GRID_W
    return pl.pallas_call(
        functools.partial(_na_kernel, rows=rows, scale=hd ** -0.5),
        grid=(nh, bsz, rows // NA_ROWS_PER_STEP),
        in_specs=[pl.BlockSpec((1, nq, hd), lambda h, b, g: (b0 + b, g, h)),
                  pl.BlockSpec((1, seq_len, hd), lambda h, b, g: (b0 + b, 0, nh + h)),
                  pl.BlockSpec((1, seq_len, hd), lambda h, b, g: (b0 + b, 0, 2 * nh + h)),
                  pl.BlockSpec((3, None, nq, 2 * nq), lambda h, b, g: (0, h, 0, 0))],
        out_specs=pl.BlockSpec((1, nq, hd), lambda h, b, g: (b, g, h)),
        out_shape=jax.ShapeDtypeStruct((bsz, seq_len, nh * hd), BF16),
        compiler_params=_cparams(("parallel", "parallel", "arbitrary")),
        name="neighbourhood_attention",
    )(proj3, proj3, proj3, bias)


def _conv3_kernel(x_ref, w_ref, b_ref, o_ref, *, seq_len, rb):
    w = w_ref[...]
    bias = b_ref[...]
    n_chunks = seq_len // rb

    def body(c, carry):
        r0 = pl.multiple_of(c * rb, rb)
        x = x_ref[0, pl.ds(r0, rb), :].astype(F32)
        row = lax.broadcasted_iota(jnp.int32, x.shape, 0)
        p0 = pl.multiple_of(jnp.maximum(r0 - 16, 0), 16)
        n0 = pl.multiple_of(jnp.minimum(r0 + rb, seq_len - 16), 16)
        prev_row = x_ref[0, pl.ds(p0, 16), :].astype(F32)[15:16]
        next_row = x_ref[0, pl.ds(n0, 16), :].astype(F32)[0:1]
        prev_row = jnp.where(c == 0, 0.0, prev_row)
        next_row = jnp.where(c == n_chunks - 1, 0.0, next_row)
        up = jnp.where(row == 0, prev_row, pltpu.roll(x, 1, 0))
        down = jnp.where(row == rb - 1, next_row, pltpu.roll(x, rb - 1, 0))
        y = up * w[0:1] + x * w[1:2] + down * w[2:3] + bias
        o_ref[0, pl.ds(r0, rb), :] = y.astype(o_ref.dtype)
        return carry

    lax.fori_loop(0, n_chunks, body, 0)


def _short_conv3(proj3, b0, bsz, seq_len, col0, w, b, cb=256, rb=512):
    width = w.shape[1]
    rb = min(rb, seq_len)
    c0 = col0 // cb
    return pl.pallas_call(
        functools.partial(_conv3_kernel, seq_len=seq_len, rb=rb),
        grid=(bsz, width // cb),
        in_specs=[pl.BlockSpec((1, seq_len, cb), lambda bi, c: (b0 + bi, 0, c0 + c)),
                  pl.BlockSpec((3, cb), lambda bi, c: (0, c)),
                  pl.BlockSpec((1, cb), lambda bi, c: (0, c))],
        out_specs=pl.BlockSpec((1, seq_len, cb), lambda bi, c: (bi, 0, c)),
        out_shape=jax.ShapeDtypeStruct((bsz, seq_len, width), BF16),
        compiler_params=_cparams(("parallel", "parallel")),
        name="hyena_short_conv",
    )(proj3, w, b.reshape(1, width))


def _filter_kernel(z_ref, w1_ref, b1_ref, fr_ref, w2_ref, b2_ref, w3_ref, b3_ref, dl_ref, o_ref, *,
                   seq_len, rb):
    i = pl.program_id(0)
    hi = lax.Precision.HIGHEST
    z = z_ref[...]
    h = jnp.sin(fr_ref[0:1] * (jnp.dot(z, w1_ref[...], precision=hi, preferred_element_type=F32)
                               + b1_ref[...]))
    h = jnp.sin(fr_ref[1:2] * (jnp.dot(h, w2_ref[...], precision=hi, preferred_element_type=F32)
                               + b2_ref[...]))
    h = jnp.dot(h, w3_ref[...], precision=hi, preferred_element_type=F32) + b3_ref[...]
    decay = jnp.exp(-z[:, 0:1] * dl_ref[...])
    n = i * rb + lax.broadcasted_iota(jnp.int32, h.shape, 0)
    sign = jnp.where(n < seq_len, 1.0, jnp.where(n == seq_len, 0.0, -1.0))
    o_ref[...] = (h * decay * sign).astype(o_ref.dtype)


def _hyena_filter_signal(seq_len, w1, b1, freq, w2, b2, w3, b3, order, ch, rb=512):
    n = 2 * seq_len
    emb, hid = w1.shape
    pad = LANES
    t = jnp.linspace(0.0, 1.0, seq_len, dtype=F32)[:, None]
    bands = (emb - 1) // 2
    fr = jnp.linspace(1e-4, bands - 1, bands, dtype=F32)[None, :]
    wpos = 2.0 * math.pi * jnp.arange(seq_len, dtype=F32)[:, None] / seq_len
    z = jnp.concatenate([t, jnp.cos(fr * wpos), -jnp.sin(fr * wpos)], axis=-1)
    pos = np.arange(n)
    src = np.clip(np.where(pos < seq_len, pos, n - pos), 0, seq_len - 1)
    z2 = jnp.pad(z[src], ((0, 0), (0, pad - emb)))
    w1p = jnp.pad(w1.astype(F32), ((0, pad - emb), (0, pad - hid)))
    b1p = jnp.pad(b1.astype(F32), (0, pad - hid)).reshape(1, pad)
    frp = jnp.pad(freq.astype(F32), ((0, 0), (0, pad - hid)), constant_values=1.0)
    w2p = jnp.pad(w2.astype(F32), ((0, pad - hid), (0, pad - hid)))
    b2p = jnp.pad(b2.astype(F32), (0, pad - hid)).reshape(1, pad)
    w3p = jnp.pad(w3.astype(F32), ((0, pad - hid), (0, 0))).reshape(pad, order * 2, ch).transpose(1, 0, 2)
    b3p = b3.astype(F32).reshape(order * 2, 1, ch)
    deltas = jnp.abs(jnp.linspace(HY_MIN_DECAY, HY_MAX_DECAY, ch, dtype=F32)).reshape(1, ch)
    nb = n // rb
    half = nb // 2
    full = lambda i, o: (0, 0)
    return pl.pallas_call(
        functools.partial(_filter_kernel, seq_len=seq_len, rb=rb),
        grid=(nb, order),
        in_specs=[pl.BlockSpec((rb, pad), lambda i, o: (i, 0)),
                  pl.BlockSpec((pad, pad), full), pl.BlockSpec((1, pad), full),
                  pl.BlockSpec((2, pad), full),
                  pl.BlockSpec((pad, pad), full), pl.BlockSpec((1, pad), full),
                  pl.BlockSpec((None, pad, ch), lambda i, o: (2 * o + i // half, 0, 0)),
                  pl.BlockSpec((None, 1, ch), lambda i, o: (2 * o + i // half, 0, 0)),
                  pl.BlockSpec((1, ch), full)],
        out_specs=pl.BlockSpec((rb, ch), lambda i, o: (i, o)),
        out_shape=jax.ShapeDtypeStruct((n, order * ch), BF16),
        compiler_params=_cparams(("parallel", "arbitrary")),
        name="hyena_filter_mlp",
    )(z2, w1p, b1p, frp, w2p, b2p, w3p, b3p, deltas)


def _dft_tables(seq_len):
    n = 2 * seq_len
    n2 = DFT_N2
    n1 = n // n2
    hk = n2 // 2
    kg = min(n1, DFT_GROUP)
    ng = n1 // kg
    pi = math.pi
    k1 = jnp.arange(n1, dtype=jnp.int32)
    ph1 = ((2 * k1[:, None] + 1) * k1[None, :]) % (2 * n1)
    th1 = ph1.astype(F32) * (pi / n1)
    c1 = jnp.cos(th1).reshape(ng, kg, n1)
    s1 = jnp.sin(th1).reshape(ng, kg, n1)
    f1 = jnp.concatenate([c1, -s1], axis=1)
    fb = jnp.concatenate([c1, -s1], axis=1).transpose(0, 2, 1)[:, : n1 // 2] * (2.0 / n)
    k = k1[:, None, None] + n1 * jnp.arange(hk, dtype=jnp.int32)[None, :, None]
    ph2 = ((2 * k + 1) * jnp.arange(n2, dtype=jnp.int32)[None, None, :]) % (2 * n)
    th2 = ph2.astype(F32) * (pi / n)
    c2, s2 = jnp.cos(th2), jnp.sin(th2)
    g = jnp.concatenate([jnp.concatenate([c2, s2], axis=2),
                         jnp.concatenate([-s2, c2], axis=2)], axis=1)
    c2t, s2t = c2.transpose(0, 2, 1), s2.transpose(0, 2, 1)
    h = jnp.concatenate([jnp.concatenate([c2t, -s2t], axis=2),
                         jnp.concatenate([s2t, c2t], axis=2)], axis=1)
    return dict(n1=n1, kg=kg, ng=ng, f1_full=f1.astype(BF16), f1_half=f1[:, :, : n1 // 2].astype(BF16),
                fb=fb.astype(BF16), g=g.astype(BF16), h=h.astype(BF16))


def _fill_slabs(x_ref, xs, slabs, pitch):
    def body(n1, carry):
        src = pl.ds(pl.multiple_of(n1 * DFT_N2, DFT_N2), DFT_N2)
        xs[pl.ds(pl.multiple_of(n1 * pitch, SUBLANES), DFT_N2), :] = x_ref[src, :].astype(F32)
        return carry
    lax.fori_loop(0, slabs, body, 0, unroll=4)


def _slab_stage(xs, a_s, f1, slabs, p_x, p_a):
    rows = f1.shape[0]

    def body(i, carry):
        xn = xs[pl.ds(i, slabs, stride=p_x), :].astype(BF16)
        a_s[pl.ds(pl.multiple_of(i * p_a, SUBLANES), rows), :] = jnp.dot(f1, xn, preferred_element_type=F32)
        return carry
    lax.fori_loop(0, DFT_N2, body, 0, unroll=8)


def _load_slab_freq(a_s, j, kg, p_a):
    are = a_s[pl.ds(j, DFT_N2, stride=p_a), :]
    aim = a_s[pl.ds(kg + j, DFT_N2, stride=p_a), :]
    return jnp.concatenate([are, aim], axis=0).astype(BF16)


def _filter_spec_kernel(x_ref, f1_ref, g_ref, o_ref, xs, a_s, *, slabs, kg):
    grp = pl.program_id(1)
    p_x, p_a = _pitch(DFT_N2), _pitch(2 * kg)

    @pl.when(grp == 0)
    def _():
        _fill_slabs(x_ref, xs, slabs, p_x)

    _slab_stage(xs, a_s, f1_ref[0], slabs, p_x, p_a)

    def body(j, carry):
        o_ref[j] = jnp.dot(g_ref[j], _load_slab_freq(a_s, j, kg, p_a), preferred_element_type=F32)
        return carry
    lax.fori_loop(0, kg, body, 0, unroll=2)


def _filter_spectrum(f, tables):
    n, cols = f.shape
    n1, kg, ng = tables["n1"], tables["kg"], tables["ng"]
    p_x, p_a = _pitch(DFT_N2), _pitch(2 * kg)
    return pl.pallas_call(
        functools.partial(_filter_spec_kernel, slabs=n1, kg=kg),
        grid=(cols // LANES, ng),
        in_specs=[pl.BlockSpec((n, LANES), lambda c, g: (0, c)),
                  pl.BlockSpec((1, 2 * kg, n1), lambda c, g: (g, 0, 0)),
                  pl.BlockSpec((kg, DFT_N2, 2 * DFT_N2), lambda c, g: (g, 0, 0))],
        out_specs=pl.BlockSpec((kg, DFT_N2, LANES), lambda c, g: (g, 0, c)),
        out_shape=jax.ShapeDtypeStruct((n1, DFT_N2, cols), F32),
        scratch_shapes=[pltpu.VMEM((n1 * p_x, LANES), F32),
                        pltpu.VMEM((DFT_N2 * p_a, LANES), F32)],
        compiler_params=_cparams(("parallel", "arbitrary")),
        name="hyena_filter_spectrum",
    )(f, tables["f1_full"], tables["g"])


def _long_conv_kernel(x_ref, gate_ref, f1_ref, g_ref, h_ref, kf_ref, fb_ref, skip_ref, o_ref,
                      xs, a_s, z_s, y_s, *, slabs, kg, ng):
    grp = pl.program_id(2)
    n2 = DFT_N2
    hk = n2 // 2
    p_x, p_a, p_z, p_y = _pitch(n2), _pitch(2 * kg), _pitch(2 * n2), _pitch(slabs)

    @pl.when(grp == 0)
    def _():
        _fill_slabs(x_ref.at[0], xs, slabs, p_x)
        y_s[...] = jnp.zeros_like(y_s)

    _slab_stage(xs, a_s, f1_ref[0], slabs, p_x, p_a)

    def freq_body(j, carry):
        x = jnp.dot(g_ref[j], _load_slab_freq(a_s, j, kg, p_a), preferred_element_type=F32)
        kf = kf_ref[j]
        xr, xi = x[:hk], x[hk:]
        kr, ki = kf[:hk], kf[hk:]
        y = jnp.concatenate([xr * kr - xi * ki, xr * ki + xi * kr], axis=0).astype(BF16)
        z_s[pl.ds(pl.multiple_of(j * p_z, SUBLANES), 2 * n2), :] = jnp.dot(
            h_ref[j], y, preferred_element_type=F32)
        return carry
    lax.fori_loop(0, kg, freq_body, 0, unroll=4)

    fb = fb_ref[0]

    def inv_body(t, carry):
        zre = z_s[pl.ds(t, kg, stride=p_z), :]
        zim = z_s[pl.ds(n2 + t, kg, stride=p_z), :]
        zz = jnp.concatenate([zre, zim], axis=0).astype(BF16)
        dst = pl.ds(pl.multiple_of(t * p_y, SUBLANES), slabs)
        y_s[dst, :] = y_s[dst, :] + jnp.dot(fb, zz, preferred_element_type=F32)
        return carry
    lax.fori_loop(0, n2, inv_body, 0, unroll=8)

    @pl.when(grp == ng - 1)
    def _():
        skip = skip_ref[...]

        def out_body(t1, carry):
            y = y_s[pl.ds(t1, n2, stride=p_y), :]
            u = xs[pl.ds(pl.multiple_of(t1 * p_x, SUBLANES), n2), :]
            rows = pl.ds(pl.multiple_of(t1 * n2, n2), n2)
            gate = gate_ref[0, rows, :].astype(F32)
            o_ref[0, rows, :] = ((y + skip * u) * gate).astype(o_ref.dtype)
            return carry
        lax.fori_loop(0, slabs, out_body, 0, unroll=2)


def _long_conv_gate(u, u_blk0, gate, gate_blk0, tables, kf, kf_blk0, skip, ch):
    bsz, seq_len, _ = u.shape
    n1, kg, ng = tables["n1"], tables["kg"], tables["ng"]
    slabs = n1 // 2
    p_x, p_a, p_z, p_y = _pitch(DFT_N2), _pitch(2 * kg), _pitch(2 * DFT_N2), _pitch(slabs)
    return pl.pallas_call(
        functools.partial(_long_conv_kernel, slabs=slabs, kg=kg, ng=ng),
        grid=(bsz, ch // LANES, ng),
        in_specs=[pl.BlockSpec((1, seq_len, LANES), lambda b, c, g: (b, 0, u_blk0 + c)),
                  pl.BlockSpec((1, seq_len, LANES), lambda b, c, g: (b, 0, gate_blk0 + c)),
                  pl.BlockSpec((1, 2 * kg, slabs), lambda b, c, g: (g, 0, 0)),
                  pl.BlockSpec((kg, DFT_N2, 2 * DFT_N2), lambda b, c, g: (g, 0, 0)),
                  pl.BlockSpec((kg, 2 * DFT_N2, DFT_N2), lambda b, c, g: (g, 0, 0)),
                  pl.BlockSpec((kg, DFT_N2, LANES), lambda b, c, g: (g, 0, kf_blk0 + c)),
                  pl.BlockSpec((1, slabs, 2 * kg), lambda b, c, g: (g, 0, 0)),
                  pl.BlockSpec((1, LANES), lambda b, c, g: (0, c))],
        out_specs=pl.BlockSpec((1, seq_len, LANES), lambda b, c, g: (b, 0, c)),
        out_shape=jax.ShapeDtypeStruct((bsz, seq_len, ch), BF16),
        scratch_shapes=[pltpu.VMEM((slabs * p_x, LANES), F32),
                        pltpu.VMEM((DFT_N2 * p_a, LANES), F32),
                        pltpu.VMEM((kg * p_z, LANES), F32),
                        pltpu.VMEM((DFT_N2 * p_y, LANES), F32)],
        compiler_params=_cparams(("parallel", "parallel", "arbitrary")),
        name="hyena_long_conv",
    )(u, gate, tables["f1_half"], tables["g"], tables["h"], kf, tables["fb"],
      skip.astype(F32).reshape(1, ch))


def _hyena(hy, tables, kf, skip, order, ch):
    nblk = ch // LANES
    z, z_blk0 = hy, 0
    for o in range(order):
        z = _long_conv_gate(z, z_blk0, hy, (o + 1) * nblk, tables, kf, o * nblk, skip[o], ch)
    return z


def _rope(x, cos, sin):
    half = x.shape[-1] // 2
    x1, x2 = x[:, :half], x[:, half:]
    return jnp.concatenate([x1 * cos - x2 * sin, x1 * sin + x2 * cos], axis=-1)


def _ret_kernel(lg_ref, q_ref, k_ref, v_ref, g_ref, cos_ref, sin_ref, o_ref, sb_all, sf_ref, sb_ref, *,
                n_super, chunks_per_super, k_scale):
    h = pl.program_id(1)
    s = pl.program_id(2)
    cc = RET_CHUNK
    dh = q_ref.shape[-1]
    lgf = lg_ref[0, h]
    lgb = lg_ref[1, h]
    row = lax.broadcasted_iota(jnp.int32, (cc, dh), 0).astype(F32)

    def load_k(c0):
        cos = cos_ref[pl.ds(c0, cc), :]
        sin = sin_ref[pl.ds(c0, cc), :]
        return _rope(k_ref[0, pl.ds(c0, cc), :].astype(F32), cos, sin) * k_scale

    @pl.when(s < n_super)
    def _backward_sweep():
        @pl.when(s == 0)
        def _():
            sb_ref[...] = jnp.zeros_like(sb_ref)

        sup = n_super - 1 - s
        zeta_b = jnp.exp(lgb * row)
        chunk_decay = jnp.exp(lgb * jnp.full((1, dh), float(cc), F32))

        def body(t, carry):
            c = chunks_per_super - 1 - t
            c0 = pl.multiple_of(c * cc, cc)
            sb_all[sup * chunks_per_super + c] = sb_ref[...].astype(BF16)
            kz = (load_k(c0) * zeta_b).astype(BF16)
            v = v_ref[0, pl.ds(c0, cc), :]
            upd = lax.dot_general(kz, v, (((0,), (0,)), ((), ())), preferred_element_type=F32)
            sb_ref[...] = sb_ref[...] * chunk_decay + upd
            return carry

        lax.fori_loop(0, chunks_per_super, body, 0)

    @pl.when(s >= n_super)
    def _forward_sweep():
        @pl.when(s == n_super)
        def _():
            sf_ref[...] = jnp.zeros_like(sf_ref)

        sup = s - n_super
        col = lax.broadcasted_iota(jnp.int32, (cc, cc), 1).astype(F32)
        rowc = lax.broadcasted_iota(jnp.int32, (cc, cc), 0).astype(F32)
        diff = rowc - col
        inner_decay = jnp.where(diff >= 0, jnp.exp(lgf * jnp.maximum(diff, 0.0)),
                                jnp.exp(lgb * jnp.maximum(-diff, 0.0)))
        xi_f = jnp.exp(lgf * (row + 1.0))
        xi_b = jnp.exp(lgb * (cc - row))
        zeta_f = jnp.exp(lgf * (cc - 1.0 - row))
        chunk_decay = jnp.exp(lgf * jnp.full((1, dh), float(cc), F32))

        def body(c, carry):
            c0 = pl.multiple_of(c * cc, cc)
            cos = cos_ref[pl.ds(c0, cc), :]
            sin = sin_ref[pl.ds(c0, cc), :]
            q = _rope(q_ref[0, pl.ds(c0, cc), :].astype(F32), cos, sin)
            k = load_k(c0)
            v = v_ref[0, pl.ds(c0, cc), :]
            sc = lax.dot_general(q.astype(BF16), k.astype(BF16), (((1,), (1,)), ((), ())),
                                 preferred_element_type=F32) * inner_decay
            ret = jnp.dot(sc.astype(BF16), v, preferred_element_type=F32)
            ret = ret + jnp.dot((q * xi_f).astype(BF16), sf_ref[...].astype(BF16),
                                preferred_element_type=F32)
            ret = ret + jnp.dot((q * xi_b).astype(BF16), sb_all[sup * chunks_per_super + c],
                                preferred_element_type=F32)
            upd = lax.dot_general((k * zeta_f).astype(BF16), v, (((0,), (0,)), ((), ())),
                                  preferred_element_type=F32)
            sf_ref[...] = sf_ref[...] * chunk_decay + upd
            ret = ret * lax.rsqrt(jnp.mean(ret * ret, -1, keepdims=True) + RMS_EPS)
            gate = g_ref[0, pl.ds(c0, cc), :].astype(F32)
            o_ref[0, pl.ds(c0, cc), :] = (ret * (gate / (1.0 + jnp.exp(-gate)))).astype(o_ref.dtype)
            return carry

        lax.fori_loop(0, chunks_per_super, body, 0)


def _retention(proj3, b0, bsz, seq_len, col0, nh, dh, log_gamma, cos, sin):
    sup_len = min(RET_SUPER, seq_len)
    n_super = seq_len // sup_len
    cps = sup_len // RET_CHUNK
    cb0 = col0 // dh

    def fwd_idx(s):
        return jnp.maximum(s - n_super, 0)

    def kv_idx(s):
        return jnp.where(s < n_super, n_super - 1 - s, s - n_super)

    return pl.pallas_call(
        functools.partial(_ret_kernel, n_super=n_super, chunks_per_super=cps, k_scale=dh ** -0.5),
        grid=(bsz, nh, 2 * n_super),
        in_specs=[pl.BlockSpec(memory_space=pltpu.SMEM),
                  pl.BlockSpec((1, sup_len, dh), lambda b, h, s: (b0 + b, fwd_idx(s), cb0 + h)),
                  pl.BlockSpec((1, sup_len, dh), lambda b, h, s: (b0 + b, kv_idx(s), cb0 + nh + h)),
                  pl.BlockSpec((1, sup_len, dh), lambda b, h, s: (b0 + b, kv_idx(s), cb0 + 2 * nh + h)),
                  pl.BlockSpec((1, sup_len, dh), lambda b, h, s: (b0 + b, fwd_idx(s), cb0 + 3 * nh + h)),
                  pl.BlockSpec((sup_len, dh // 2), lambda b, h, s: (kv_idx(s), 0)),
                  pl.BlockSpec((sup_len, dh // 2), lambda b, h, s: (kv_idx(s), 0))],
        out_specs=pl.BlockSpec((1, sup_len, dh), lambda b, h, s: (b, fwd_idx(s), h)),
        out_shape=jax.ShapeDtypeStruct((bsz, seq_len, nh * dh), BF16),
        scratch_shapes=[pltpu.VMEM((seq_len // RET_CHUNK, dh, dh), BF16),
                        pltpu.VMEM((dh, dh), F32),
                        pltpu.VMEM((dh, dh), F32)],
        compiler_params=_cparams(("parallel", "parallel", "arbitrary")),
        name="retention",
    )(log_gamma, proj3, proj3, proj3, proj3, cos, sin)


def _assemble_kernel(a1_ref, a2_ref, z1_ref, z2_ref, c1_ref, c2_ref, ga_ref, gb_ref, o_ref, *, nb1):
    first = pl.program_id(0) < nb1
    wa = a1_ref.shape[-1]
    wb = z1_ref.shape[-1]
    a = jnp.where(first, a1_ref[...], a2_ref[...]).astype(F32)
    a = a * lax.rsqrt(jnp.mean(a * a, -1, keepdims=True) + RMS_EPS) * ga_ref[...]
    z = jnp.where(first, z1_ref[...], z2_ref[...]).astype(F32)
    z = z * lax.rsqrt(jnp.mean(z * z, -1, keepdims=True) + RMS_EPS) * gb_ref[...]
    o_ref[:, :wa] = a.astype(o_ref.dtype)
    o_ref[:, wa:wa + wb] = z.astype(o_ref.dtype)
    o_ref[:, wa + wb:] = jnp.where(first, c1_ref[...], c2_ref[...])


def _assemble(a_parts, z_parts, c_parts, gain_a, gain_b, bm=512):
    m1, wa = a_parts[0].shape
    m = m1 + a_parts[1].shape[0]
    wb, wc = z_parts[0].shape[1], c_parts[0].shape[1]
    nb1 = m1 // bm
    part_specs = []
    for w in (wa, wb, wc):
        part_specs += [pl.BlockSpec((bm, w), _first_rows(nb1)), pl.BlockSpec((bm, w), _second_rows(nb1))]
    return pl.pallas_call(
        functools.partial(_assemble_kernel, nb1=nb1),
        grid=(m // bm,),
        in_specs=part_specs + [pl.BlockSpec((1, wa), lambda i: (0, 0)),
                               pl.BlockSpec((1, wb), lambda i: (0, 0))],
        out_specs=pl.BlockSpec((bm, wa + wb + wc), lambda i: (i, 0)),
        out_shape=jax.ShapeDtypeStruct((m, wa + wb + wc), BF16),
        compiler_params=_cparams(("arbitrary",)),
        name="assemble_mixers",
    )(*a_parts, *z_parts, *c_parts, gain_a.astype(F32).reshape(1, wa), gain_b.astype(F32).reshape(1, wb))


def _token_mixers(proj, requests, p, l, dims):
    nh_a, hd_a, ch, order, nh_r, dh_r = dims
    in_cols = proj.shape[1]
    total = proj.shape[0]
    wa = nh_a * hd_a
    hy0 = 3 * wa
    ret0 = hy0 + (order + 1) * ch
    log_gamma = jnp.log1p(-jnp.exp2(-p["ret_decay_exp"][l].astype(F32)))
    a_parts, z_parts, c_parts = [], [], []
    bias = _na_bias_tables(p["na_rpb"][l])
    for (bsz, seq_len, tok0) in requests:
        assert tok0 % seq_len == 0 and total % seq_len == 0
        proj3 = proj.reshape(total // seq_len, seq_len, in_cols)
        b0 = tok0 // seq_len
        a = _neighbourhood_attention(proj3, b0, bsz, seq_len, bias, nh_a, hd_a)
        a_parts.append(a.reshape(bsz * seq_len, wa))
        tables = _dft_tables(seq_len)
        f = _hyena_filter_signal(seq_len, p["hy_f_w1"][l], p["hy_f_b1"][l], p["hy_f_freq"][l],
                                 p["hy_f_w2"][l], p["hy_f_b2"][l], p["hy_f_w3"][l], p["hy_f_b3"][l],
                                 order, ch)
        kf = _filter_spectrum(f, tables)
        hy = _short_conv3(proj3, b0, bsz, seq_len, hy0, p["hy_conv_w"][l].astype(F32),
                          p["hy_conv_b"][l].astype(F32))
        z = _hyena(hy, tables, kf, p["hy_skip"][l], order, ch)
        z_parts.append(z.reshape(bsz * seq_len, ch))
        half = dh_r // 2
        inv_freq = 1.0 / (ROPE_BASE ** jnp.linspace(0.0, 1.0, half, dtype=F32))
        ang = jnp.arange(seq_len, dtype=F32)[:, None] * inv_freq[None, :]
        c = _retention(proj3, b0, bsz, seq_len, ret0, nh_r, dh_r, log_gamma, jnp.cos(ang), jnp.sin(ang))
        c_parts.append(c.reshape(bsz * seq_len, nh_r * dh_r))
    return _assemble(a_parts, z_parts, c_parts, p["grp_gain_a"][l], p["grp_gain_b"][l])


def _pick(n, candidates):
    for c in candidates:
        if n % c == 0:
            return c
    raise ValueError(f"no block size for {n}")


def kernel(x_prompt, x_sample, ln_in_g, ln_in_b, w_in, na_rpb, hy_conv_w, hy_conv_b, hy_f_w1, hy_f_b1,
           hy_f_freq, hy_f_w2, hy_f_b2, hy_f_w3, hy_f_b3, hy_skip, ret_decay_exp, grp_gain_a, grp_gain_b,
           w_out, ln1_g, ln1_b, w_ffn_in, w_ffn_out, ln2_g, ln2_b):
    p = dict(na_rpb=na_rpb, hy_conv_w=hy_conv_w, hy_conv_b=hy_conv_b, hy_f_w1=hy_f_w1, hy_f_b1=hy_f_b1,
             hy_f_freq=hy_f_freq, hy_f_w2=hy_f_w2, hy_f_b2=hy_f_b2, hy_f_w3=hy_f_w3, hy_f_b3=hy_f_b3,
             hy_skip=hy_skip, ret_decay_exp=ret_decay_exp, grp_gain_a=grp_gain_a, grp_gain_b=grp_gain_b)
    depth, d_model, in_cols = w_in.shape
    nh_a = na_rpb.shape[1]
    wa = grp_gain_a.shape[1]
    ch = grp_gain_b.shape[1]
    order = hy_skip.shape[1]
    nh_r = ret_decay_exp.shape[2]
    wr = (in_cols - 3 * wa - (order + 1) * ch) // 4
    dims = (nh_a, wa // nh_a, ch, order, nh_r, wr // nh_r)
    d_ff = w_ffn_out.shape[1]
    alpha = (2.0 * depth) ** 0.25

    bp, lp, _ = x_prompt.shape
    bs, ls, _ = x_sample.shape
    requests = [(bp, lp, 0), (bs, ls, bp * lp)]
    m1 = bp * lp
    m = m1 + bs * ls
    bm = _pick(m, (1024, 512, 256))
    bm_ln = 256
    assert m1 % bm_ln == 0 and m % bm_ln == 0 and m1 % 512 == 0

    xf, xb = _layernorm_join(x_prompt.reshape(m1, d_model), x_sample.reshape(bs * ls, d_model),
                             ln_in_g.astype(F32), ln_in_b.astype(F32), bm_ln)
    for l in range(depth):
        proj = _matmul(xb, w_in[l].astype(BF16), bm, _pick(in_cols, (1024, 512, 256)))
        mixed = _token_mixers(proj, requests, p, l, dims)
        y = _matmul_residual(mixed, w_out[l].astype(BF16), xf, alpha, bm, _pick(d_model, (1024, 512, 256)))
        xf, xb = _layernorm(y, ln1_g[l].astype(F32), ln1_b[l].astype(F32), bm_ln)
        hmid = _matmul_swiglu(xb, w_ffn_in[l].astype(BF16), bm, _pick(d_ff, (512, 256, 128)))
        y = _matmul_residual(hmid, w_ffn_out[l].astype(BF16), xf, alpha, _pick(m, (512, 256)),
                             _pick(d_model, (512, 256)))
        if l + 1 < depth:
            xf, xb = _layernorm(y, ln2_g[l].astype(F32), ln2_b[l].astype(F32), bm_ln)
    o1, o2 = _layernorm_split(y, ln2_g[depth - 1].astype(F32), ln2_b[depth - 1].astype(F32), m1, bm_ln)
    return (o1.reshape(bp, lp, d_model), o2.reshape(bs, ls, d_model))
```

```python
import functools
import math

import numpy as np
import jax
import jax.numpy as jnp
from jax import lax
from jax.experimental import pallas as pl
from jax.experimental.pallas import tpu as pltpu

F32 = jnp.float32
BF16 = jnp.bfloat16

GRID_W = 64
WIN_R = 8
WIN_C = 16
ROPE_BASE = 10000.0
LN_EPS = 1e-5
RMS_EPS = 1e-6
NEG_INF = -1e30
HY_TARGET = 1e-2
HY_FAST_DECAY = 0.3
HY_SLOW_DECAY = 1.5
HY_MIN_DECAY = math.log(HY_TARGET) / HY_SLOW_DECAY
HY_MAX_DECAY = math.log(HY_TARGET) / HY_FAST_DECAY

V7X_VMEM_LIMIT = 56 * 1024 * 1024
LANES = 128
SUBLANES = 8
DFT_N2 = 128
DFT_GROUP = 32
NA_ROWS_PER_STEP = 8
RET_CHUNK = 256
RET_SUPER = 2048


def _cparams(sem, vmem=V7X_VMEM_LIMIT):
    return pltpu.CompilerParams(dimension_semantics=sem, vmem_limit_bytes=vmem)


def _pitch(rows):
    p = -(-rows // SUBLANES)
    return SUBLANES * (p + 1 - p % 2)


def _ln(x, g, b):
    mu = jnp.mean(x, -1, keepdims=True)
    xc = x - mu
    var = jnp.mean(xc * xc, -1, keepdims=True)
    return xc * lax.rsqrt(var + LN_EPS) * g + b


def _ln_kernel(x_ref, g_ref, b_ref, of_ref, ob_ref):
    y = _ln(x_ref[...], g_ref[...], b_ref[...])
    of_ref[...] = y
    ob_ref[...] = y.astype(BF16)


def _ln_join_kernel(x1_ref, x2_ref, g_ref, b_ref, of_ref, ob_ref, *, nb1):
    x = jnp.where(pl.program_id(0) < nb1, x1_ref[...], x2_ref[...])
    y = _ln(x, g_ref[...], b_ref[...])
    of_ref[...] = y
    ob_ref[...] = y.astype(BF16)


def _ln_split_kernel(x_ref, g_ref, b_ref, o1_ref, o2_ref, *, nb1):
    y = _ln(x_ref[...], g_ref[...], b_ref[...])

    @pl.when(pl.program_id(0) < nb1)
    def _():
        o1_ref[...] = y

    @pl.when(pl.program_id(0) >= nb1)
    def _():
        o2_ref[...] = y


def _first_rows(nb1):
    return lambda i: (jnp.minimum(i, nb1 - 1), 0)


def _second_rows(nb1):
    return lambda i: (jnp.maximum(i - nb1, 0), 0)


def _layernorm_join(x1, x2, g, b, bm=256):
    m1, d = x1.shape
    m = m1 + x2.shape[0]
    nb1 = m1 // bm
    return pl.pallas_call(
        functools.partial(_ln_join_kernel, nb1=nb1),
        grid=(m // bm,),
        in_specs=[pl.BlockSpec((bm, d), _first_rows(nb1)),
                  pl.BlockSpec((bm, d), _second_rows(nb1)),
                  pl.BlockSpec((1, d), lambda i: (0, 0)),
                  pl.BlockSpec((1, d), lambda i: (0, 0))],
        out_specs=[pl.BlockSpec((bm, d), lambda i: (i, 0)),
                   pl.BlockSpec((bm, d), lambda i: (i, 0))],
        out_shape=[jax.ShapeDtypeStruct((m, d), F32), jax.ShapeDtypeStruct((m, d), BF16)],
        compiler_params=_cparams(("arbitrary",)),
        name="layernorm_join",
    )(x1, x2, g.reshape(1, d), b.reshape(1, d))


def _layernorm_split(x, g, b, m1, bm=256):
    m, d = x.shape
    nb1 = m1 // bm
    return pl.pallas_call(
        functools.partial(_ln_split_kernel, nb1=nb1),
        grid=(m // bm,),
        in_specs=[pl.BlockSpec((bm, d), lambda i: (i, 0)),
                  pl.BlockSpec((1, d), lambda i: (0, 0)),
                  pl.BlockSpec((1, d), lambda i: (0, 0))],
        out_specs=[pl.BlockSpec((bm, d), _first_rows(nb1)),
                   pl.BlockSpec((bm, d), _second_rows(nb1))],
        out_shape=[jax.ShapeDtypeStruct((m1, d), F32), jax.ShapeDtypeStruct((m - m1, d), F32)],
        compiler_params=_cparams(("arbitrary",)),
        name="layernorm_split",
    )(x, g.reshape(1, d), b.reshape(1, d))


def _layernorm(x, g, b, bm=256):
    m, d = x.shape
    return pl.pallas_call(
        _ln_kernel,
        grid=(m // bm,),
        in_specs=[pl.BlockSpec((bm, d), lambda i: (i, 0)),
                  pl.BlockSpec((1, d), lambda i: (0, 0)),
                  pl.BlockSpec((1, d), lambda i: (0, 0))],
        out_specs=[pl.BlockSpec((bm, d), lambda i: (i, 0)),
                   pl.BlockSpec((bm, d), lambda i: (i, 0))],
        out_shape=[jax.ShapeDtypeStruct((m, d), F32), jax.ShapeDtypeStruct((m, d), BF16)],
        compiler_params=_cparams(("parallel",)),
        name="layernorm",
    )(x, g.reshape(1, d), b.reshape(1, d))


def _mm_kernel(x_ref, w_ref, o_ref):
    o_ref[...] = jnp.dot(x_ref[...], w_ref[...], preferred_element_type=F32).astype(o_ref.dtype)


def _mm_res_kernel(x_ref, w_ref, r_ref, o_ref, *, alpha):
    o_ref[...] = alpha * r_ref[...] + jnp.dot(x_ref[...], w_ref[...], preferred_element_type=F32)


def _mm_swiglu_kernel(x_ref, wg_ref, wu_ref, o_ref):
    x = x_ref[...]
    g = jnp.dot(x, wg_ref[...], preferred_element_type=F32)
    u = jnp.dot(x, wu_ref[...], preferred_element_type=F32)
    o_ref[...] = (g / (1.0 + jnp.exp(-g)) * u).astype(o_ref.dtype)


def _matmul(x, w, layer, bm, bn, out_dtype=BF16):
    m, k = x.shape
    n = w.shape[2]
    return pl.pallas_call(
        _mm_kernel,
        grid=(m // bm, n // bn),
        in_specs=[pl.BlockSpec((bm, k), lambda i, j: (i, 0)),
                  pl.BlockSpec((None, k, bn), lambda i, j: (layer, 0, j))],
        out_specs=pl.BlockSpec((bm, bn), lambda i, j: (i, j)),
        out_shape=jax.ShapeDtypeStruct((m, n), out_dtype),
        compiler_params=_cparams(("parallel", "arbitrary")),
        name="matmul",
    )(x, w)


def _matmul_residual(x, w, layer, r, alpha, bm, bn):
    m, k = x.shape
    n = w.shape[2]
    return pl.pallas_call(
        functools.partial(_mm_res_kernel, alpha=alpha),
        grid=(m // bm, n // bn),
        in_specs=[pl.BlockSpec((bm, k), lambda i, j: (i, 0)),
                  pl.BlockSpec((None, k, bn), lambda i, j: (layer, 0, j)),
                  pl.BlockSpec((bm, bn), lambda i, j: (i, j))],
        out_specs=pl.BlockSpec((bm, bn), lambda i, j: (i, j)),
        out_shape=jax.ShapeDtypeStruct((m, n), F32),
        compiler_params=_cparams(("parallel", "arbitrary")),
        name="matmul_residual",
    )(x, w, r)


def _matmul_swiglu(x, w, layer, bm, bn):
    m, k = x.shape
    f = w.shape[2] // 2
    nb = f // bn
    return pl.pallas_call(
        _mm_swiglu_kernel,
        grid=(m // bm, nb),
        in_specs=[pl.BlockSpec((bm, k), lambda i, j: (i, 0)),
                  pl.BlockSpec((None, k, bn), lambda i, j: (layer, 0, j)),
                  pl.BlockSpec((None, k, bn), lambda i, j: (layer, 0, j + nb))],
        out_specs=pl.BlockSpec((bm, bn), lambda i, j: (i, j)),
        out_shape=jax.ShapeDtypeStruct((m, f), BF16),
        compiler_params=_cparams(("parallel", "arbitrary")),
        name="matmul_swiglu",
    )(x, w, w)


def _na_bias_tables(rpb):
    nh = rpb.shape[0]
    qc = np.arange(GRID_W)[:, None]
    kc = np.arange(GRID_W)[None, :]
    win_start = np.clip(qc - WIN_C // 2, 0, GRID_W - WIN_C)
    col_ok = (kc >= win_start) & (kc < win_start + WIN_C)
    dc_idx = np.clip(kc - qc + (WIN_C - 1), 0, 2 * WIN_C - 2)
    onehot = (dc_idx.reshape(-1)[None, :] == np.arange(2 * WIN_C - 1)[:, None]).astype(np.float32)
    tiles = jnp.einsum("hdc,cq->hdq", rpb.astype(F32), jnp.asarray(onehot),
                       precision=lax.Precision.HIGHEST).reshape(nh, 2 * WIN_R - 1, GRID_W, GRID_W)
    tiles = jnp.where(jnp.asarray(col_ok)[None, None], tiles, NEG_INF)
    dr = np.arange(WIN_R)[:, None] + np.arange(WIN_R)[None, :]
    t = tiles[:, dr]
    return t.transpose(0, 1, 3, 2, 4).reshape(nh, WIN_R, GRID_W, WIN_R * GRID_W)


def _na_kernel(q_ref, k_ref, v_ref, bias_ref, o_ref, *, rows, scale):
    g = pl.program_id(2)
    nk = WIN_R * GRID_W
    ks, vs, bias = [], [], []
    for i in range(NA_ROWS_PER_STEP):
        r = g * NA_ROWS_PER_STEP + i
        row_start = jnp.clip(r - WIN_R // 2, 0, rows - WIN_R)
        k0 = pl.multiple_of(row_start * GRID_W, GRID_W)
        ks.append(k_ref[0, pl.ds(k0, nk), :])
        vs.append(v_ref[0, pl.ds(k0, nk), :])
        bias.append(bias_ref[row_start - r + (WIN_R - 1)])
    q = q_ref[0].reshape(NA_ROWS_PER_STEP, GRID_W, q_ref.shape[-1])
    s = jnp.einsum("rqd,rkd->rqk", q, jnp.stack(ks), preferred_element_type=F32)
    s = s * scale + jnp.stack(bias)
    m = jnp.max(s, -1, keepdims=True)
    p = jnp.exp(s - m)
    l = jnp.sum(p, -1, keepdims=True)
    o = jnp.einsum("rqk,rkd->rqd", p.astype(BF16), jnp.stack(vs), preferred_element_type=F32) / l
    o_ref[0] = o.reshape(o_ref.shape[1:]).astype(o_ref.dtype)


def _neighbourhood_attention(proj3, b0, bsz, seq_len, bias, nh, hd):
    rows = seq_len // GRID_W
    assert rows % NA_ROWS_PER_STEP == 0 and rows >= WIN_R
    nq = NA_ROWS_PER_STEP * GRID_W
    return pl.pallas_call(
        functools.partial(_na_kernel, rows=rows, scale=hd ** -0.5),
        grid=(nh, bsz, rows // NA_ROWS_PER_STEP),
        in_specs=[pl.BlockSpec((1, nq, hd), lambda h, b, g: (b0 + b, g, h)),
                  pl.BlockSpec((1, seq_len, hd), lambda h, b, g: (b0 + b, 0, nh + h)),
                  pl.BlockSpec((1, seq_len, hd), lambda h, b, g: (b0 + b, 0, 2 * nh + h)),
                  pl.BlockSpec((None, WIN_R, GRID_W, WIN_R * GRID_W), lambda h, b, g: (h, 0, 0, 0))],
        out_specs=pl.BlockSpec((1, nq, hd), lambda h, b, g: (b, g, h)),
        out_shape=jax.ShapeDtypeStruct((bsz, seq_len, nh * hd), BF16),
        compiler_params=_cparams(("parallel", "parallel", "arbitrary")),
        name="neighbourhood_attention",
    )(proj3, proj3, proj3, bias)


def _conv3_kernel(x_ref, w_ref, b_ref, o_ref, *, seq_len, rb):
    w = w_ref[...]
    bias = b_ref[...]
    n_chunks = seq_len // rb

    def body(c, carry):
        r0 = pl.multiple_of(c * rb, rb)
        x = x_ref[0, pl.ds(r0, rb), :].astype(F32)
        row = lax.broadcasted_iota(jnp.int32, x.shape, 0)
        p0 = pl.multiple_of(jnp.maximum(r0 - 16, 0), 16)
        n0 = pl.multiple_of(jnp.minimum(r0 + rb, seq_len - 16), 16)
        prev_row = x_ref[0, pl.ds(p0, 16), :].astype(F32)[15:16]
        next_row = x_ref[0, pl.ds(n0, 16), :].astype(F32)[0:1]
        prev_row = jnp.where(c == 0, 0.0, prev_row)
        next_row = jnp.where(c == n_chunks - 1, 0.0, next_row)
        up = jnp.where(row == 0, prev_row, pltpu.roll(x, 1, 0))
        down = jnp.where(row == rb - 1, next_row, pltpu.roll(x, rb - 1, 0))
        y = up * w[0:1] + x * w[1:2] + down * w[2:3] + bias
        o_ref[0, pl.ds(r0, rb), :] = y.astype(o_ref.dtype)
        return carry

    lax.fori_loop(0, n_chunks, body, 0)


def _short_conv3(proj3, b0, bsz, seq_len, col0, w, b, cb=256, rb=512):
    width = w.shape[1]
    rb = min(rb, seq_len)
    c0 = col0 // cb
    return pl.pallas_call(
        functools.partial(_conv3_kernel, seq_len=seq_len, rb=rb),
        grid=(bsz, width // cb),
        in_specs=[pl.BlockSpec((1, seq_len, cb), lambda bi, c: (b0 + bi, 0, c0 + c)),
                  pl.BlockSpec((3, cb), lambda bi, c: (0, c)),
                  pl.BlockSpec((1, cb), lambda bi, c: (0, c))],
        out_specs=pl.BlockSpec((1, seq_len, cb), lambda bi, c: (bi, 0, c)),
        out_shape=jax.ShapeDtypeStruct((bsz, seq_len, width), BF16),
        compiler_params=_cparams(("parallel", "parallel")),
        name="hyena_short_conv",
    )(proj3, w, b.reshape(1, width))


def _filter_kernel(z_ref, w1_ref, b1_ref, fr_ref, w2_ref, b2_ref, w3_ref, b3_ref, dl_ref, o_ref, *,
                   seq_len, rb):
    i = pl.program_id(0)
    hi = lax.Precision.HIGHEST
    z = z_ref[...]
    h = jnp.sin(fr_ref[0:1] * (jnp.dot(z, w1_ref[...], precision=hi, preferred_element_type=F32)
                               + b1_ref[...]))
    h = jnp.sin(fr_ref[1:2] * (jnp.dot(h, w2_ref[...], precision=hi, preferred_element_type=F32)
                               + b2_ref[...]))
    ch = dl_ref.shape[-1]
    n = i * rb + lax.broadcasted_iota(jnp.int32, (rb, ch), 0)
    sign = jnp.where(n < seq_len, 1.0, jnp.where(n == seq_len, 0.0, -1.0))
    window = jnp.exp(-z[:, 0:1] * dl_ref[...]) * sign
    for o in range(w3_ref.shape[0]):
        f = jnp.dot(h, w3_ref[o], precision=hi, preferred_element_type=F32) + b3_ref[o]
        o_ref[:, o * ch:(o + 1) * ch] = (f * window).astype(o_ref.dtype)


def _hyena_filter_signal(seq_len, w1, b1, freq, w2, b2, w3, b3, order, ch, rb=512):
    n = 2 * seq_len
    emb, hid = w1.shape
    pad = LANES
    t = jnp.linspace(0.0, 1.0, seq_len, dtype=F32)[:, None]
    bands = (emb - 1) // 2
    fr = jnp.linspace(1e-4, bands - 1, bands, dtype=F32)[None, :]
    wpos = 2.0 * math.pi * jnp.arange(seq_len, dtype=F32)[:, None] / seq_len
    z = jnp.concatenate([t, jnp.cos(fr * wpos), -jnp.sin(fr * wpos)], axis=-1)
    pos = np.arange(n)
    src = np.clip(np.where(pos < seq_len, pos, n - pos), 0, seq_len - 1)
    z2 = jnp.pad(z[src], ((0, 0), (0, pad - emb)))
    w1p = jnp.pad(w1.astype(F32), ((0, pad - emb), (0, pad - hid)))
    b1p = jnp.pad(b1.astype(F32), (0, pad - hid)).reshape(1, pad)
    frp = jnp.pad(freq.astype(F32), ((0, 0), (0, pad - hid)), constant_values=1.0)
    w2p = jnp.pad(w2.astype(F32), ((0, pad - hid), (0, pad - hid)))
    b2p = jnp.pad(b2.astype(F32), (0, pad - hid)).reshape(1, pad)
    w3p = jnp.pad(w3.astype(F32), ((0, pad - hid), (0, 0))).reshape(pad, order, 2, ch).transpose(2, 1, 0, 3)
    b3p = b3.astype(F32).reshape(order, 2, 1, ch).transpose(1, 0, 2, 3)
    deltas = jnp.abs(jnp.linspace(HY_MIN_DECAY, HY_MAX_DECAY, ch, dtype=F32)).reshape(1, ch)
    nb = n // rb
    half = nb // 2
    full = lambda i: (0, 0)
    return pl.pallas_call(
        functools.partial(_filter_kernel, seq_len=seq_len, rb=rb),
        grid=(nb,),
        in_specs=[pl.BlockSpec((rb, pad), lambda i: (i, 0)),
                  pl.BlockSpec((pad, pad), full), pl.BlockSpec((1, pad), full),
                  pl.BlockSpec((2, pad), full),
                  pl.BlockSpec((pad, pad), full), pl.BlockSpec((1, pad), full),
                  pl.BlockSpec((None, order, pad, ch), lambda i: (i // half, 0, 0, 0)),
                  pl.BlockSpec((None, order, 1, ch), lambda i: (i // half, 0, 0, 0)),
                  pl.BlockSpec((1, ch), full)],
        out_specs=pl.BlockSpec((rb, order * ch), lambda i: (i, 0)),
        out_shape=jax.ShapeDtypeStruct((n, order * ch), BF16),
        compiler_params=_cparams(("parallel",)),
        name="hyena_filter_mlp",
    )(z2, w1p, b1p, frp, w2p, b2p, w3p, b3p, deltas)


def _dft_tables(seq_len):
    n = 2 * seq_len
    n2 = DFT_N2
    n1 = n // n2
    hk = n2 // 2
    kg = min(n1, DFT_GROUP)
    ng = n1 // kg
    pi = math.pi
    k1 = jnp.arange(n1, dtype=jnp.int32)
    ph1 = ((2 * k1[:, None] + 1) * k1[None, :]) % (2 * n1)
    th1 = ph1.astype(F32) * (pi / n1)
    c1 = jnp.cos(th1).reshape(ng, kg, n1)
    s1 = jnp.sin(th1).reshape(ng, kg, n1)
    f1 = jnp.concatenate([c1, -s1], axis=1)
    fb = jnp.concatenate([c1, -s1], axis=1).transpose(0, 2, 1)[:, : n1 // 2] * (2.0 / n)
    k = k1[:, None, None] + n1 * jnp.arange(hk, dtype=jnp.int32)[None, :, None]
    ph2 = ((2 * k + 1) * jnp.arange(n2, dtype=jnp.int32)[None, None, :]) % (2 * n)
    th2 = ph2.astype(F32) * (pi / n)
    c2, s2 = jnp.cos(th2), jnp.sin(th2)
    g = jnp.concatenate([jnp.concatenate([c2, s2], axis=2),
                         jnp.concatenate([-s2, c2], axis=2)], axis=1)
    c2t, s2t = c2.transpose(0, 2, 1), s2.transpose(0, 2, 1)
    h = jnp.concatenate([jnp.concatenate([c2t, -s2t], axis=2),
                         jnp.concatenate([s2t, c2t], axis=2)], axis=1)
    return dict(n1=n1, kg=kg, ng=ng, f1_full=f1.astype(BF16), f1_half=f1[:, :, : n1 // 2].astype(BF16),
                fb=fb.astype(BF16), g=g.astype(BF16), h=h.astype(BF16))


def _fill_slabs(x_ref, xs, slabs, pitch):
    def body(n1, carry):
        src = pl.ds(pl.multiple_of(n1 * DFT_N2, DFT_N2), DFT_N2)
        xs[pl.ds(pl.multiple_of(n1 * pitch, SUBLANES), DFT_N2), :] = x_ref[src, :].astype(F32)
        return carry
    lax.fori_loop(0, slabs, body, 0, unroll=4)


def _slab_stage(xs, a_s, f1, slabs, p_x, p_a):
    rows = f1.shape[0]

    def body(i, carry):
        xn = xs[pl.ds(i, slabs, stride=p_x), :].astype(BF16)
        a_s[pl.ds(pl.multiple_of(i * p_a, SUBLANES), rows), :] = jnp.dot(f1, xn, preferred_element_type=F32)
        return carry
    lax.fori_loop(0, DFT_N2, body, 0, unroll=32)


def _load_slab_freq(a_s, j, kg, p_a):
    are = a_s[pl.ds(j, DFT_N2, stride=p_a), :]
    aim = a_s[pl.ds(kg + j, DFT_N2, stride=p_a), :]
    return jnp.concatenate([are, aim], axis=0).astype(BF16)


def _filter_spec_kernel(x_ref, f1_ref, g_ref, o_ref, xs, a_s, *, slabs, kg):
    grp = pl.program_id(1)
    p_x, p_a = _pitch(DFT_N2), _pitch(2 * kg)

    @pl.when(grp == 0)
    def _():
        _fill_slabs(x_ref, xs, slabs, p_x)

    _slab_stage(xs, a_s, f1_ref[0], slabs, p_x, p_a)

    def body(j, carry):
        o_ref[j] = jnp.dot(g_ref[j], _load_slab_freq(a_s, j, kg, p_a), preferred_element_type=F32)
        return carry
    lax.fori_loop(0, kg, body, 0, unroll=8)


def _filter_spectrum(f, tables):
    n, cols = f.shape
    n1, kg, ng = tables["n1"], tables["kg"], tables["ng"]
    p_x, p_a = _pitch(DFT_N2), _pitch(2 * kg)
    return pl.pallas_call(
        functools.partial(_filter_spec_kernel, slabs=n1, kg=kg),
        grid=(cols // LANES, ng),
        in_specs=[pl.BlockSpec((n, LANES), lambda c, g: (0, c)),
                  pl.BlockSpec((1, 2 * kg, n1), lambda c, g: (g, 0, 0)),
                  pl.BlockSpec((kg, DFT_N2, 2 * DFT_N2), lambda c, g: (g, 0, 0))],
        out_specs=pl.BlockSpec((kg, DFT_N2, LANES), lambda c, g: (g, 0, c)),
        out_shape=jax.ShapeDtypeStruct((n1, DFT_N2, cols), F32),
        scratch_shapes=[pltpu.VMEM((n1 * p_x, LANES), F32),
                        pltpu.VMEM((DFT_N2 * p_a, LANES), F32)],
        compiler_params=_cparams(("parallel", "arbitrary")),
        name="hyena_filter_spectrum",
    )(f, tables["f1_full"], tables["g"])


def _long_conv_kernel(x_ref, gate_ref, f1_ref, g_ref, h_ref, kf_ref, fb_ref, skip_ref, o_ref,
                      xs, a_s, z_s, y_s, *, slabs, kg, ng):
    grp = pl.program_id(2)
    n2 = DFT_N2
    hk = n2 // 2
    p_x, p_a, p_z, p_y = _pitch(n2), _pitch(2 * kg), _pitch(2 * n2), _pitch(slabs)

    @pl.when(grp == 0)
    def _():
        _fill_slabs(x_ref.at[0], xs, slabs, p_x)
        y_s[...] = jnp.zeros_like(y_s)

    _slab_stage(xs, a_s, f1_ref[0], slabs, p_x, p_a)

    def freq_body(j, carry):
        x = jnp.dot(g_ref[j], _load_slab_freq(a_s, j, kg, p_a), preferred_element_type=F32)
        kf = kf_ref[j]
        xr, xi = x[:hk], x[hk:]
        kr, ki = kf[:hk], kf[hk:]
        y = jnp.concatenate([xr * kr - xi * ki, xr * ki + xi * kr], axis=0).astype(BF16)
        z_s[pl.ds(pl.multiple_of(j * p_z, SUBLANES), 2 * n2), :] = jnp.dot(
            h_ref[j], y, preferred_element_type=F32)
        return carry
    lax.fori_loop(0, kg, freq_body, 0, unroll=8)

    fb = fb_ref[0]

    def inv_body(t, carry):
        zre = z_s[pl.ds(t, kg, stride=p_z), :]
        zim = z_s[pl.ds(n2 + t, kg, stride=p_z), :]
        zz = jnp.concatenate([zre, zim], axis=0).astype(BF16)
        dst = pl.ds(pl.multiple_of(t * p_y, SUBLANES), slabs)
        y_s[dst, :] = y_s[dst, :] + jnp.dot(fb, zz, preferred_element_type=F32)
        return carry
    lax.fori_loop(0, n2, inv_body, 0, unroll=32)

    @pl.when(grp == ng - 1)
    def _():
        skip = skip_ref[...]

        def out_body(t1, carry):
            y = y_s[pl.ds(t1, n2, stride=p_y), :]
            u = xs[pl.ds(pl.multiple_of(t1 * p_x, SUBLANES), n2), :]
            rows = pl.ds(pl.multiple_of(t1 * n2, n2), n2)
            gate = gate_ref[0, rows, :].astype(F32)
            o_ref[0, rows, :] = ((y + skip * u) * gate).astype(o_ref.dtype)
            return carry
        lax.fori_loop(0, slabs, out_body, 0, unroll=4)


def _long_conv_gate(u, u_blk0, gate, gate_blk0, tables, kf, kf_blk0, skip, ch):
    bsz, seq_len, _ = u.shape
    n1, kg, ng = tables["n1"], tables["kg"], tables["ng"]
    slabs = n1 // 2
    p_x, p_a, p_z, p_y = _pitch(DFT_N2), _pitch(2 * kg), _pitch(2 * DFT_N2), _pitch(slabs)
    return pl.pallas_call(
        functools.partial(_long_conv_kernel, slabs=slabs, kg=kg, ng=ng),
        grid=(bsz, ch // LANES, ng),
        in_specs=[pl.BlockSpec((1, seq_len, LANES), lambda b, c, g: (b, 0, u_blk0 + c)),
                  pl.BlockSpec((1, seq_len, LANES), lambda b, c, g: (b, 0, gate_blk0 + c)),
                  pl.BlockSpec((1, 2 * kg, slabs), lambda b, c, g: (g, 0, 0)),
                  pl.BlockSpec((kg, DFT_N2, 2 * DFT_N2), lambda b, c, g: (g, 0, 0)),
                  pl.BlockSpec((kg, 2 * DFT_N2, DFT_N2), lambda b, c, g: (g, 0, 0)),
                  pl.BlockSpec((kg, DFT_N2, LANES), lambda b, c, g: (g, 0, kf_blk0 + c)),
                  pl.BlockSpec((1, slabs, 2 * kg), lambda b, c, g: (g, 0, 0)),
                  pl.BlockSpec((1, LANES), lambda b, c, g: (0, c))],
        out_specs=pl.BlockSpec((1, seq_len, LANES), lambda b, c, g: (b, 0, c)),
        out_shape=jax.ShapeDtypeStruct((bsz, seq_len, ch), BF16),
        scratch_shapes=[pltpu.VMEM((slabs * p_x, LANES), F32),
                        pltpu.VMEM((DFT_N2 * p_a, LANES), F32),
                        pltpu.VMEM((kg * p_z, LANES), F32),
                        pltpu.VMEM((DFT_N2 * p_y, LANES), F32)],
        compiler_params=_cparams(("parallel", "parallel", "arbitrary")),
        name="hyena_long_conv",
    )(u, gate, tables["f1_half"], tables["g"], tables["h"], kf, tables["fb"],
      skip.astype(F32).reshape(1, ch))


def _hyena(hy, tables, kf, skip, order, ch):
    nblk = ch // LANES
    z, z_blk0 = hy, 0
    for o in range(order):
        z = _long_conv_gate(z, z_blk0, hy, (o + 1) * nblk, tables, kf, o * nblk, skip[o], ch)
    return z


def _rope(x, cos, sin):
    half = x.shape[-1] // 2
    x1, x2 = x[:, :half], x[:, half:]
    return jnp.concatenate([x1 * cos - x2 * sin, x1 * sin + x2 * cos], axis=-1)


def _ret_kernel(lg_ref, q_ref, k_ref, v_ref, g_ref, cos_ref, sin_ref, o_ref, sb_all, sf_ref, sb_ref, *,
                n_super, chunks_per_super, k_scale):
    h = pl.program_id(1)
    s = pl.program_id(2)
    cc = RET_CHUNK
    dh = q_ref.shape[-1]
    lgf = lg_ref[0, h]
    lgb = lg_ref[1, h]
    row = lax.broadcasted_iota(jnp.int32, (cc, dh), 0).astype(F32)

    def load_k(c0):
        cos = cos_ref[pl.ds(c0, cc), :]
        sin = sin_ref[pl.ds(c0, cc), :]
        return _rope(k_ref[0, pl.ds(c0, cc), :].astype(F32), cos, sin) * k_scale

    @pl.when(s < n_super)
    def _backward_sweep():
        @pl.when(s == 0)
        def _():
            sb_ref[...] = jnp.zeros_like(sb_ref)

        sup = n_super - 1 - s
        zeta_b = jnp.exp(lgb * row)
        chunk_decay = jnp.exp(lgb * jnp.full((1, dh), float(cc), F32))

        def body(t, carry):
            c = chunks_per_super - 1 - t
            c0 = pl.multiple_of(c * cc, cc)
            sb_all[sup * chunks_per_super + c] = sb_ref[...].astype(BF16)
            kz = (load_k(c0) * zeta_b).astype(BF16)
            v = v_ref[0, pl.ds(c0, cc), :]
            upd = lax.dot_general(kz, v, (((0,), (0,)), ((), ())), preferred_element_type=F32)
            sb_ref[...] = sb_ref[...] * chunk_decay + upd
            return carry

        lax.fori_loop(0, chunks_per_super, body, 0)

    @pl.when(s >= n_super)
    def _forward_sweep():
        @pl.when(s == n_super)
        def _():
            sf_ref[...] = jnp.zeros_like(sf_ref)

        sup = s - n_super
        col = lax.broadcasted_iota(jnp.int32, (cc, cc), 1).astype(F32)
        rowc = lax.broadcasted_iota(jnp.int32, (cc, cc), 0).astype(F32)
        diff = rowc - col
        inner_decay = jnp.where(diff >= 0, jnp.exp(lgf * jnp.maximum(diff, 0.0)),
                                jnp.exp(lgb * jnp.maximum(-diff, 0.0)))
        xi_f = jnp.exp(lgf * (row + 1.0))
        xi_b = jnp.exp(lgb * (cc - row))
        zeta_f = jnp.exp(lgf * (cc - 1.0 - row))
        chunk_decay = jnp.exp(lgf * jnp.full((1, dh), float(cc), F32))

        def body(c, carry):
            c0 = pl.multiple_of(c * cc, cc)
            cos = cos_ref[pl.ds(c0, cc), :]
            sin = sin_ref[pl.ds(c0, cc), :]
            q = _rope(q_ref[0, pl.ds(c0, cc), :].astype(F32), cos, sin)
            k = load_k(c0)
            v = v_ref[0, pl.ds(c0, cc), :]
            sc = lax.dot_general(q.astype(BF16), k.astype(BF16), (((1,), (1,)), ((), ())),
                                 preferred_element_type=F32) * inner_decay
            ret = jnp.dot(sc.astype(BF16), v, preferred_element_type=F32)
            ret = ret + jnp.dot((q * xi_f).astype(BF16), sf_ref[...].astype(BF16),
                                preferred_element_type=F32)
            ret = ret + jnp.dot((q * xi_b).astype(BF16), sb_all[sup * chunks_per_super + c],
                                preferred_element_type=F32)
            upd = lax.dot_general((k * zeta_f).astype(BF16), v, (((0,), (0,)), ((), ())),
                                  preferred_element_type=F32)
            sf_ref[...] = sf_ref[...] * chunk_decay + upd
            ret = ret * lax.rsqrt(jnp.mean(ret * ret, -1, keepdims=True) + RMS_EPS)
            gate = g_ref[0, pl.ds(c0, cc), :].astype(F32)
            o_ref[0, pl.ds(c0, cc), :] = (ret * (gate / (1.0 + jnp.exp(-gate)))).astype(o_ref.dtype)
            return carry

        lax.fori_loop(0, chunks_per_super, body, 0)


def _retention(proj3, b0, bsz, seq_len, col0, nh, dh, log_gamma, cos, sin):
    sup_len = min(RET_SUPER, seq_len)
    n_super = seq_len // sup_len
    cps = sup_len // RET_CHUNK
    cb0 = col0 // dh

    def fwd_idx(s):
        return jnp.maximum(s - n_super, 0)

    def kv_idx(s):
        return jnp.where(s < n_super, n_super - 1 - s, s - n_super)

    return pl.pallas_call(
        functools.partial(_ret_kernel, n_super=n_super, chunks_per_super=cps, k_scale=dh ** -0.5),
        grid=(bsz, nh, 2 * n_super),
        in_specs=[pl.BlockSpec(memory_space=pltpu.SMEM),
                  pl.BlockSpec((1, sup_len, dh), lambda b, h, s: (b0 + b, fwd_idx(s), cb0 + h)),
                  pl.BlockSpec((1, sup_len, dh), lambda b, h, s: (b0 + b, kv_idx(s), cb0 + nh + h)),
                  pl.BlockSpec((1, sup_len, dh), lambda b, h, s: (b0 + b, kv_idx(s), cb0 + 2 * nh + h)),
                  pl.BlockSpec((1, sup_len, dh), lambda b, h, s: (b0 + b, fwd_idx(s), cb0 + 3 * nh + h)),
                  pl.BlockSpec((sup_len, dh // 2), lambda b, h, s: (kv_idx(s), 0)),
                  pl.BlockSpec((sup_len, dh // 2), lambda b, h, s: (kv_idx(s), 0))],
        out_specs=pl.BlockSpec((1, sup_len, dh), lambda b, h, s: (b, fwd_idx(s), h)),
        out_shape=jax.ShapeDtypeStruct((bsz, seq_len, nh * dh), BF16),
        scratch_shapes=[pltpu.VMEM((seq_len // RET_CHUNK, dh, dh), BF16),
                        pltpu.VMEM((dh, dh), F32),
                        pltpu.VMEM((dh, dh), F32)],
        compiler_params=_cparams(("parallel", "parallel", "arbitrary")),
        name="retention",
    )(log_gamma, proj3, proj3, proj3, proj3, cos, sin)


def _assemble_kernel(a1_ref, a2_ref, z1_ref, z2_ref, c1_ref, c2_ref, ga_ref, gb_ref, o_ref, *, nb1):
    first = pl.program_id(0) < nb1
    wa = a1_ref.shape[-1]
    wb = z1_ref.shape[-1]
    a = jnp.where(first, a1_ref[...], a2_ref[...]).astype(F32)
    a = a * lax.rsqrt(jnp.mean(a * a, -1, keepdims=True) + RMS_EPS) * ga_ref[...]
    z = jnp.where(first, z1_ref[...], z2_ref[...]).astype(F32)
    z = z * lax.rsqrt(jnp.mean(z * z, -1, keepdims=True) + RMS_EPS) * gb_ref[...]
    o_ref[:, :wa] = a.astype(o_ref.dtype)
    o_ref[:, wa:wa + wb] = z.astype(o_ref.dtype)
    o_ref[:, wa + wb:] = jnp.where(first, c1_ref[...], c2_ref[...])


def _assemble(a_parts, z_parts, c_parts, gain_a, gain_b, bm=512):
    m1, wa = a_parts[0].shape
    m = m1 + a_parts[1].shape[0]
    wb, wc = z_parts[0].shape[1], c_parts[0].shape[1]
    nb1 = m1 // bm
    part_specs = []
    for w in (wa, wb, wc):
        part_specs += [pl.BlockSpec((bm, w), _first_rows(nb1)), pl.BlockSpec((bm, w), _second_rows(nb1))]
    return pl.pallas_call(
        functools.partial(_assemble_kernel, nb1=nb1),
        grid=(m // bm,),
        in_specs=part_specs + [pl.BlockSpec((1, wa), lambda i: (0, 0)),
                               pl.BlockSpec((1, wb), lambda i: (0, 0))],
        out_specs=pl.BlockSpec((bm, wa + wb + wc), lambda i: (i, 0)),
        out_shape=jax.ShapeDtypeStruct((m, wa + wb + wc), BF16),
        compiler_params=_cparams(("arbitrary",)),
        name="assemble_mixers",
    )(*a_parts, *z_parts, *c_parts, gain_a.astype(F32).reshape(1, wa), gain_b.astype(F32).reshape(1, wb))


def _token_mixers(proj, requests, p, l, dims):
    nh_a, hd_a, ch, order, nh_r, dh_r = dims
    in_cols = proj.shape[1]
    total = proj.shape[0]
    wa = nh_a * hd_a
    hy0 = 3 * wa
    ret0 = hy0 + (order + 1) * ch
    log_gamma = jnp.log1p(-jnp.exp2(-p["ret_decay_exp"][l].astype(F32)))
    a_parts, z_parts, c_parts = [], [], []
    bias = _na_bias_tables(p["na_rpb"][l])
    for (bsz, seq_len, tok0) in requests:
        assert tok0 % seq_len == 0 and total % seq_len == 0
        proj3 = proj.reshape(total // seq_len, seq_len, in_cols)
        b0 = tok0 // seq_len
        a = _neighbourhood_attention(proj3, b0, bsz, seq_len, bias, nh_a, hd_a)
        a_parts.append(a.reshape(bsz * seq_len, wa))
        tables = _dft_tables(seq_len)
        f = _hyena_filter_signal(seq_len, p["hy_f_w1"][l], p["hy_f_b1"][l], p["hy_f_freq"][l],
                                 p["hy_f_w2"][l], p["hy_f_b2"][l], p["hy_f_w3"][l], p["hy_f_b3"][l],
                                 order, ch)
        kf = _filter_spectrum(f, tables)
        hy = _short_conv3(proj3, b0, bsz, seq_len, hy0, p["hy_conv_w"][l].astype(F32),
                          p["hy_conv_b"][l].astype(F32))
        z = _hyena(hy, tables, kf, p["hy_skip"][l], order, ch)
        z_parts.append(z.reshape(bsz * seq_len, ch))
        half = dh_r // 2
        inv_freq = 1.0 / (ROPE_BASE ** jnp.linspace(0.0, 1.0, half, dtype=F32))
        ang = jnp.arange(seq_len, dtype=F32)[:, None] * inv_freq[None, :]
        c = _retention(proj3, b0, bsz, seq_len, ret0, nh_r, dh_r, log_gamma, jnp.cos(ang), jnp.sin(ang))
        c_parts.append(c.reshape(bsz * seq_len, nh_r * dh_r))
    return _assemble(a_parts, z_parts, c_parts, p["grp_gain_a"][l], p["grp_gain_b"][l])


def _pick(n, candidates):
    for c in candidates:
        if n % c == 0:
            return c
    raise ValueError(f"no block size for {n}")


def kernel(x_prompt, x_sample, ln_in_g, ln_in_b, w_in, na_rpb, hy_conv_w, hy_conv_b, hy_f_w1, hy_f_b1,
           hy_f_freq, hy_f_w2, hy_f_b2, hy_f_w3, hy_f_b3, hy_skip, ret_decay_exp, grp_gain_a, grp_gain_b,
           w_out, ln1_g, ln1_b, w_ffn_in, w_ffn_out, ln2_g, ln2_b):
    p = dict(na_rpb=na_rpb, hy_conv_w=hy_conv_w, hy_conv_b=hy_conv_b, hy_f_w1=hy_f_w1, hy_f_b1=hy_f_b1,
             hy_f_freq=hy_f_freq, hy_f_w2=hy_f_w2, hy_f_b2=hy_f_b2, hy_f_w3=hy_f_w3, hy_f_b3=hy_f_b3,
             hy_skip=hy_skip, ret_decay_exp=ret_decay_exp, grp_gain_a=grp_gain_a, grp_gain_b=grp_gain_b)
    depth, d_model, in_cols = w_in.shape
    nh_a = na_rpb.shape[1]
    wa = grp_gain_a.shape[1]
    ch = grp_gain_b.shape[1]
    order = hy_skip.shape[1]
    nh_r = ret_decay_exp.shape[2]
    wr = (in_cols - 3 * wa - (order + 1) * ch) // 4
    dims = (nh_a, wa // nh_a, ch, order, nh_r, wr // nh_r)
    d_ff = w_ffn_out.shape[1]
    alpha = (2.0 * depth) ** 0.25

    bp, lp, _ = x_prompt.shape
    bs, ls, _ = x_sample.shape
    requests = [(bp, lp, 0), (bs, ls, bp * lp)]
    m1 = bp * lp
    m = m1 + bs * ls
    bm = _pick(m, (1024, 512, 256))
    bm_ln = 256
    assert m1 % bm_ln == 0 and m % bm_ln == 0 and m1 % 512 == 0

    xf, xb = _layernorm_join(x_prompt.reshape(m1, d_model), x_sample.reshape(bs * ls, d_model),
                             ln_in_g.astype(F32), ln_in_b.astype(F32), bm_ln)
    w_in_b, w_out_b = w_in.astype(BF16), w_out.astype(BF16)
    w_ffn_in_b, w_ffn_out_b = w_ffn_in.astype(BF16), w_ffn_out.astype(BF16)
    for l in range(depth):
        proj = _matmul(xb, w_in_b, l, bm, _pick(in_cols, (1024, 512, 256)))
        mixed = _token_mixers(proj, requests, p, l, dims)
        y = _matmul_residual(mixed, w_out_b, l, xf, alpha, bm, _pick(d_model, (1024, 512, 256)))
        xf, xb = _layernorm(y, ln1_g[l].astype(F32), ln1_b[l].astype(F32), bm_ln)
        hmid = _matmul_swiglu(xb, w_ffn_in_b, l, bm, _pick(d_ff, (512, 256, 128)))
        y = _matmul_residual(hmid, w_ffn_out_b, l, xf, alpha, _pick(m, (512, 256)),
                             _pick(d_model, (512, 256)))
        if l + 1 < depth:
            xf, xb = _layernorm(y, ln2_g[l].astype(F32), ln2_b[l].astype(F32), bm_ln)
    o1, o2 = _layernorm_split(y, ln2_g[depth - 1].astype(F32), ln2_b[depth - 1].astype(F32), m1, bm_ln)
    return (o1.reshape(bp, lp, d_model), o2.reshape(bs, ls, d_model))
```

```python
import functools
import math

import numpy as np
import jax
import jax.numpy as jnp
from jax import lax
from jax.experimental import pallas as pl
from jax.experimental.pallas import tpu as pltpu

F32 = jnp.float32
BF16 = jnp.bfloat16

GRID_W = 64
WIN_R = 8
WIN_C = 16
ROPE_BASE = 10000.0
LN_EPS = 1e-5
RMS_EPS = 1e-6
NEG_INF = -1e30
HY_TARGET = 1e-2
HY_FAST_DECAY = 0.3
HY_SLOW_DECAY = 1.5
HY_MIN_DECAY = math.log(HY_TARGET) / HY_SLOW_DECAY
HY_MAX_DECAY = math.log(HY_TARGET) / HY_FAST_DECAY

V7X_VMEM_LIMIT = 56 * 1024 * 1024
LANES = 128
SUBLANES = 8
DFT_N2 = 128
DFT_GROUP = 32
NA_ROWS_PER_STEP = 8
RET_CHUNK = 256
RET_SUPER = 2048


def _cparams(sem, vmem=V7X_VMEM_LIMIT):
    return pltpu.CompilerParams(dimension_semantics=sem, vmem_limit_bytes=vmem)


def _pitch(rows):
    p = -(-rows // SUBLANES)
    return SUBLANES * (p + 1 - p % 2)


def _ln(x, g, b):
    mu = jnp.mean(x, -1, keepdims=True)
    xc = x - mu
    var = jnp.mean(xc * xc, -1, keepdims=True)
    return xc * lax.rsqrt(var + LN_EPS) * g + b


def _ln_stats_kernel(x_ref, g_ref, b_ref, ob_ref, st_ref):
    x = x_ref[...]
    mu = jnp.mean(x, -1, keepdims=True)
    xc = x - mu
    rstd = lax.rsqrt(jnp.mean(xc * xc, -1, keepdims=True) + LN_EPS)
    ob_ref[...] = (xc * rstd * g_ref[...] + b_ref[...]).astype(BF16)
    st_ref[:, :LANES] = jnp.broadcast_to(mu, (x.shape[0], LANES))
    st_ref[:, LANES:] = jnp.broadcast_to(rstd, (x.shape[0], LANES))


def _ln_join_kernel(x1_ref, x2_ref, g_ref, b_ref, of_ref, ob_ref, *, nb1):
    x = jnp.where(pl.program_id(0) < nb1, x1_ref[...], x2_ref[...])
    y = _ln(x, g_ref[...], b_ref[...])
    of_ref[...] = y
    ob_ref[...] = y.astype(BF16)


def _ln_split_kernel(x_ref, g_ref, b_ref, o1_ref, o2_ref, *, nb1):
    y = _ln(x_ref[...], g_ref[...], b_ref[...])

    @pl.when(pl.program_id(0) < nb1)
    def _():
        o1_ref[...] = y

    @pl.when(pl.program_id(0) >= nb1)
    def _():
        o2_ref[...] = y


def _first_rows(nb1):
    return lambda i: (jnp.minimum(i, nb1 - 1), 0)


def _second_rows(nb1):
    return lambda i: (jnp.maximum(i - nb1, 0), 0)


def _layernorm_join(x1, x2, g, b, bm=256):
    m1, d = x1.shape
    m = m1 + x2.shape[0]
    nb1 = m1 // bm
    return pl.pallas_call(
        functools.partial(_ln_join_kernel, nb1=nb1),
        grid=(m // bm,),
        in_specs=[pl.BlockSpec((bm, d), _first_rows(nb1)),
                  pl.BlockSpec((bm, d), _second_rows(nb1)),
                  pl.BlockSpec((1, d), lambda i: (0, 0)),
                  pl.BlockSpec((1, d), lambda i: (0, 0))],
        out_specs=[pl.BlockSpec((bm, d), lambda i: (i, 0)),
                   pl.BlockSpec((bm, d), lambda i: (i, 0))],
        out_shape=[jax.ShapeDtypeStruct((m, d), F32), jax.ShapeDtypeStruct((m, d), BF16)],
        compiler_params=_cparams(("arbitrary",)),
        name="layernorm_join",
    )(x1, x2, g.reshape(1, d), b.reshape(1, d))


def _layernorm_split(x, g, b, m1, bm=256):
    m, d = x.shape
    nb1 = m1 // bm
    return pl.pallas_call(
        functools.partial(_ln_split_kernel, nb1=nb1),
        grid=(m // bm,),
        in_specs=[pl.BlockSpec((bm, d), lambda i: (i, 0)),
                  pl.BlockSpec((1, d), lambda i: (0, 0)),
                  pl.BlockSpec((1, d), lambda i: (0, 0))],
        out_specs=[pl.BlockSpec((bm, d), _first_rows(nb1)),
                   pl.BlockSpec((bm, d), _second_rows(nb1))],
        out_shape=[jax.ShapeDtypeStruct((m1, d), F32), jax.ShapeDtypeStruct((m - m1, d), F32)],
        compiler_params=_cparams(("arbitrary",)),
        name="layernorm_split",
    )(x, g.reshape(1, d), b.reshape(1, d))


def _layernorm_stats(x, g, b, bm=256):
    m, d = x.shape
    return pl.pallas_call(
        _ln_stats_kernel,
        grid=(m // bm,),
        in_specs=[pl.BlockSpec((bm, d), lambda i: (i, 0)),
                  pl.BlockSpec((1, d), lambda i: (0, 0)),
                  pl.BlockSpec((1, d), lambda i: (0, 0))],
        out_specs=[pl.BlockSpec((bm, d), lambda i: (i, 0)),
                   pl.BlockSpec((bm, 2 * LANES), lambda i: (i, 0))],
        out_shape=[jax.ShapeDtypeStruct((m, d), BF16), jax.ShapeDtypeStruct((m, 2 * LANES), F32)],
        compiler_params=_cparams(("parallel",)),
        name="layernorm_stats",
    )(x, g.reshape(1, d), b.reshape(1, d))


def _mm_kernel(x_ref, w_ref, o_ref):
    o_ref[...] = jnp.dot(x_ref[...], w_ref[...], preferred_element_type=F32).astype(o_ref.dtype)


def _mm_res_kernel(x_ref, w_ref, r_ref, o_ref, *, alpha):
    o_ref[...] = alpha * r_ref[...] + jnp.dot(x_ref[...], w_ref[...], preferred_element_type=F32)


def _mm_res_ln_kernel(x_ref, w_ref, y_ref, st_ref, g_ref, b_ref, o_ref, *, alpha):
    acc = jnp.dot(x_ref[...], w_ref[...], preferred_element_type=F32)
    mu = st_ref[:, :LANES]
    rstd = st_ref[:, LANES:]
    for c in range(y_ref.shape[1] // LANES):
        cols = slice(c * LANES, (c + 1) * LANES)
        r = (y_ref[:, cols] - mu) * rstd * g_ref[:, cols] + b_ref[:, cols]
        o_ref[:, cols] = alpha * r + acc[:, cols]


def _mm_swiglu_kernel(x_ref, wg_ref, wu_ref, o_ref):
    x = x_ref[...]
    g = jnp.dot(x, wg_ref[...], preferred_element_type=F32)
    u = jnp.dot(x, wu_ref[...], preferred_element_type=F32)
    o_ref[...] = (g / (1.0 + jnp.exp(-g)) * u).astype(o_ref.dtype)


def _matmul(x, w, layer, bm, bn, out_dtype=BF16):
    m, k = x.shape
    n = w.shape[2]
    return pl.pallas_call(
        _mm_kernel,
        grid=(m // bm, n // bn),
        in_specs=[pl.BlockSpec((bm, k), lambda i, j: (i, 0)),
                  pl.BlockSpec((None, k, bn), lambda i, j: (layer, 0, j))],
        out_specs=pl.BlockSpec((bm, bn), lambda i, j: (i, j)),
        out_shape=jax.ShapeDtypeStruct((m, n), out_dtype),
        compiler_params=_cparams(("parallel", "arbitrary")),
        name="matmul",
    )(x, w)


def _matmul_residual(x, w, layer, resid, alpha, bm, bn):
    m, k = x.shape
    n = w.shape[2]
    tile = pl.BlockSpec((bm, bn), lambda i, j: (i, j))
    if len(resid) == 1:
        body, resid_specs = _mm_res_kernel, [tile]
    else:
        body = _mm_res_ln_kernel
        resid = (resid[0], resid[1], resid[2].reshape(1, n), resid[3].reshape(1, n))
        resid_specs = [tile, pl.BlockSpec((bm, 2 * LANES), lambda i, j: (i, 0)),
                       pl.BlockSpec((1, bn), lambda i, j: (0, j)), pl.BlockSpec((1, bn), lambda i, j: (0, j))]
    return pl.pallas_call(
        functools.partial(body, alpha=alpha),
        grid=(m // bm, n // bn),
        in_specs=[pl.BlockSpec((bm, k), lambda i, j: (i, 0)),
                  pl.BlockSpec((None, k, bn), lambda i, j: (layer, 0, j))] + resid_specs,
        out_specs=tile,
        out_shape=jax.ShapeDtypeStruct((m, n), F32),
        compiler_params=_cparams(("parallel", "arbitrary")),
        name="matmul_residual",
    )(x, w, *resid)


def _matmul_swiglu(x, w, layer, bm, bn):
    m, k = x.shape
    f = w.shape[2] // 2
    nb = f // bn
    return pl.pallas_call(
        _mm_swiglu_kernel,
        grid=(m // bm, nb),
        in_specs=[pl.BlockSpec((bm, k), lambda i, j: (i, 0)),
                  pl.BlockSpec((None, k, bn), lambda i, j: (layer, 0, j)),
                  pl.BlockSpec((None, k, bn), lambda i, j: (layer, 0, j + nb))],
        out_specs=pl.BlockSpec((bm, bn), lambda i, j: (i, j)),
        out_shape=jax.ShapeDtypeStruct((m, f), BF16),
        compiler_params=_cparams(("parallel", "arbitrary")),
        name="matmul_swiglu",
    )(x, w, w)


def _na_bias_tables(rpb):
    nh = rpb.shape[0]
    qc = np.arange(GRID_W)[:, None]
    kc = np.arange(GRID_W)[None, :]
    win_start = np.clip(qc - WIN_C // 2, 0, GRID_W - WIN_C)
    col_ok = (kc >= win_start) & (kc < win_start + WIN_C)
    dc_idx = np.clip(kc - qc + (WIN_C - 1), 0, 2 * WIN_C - 2)
    onehot = (dc_idx.reshape(-1)[None, :] == np.arange(2 * WIN_C - 1)[:, None]).astype(np.float32)
    tiles = jnp.einsum("hdc,cq->hdq", rpb.astype(F32), jnp.asarray(onehot),
                       precision=lax.Precision.HIGHEST).reshape(nh, 2 * WIN_R - 1, GRID_W, GRID_W)
    tiles = jnp.where(jnp.asarray(col_ok)[None, None], tiles, NEG_INF)
    dr = np.arange(WIN_R)[:, None] + np.arange(WIN_R)[None, :]
    t = tiles[:, dr]
    return t.transpose(0, 1, 3, 2, 4).reshape(nh, WIN_R, GRID_W, WIN_R * GRID_W)


def _na_kernel(q_ref, k_ref, v_ref, bias_ref, o_ref, *, rows, scale):
    g = pl.program_id(2)
    nk = WIN_R * GRID_W
    ks, vs, bias = [], [], []
    for i in range(NA_ROWS_PER_STEP):
        r = g * NA_ROWS_PER_STEP + i
        row_start = jnp.clip(r - WIN_R // 2, 0, rows - WIN_R)
        k0 = pl.multiple_of(row_start * GRID_W, GRID_W)
        ks.append(k_ref[0, pl.ds(k0, nk), :])
        vs.append(v_ref[0, pl.ds(k0, nk), :])
        bias.append(bias_ref[row_start - r + (WIN_R - 1)])
    q = q_ref[0].reshape(NA_ROWS_PER_STEP, GRID_W, q_ref.shape[-1])
    s = jnp.einsum("rqd,rkd->rqk", q, jnp.stack(ks), preferred_element_type=F32)
    s = s * scale + jnp.stack(bias)
    m = jnp.max(s, -1, keepdims=True)
    p = jnp.exp(s - m)
    l = jnp.sum(p, -1, keepdims=True)
    o = jnp.einsum("rqk,rkd->rqd", p.astype(BF16), jnp.stack(vs), preferred_element_type=F32) / l
    o_ref[0] = o.reshape(o_ref.shape[1:]).astype(o_ref.dtype)


def _neighbourhood_attention(proj3, b0, bsz, seq_len, bias, nh, hd):
    rows = seq_len // GRID_W
    assert rows % NA_ROWS_PER_STEP == 0 and rows >= WIN_R
    nq = NA_ROWS_PER_STEP * GRID_W
    return pl.pallas_call(
        functools.partial(_na_kernel, rows=rows, scale=hd ** -0.5),
        grid=(nh, bsz, rows // NA_ROWS_PER_STEP),
        in_specs=[pl.BlockSpec((1, nq, hd), lambda h, b, g: (b0 + b, g, h)),
                  pl.BlockSpec((1, seq_len, hd), lambda h, b, g: (b0 + b, 0, nh + h)),
                  pl.BlockSpec((1, seq_len, hd), lambda h, b, g: (b0 + b, 0, 2 * nh + h)),
                  pl.BlockSpec((None, WIN_R, GRID_W, WIN_R * GRID_W), lambda h, b, g: (h, 0, 0, 0))],
        out_specs=pl.BlockSpec((1, nq, hd), lambda h, b, g: (b, g, h)),
        out_shape=jax.ShapeDtypeStruct((bsz, seq_len, nh * hd), BF16),
        compiler_params=_cparams(("parallel", "parallel", "arbitrary")),
        name="neighbourhood_attention",
    )(proj3, proj3, proj3, bias)


def _conv3_kernel(x_ref, w_ref, b_ref, o_ref, *, seq_len, rb):
    w = w_ref[...]
    bias = b_ref[...]
    n_chunks = seq_len // rb

    def body(c, carry):
        r0 = pl.multiple_of(c * rb, rb)
        x = x_ref[0, pl.ds(r0, rb), :].astype(F32)
        row = lax.broadcasted_iota(jnp.int32, x.shape, 0)
        p0 = pl.multiple_of(jnp.maximum(r0 - 16, 0), 16)
        n0 = pl.multiple_of(jnp.minimum(r0 + rb, seq_len - 16), 16)
        prev_row = x_ref[0, pl.ds(p0, 16), :].astype(F32)[15:16]
        next_row = x_ref[0, pl.ds(n0, 16), :].astype(F32)[0:1]
        prev_row = jnp.where(c == 0, 0.0, prev_row)
        next_row = jnp.where(c == n_chunks - 1, 0.0, next_row)
        up = jnp.where(row == 0, prev_row, pltpu.roll(x, 1, 0))
        down = jnp.where(row == rb - 1, next_row, pltpu.roll(x, rb - 1, 0))
        y = up * w[0:1] + x * w[1:2] + down * w[2:3] + bias
        o_ref[0, pl.ds(r0, rb), :] = y.astype(o_ref.dtype)
        return carry

    lax.fori_loop(0, n_chunks, body, 0)


def _short_conv3(proj3, b0, bsz, seq_len, col0, w, b, cb=256, rb=512):
    width = w.shape[1]
    rb = min(rb, seq_len)
    c0 = col0 // cb
    return pl.pallas_call(
        functools.partial(_conv3_kernel, seq_len=seq_len, rb=rb),
        grid=(bsz, width // cb),
        in_specs=[pl.BlockSpec((1, seq_len, cb), lambda bi, c: (b0 + bi, 0, c0 + c)),
                  pl.BlockSpec((3, cb), lambda bi, c: (0, c)),
                  pl.BlockSpec((1, cb), lambda bi, c: (0, c))],
        out_specs=pl.BlockSpec((1, seq_len, cb), lambda bi, c: (bi, 0, c)),
        out_shape=jax.ShapeDtypeStruct((bsz, seq_len, width), BF16),
        compiler_params=_cparams(("parallel", "parallel")),
        name="hyena_short_conv",
    )(proj3, w, b.reshape(1, width))


def _filter_kernel(z_ref, w1_ref, b1_ref, fr_ref, w2_ref, b2_ref, w3_ref, b3_ref, dl_ref, o_ref, *,
                   seq_len, rb):
    i = pl.program_id(0)
    hi = lax.Precision.HIGHEST
    z = z_ref[...]
    h = jnp.sin(fr_ref[0:1] * (jnp.dot(z, w1_ref[...], precision=hi, preferred_element_type=F32)
                               + b1_ref[...]))
    h = jnp.sin(fr_ref[1:2] * (jnp.dot(h, w2_ref[...], precision=hi, preferred_element_type=F32)
                               + b2_ref[...]))
    ch = dl_ref.shape[-1]
    n = i * rb + lax.broadcasted_iota(jnp.int32, (rb, ch), 0)
    sign = jnp.where(n < seq_len, 1.0, jnp.where(n == seq_len, 0.0, -1.0))
    window = jnp.exp(-z[:, 0:1] * dl_ref[...]) * sign
    for o in range(w3_ref.shape[0]):
        f = jnp.dot(h, w3_ref[o], precision=hi, preferred_element_type=F32) + b3_ref[o]
        o_ref[:, o * ch:(o + 1) * ch] = (f * window).astype(o_ref.dtype)


def _hyena_filter_signal(seq_len, w1, b1, freq, w2, b2, w3, b3, order, ch, rb=512):
    n = 2 * seq_len
    emb, hid = w1.shape
    pad = LANES
    t = jnp.linspace(0.0, 1.0, seq_len, dtype=F32)[:, None]
    bands = (emb - 1) // 2
    fr = jnp.linspace(1e-4, bands - 1, bands, dtype=F32)[None, :]
    wpos = 2.0 * math.pi * jnp.arange(seq_len, dtype=F32)[:, None] / seq_len
    z = jnp.concatenate([t, jnp.cos(fr * wpos), -jnp.sin(fr * wpos)], axis=-1)
    pos = np.arange(n)
    src = np.clip(np.where(pos < seq_len, pos, n - pos), 0, seq_len - 1)
    z2 = jnp.pad(z[src], ((0, 0), (0, pad - emb)))
    w1p = jnp.pad(w1.astype(F32), ((0, pad - emb), (0, pad - hid)))
    b1p = jnp.pad(b1.astype(F32), (0, pad - hid)).reshape(1, pad)
    frp = jnp.pad(freq.astype(F32), ((0, 0), (0, pad - hid)), constant_values=1.0)
    w2p = jnp.pad(w2.astype(F32), ((0, pad - hid), (0, pad - hid)))
    b2p = jnp.pad(b2.astype(F32), (0, pad - hid)).reshape(1, pad)
    w3p = jnp.pad(w3.astype(F32), ((0, pad - hid), (0, 0))).reshape(pad, order, 2, ch).transpose(2, 1, 0, 3)
    b3p = b3.astype(F32).reshape(order, 2, 1, ch).transpose(1, 0, 2, 3)
    deltas = jnp.abs(jnp.linspace(HY_MIN_DECAY, HY_MAX_DECAY, ch, dtype=F32)).reshape(1, ch)
    nb = n // rb
    half = nb // 2
    full = lambda i: (0, 0)
    return pl.pallas_call(
        functools.partial(_filter_kernel, seq_len=seq_len, rb=rb),
        grid=(nb,),
        in_specs=[pl.BlockSpec((rb, pad), lambda i: (i, 0)),
                  pl.BlockSpec((pad, pad), full), pl.BlockSpec((1, pad), full),
                  pl.BlockSpec((2, pad), full),
                  pl.BlockSpec((pad, pad), full), pl.BlockSpec((1, pad), full),
                  pl.BlockSpec((None, order, pad, ch), lambda i: (i // half, 0, 0, 0)),
                  pl.BlockSpec((None, order, 1, ch), lambda i: (i // half, 0, 0, 0)),
                  pl.BlockSpec((1, ch), full)],
        out_specs=pl.BlockSpec((rb, order * ch), lambda i: (i, 0)),
        out_shape=jax.ShapeDtypeStruct((n, order * ch), BF16),
        compiler_params=_cparams(("parallel",)),
        name="hyena_filter_mlp",
    )(z2, w1p, b1p, frp, w2p, b2p, w3p, b3p, deltas)


def _dft_tables(seq_len):
    n = 2 * seq_len
    n2 = DFT_N2
    n1 = n // n2
    hk = n2 // 2
    kg = min(n1, DFT_GROUP)
    ng = n1 // kg
    pi = math.pi
    k1 = jnp.arange(n1, dtype=jnp.int32)
    ph1 = ((2 * k1[:, None] + 1) * k1[None, :]) % (2 * n1)
    th1 = ph1.astype(F32) * (pi / n1)
    c1 = jnp.cos(th1).reshape(ng, kg, n1)
    s1 = jnp.sin(th1).reshape(ng, kg, n1)
    f1 = jnp.concatenate([c1, -s1], axis=1)
    fb = jnp.concatenate([c1, -s1], axis=1).transpose(0, 2, 1)[:, : n1 // 2] * (2.0 / n)
    k = k1[:, None, None] + n1 * jnp.arange(hk, dtype=jnp.int32)[None, :, None]
    ph2 = ((2 * k + 1) * jnp.arange(n2, dtype=jnp.int32)[None, None, :]) % (2 * n)
    th2 = ph2.astype(F32) * (pi / n)
    c2, s2 = jnp.cos(th2), jnp.sin(th2)
    g = jnp.concatenate([jnp.concatenate([c2, s2], axis=2),
                         jnp.concatenate([-s2, c2], axis=2)], axis=1)
    c2t, s2t = c2.transpose(0, 2, 1), s2.transpose(0, 2, 1)
    h = jnp.concatenate([jnp.concatenate([c2t, -s2t], axis=2),
                         jnp.concatenate([s2t, c2t], axis=2)], axis=1)
    nl = 2 if n1 <= DFT_GROUP else 1
    return dict(n1=n1, kg=kg, ng=ng, nl=nl, f1_full=f1.astype(BF16), f1_half=f1[:, :, : n1 // 2].astype(BF16),
                fb=fb.astype(BF16), g=g.astype(BF16), h=h.astype(BF16))


def _load_lanes(ref, rows):
    parts = [ref[s, rows, :] for s in range(ref.shape[0])]
    return parts[0] if len(parts) == 1 else jnp.concatenate(parts, axis=1)


def _store_lanes(ref, rows, val):
    for s in range(ref.shape[0]):
        ref[s, rows, :] = val[:, s * LANES:(s + 1) * LANES]


def _fill_slabs(x_ref, xs, slabs, pitch):
    def body(n1, carry):
        src = pl.ds(pl.multiple_of(n1 * DFT_N2, DFT_N2), DFT_N2)
        _store_lanes(xs, pl.ds(pl.multiple_of(n1 * pitch, SUBLANES), DFT_N2), x_ref[src, :].astype(F32))
        return carry
    lax.fori_loop(0, slabs, body, 0, unroll=4)


def _slab_stage(xs, a_s, f1, slabs, p_x, p_a):
    rows = f1.shape[0]

    def body(i, carry):
        xn = _load_lanes(xs, pl.ds(i, slabs, stride=p_x)).astype(BF16)
        _store_lanes(a_s, pl.ds(pl.multiple_of(i * p_a, SUBLANES), rows),
                     jnp.dot(f1, xn, preferred_element_type=F32))
        return carry
    lax.fori_loop(0, DFT_N2, body, 0, unroll=32)


def _load_slab_freq(a_s, j, kg, p_a):
    are = _load_lanes(a_s, pl.ds(j, DFT_N2, stride=p_a))
    aim = _load_lanes(a_s, pl.ds(kg + j, DFT_N2, stride=p_a))
    return jnp.concatenate([are, aim], axis=0).astype(BF16)


def _filter_spec_kernel(x_ref, f1_ref, g_ref, o_ref, xs, a_s, *, slabs, kg):
    grp = pl.program_id(1)
    p_x, p_a = _pitch(DFT_N2), _pitch(2 * kg)

    @pl.when(grp == 0)
    def _():
        _fill_slabs(x_ref, xs, slabs, p_x)

    _slab_stage(xs, a_s, f1_ref[0], slabs, p_x, p_a)

    def body(j, carry):
        o_ref[j] = jnp.dot(g_ref[j], _load_slab_freq(a_s, j, kg, p_a), preferred_element_type=F32)
        return carry
    lax.fori_loop(0, kg, body, 0, unroll=8)


def _filter_spectrum(f, tables):
    n, cols = f.shape
    n1, kg, ng, nl = tables["n1"], tables["kg"], tables["ng"], tables["nl"]
    p_x, p_a = _pitch(DFT_N2), _pitch(2 * kg)
    wl = nl * LANES
    return pl.pallas_call(
        functools.partial(_filter_spec_kernel, slabs=n1, kg=kg),
        grid=(cols // wl, ng),
        in_specs=[pl.BlockSpec((n, wl), lambda c, g: (0, c)),
                  pl.BlockSpec((1, 2 * kg, n1), lambda c, g: (g, 0, 0)),
                  pl.BlockSpec((kg, DFT_N2, 2 * DFT_N2), lambda c, g: (g, 0, 0))],
        out_specs=pl.BlockSpec((kg, DFT_N2, wl), lambda c, g: (g, 0, c)),
        out_shape=jax.ShapeDtypeStruct((n1, DFT_N2, cols), F32),
        scratch_shapes=[pltpu.VMEM((nl, n1 * p_x, LANES), F32),
                        pltpu.VMEM((nl, DFT_N2 * p_a, LANES), F32)],
        compiler_params=_cparams(("parallel", "arbitrary")),
        name="hyena_filter_spectrum",
    )(f, tables["f1_full"], tables["g"])


def _long_conv_kernel(x_ref, gate_ref, f1_ref, g_ref, h_ref, kf_ref, fb_ref, skip_ref, o_ref,
                      xs, a_s, z_s, y_s, *, slabs, kg, ng):
    grp = pl.program_id(2)
    n2 = DFT_N2
    hk = n2 // 2
    p_x, p_a, p_z, p_y = _pitch(n2), _pitch(2 * kg), _pitch(2 * n2), _pitch(slabs)

    @pl.when(grp == 0)
    def _():
        _fill_slabs(x_ref.at[0], xs, slabs, p_x)
        y_s[...] = jnp.zeros_like(y_s)

    _slab_stage(xs, a_s, f1_ref[0], slabs, p_x, p_a)

    def freq_body(j, carry):
        x = jnp.dot(g_ref[j], _load_slab_freq(a_s, j, kg, p_a), preferred_element_type=F32)
        kf = kf_ref[j]
        xr, xi = x[:hk], x[hk:]
        kr, ki = kf[:hk], kf[hk:]
        y = jnp.concatenate([xr * kr - xi * ki, xr * ki + xi * kr], axis=0).astype(BF16)
        _store_lanes(z_s, pl.ds(pl.multiple_of(j * p_z, SUBLANES), 2 * n2),
                     jnp.dot(h_ref[j], y, preferred_element_type=F32))
        return carry
    lax.fori_loop(0, kg, freq_body, 0, unroll=8)

    fb = fb_ref[0]

    def inv_body(t, carry):
        zre = _load_lanes(z_s, pl.ds(t, kg, stride=p_z))
        zim = _load_lanes(z_s, pl.ds(n2 + t, kg, stride=p_z))
        zz = jnp.concatenate([zre, zim], axis=0).astype(BF16)
        dst = pl.ds(pl.multiple_of(t * p_y, SUBLANES), slabs)
        _store_lanes(y_s, dst, _load_lanes(y_s, dst) + jnp.dot(fb, zz, preferred_element_type=F32))
        return carry
    lax.fori_loop(0, n2, inv_body, 0, unroll=32)

    @pl.when(grp == ng - 1)
    def _():
        skip = skip_ref[...]

        def out_body(t1, carry):
            y = _load_lanes(y_s, pl.ds(t1, n2, stride=p_y))
            u = _load_lanes(xs, pl.ds(pl.multiple_of(t1 * p_x, SUBLANES), n2))
            rows = pl.ds(pl.multiple_of(t1 * n2, n2), n2)
            gate = gate_ref[0, rows, :].astype(F32)
            o_ref[0, rows, :] = ((y + skip * u) * gate).astype(o_ref.dtype)
            return carry
        lax.fori_loop(0, slabs, out_body, 0, unroll=4)


def _long_conv_gate(u, u_blk0, gate, gate_blk0, tables, kf, kf_blk0, skip, ch):
    bsz, seq_len, _ = u.shape
    n1, kg, ng, nl = tables["n1"], tables["kg"], tables["ng"], tables["nl"]
    slabs = n1 // 2
    p_x, p_a, p_z, p_y = _pitch(DFT_N2), _pitch(2 * kg), _pitch(2 * DFT_N2), _pitch(slabs)
    wl = nl * LANES
    return pl.pallas_call(
        functools.partial(_long_conv_kernel, slabs=slabs, kg=kg, ng=ng),
        grid=(bsz, ch // wl, ng),
        in_specs=[pl.BlockSpec((1, seq_len, wl), lambda b, c, g: (b, 0, u_blk0 + c)),
                  pl.BlockSpec((1, seq_len, wl), lambda b, c, g: (b, 0, gate_blk0 + c)),
                  pl.BlockSpec((1, 2 * kg, slabs), lambda b, c, g: (g, 0, 0)),
                  pl.BlockSpec((kg, DFT_N2, 2 * DFT_N2), lambda b, c, g: (g, 0, 0)),
                  pl.BlockSpec((kg, 2 * DFT_N2, DFT_N2), lambda b, c, g: (g, 0, 0)),
                  pl.BlockSpec((kg, DFT_N2, wl), lambda b, c, g: (g, 0, kf_blk0 + c)),
                  pl.BlockSpec((1, slabs, 2 * kg), lambda b, c, g: (g, 0, 0)),
                  pl.BlockSpec((1, wl), lambda b, c, g: (0, c))],
        out_specs=pl.BlockSpec((1, seq_len, wl), lambda b, c, g: (b, 0, c)),
        out_shape=jax.ShapeDtypeStruct((bsz, seq_len, ch), BF16),
        scratch_shapes=[pltpu.VMEM((nl, slabs * p_x, LANES), F32),
                        pltpu.VMEM((nl, DFT_N2 * p_a, LANES), F32),
                        pltpu.VMEM((nl, kg * p_z, LANES), F32),
                        pltpu.VMEM((nl, DFT_N2 * p_y, LANES), F32)],
        compiler_params=_cparams(("parallel", "parallel", "arbitrary")),
        name="hyena_long_conv",
    )(u, gate, tables["f1_half"], tables["g"], tables["h"], kf, tables["fb"],
      skip.astype(F32).reshape(1, ch))


def _hyena(hy, tables, kf, skip, order, ch):
    nblk = ch // (tables["nl"] * LANES)
    z, z_blk0 = hy, 0
    for o in range(order):
        z = _long_conv_gate(z, z_blk0, hy, (o + 1) * nblk, tables, kf, o * nblk, skip[o], ch)
    return z


def _rope(x, cos, sin):
    half = x.shape[-1] // 2
    x1, x2 = x[:, :half], x[:, half:]
    return jnp.concatenate([x1 * cos - x2 * sin, x1 * sin + x2 * cos], axis=-1)


def _ret_kernel(lg_ref, q_ref, k_ref, v_ref, g_ref, cos_ref, sin_ref, o_ref,
                sb_all, kr_all, kz_all, sf_ref, sb_ref, *, n_super, chunks_per_super, k_scale):
    h = pl.program_id(1)
    s = pl.program_id(2)
    cc = RET_CHUNK
    dh = q_ref.shape[-1]
    lgf = lg_ref[0, h]
    lgb = lg_ref[1, h]
    row = lax.broadcasted_iota(jnp.int32, (cc, dh), 0).astype(F32)
    zeta_f = jnp.exp(lgf * (cc - 1.0 - row))

    @pl.when(s < n_super)
    def _backward_sweep():
        @pl.when(s == 0)
        def _():
            sb_ref[...] = jnp.zeros_like(sb_ref)

        sup = n_super - 1 - s
        zeta_b = jnp.exp(lgb * row)
        chunk_decay = jnp.exp(lgb * jnp.full((1, dh), float(cc), F32))

        def body(t, carry):
            c = chunks_per_super - 1 - t
            c0 = pl.multiple_of(c * cc, cc)
            n = sup * chunks_per_super + c
            sb_all[n] = sb_ref[...].astype(BF16)
            k = _rope(k_ref[0, pl.ds(c0, cc), :].astype(F32), cos_ref[pl.ds(c0, cc), :],
                      sin_ref[pl.ds(c0, cc), :]) * k_scale
            kr_all[n] = k.astype(BF16)
            kz_all[n] = (k * zeta_f).astype(BF16)
            kz = (k * zeta_b).astype(BF16)
            v = v_ref[0, pl.ds(c0, cc), :]
            upd = lax.dot_general(kz, v, (((0,), (0,)), ((), ())), preferred_element_type=F32)
            sb_ref[...] = sb_ref[...] * chunk_decay + upd
            return carry

        lax.fori_loop(0, chunks_per_super, body, 0)

    @pl.when(s >= n_super)
    def _forward_sweep():
        @pl.when(s == n_super)
        def _():
            sf_ref[...] = jnp.zeros_like(sf_ref)

        sup = s - n_super
        col = lax.broadcasted_iota(jnp.int32, (cc, cc), 1).astype(F32)
        rowc = lax.broadcasted_iota(jnp.int32, (cc, cc), 0).astype(F32)
        diff = rowc - col
        inner_decay = jnp.where(diff >= 0, jnp.exp(lgf * jnp.maximum(diff, 0.0)),
                                jnp.exp(lgb * jnp.maximum(-diff, 0.0)))
        xi_f = jnp.exp(lgf * (row + 1.0))
        xi_b = jnp.exp(lgb * (cc - row))
        chunk_decay = jnp.exp(lgf * jnp.full((1, dh), float(cc), F32))

        def body(c, carry):
            c0 = pl.multiple_of(c * cc, cc)
            n = sup * chunks_per_super + c
            q = _rope(q_ref[0, pl.ds(c0, cc), :].astype(F32), cos_ref[pl.ds(c0, cc), :],
                      sin_ref[pl.ds(c0, cc), :]).astype(BF16)
            v = v_ref[0, pl.ds(c0, cc), :]
            sc = lax.dot_general(q, kr_all[n], (((1,), (1,)), ((), ())),
                                 preferred_element_type=F32) * inner_decay
            ret = jnp.dot(sc.astype(BF16), v, preferred_element_type=F32)
            ret = ret + xi_f * jnp.dot(q, sf_ref[...].astype(BF16), preferred_element_type=F32)
            ret = ret + xi_b * jnp.dot(q, sb_all[n], preferred_element_type=F32)
            upd = lax.dot_general(kz_all[n], v, (((0,), (0,)), ((), ())), preferred_element_type=F32)
            sf_ref[...] = sf_ref[...] * chunk_decay + upd
            ret = ret * lax.rsqrt(jnp.mean(ret * ret, -1, keepdims=True) + RMS_EPS)
            gate = g_ref[0, pl.ds(c0, cc), :].astype(F32)
            o_ref[0, pl.ds(c0, cc), :] = (ret * (gate / (1.0 + jnp.exp(-gate)))).astype(o_ref.dtype)
            return carry

        lax.fori_loop(0, chunks_per_super, body, 0)


def _retention(proj3, b0, bsz, seq_len, col0, nh, dh, log_gamma, cos, sin):
    sup_len = min(RET_SUPER, seq_len)
    n_super = seq_len // sup_len
    cps = sup_len // RET_CHUNK
    cb0 = col0 // dh

    def fwd_idx(s):
        return jnp.maximum(s - n_super, 0)

    def kv_idx(s):
        return jnp.where(s < n_super, n_super - 1 - s, s - n_super)

    def k_idx(s):
        return jnp.maximum(n_super - 1 - s, 0)

    n_chunks = seq_len // RET_CHUNK
    return pl.pallas_call(
        functools.partial(_ret_kernel, n_super=n_super, chunks_per_super=cps, k_scale=dh ** -0.5),
        grid=(bsz, nh, 2 * n_super),
        in_specs=[pl.BlockSpec(memory_space=pltpu.SMEM),
                  pl.BlockSpec((1, sup_len, dh), lambda b, h, s: (b0 + b, fwd_idx(s), cb0 + h)),
                  pl.BlockSpec((1, sup_len, dh), lambda b, h, s: (b0 + b, k_idx(s), cb0 + nh + h)),
                  pl.BlockSpec((1, sup_len, dh), lambda b, h, s: (b0 + b, kv_idx(s), cb0 + 2 * nh + h)),
                  pl.BlockSpec((1, sup_len, dh), lambda b, h, s: (b0 + b, fwd_idx(s), cb0 + 3 * nh + h)),
                  pl.BlockSpec((sup_len, dh // 2), lambda b, h, s: (kv_idx(s), 0)),
                  pl.BlockSpec((sup_len, dh // 2), lambda b, h, s: (kv_idx(s), 0))],
        out_specs=pl.BlockSpec((1, sup_len, dh), lambda b, h, s: (b, fwd_idx(s), h)),
        out_shape=jax.ShapeDtypeStruct((bsz, seq_len, nh * dh), BF16),
        scratch_shapes=[pltpu.VMEM((n_chunks, dh, dh), BF16),
                        pltpu.VMEM((n_chunks, RET_CHUNK, dh), BF16),
                        pltpu.VMEM((n_chunks, RET_CHUNK, dh), BF16),
                        pltpu.VMEM((dh, dh), F32),
                        pltpu.VMEM((dh, dh), F32)],
        compiler_params=_cparams(("parallel", "parallel", "arbitrary")),
        name="retention",
    )(log_gamma, proj3, proj3, proj3, proj3, cos, sin)


def _assemble_kernel(a1_ref, a2_ref, z1_ref, z2_ref, c1_ref, c2_ref, ga_ref, gb_ref, o_ref, *, nb1):
    first = pl.program_id(0) < nb1
    wa = a1_ref.shape[-1]
    wb = z1_ref.shape[-1]
    a = jnp.where(first, a1_ref[...], a2_ref[...]).astype(F32)
    a = a * lax.rsqrt(jnp.mean(a * a, -1, keepdims=True) + RMS_EPS) * ga_ref[...]
    z = jnp.where(first, z1_ref[...], z2_ref[...]).astype(F32)
    z = z * lax.rsqrt(jnp.mean(z * z, -1, keepdims=True) + RMS_EPS) * gb_ref[...]
    o_ref[:, :wa] = a.astype(o_ref.dtype)
    o_ref[:, wa:wa + wb] = z.astype(o_ref.dtype)
    o_ref[:, wa + wb:] = jnp.where(first, c1_ref[...], c2_ref[...])


def _assemble(a_parts, z_parts, c_parts, gain_a, gain_b, bm=512):
    m1, wa = a_parts[0].shape
    m = m1 + a_parts[1].shape[0]
    wb, wc = z_parts[0].shape[1], c_parts[0].shape[1]
    nb1 = m1 // bm
    part_specs = []
    for w in (wa, wb, wc):
        part_specs += [pl.BlockSpec((bm, w), _first_rows(nb1)), pl.BlockSpec((bm, w), _second_rows(nb1))]
    return pl.pallas_call(
        functools.partial(_assemble_kernel, nb1=nb1),
        grid=(m // bm,),
        in_specs=part_specs + [pl.BlockSpec((1, wa), lambda i: (0, 0)),
                               pl.BlockSpec((1, wb), lambda i: (0, 0))],
        out_specs=pl.BlockSpec((bm, wa + wb + wc), lambda i: (i, 0)),
        out_shape=jax.ShapeDtypeStruct((m, wa + wb + wc), BF16),
        compiler_params=_cparams(("arbitrary",)),
        name="assemble_mixers",
    )(*a_parts, *z_parts, *c_parts, gain_a.astype(F32).reshape(1, wa), gain_b.astype(F32).reshape(1, wb))


def _token_mixers(proj, requests, p, l, dims):
    nh_a, hd_a, ch, order, nh_r, dh_r = dims
    in_cols = proj.shape[1]
    total = proj.shape[0]
    wa = nh_a * hd_a
    hy0 = 3 * wa
    ret0 = hy0 + (order + 1) * ch
    log_gamma = jnp.log1p(-jnp.exp2(-p["ret_decay_exp"][l].astype(F32)))
    a_parts, z_parts, c_parts = [], [], []
    bias = _na_bias_tables(p["na_rpb"][l])
    for (bsz, seq_len, tok0) in requests:
        assert tok0 % seq_len == 0 and total % seq_len == 0
        proj3 = proj.reshape(total // seq_len, seq_len, in_cols)
        b0 = tok0 // seq_len
        a = _neighbourhood_attention(proj3, b0, bsz, seq_len, bias, nh_a, hd_a)
        a_parts.append(a.reshape(bsz * seq_len, wa))
        tables = _dft_tables(seq_len)
        f = _hyena_filter_signal(seq_len, p["hy_f_w1"][l], p["hy_f_b1"][l], p["hy_f_freq"][l],
                                 p["hy_f_w2"][l], p["hy_f_b2"][l], p["hy_f_w3"][l], p["hy_f_b3"][l],
                                 order, ch)
        kf = _filter_spectrum(f, tables)
        hy = _short_conv3(proj3, b0, bsz, seq_len, hy0, p["hy_conv_w"][l].astype(F32),
                          p["hy_conv_b"][l].astype(F32))
        z = _hyena(hy, tables, kf, p["hy_skip"][l], order, ch)
        z_parts.append(z.reshape(bsz * seq_len, ch))
        half = dh_r // 2
        inv_freq = 1.0 / (ROPE_BASE ** jnp.linspace(0.0, 1.0, half, dtype=F32))
        ang = jnp.arange(seq_len, dtype=F32)[:, None] * inv_freq[None, :]
        c = _retention(proj3, b0, bsz, seq_len, ret0, nh_r, dh_r, log_gamma, jnp.cos(ang), jnp.sin(ang))
        c_parts.append(c.reshape(bsz * seq_len, nh_r * dh_r))
    return _assemble(a_parts, z_parts, c_parts, p["grp_gain_a"][l], p["grp_gain_b"][l])


def _pick(n, candidates):
    for c in candidates:
        if n % c == 0:
            return c
    raise ValueError(f"no block size for {n}")


def kernel(x_prompt, x_sample, ln_in_g, ln_in_b, w_in, na_rpb, hy_conv_w, hy_conv_b, hy_f_w1, hy_f_b1,
           hy_f_freq, hy_f_w2, hy_f_b2, hy_f_w3, hy_f_b3, hy_skip, ret_decay_exp, grp_gain_a, grp_gain_b,
           w_out, ln1_g, ln1_b, w_ffn_in, w_ffn_out, ln2_g, ln2_b):
    p = dict(na_rpb=na_rpb, hy_conv_w=hy_conv_w, hy_conv_b=hy_conv_b, hy_f_w1=hy_f_w1, hy_f_b1=hy_f_b1,
             hy_f_freq=hy_f_freq, hy_f_w2=hy_f_w2, hy_f_b2=hy_f_b2, hy_f_w3=hy_f_w3, hy_f_b3=hy_f_b3,
             hy_skip=hy_skip, ret_decay_exp=ret_decay_exp, grp_gain_a=grp_gain_a, grp_gain_b=grp_gain_b)
    depth, d_model, in_cols = w_in.shape
    nh_a = na_rpb.shape[1]
    wa = grp_gain_a.shape[1]
    ch = grp_gain_b.shape[1]
    order = hy_skip.shape[1]
    nh_r = ret_decay_exp.shape[2]
    wr = (in_cols - 3 * wa - (order + 1) * ch) // 4
    dims = (nh_a, wa // nh_a, ch, order, nh_r, wr // nh_r)
    d_ff = w_ffn_out.shape[1]
    alpha = (2.0 * depth) ** 0.25

    bp, lp, _ = x_prompt.shape
    bs, ls, _ = x_sample.shape
    requests = [(bp, lp, 0), (bs, ls, bp * lp)]
    m1 = bp * lp
    m = m1 + bs * ls
    bm = _pick(m, (1024, 512, 256))
    bm_ln = 256
    assert m1 % bm_ln == 0 and m % bm_ln == 0 and m1 % 512 == 0

    xf, xb = _layernorm_join(x_prompt.reshape(m1, d_model), x_sample.reshape(bs * ls, d_model),
                             ln_in_g.astype(F32), ln_in_b.astype(F32), bm_ln)
    w_in_b, w_out_b = w_in.astype(BF16), w_out.astype(BF16)
    w_ffn_in_b, w_ffn_out_b = w_ffn_in.astype(BF16), w_ffn_out.astype(BF16)
    resid = (xf,)
    for l in range(depth):
        proj = _matmul(xb, w_in_b, l, bm, _pick(in_cols, (1024, 512, 256)))
        mixed = _token_mixers(proj, requests, p, l, dims)
        bn_out = _pick(d_model, (1024, 512, 256) if len(resid) == 1 else (512, 256))
        y = _matmul_residual(mixed, w_out_b, l, resid, alpha, bm, bn_out)
        g1, b1 = ln1_g[l].astype(F32), ln1_b[l].astype(F32)
        xb, stats = _layernorm_stats(y, g1, b1, bm_ln)
        hmid = _matmul_swiglu(xb, w_ffn_in_b, l, bm, _pick(d_ff, (512, 256, 128)))
        y = _matmul_residual(hmid, w_ffn_out_b, l, (y, stats, g1, b1), alpha, _pick(m, (512, 256)),
                             _pick(d_model, (512, 256)))
        if l + 1 < depth:
            g2, b2 = ln2_g[l].astype(F32), ln2_b[l].astype(F32)
            xb, stats = _layernorm_stats(y, g2, b2, bm_ln)
            resid = (y, stats, g2, b2)
    o1, o2 = _layernorm_split(y, ln2_g[depth - 1].astype(F32), ln2_b[depth - 1].astype(F32), m1, bm_ln)
    return (o1.reshape(bp, lp, d_model), o2.reshape(bs, ls, d_model))
```

```python
import functools
import math

import numpy as np
import jax
import jax.numpy as jnp
from jax import lax
from jax.experimental import pallas as pl
from jax.experimental.pallas import tpu as pltpu

F32 = jnp.float32
BF16 = jnp.bfloat16

GRID_W = 64
WIN_R = 8
WIN_C = 16
ROPE_BASE = 10000.0
LN_EPS = 1e-5
RMS_EPS = 1e-6
NEG_INF = -1e30
HY_TARGET = 1e-2
HY_FAST_DECAY = 0.3
HY_SLOW_DECAY = 1.5
HY_MIN_DECAY = math.log(HY_TARGET) / HY_SLOW_DECAY
HY_MAX_DECAY = math.log(HY_TARGET) / HY_FAST_DECAY

V7X_VMEM_LIMIT = 56 * 1024 * 1024
LANES = 128
SUBLANES = 8
DFT_N2 = 128
DFT_GROUP = 32
NA_ROWS_PER_STEP = 8
RET_CHUNK = 256
RET_SUPER = 2048


def _cparams(sem, vmem=V7X_VMEM_LIMIT):
    return pltpu.CompilerParams(dimension_semantics=sem, vmem_limit_bytes=vmem)


def _pitch(rows):
    p = -(-rows // SUBLANES)
    return SUBLANES * (p + 1 - p % 2)


def _ln(x, g, b):
    mu = jnp.mean(x, -1, keepdims=True)
    xc = x - mu
    var = jnp.mean(xc * xc, -1, keepdims=True)
    return xc * lax.rsqrt(var + LN_EPS) * g + b


def _ln_stats_kernel(x_ref, g_ref, b_ref, ob_ref, st_ref):
    x = x_ref[...]
    mu = jnp.mean(x, -1, keepdims=True)
    xc = x - mu
    rstd = lax.rsqrt(jnp.mean(xc * xc, -1, keepdims=True) + LN_EPS)
    ob_ref[...] = (xc * rstd * g_ref[...] + b_ref[...]).astype(BF16)
    st_ref[:, :LANES] = jnp.broadcast_to(mu, (x.shape[0], LANES))
    st_ref[:, LANES:] = jnp.broadcast_to(rstd, (x.shape[0], LANES))


def _ln_join_kernel(x1_ref, x2_ref, g_ref, b_ref, of_ref, ob_ref, *, nb1):
    x = jnp.where(pl.program_id(0) < nb1, x1_ref[...], x2_ref[...])
    y = _ln(x, g_ref[...], b_ref[...])
    of_ref[...] = y
    ob_ref[...] = y.astype(BF16)


def _ln_split_kernel(x_ref, g_ref, b_ref, o1_ref, o2_ref, *, nb1):
    y = _ln(x_ref[...], g_ref[...], b_ref[...])

    @pl.when(pl.program_id(0) < nb1)
    def _():
        o1_ref[...] = y

    @pl.when(pl.program_id(0) >= nb1)
    def _():
        o2_ref[...] = y


def _first_rows(nb1):
    return lambda i: (jnp.minimum(i, nb1 - 1), 0)


def _second_rows(nb1):
    return lambda i: (jnp.maximum(i - nb1, 0), 0)


def _layernorm_join(x1, x2, g, b, bm=256):
    m1, d = x1.shape
    m = m1 + x2.shape[0]
    nb1 = m1 // bm
    return pl.pallas_call(
        functools.partial(_ln_join_kernel, nb1=nb1),
        grid=(m // bm,),
        in_specs=[pl.BlockSpec((bm, d), _first_rows(nb1)),
                  pl.BlockSpec((bm, d), _second_rows(nb1)),
                  pl.BlockSpec((1, d), lambda i: (0, 0)),
                  pl.BlockSpec((1, d), lambda i: (0, 0))],
        out_specs=[pl.BlockSpec((bm, d), lambda i: (i, 0)),
                   pl.BlockSpec((bm, d), lambda i: (i, 0))],
        out_shape=[jax.ShapeDtypeStruct((m, d), F32), jax.ShapeDtypeStruct((m, d), BF16)],
        compiler_params=_cparams(("arbitrary",)),
        name="layernorm_join",
    )(x1, x2, g.reshape(1, d), b.reshape(1, d))


def _layernorm_split(x, g, b, m1, bm=256):
    m, d = x.shape
    nb1 = m1 // bm
    return pl.pallas_call(
        functools.partial(_ln_split_kernel, nb1=nb1),
        grid=(m // bm,),
        in_specs=[pl.BlockSpec((bm, d), lambda i: (i, 0)),
                  pl.BlockSpec((1, d), lambda i: (0, 0)),
                  pl.BlockSpec((1, d), lambda i: (0, 0))],
        out_specs=[pl.BlockSpec((bm, d), _first_rows(nb1)),
                   pl.BlockSpec((bm, d), _second_rows(nb1))],
        out_shape=[jax.ShapeDtypeStruct((m1, d), F32), jax.ShapeDtypeStruct((m - m1, d), F32)],
        compiler_params=_cparams(("arbitrary",)),
        name="layernorm_split",
    )(x, g.reshape(1, d), b.reshape(1, d))


def _layernorm_stats(x, g, b, bm=256):
    m, d = x.shape
    return pl.pallas_call(
        _ln_stats_kernel,
        grid=(m // bm,),
        in_specs=[pl.BlockSpec((bm, d), lambda i: (i, 0)),
                  pl.BlockSpec((1, d), lambda i: (0, 0)),
                  pl.BlockSpec((1, d), lambda i: (0, 0))],
        out_specs=[pl.BlockSpec((bm, d), lambda i: (i, 0)),
                   pl.BlockSpec((bm, 2 * LANES), lambda i: (i, 0))],
        out_shape=[jax.ShapeDtypeStruct((m, d), BF16), jax.ShapeDtypeStruct((m, 2 * LANES), F32)],
        compiler_params=_cparams(("parallel",)),
        name="layernorm_stats",
    )(x, g.reshape(1, d), b.reshape(1, d))


def _mm_kernel(x_ref, w_ref, o_ref):
    o_ref[...] = jnp.dot(x_ref[...], w_ref[...], preferred_element_type=F32).astype(o_ref.dtype)


def _mm_res_kernel(x_ref, w_ref, r_ref, o_ref, *, alpha):
    o_ref[...] = alpha * r_ref[...] + jnp.dot(x_ref[...], w_ref[...], preferred_element_type=F32)


def _mm_res_ln_kernel(x_ref, w_ref, y_ref, st_ref, g_ref, b_ref, o_ref, *, alpha):
    acc = jnp.dot(x_ref[...], w_ref[...], preferred_element_type=F32)
    mu = st_ref[:, :LANES]
    rstd = st_ref[:, LANES:]
    for c in range(y_ref.shape[1] // LANES):
        cols = slice(c * LANES, (c + 1) * LANES)
        r = (y_ref[:, cols] - mu) * rstd * g_ref[:, cols] + b_ref[:, cols]
        o_ref[:, cols] = alpha * r + acc[:, cols]


def _mm_swiglu_kernel(x_ref, wg_ref, wu_ref, o_ref):
    x = x_ref[...]
    g = jnp.dot(x, wg_ref[...], preferred_element_type=F32)
    u = jnp.dot(x, wu_ref[...], preferred_element_type=F32)
    o_ref[...] = (g / (1.0 + jnp.exp(-g)) * u).astype(o_ref.dtype)


def _matmul(x, w, layer, bm, bn, out_dtype=BF16):
    m, k = x.shape
    n = w.shape[2]
    return pl.pallas_call(
        _mm_kernel,
        grid=(m // bm, n // bn),
        in_specs=[pl.BlockSpec((bm, k), lambda i, j: (i, 0)),
                  pl.BlockSpec((None, k, bn), lambda i, j: (layer, 0, j))],
        out_specs=pl.BlockSpec((bm, bn), lambda i, j: (i, j)),
        out_shape=jax.ShapeDtypeStruct((m, n), out_dtype),
        compiler_params=_cparams(("parallel", "arbitrary")),
        name="matmul",
    )(x, w)


def _matmul_residual(x, w, layer, resid, alpha, bm, bn):
    m, k = x.shape
    n = w.shape[2]
    tile = pl.BlockSpec((bm, bn), lambda i, j: (i, j))
    if len(resid) == 1:
        body, resid_specs = _mm_res_kernel, [tile]
    else:
        body = _mm_res_ln_kernel
        resid = (resid[0], resid[1], resid[2].reshape(1, n), resid[3].reshape(1, n))
        resid_specs = [tile, pl.BlockSpec((bm, 2 * LANES), lambda i, j: (i, 0)),
                       pl.BlockSpec((1, bn), lambda i, j: (0, j)), pl.BlockSpec((1, bn), lambda i, j: (0, j))]
    return pl.pallas_call(
        functools.partial(body, alpha=alpha),
        grid=(m // bm, n // bn),
        in_specs=[pl.BlockSpec((bm, k), lambda i, j: (i, 0)),
                  pl.BlockSpec((None, k, bn), lambda i, j: (layer, 0, j))] + resid_specs,
        out_specs=tile,
        out_shape=jax.ShapeDtypeStruct((m, n), F32),
        compiler_params=_cparams(("parallel", "arbitrary")),
        name="matmul_residual",
    )(x, w, *resid)


def _matmul_swiglu(x, w, layer, bm, bn):
    m, k = x.shape
    f = w.shape[2] // 2
    nb = f // bn
    return pl.pallas_call(
        _mm_swiglu_kernel,
        grid=(m // bm, nb),
        in_specs=[pl.BlockSpec((bm, k), lambda i, j: (i, 0)),
                  pl.BlockSpec((None, k, bn), lambda i, j: (layer, 0, j)),
                  pl.BlockSpec((None, k, bn), lambda i, j: (layer, 0, j + nb))],
        out_specs=pl.BlockSpec((bm, bn), lambda i, j: (i, j)),
        out_shape=jax.ShapeDtypeStruct((m, f), BF16),
        compiler_params=_cparams(("parallel", "arbitrary")),
        name="matmul_swiglu",
    )(x, w, w)


def _na_bias_tables(rpb):
    nh = rpb.shape[0]
    qc = np.arange(GRID_W)[:, None]
    kc = np.arange(GRID_W)[None, :]
    win_start = np.clip(qc - WIN_C // 2, 0, GRID_W - WIN_C)
    col_ok = (kc >= win_start) & (kc < win_start + WIN_C)
    dc_idx = np.clip(kc - qc + (WIN_C - 1), 0, 2 * WIN_C - 2)
    onehot = (dc_idx.reshape(-1)[None, :] == np.arange(2 * WIN_C - 1)[:, None]).astype(np.float32)
    tiles = jnp.einsum("hdc,cq->hdq", rpb.astype(F32), jnp.asarray(onehot),
                       precision=lax.Precision.HIGHEST).reshape(nh, 2 * WIN_R - 1, GRID_W, GRID_W)
    tiles = jnp.where(jnp.asarray(col_ok)[None, None], tiles, NEG_INF)
    dr = np.arange(WIN_R)[:, None] + np.arange(WIN_R)[None, :]
    t = tiles[:, dr]
    return t.transpose(0, 1, 3, 2, 4).reshape(nh, WIN_R, GRID_W, WIN_R * GRID_W)


def _na_kernel(q_ref, k_ref, v_ref, bias_ref, o_ref, *, rows, scale):
    g = pl.program_id(2)
    nk = WIN_R * GRID_W
    ks, vs, bias = [], [], []
    for i in range(NA_ROWS_PER_STEP):
        r = g * NA_ROWS_PER_STEP + i
        row_start = jnp.clip(r - WIN_R // 2, 0, rows - WIN_R)
        k0 = pl.multiple_of(row_start * GRID_W, GRID_W)
        ks.append(k_ref[0, pl.ds(k0, nk), :])
        vs.append(v_ref[0, pl.ds(k0, nk), :])
        bias.append(bias_ref[row_start - r + (WIN_R - 1)])
    q = q_ref[0].reshape(NA_ROWS_PER_STEP, GRID_W, q_ref.shape[-1])
    s = jnp.einsum("rqd,rkd->rqk", q, jnp.stack(ks), preferred_element_type=F32)
    s = s * scale + jnp.stack(bias)
    m = jnp.max(s, -1, keepdims=True)
    p = jnp.exp(s - m)
    l = jnp.sum(p, -1, keepdims=True)
    o = jnp.einsum("rqk,rkd->rqd", p.astype(BF16), jnp.stack(vs), preferred_element_type=F32) / l
    o_ref[0] = o.reshape(o_ref.shape[1:]).astype(o_ref.dtype)


def _neighbourhood_attention(proj3, b0, bsz, seq_len, bias, nh, hd):
    rows = seq_len // GRID_W
    assert rows % NA_ROWS_PER_STEP == 0 and rows >= WIN_R
    nq = NA_ROWS_PER_STEP * GRID_W
    return pl.pallas_call(
        functools.partial(_na_kernel, rows=rows, scale=hd ** -0.5),
        grid=(nh, bsz, rows // NA_ROWS_PER_STEP),
        in_specs=[pl.BlockSpec((1, nq, hd), lambda h, b, g: (b0 + b, g, h)),
                  pl.BlockSpec((1, seq_len, hd), lambda h, b, g: (b0 + b, 0, nh + h)),
                  pl.BlockSpec((1, seq_len, hd), lambda h, b, g: (b0 + b, 0, 2 * nh + h)),
                  pl.BlockSpec((None, WIN_R, GRID_W, WIN_R * GRID_W), lambda h, b, g: (h, 0, 0, 0))],
        out_specs=pl.BlockSpec((1, nq, hd), lambda h, b, g: (b, g, h)),
        out_shape=jax.ShapeDtypeStruct((bsz, seq_len, nh * hd), BF16),
        compiler_params=_cparams(("parallel", "parallel", "arbitrary")),
        name="neighbourhood_attention",
    )(proj3, proj3, proj3, bias)


def _conv3_kernel(x_ref, w_ref, b_ref, o_ref, *, seq_len, rb):
    w = w_ref[...]
    bias = b_ref[...]
    n_chunks = seq_len // rb

    def body(c, carry):
        r0 = pl.multiple_of(c * rb, rb)
        x = x_ref[0, pl.ds(r0, rb), :].astype(F32)
        row = lax.broadcasted_iota(jnp.int32, x.shape, 0)
        p0 = pl.multiple_of(jnp.maximum(r0 - 16, 0), 16)
        n0 = pl.multiple_of(jnp.minimum(r0 + rb, seq_len - 16), 16)
        prev_row = x_ref[0, pl.ds(p0, 16), :].astype(F32)[15:16]
        next_row = x_ref[0, pl.ds(n0, 16), :].astype(F32)[0:1]
        prev_row = jnp.where(c == 0, 0.0, prev_row)
        next_row = jnp.where(c == n_chunks - 1, 0.0, next_row)
        up = jnp.where(row == 0, prev_row, pltpu.roll(x, 1, 0))
        down = jnp.where(row == rb - 1, next_row, pltpu.roll(x, rb - 1, 0))
        y = up * w[0:1] + x * w[1:2] + down * w[2:3] + bias
        o_ref[0, pl.ds(r0, rb), :] = y.astype(o_ref.dtype)
        return carry

    lax.fori_loop(0, n_chunks, body, 0)


def _short_conv3(proj3, b0, bsz, seq_len, col0, w, b, cb=256, rb=512):
    width = w.shape[1]
    rb = min(rb, seq_len)
    c0 = col0 // cb
    return pl.pallas_call(
        functools.partial(_conv3_kernel, seq_len=seq_len, rb=rb),
        grid=(bsz, width // cb),
        in_specs=[pl.BlockSpec((1, seq_len, cb), lambda bi, c: (b0 + bi, 0, c0 + c)),
                  pl.BlockSpec((3, cb), lambda bi, c: (0, c)),
                  pl.BlockSpec((1, cb), lambda bi, c: (0, c))],
        out_specs=pl.BlockSpec((1, seq_len, cb), lambda bi, c: (bi, 0, c)),
        out_shape=jax.ShapeDtypeStruct((bsz, seq_len, width), BF16),
        compiler_params=_cparams(("parallel", "parallel")),
        name="hyena_short_conv",
    )(proj3, w, b.reshape(1, width))


def _filter_kernel(z_ref, w1_ref, b1_ref, fr_ref, w2_ref, b2_ref, w3_ref, b3_ref, dl_ref, o_ref, *,
                   seq_len, rb):
    i = pl.program_id(0)
    hi = lax.Precision.HIGHEST
    z = z_ref[...]
    h = jnp.sin(fr_ref[0:1] * (jnp.dot(z, w1_ref[...], precision=hi, preferred_element_type=F32)
                               + b1_ref[...]))
    h = jnp.sin(fr_ref[1:2] * (jnp.dot(h, w2_ref[...], precision=hi, preferred_element_type=F32)
                               + b2_ref[...]))
    ch = dl_ref.shape[-1]
    n = i * rb + lax.broadcasted_iota(jnp.int32, (rb, ch), 0)
    sign = jnp.where(n < seq_len, 1.0, jnp.where(n == seq_len, 0.0, -1.0))
    window = jnp.exp(-z[:, 0:1] * dl_ref[...]) * sign
    hb = h.astype(BF16)
    for o in range(w3_ref.shape[0]):
        f = jnp.dot(hb, w3_ref[o].astype(BF16), preferred_element_type=F32) + b3_ref[o]
        o_ref[:, o * ch:(o + 1) * ch] = (f * window).astype(o_ref.dtype)


def _hyena_filter_signal(seq_len, w1, b1, freq, w2, b2, w3, b3, order, ch, rb=512):
    n = 2 * seq_len
    emb, hid = w1.shape
    pad = LANES
    t = jnp.linspace(0.0, 1.0, seq_len, dtype=F32)[:, None]
    bands = (emb - 1) // 2
    fr = jnp.linspace(1e-4, bands - 1, bands, dtype=F32)[None, :]
    wpos = 2.0 * math.pi * jnp.arange(seq_len, dtype=F32)[:, None] / seq_len
    z = jnp.concatenate([t, jnp.cos(fr * wpos), -jnp.sin(fr * wpos)], axis=-1)
    pos = np.arange(n)
    src = np.clip(np.where(pos < seq_len, pos, n - pos), 0, seq_len - 1)
    z2 = jnp.pad(z[src], ((0, 0), (0, pad - emb)))
    w1p = jnp.pad(w1.astype(F32), ((0, pad - emb), (0, pad - hid)))
    b1p = jnp.pad(b1.astype(F32), (0, pad - hid)).reshape(1, pad)
    frp = jnp.pad(freq.astype(F32), ((0, 0), (0, pad - hid)), constant_values=1.0)
    w2p = jnp.pad(w2.astype(F32), ((0, pad - hid), (0, pad - hid)))
    b2p = jnp.pad(b2.astype(F32), (0, pad - hid)).reshape(1, pad)
    w3p = jnp.pad(w3.astype(F32), ((0, pad - hid), (0, 0))).reshape(pad, order, 2, ch).transpose(2, 1, 0, 3)
    b3p = b3.astype(F32).reshape(order, 2, 1, ch).transpose(1, 0, 2, 3)
    deltas = jnp.abs(jnp.linspace(HY_MIN_DECAY, HY_MAX_DECAY, ch, dtype=F32)).reshape(1, ch)
    nb = n // rb
    half = nb // 2
    full = lambda i: (0, 0)
    return pl.pallas_call(
        functools.partial(_filter_kernel, seq_len=seq_len, rb=rb),
        grid=(nb,),
        in_specs=[pl.BlockSpec((rb, pad), lambda i: (i, 0)),
                  pl.BlockSpec((pad, pad), full), pl.BlockSpec((1, pad), full),
                  pl.BlockSpec((2, pad), full),
                  pl.BlockSpec((pad, pad), full), pl.BlockSpec((1, pad), full),
                  pl.BlockSpec((None, order, pad, ch), lambda i: (i // half, 0, 0, 0)),
                  pl.BlockSpec((None, order, 1, ch), lambda i: (i // half, 0, 0, 0)),
                  pl.BlockSpec((1, ch), full)],
        out_specs=pl.BlockSpec((rb, order * ch), lambda i: (i, 0)),
        out_shape=jax.ShapeDtypeStruct((n, order * ch), BF16),
        compiler_params=_cparams(("parallel",)),
        name="hyena_filter_mlp",
    )(z2, w1p, b1p, frp, w2p, b2p, w3p, b3p, deltas)


def _dft_tables(seq_len):
    n = 2 * seq_len
    n2 = DFT_N2
    n1 = n // n2
    hk = n2 // 2
    kg = min(n1, DFT_GROUP)
    ng = n1 // kg
    pi = math.pi
    k1 = jnp.arange(n1, dtype=jnp.int32)
    ph1 = ((2 * k1[:, None] + 1) * k1[None, :]) % (2 * n1)
    th1 = ph1.astype(F32) * (pi / n1)
    c1 = jnp.cos(th1).reshape(ng, kg, n1)
    s1 = jnp.sin(th1).reshape(ng, kg, n1)
    f1 = jnp.concatenate([c1, -s1], axis=1)
    fb = jnp.concatenate([c1, -s1], axis=1).transpose(0, 2, 1)[:, : n1 // 2] * (2.0 / n)
    k = k1[:, None, None] + n1 * jnp.arange(hk, dtype=jnp.int32)[None, :, None]
    ph2 = ((2 * k + 1) * jnp.arange(n2, dtype=jnp.int32)[None, None, :]) % (2 * n)
    th2 = ph2.astype(F32) * (pi / n)
    c2, s2 = jnp.cos(th2), jnp.sin(th2)
    g = jnp.concatenate([jnp.concatenate([c2, s2], axis=2),
                         jnp.concatenate([-s2, c2], axis=2)], axis=1)
    c2t, s2t = c2.transpose(0, 2, 1), s2.transpose(0, 2, 1)
    h = jnp.concatenate([jnp.concatenate([c2t, -s2t], axis=2),
                         jnp.concatenate([s2t, c2t], axis=2)], axis=1)
    nl = 2 if n1 <= DFT_GROUP else 1
    return dict(n1=n1, kg=kg, ng=ng, nl=nl, f1_full=f1.astype(BF16), f1_half=f1[:, :, : n1 // 2].astype(BF16),
                fb=fb.astype(BF16), g=g.astype(BF16), h=h.astype(BF16))


def _load_lanes(ref, rows):
    parts = [ref[s, rows, :] for s in range(ref.shape[0])]
    return parts[0] if len(parts) == 1 else jnp.concatenate(parts, axis=1)


def _store_lanes(ref, rows, val):
    for s in range(ref.shape[0]):
        ref[s, rows, :] = val[:, s * LANES:(s + 1) * LANES]


def _fill_slabs(x_ref, xs, slabs, pitch):
    def body(n1, carry):
        src = pl.ds(pl.multiple_of(n1 * DFT_N2, DFT_N2), DFT_N2)
        _store_lanes(xs, pl.ds(pl.multiple_of(n1 * pitch, SUBLANES), DFT_N2), x_ref[src, :].astype(F32))
        return carry
    lax.fori_loop(0, slabs, body, 0, unroll=4)


def _slab_stage(xs, a_s, f1, slabs, p_x, p_a):
    rows = f1.shape[0]

    def body(i, carry):
        xn = _load_lanes(xs, pl.ds(i, slabs, stride=p_x)).astype(BF16)
        _store_lanes(a_s, pl.ds(pl.multiple_of(i * p_a, SUBLANES), rows),
                     jnp.dot(f1, xn, preferred_element_type=F32))
        return carry
    lax.fori_loop(0, DFT_N2, body, 0, unroll=32)


def _load_slab_freq(a_s, j, kg, p_a):
    are = _load_lanes(a_s, pl.ds(j, DFT_N2, stride=p_a))
    aim = _load_lanes(a_s, pl.ds(kg + j, DFT_N2, stride=p_a))
    return jnp.concatenate([are, aim], axis=0).astype(BF16)


def _filter_spec_kernel(x_ref, f1_ref, g_ref, o_ref, xs, a_s, *, slabs, kg):
    grp = pl.program_id(1)
    p_x, p_a = _pitch(DFT_N2), _pitch(2 * kg)

    @pl.when(grp == 0)
    def _():
        _fill_slabs(x_ref, xs, slabs, p_x)

    _slab_stage(xs, a_s, f1_ref[0], slabs, p_x, p_a)

    def body(j, carry):
        o_ref[j] = jnp.dot(g_ref[j], _load_slab_freq(a_s, j, kg, p_a), preferred_element_type=F32)
        return carry
    lax.fori_loop(0, kg, body, 0, unroll=8)


def _filter_spectrum(f, tables):
    n, cols = f.shape
    n1, kg, ng, nl = tables["n1"], tables["kg"], tables["ng"], tables["nl"]
    p_x, p_a = _pitch(DFT_N2), _pitch(2 * kg)
    wl = nl * LANES
    return pl.pallas_call(
        functools.partial(_filter_spec_kernel, slabs=n1, kg=kg),
        grid=(cols // wl, ng),
        in_specs=[pl.BlockSpec((n, wl), lambda c, g: (0, c)),
                  pl.BlockSpec((1, 2 * kg, n1), lambda c, g: (g, 0, 0)),
                  pl.BlockSpec((kg, DFT_N2, 2 * DFT_N2), lambda c, g: (g, 0, 0))],
        out_specs=pl.BlockSpec((kg, DFT_N2, wl), lambda c, g: (g, 0, c)),
        out_shape=jax.ShapeDtypeStruct((n1, DFT_N2, cols), F32),
        scratch_shapes=[pltpu.VMEM((nl, n1 * p_x, LANES), F32),
                        pltpu.VMEM((nl, DFT_N2 * p_a, LANES), F32)],
        compiler_params=_cparams(("parallel", "arbitrary")),
        name="hyena_filter_spectrum",
    )(f, tables["f1_full"], tables["g"])


def _long_conv_kernel(x_ref, gate_ref, f1_ref, g_ref, h_ref, kf_ref, fb_ref, skip_ref, o_ref,
                      xs, a_s, z_s, y_s, *, slabs, kg, ng):
    grp = pl.program_id(2)
    n2 = DFT_N2
    hk = n2 // 2
    p_x, p_a, p_z, p_y = _pitch(n2), _pitch(2 * kg), _pitch(2 * n2), _pitch(slabs)

    @pl.when(grp == 0)
    def _():
        _fill_slabs(x_ref.at[0], xs, slabs, p_x)
        y_s[...] = jnp.zeros_like(y_s)

    _slab_stage(xs, a_s, f1_ref[0], slabs, p_x, p_a)

    def freq_body(j, carry):
        x = jnp.dot(g_ref[j], _load_slab_freq(a_s, j, kg, p_a), preferred_element_type=F32)
        kf = kf_ref[j]
        xr, xi = x[:hk], x[hk:]
        kr, ki = kf[:hk], kf[hk:]
        y = jnp.concatenate([xr * kr - xi * ki, xr * ki + xi * kr], axis=0)
        _store_lanes(z_s, pl.ds(pl.multiple_of(j * p_z, SUBLANES), n2), y)
        return carry
    lax.fori_loop(0, kg, freq_body, 0, unroll=8)

    def inv_freq_body(j, carry):
        y = _load_lanes(z_s, pl.ds(pl.multiple_of(j * p_z, SUBLANES), n2)).astype(BF16)
        _store_lanes(z_s, pl.ds(pl.multiple_of(j * p_z, SUBLANES), 2 * n2),
                     jnp.dot(h_ref[j], y, preferred_element_type=F32))
        return carry
    lax.fori_loop(0, kg, inv_freq_body, 0, unroll=8)

    fb = fb_ref[0]

    def inv_body(t, carry):
        zre = _load_lanes(z_s, pl.ds(t, kg, stride=p_z))
        zim = _load_lanes(z_s, pl.ds(n2 + t, kg, stride=p_z))
        zz = jnp.concatenate([zre, zim], axis=0).astype(BF16)
        dst = pl.ds(pl.multiple_of(t * p_y, SUBLANES), slabs)
        _store_lanes(y_s, dst, _load_lanes(y_s, dst) + jnp.dot(fb, zz, preferred_element_type=F32))
        return carry
    lax.fori_loop(0, n2, inv_body, 0, unroll=32)

    @pl.when(grp == ng - 1)
    def _():
        skip = skip_ref[...]

        def out_body(t1, carry):
            y = _load_lanes(y_s, pl.ds(t1, n2, stride=p_y))
            u = _load_lanes(xs, pl.ds(pl.multiple_of(t1 * p_x, SUBLANES), n2))
            rows = pl.ds(pl.multiple_of(t1 * n2, n2), n2)
            gate = gate_ref[0, rows, :].astype(F32)
            o_ref[0, rows, :] = ((y + skip * u) * gate).astype(o_ref.dtype)
            return carry
        lax.fori_loop(0, slabs, out_body, 0, unroll=4)


def _long_conv_gate(u, u_blk0, gate, gate_blk0, tables, kf, kf_blk0, skip, ch):
    bsz, seq_len, _ = u.shape
    n1, kg, ng, nl = tables["n1"], tables["kg"], tables["ng"], tables["nl"]
    slabs = n1 // 2
    p_x, p_a, p_z, p_y = _pitch(DFT_N2), _pitch(2 * kg), _pitch(2 * DFT_N2), _pitch(slabs)
    wl = nl * LANES
    return pl.pallas_call(
        functools.partial(_long_conv_kernel, slabs=slabs, kg=kg, ng=ng),
        grid=(bsz, ch // wl, ng),
        in_specs=[pl.BlockSpec((1, seq_len, wl), lambda b, c, g: (b, 0, u_blk0 + c)),
                  pl.BlockSpec((1, seq_len, wl), lambda b, c, g: (b, 0, gate_blk0 + c)),
                  pl.BlockSpec((1, 2 * kg, slabs), lambda b, c, g: (g, 0, 0)),
                  pl.BlockSpec((kg, DFT_N2, 2 * DFT_N2), lambda b, c, g: (g, 0, 0)),
                  pl.BlockSpec((kg, 2 * DFT_N2, DFT_N2), lambda b, c, g: (g, 0, 0)),
                  pl.BlockSpec((kg, DFT_N2, wl), lambda b, c, g: (g, 0, kf_blk0 + c)),
                  pl.BlockSpec((1, slabs, 2 * kg), lambda b, c, g: (g, 0, 0)),
                  pl.BlockSpec((1, wl), lambda b, c, g: (0, c))],
        out_specs=pl.BlockSpec((1, seq_len, wl), lambda b, c, g: (b, 0, c)),
        out_shape=jax.ShapeDtypeStruct((bsz, seq_len, ch), BF16),
        scratch_shapes=[pltpu.VMEM((nl, slabs * p_x, LANES), F32),
                        pltpu.VMEM((nl, DFT_N2 * p_a, LANES), F32),
                        pltpu.VMEM((nl, kg * p_z, LANES), F32),
                        pltpu.VMEM((nl, DFT_N2 * p_y, LANES), F32)],
        compiler_params=_cparams(("parallel", "parallel", "arbitrary")),
        name="hyena_long_conv",
    )(u, gate, tables["f1_half"], tables["g"], tables["h"], kf, tables["fb"],
      skip.astype(F32).reshape(1, ch))


def _hyena(hy, tables, kf, skip, order, ch):
    nblk = ch // (tables["nl"] * LANES)
    z, z_blk0 = hy, 0
    for o in range(order):
        z = _long_conv_gate(z, z_blk0, hy, (o + 1) * nblk, tables, kf, o * nblk, skip[o], ch)
    return z


def _rope(x, cos, sin):
    half = x.shape[-1] // 2
    x1, x2 = x[:, :half], x[:, half:]
    return jnp.concatenate([x1 * cos - x2 * sin, x1 * sin + x2 * cos], axis=-1)


def _ret_kernel(lg_ref, q_ref, k_ref, v_ref, g_ref, cos_ref, sin_ref, o_ref,
                sb_all, kr_all, kz_all, sf_ref, sb_ref, *, n_super, chunks_per_super, k_scale):
    h = pl.program_id(1)
    s = pl.program_id(2)
    cc = RET_CHUNK
    dh = q_ref.shape[-1]
    lgf = lg_ref[0, h]
    lgb = lg_ref[1, h]
    row = lax.broadcasted_iota(jnp.int32, (cc, dh), 0).astype(F32)
    zeta_f = jnp.exp(lgf * (cc - 1.0 - row))

    @pl.when(s < n_super)
    def _backward_sweep():
        @pl.when(s == 0)
        def _():
            sb_ref[...] = jnp.zeros_like(sb_ref)

        sup = n_super - 1 - s
        zeta_b = jnp.exp(lgb * row)
        chunk_decay = jnp.exp(lgb * jnp.full((1, dh), float(cc), F32))

        def body(t, carry):
            c = chunks_per_super - 1 - t
            c0 = pl.multiple_of(c * cc, cc)
            n = sup * chunks_per_super + c
            sb_all[n] = sb_ref[...].astype(BF16)
            k = _rope(k_ref[0, pl.ds(c0, cc), :].astype(F32), cos_ref[pl.ds(c0, cc), :],
                      sin_ref[pl.ds(c0, cc), :]) * k_scale
            kr_all[n] = k.astype(BF16)
            kz_all[n] = (k * zeta_f).astype(BF16)
            kz = (k * zeta_b).astype(BF16)
            v = v_ref[0, pl.ds(c0, cc), :]
            upd = lax.dot_general(kz, v, (((0,), (0,)), ((), ())), preferred_element_type=F32)
            sb_ref[...] = sb_ref[...] * chunk_decay + upd
            return carry

        lax.fori_loop(0, chunks_per_super, body, 0)

    @pl.when(s >= n_super)
    def _forward_sweep():
        @pl.when(s == n_super)
        def _():
            sf_ref[...] = jnp.zeros_like(sf_ref)

        sup = s - n_super
        col = lax.broadcasted_iota(jnp.int32, (cc, cc), 1).astype(F32)
        rowc = lax.broadcasted_iota(jnp.int32, (cc, cc), 0).astype(F32)
        diff = rowc - col
        inner_decay = jnp.where(diff >= 0, jnp.exp(lgf * jnp.maximum(diff, 0.0)),
                                jnp.exp(lgb * jnp.maximum(-diff, 0.0)))
        xi_f = jnp.exp(lgf * (row + 1.0))
        xi_b = jnp.exp(lgb * (cc - row))
        chunk_decay = jnp.exp(lgf * jnp.full((1, dh), float(cc), F32))

        def body(c, carry):
            c0 = pl.multiple_of(c * cc, cc)
            n = sup * chunks_per_super + c
            q = _rope(q_ref[0, pl.ds(c0, cc), :].astype(F32), cos_ref[pl.ds(c0, cc), :],
                      sin_ref[pl.ds(c0, cc), :]).astype(BF16)
            v = v_ref[0, pl.ds(c0, cc), :]
            sc = lax.dot_general(q, kr_all[n], (((1,), (1,)), ((), ())),
                                 preferred_element_type=F32) * inner_decay
            ret = jnp.dot(sc.astype(BF16), v, preferred_element_type=F32)
            ret = ret + xi_f * jnp.dot(q, sf_ref[...].astype(BF16), preferred_element_type=F32)
            ret = ret + xi_b * jnp.dot(q, sb_all[n], preferred_element_type=F32)
            upd = lax.dot_general(kz_all[n], v, (((0,), (0,)), ((), ())), preferred_element_type=F32)
            sf_ref[...] = sf_ref[...] * chunk_decay + upd
            ret = ret * lax.rsqrt(jnp.mean(ret * ret, -1, keepdims=True) + RMS_EPS)
            gate = g_ref[0, pl.ds(c0, cc), :].astype(F32)
            o_ref[0, pl.ds(c0, cc), :] = (ret * (gate / (1.0 + jnp.exp(-gate)))).astype(o_ref.dtype)
            return carry

        lax.fori_loop(0, chunks_per_super, body, 0)


def _retention(proj3, b0, bsz, seq_len, col0, nh, dh, log_gamma, cos, sin):
    sup_len = min(RET_SUPER, seq_len)
    n_super = seq_len // sup_len
    cps = sup_len // RET_CHUNK
    cb0 = col0 // dh

    def fwd_idx(s):
        return jnp.maximum(s - n_super, 0)

    def kv_idx(s):
        return jnp.where(s < n_super, n_super - 1 - s, s - n_super)

    def k_idx(s):
        return jnp.maximum(n_super - 1 - s, 0)

    n_chunks = seq_len // RET_CHUNK
    return pl.pallas_call(
        functools.partial(_ret_kernel, n_super=n_super, chunks_per_super=cps, k_scale=dh ** -0.5),
        grid=(bsz, nh, 2 * n_super),
        in_specs=[pl.BlockSpec(memory_space=pltpu.SMEM),
                  pl.BlockSpec((1, sup_len, dh), lambda b, h, s: (b0 + b, fwd_idx(s), cb0 + h)),
                  pl.BlockSpec((1, sup_len, dh), lambda b, h, s: (b0 + b, k_idx(s), cb0 + nh + h)),
                  pl.BlockSpec((1, sup_len, dh), lambda b, h, s: (b0 + b, kv_idx(s), cb0 + 2 * nh + h)),
                  pl.BlockSpec((1, sup_len, dh), lambda b, h, s: (b0 + b, fwd_idx(s), cb0 + 3 * nh + h)),
                  pl.BlockSpec((sup_len, dh // 2), lambda b, h, s: (kv_idx(s), 0)),
                  pl.BlockSpec((sup_len, dh // 2), lambda b, h, s: (kv_idx(s), 0))],
        out_specs=pl.BlockSpec((1, sup_len, dh), lambda b, h, s: (b, fwd_idx(s), h)),
        out_shape=jax.ShapeDtypeStruct((bsz, seq_len, nh * dh), BF16),
        scratch_shapes=[pltpu.VMEM((n_chunks, dh, dh), BF16),
                        pltpu.VMEM((n_chunks, RET_CHUNK, dh), BF16),
                        pltpu.VMEM((n_chunks, RET_CHUNK, dh), BF16),
                        pltpu.VMEM((dh, dh), F32),
                        pltpu.VMEM((dh, dh), F32)],
        compiler_params=_cparams(("parallel", "parallel", "arbitrary")),
        name="retention",
    )(log_gamma, proj3, proj3, proj3, proj3, cos, sin)


def _assemble_kernel(a1_ref, a2_ref, z1_ref, z2_ref, c1_ref, c2_ref, ga_ref, gb_ref, o_ref, *, nb1):
    first = pl.program_id(0) < nb1
    wa = a1_ref.shape[-1]
    wb = z1_ref.shape[-1]
    a = jnp.where(first, a1_ref[...], a2_ref[...]).astype(F32)
    a = a * lax.rsqrt(jnp.mean(a * a, -1, keepdims=True) + RMS_EPS) * ga_ref[...]
    z = jnp.where(first, z1_ref[...], z2_ref[...]).astype(F32)
    z = z * lax.rsqrt(jnp.mean(z * z, -1, keepdims=True) + RMS_EPS) * gb_ref[...]
    o_ref[:, :wa] = a.astype(o_ref.dtype)
    o_ref[:, wa:wa + wb] = z.astype(o_ref.dtype)
    o_ref[:, wa + wb:] = jnp.where(first, c1_ref[...], c2_ref[...])


def _assemble(a_parts, z_parts, c_parts, gain_a, gain_b, bm=512):
    m1, wa = a_parts[0].shape
    m = m1 + a_parts[1].shape[0]
    wb, wc = z_parts[0].shape[1], c_parts[0].shape[1]
    nb1 = m1 // bm
    part_specs = []
    for w in (wa, wb, wc):
        part_specs += [pl.BlockSpec((bm, w), _first_rows(nb1)), pl.BlockSpec((bm, w), _second_rows(nb1))]
    return pl.pallas_call(
        functools.partial(_assemble_kernel, nb1=nb1),
        grid=(m // bm,),
        in_specs=part_specs + [pl.BlockSpec((1, wa), lambda i: (0, 0)),
                               pl.BlockSpec((1, wb), lambda i: (0, 0))],
        out_specs=pl.BlockSpec((bm, wa + wb + wc), lambda i: (i, 0)),
        out_shape=jax.ShapeDtypeStruct((m, wa + wb + wc), BF16),
        compiler_params=_cparams(("arbitrary",)),
        name="assemble_mixers",
    )(*a_parts, *z_parts, *c_parts, gain_a.astype(F32).reshape(1, wa), gain_b.astype(F32).reshape(1, wb))


def _token_mixers(proj, requests, p, l, dims):
    nh_a, hd_a, ch, order, nh_r, dh_r = dims
    in_cols = proj.shape[1]
    total = proj.shape[0]
    wa = nh_a * hd_a
    hy0 = 3 * wa
    ret0 = hy0 + (order + 1) * ch
    log_gamma = jnp.log1p(-jnp.exp2(-p["ret_decay_exp"][l].astype(F32)))
    a_parts, z_parts, c_parts = [], [], []
    bias = _na_bias_tables(p["na_rpb"][l])
    for (bsz, seq_len, tok0) in requests:
        assert tok0 % seq_len == 0 and total % seq_len == 0
        proj3 = proj.reshape(total // seq_len, seq_len, in_cols)
        b0 = tok0 // seq_len
        a = _neighbourhood_attention(proj3, b0, bsz, seq_len, bias, nh_a, hd_a)
        a_parts.append(a.reshape(bsz * seq_len, wa))
        tables = _dft_tables(seq_len)
        f = _hyena_filter_signal(seq_len, p["hy_f_w1"][l], p["hy_f_b1"][l], p["hy_f_freq"][l],
                                 p["hy_f_w2"][l], p["hy_f_b2"][l], p["hy_f_w3"][l], p["hy_f_b3"][l],
                                 order, ch)
        kf = _filter_spectrum(f, tables)
        hy = _short_conv3(proj3, b0, bsz, seq_len, hy0, p["hy_conv_w"][l].astype(F32),
                          p["hy_conv_b"][l].astype(F32))
        z = _hyena(hy, tables, kf, p["hy_skip"][l], order, ch)
        z_parts.append(z.reshape(bsz * seq_len, ch))
        half = dh_r // 2
        inv_freq = 1.0 / (ROPE_BASE ** jnp.linspace(0.0, 1.0, half, dtype=F32))
        ang = jnp.arange(seq_len, dtype=F32)[:, None] * inv_freq[None, :]
        c = _retention(proj3, b0, bsz, seq_len, ret0, nh_r, dh_r, log_gamma, jnp.cos(ang), jnp.sin(ang))
        c_parts.append(c.reshape(bsz * seq_len, nh_r * dh_r))
    return _assemble(a_parts, z_parts, c_parts, p["grp_gain_a"][l], p["grp_gain_b"][l])


def _pick(n, candidates):
    for c in candidates:
        if n % c == 0:
            return c
    raise ValueError(f"no block size for {n}")


def kernel(x_prompt, x_sample, ln_in_g, ln_in_b, w_in, na_rpb, hy_conv_w, hy_conv_b, hy_f_w1, hy_f_b1,
           hy_f_freq, hy_f_w2, hy_f_b2, hy_f_w3, hy_f_b3, hy_skip, ret_decay_exp, grp_gain_a, grp_gain_b,
           w_out, ln1_g, ln1_b, w_ffn_in, w_ffn_out, ln2_g, ln2_b):
    p = dict(na_rpb=na_rpb, hy_conv_w=hy_conv_w, hy_conv_b=hy_conv_b, hy_f_w1=hy_f_w1, hy_f_b1=hy_f_b1,
             hy_f_freq=hy_f_freq, hy_f_w2=hy_f_w2, hy_f_b2=hy_f_b2, hy_f_w3=hy_f_w3, hy_f_b3=hy_f_b3,
             hy_skip=hy_skip, ret_decay_exp=ret_decay_exp, grp_gain_a=grp_gain_a, grp_gain_b=grp_gain_b)
    depth, d_model, in_cols = w_in.shape
    nh_a = na_rpb.shape[1]
    wa = grp_gain_a.shape[1]
    ch = grp_gain_b.shape[1]
    order = hy_skip.shape[1]
    nh_r = ret_decay_exp.shape[2]
    wr = (in_cols - 3 * wa - (order + 1) * ch) // 4
    dims = (nh_a, wa // nh_a, ch, order, nh_r, wr // nh_r)
    d_ff = w_ffn_out.shape[1]
    alpha = (2.0 * depth) ** 0.25

    bp, lp, _ = x_prompt.shape
    bs, ls, _ = x_sample.shape
    requests = [(bp, lp, 0), (bs, ls, bp * lp)]
    m1 = bp * lp
    m = m1 + bs * ls
    bm = _pick(m, (1024, 512, 256))
    bm_ln = 256
    assert m1 % bm_ln == 0 and m % bm_ln == 0 and m1 % 512 == 0

    xf, xb = _layernorm_join(x_prompt.reshape(m1, d_model), x_sample.reshape(bs * ls, d_model),
                             ln_in_g.astype(F32), ln_in_b.astype(F32), bm_ln)
    w_in_b, w_out_b = w_in.astype(BF16), w_out.astype(BF16)
    w_ffn_in_b, w_ffn_out_b = w_ffn_in.astype(BF16), w_ffn_out.astype(BF16)
    resid = (xf,)
    for l in range(depth):
        proj = _matmul(xb, w_in_b, l, bm, _pick(in_cols, (1024, 512, 256)))
        mixed = _token_mixers(proj, requests, p, l, dims)
        bn_out = _pick(d_model, (1024, 512, 256) if len(resid) == 1 else (512, 256))
        y = _matmul_residual(mixed, w_out_b, l, resid, alpha, bm, bn_out)
        g1, b1 = ln1_g[l].astype(F32), ln1_b[l].astype(F32)
        xb, stats = _layernorm_stats(y, g1, b1, bm_ln)
        hmid = _matmul_swiglu(xb, w_ffn_in_b, l, _pick(m, (2048, 1024, 512, 256)), _pick(d_ff, (512, 256, 128)))
        y = _matmul_residual(hmid, w_ffn_out_b, l, (y, stats, g1, b1), alpha, _pick(m, (512, 256)),
                             _pick(d_model, (512, 256)))
        if l + 1 < depth:
            g2, b2 = ln2_g[l].astype(F32), ln2_b[l].astype(F32)
            xb, stats = _layernorm_stats(y, g2, b2, bm_ln)
            resid = (y, stats, g2, b2)
    o1, o2 = _layernorm_split(y, ln2_g[depth - 1].astype(F32), ln2_b[depth - 1].astype(F32), m1, bm_ln)
    return (o1.reshape(bp, lp, d_model), o2.reshape(bs, ls, d_model))
```

```python
import functools
import math

import numpy as np
import jax
import jax.numpy as jnp
from jax import lax
from jax.experimental import pallas as pl
from jax.experimental.pallas import tpu as pltpu

F32 = jnp.float32
BF16 = jnp.bfloat16

GRID_W = 64
WIN_R = 8
WIN_C = 16
ROPE_BASE = 10000.0
LN_EPS = 1e-5
RMS_EPS = 1e-6
NEG_INF = -1e30
HY_TARGET = 1e-2
HY_FAST_DECAY = 0.3
HY_SLOW_DECAY = 1.5
HY_MIN_DECAY = math.log(HY_TARGET) / HY_SLOW_DECAY
HY_MAX_DECAY = math.log(HY_TARGET) / HY_FAST_DECAY

V7X_VMEM_LIMIT = 56 * 1024 * 1024
LANES = 128
SUBLANES = 8
DFT_N2 = 128
DFT_GROUP = 32
NA_ROWS_PER_STEP = 8
RET_CHUNK = 256
RET_SUPER = 2048


def _cparams(sem, vmem=V7X_VMEM_LIMIT):
    return pltpu.CompilerParams(dimension_semantics=sem, vmem_limit_bytes=vmem)


def _pitch(rows):
    p = -(-rows // SUBLANES)
    return SUBLANES * (p + 1 - p % 2)


def _ln(x, g, b):
    mu = jnp.mean(x, -1, keepdims=True)
    xc = x - mu
    var = jnp.mean(xc * xc, -1, keepdims=True)
    return xc * lax.rsqrt(var + LN_EPS) * g + b


def _ln_stats_kernel(x_ref, g_ref, b_ref, ob_ref, st_ref):
    x = x_ref[...]
    mu = jnp.mean(x, -1, keepdims=True)
    xc = x - mu
    rstd = lax.rsqrt(jnp.mean(xc * xc, -1, keepdims=True) + LN_EPS)
    ob_ref[...] = (xc * rstd * g_ref[...] + b_ref[...]).astype(BF16)
    st_ref[:, :LANES] = jnp.broadcast_to(mu, (x.shape[0], LANES))
    st_ref[:, LANES:] = jnp.broadcast_to(rstd, (x.shape[0], LANES))


def _ln_join_kernel(x1_ref, x2_ref, g_ref, b_ref, of_ref, ob_ref, *, nb1):
    x = jnp.where(pl.program_id(0) < nb1, x1_ref[...], x2_ref[...])
    y = _ln(x, g_ref[...], b_ref[...])
    of_ref[...] = y
    ob_ref[...] = y.astype(BF16)


def _ln_split_kernel(x_ref, g_ref, b_ref, o1_ref, o2_ref, *, nb1):
    y = _ln(x_ref[...], g_ref[...], b_ref[...])

    @pl.when(pl.program_id(0) < nb1)
    def _():
        o1_ref[...] = y

    @pl.when(pl.program_id(0) >= nb1)
    def _():
        o2_ref[...] = y


def _first_rows(nb1):
    return lambda i: (jnp.minimum(i, nb1 - 1), 0)


def _second_rows(nb1):
    return lambda i: (jnp.maximum(i - nb1, 0), 0)


def _layernorm_join(x1, x2, g, b, bm=256):
    m1, d = x1.shape
    m = m1 + x2.shape[0]
    nb1 = m1 // bm
    return pl.pallas_call(
        functools.partial(_ln_join_kernel, nb1=nb1),
        grid=(m // bm,),
        in_specs=[pl.BlockSpec((bm, d), _first_rows(nb1)),
                  pl.BlockSpec((bm, d), _second_rows(nb1)),
                  pl.BlockSpec((1, d), lambda i: (0, 0)),
                  pl.BlockSpec((1, d), lambda i: (0, 0))],
        out_specs=[pl.BlockSpec((bm, d), lambda i: (i, 0)),
                   pl.BlockSpec((bm, d), lambda i: (i, 0))],
        out_shape=[jax.ShapeDtypeStruct((m, d), F32), jax.ShapeDtypeStruct((m, d), BF16)],
        compiler_params=_cparams(("arbitrary",)),
        name="layernorm_join",
    )(x1, x2, g.reshape(1, d), b.reshape(1, d))


def _layernorm_split(x, g, b, m1, bm=256):
    m, d = x.shape
    nb1 = m1 // bm
    return pl.pallas_call(
        functools.partial(_ln_split_kernel, nb1=nb1),
        grid=(m // bm,),
        in_specs=[pl.BlockSpec((bm, d), lambda i: (i, 0)),
                  pl.BlockSpec((1, d), lambda i: (0, 0)),
                  pl.BlockSpec((1, d), lambda i: (0, 0))],
        out_specs=[pl.BlockSpec((bm, d), _first_rows(nb1)),
                   pl.BlockSpec((bm, d), _second_rows(nb1))],
        out_shape=[jax.ShapeDtypeStruct((m1, d), F32), jax.ShapeDtypeStruct((m - m1, d), F32)],
        compiler_params=_cparams(("arbitrary",)),
        name="layernorm_split",
    )(x, g.reshape(1, d), b.reshape(1, d))


def _layernorm_stats(x, g, b, bm=256):
    m, d = x.shape
    return pl.pallas_call(
        _ln_stats_kernel,
        grid=(m // bm,),
        in_specs=[pl.BlockSpec((bm, d), lambda i: (i, 0)),
                  pl.BlockSpec((1, d), lambda i: (0, 0)),
                  pl.BlockSpec((1, d), lambda i: (0, 0))],
        out_specs=[pl.BlockSpec((bm, d), lambda i: (i, 0)),
                   pl.BlockSpec((bm, 2 * LANES), lambda i: (i, 0))],
        out_shape=[jax.ShapeDtypeStruct((m, d), BF16), jax.ShapeDtypeStruct((m, 2 * LANES), F32)],
        compiler_params=_cparams(("parallel",)),
        name="layernorm_stats",
    )(x, g.reshape(1, d), b.reshape(1, d))


def _mm_kernel(x_ref, w_ref, o_ref):
    o_ref[...] = jnp.dot(x_ref[...], w_ref[...], preferred_element_type=F32).astype(o_ref.dtype)


def _mm_res_kernel(x_ref, w_ref, r_ref, o_ref, *, alpha):
    o_ref[...] = alpha * r_ref[...] + jnp.dot(x_ref[...], w_ref[...], preferred_element_type=F32)


def _mm_res_ln_kernel(x_ref, w_ref, y_ref, st_ref, g_ref, b_ref, o_ref, *, alpha):
    acc = jnp.dot(x_ref[...], w_ref[...], preferred_element_type=F32)
    mu = st_ref[:, :LANES]
    rstd = st_ref[:, LANES:]
    for c in range(y_ref.shape[1] // LANES):
        cols = slice(c * LANES, (c + 1) * LANES)
        r = (y_ref[:, cols] - mu) * rstd * g_ref[:, cols] + b_ref[:, cols]
        o_ref[:, cols] = alpha * r + acc[:, cols]


def _mm_swiglu_kernel(x_ref, wg_ref, wu_ref, o_ref):
    x = x_ref[...]
    g = jnp.dot(x, wg_ref[...], preferred_element_type=F32)
    u = jnp.dot(x, wu_ref[...], preferred_element_type=F32)
    o_ref[...] = (g / (1.0 + jnp.exp(-g)) * u).astype(o_ref.dtype)


def _matmul(x, w, layer, bm, bn, out_dtype=BF16):
    m, k = x.shape
    n = w.shape[2]
    return pl.pallas_call(
        _mm_kernel,
        grid=(m // bm, n // bn),
        in_specs=[pl.BlockSpec((bm, k), lambda i, j: (i, 0)),
                  pl.BlockSpec((None, k, bn), lambda i, j: (layer, 0, j))],
        out_specs=pl.BlockSpec((bm, bn), lambda i, j: (i, j)),
        out_shape=jax.ShapeDtypeStruct((m, n), out_dtype),
        compiler_params=_cparams(("parallel", "arbitrary")),
        name="matmul",
    )(x, w)


def _matmul_residual(x, w, layer, resid, alpha, bm, bn):
    m, k = x.shape
    n = w.shape[2]
    tile = pl.BlockSpec((bm, bn), lambda i, j: (i, j))
    if len(resid) == 1:
        body, resid_specs = _mm_res_kernel, [tile]
    else:
        body = _mm_res_ln_kernel
        resid = (resid[0], resid[1], resid[2].reshape(1, n), resid[3].reshape(1, n))
        resid_specs = [tile, pl.BlockSpec((bm, 2 * LANES), lambda i, j: (i, 0)),
                       pl.BlockSpec((1, bn), lambda i, j: (0, j)), pl.BlockSpec((1, bn), lambda i, j: (0, j))]
    return pl.pallas_call(
        functools.partial(body, alpha=alpha),
        grid=(m // bm, n // bn),
        in_specs=[pl.BlockSpec((bm, k), lambda i, j: (i, 0)),
                  pl.BlockSpec((None, k, bn), lambda i, j: (layer, 0, j))] + resid_specs,
        out_specs=tile,
        out_shape=jax.ShapeDtypeStruct((m, n), F32),
        compiler_params=_cparams(("parallel", "arbitrary")),
        name="matmul_residual",
    )(x, w, *resid)


def _matmul_swiglu(x, w, layer, bm, bn):
    m, k = x.shape
    f = w.shape[2] // 2
    nb = f // bn
    return pl.pallas_call(
        _mm_swiglu_kernel,
        grid=(m // bm, nb),
        in_specs=[pl.BlockSpec((bm, k), lambda i, j: (i, 0)),
                  pl.BlockSpec((None, k, bn), lambda i, j: (layer, 0, j)),
                  pl.BlockSpec((None, k, bn), lambda i, j: (layer, 0, j + nb))],
        out_specs=pl.BlockSpec((bm, bn), lambda i, j: (i, j)),
        out_shape=jax.ShapeDtypeStruct((m, f), BF16),
        compiler_params=_cparams(("parallel", "arbitrary")),
        name="matmul_swiglu",
    )(x, w, w)


def _na_bias_tables(rpb):
    nh = rpb.shape[0]
    qc = np.arange(GRID_W)[:, None]
    kc = np.arange(GRID_W)[None, :]
    win_start = np.clip(qc - WIN_C // 2, 0, GRID_W - WIN_C)
    col_ok = (kc >= win_start) & (kc < win_start + WIN_C)
    dc_idx = np.clip(kc - qc + (WIN_C - 1), 0, 2 * WIN_C - 2)
    onehot = (dc_idx.reshape(-1)[None, :] == np.arange(2 * WIN_C - 1)[:, None]).astype(np.float32)
    tiles = jnp.einsum("hdc,cq->hdq", rpb.astype(F32), jnp.asarray(onehot),
                       precision=lax.Precision.HIGHEST).reshape(nh, 2 * WIN_R - 1, GRID_W, GRID_W)
    tiles = jnp.where(jnp.asarray(col_ok)[None, None], tiles, NEG_INF)
    dr = np.arange(WIN_R)[:, None] + np.arange(WIN_R)[None, :]
    t = tiles[:, dr]
    return t.transpose(0, 1, 3, 2, 4).reshape(nh, WIN_R, GRID_W, WIN_R * GRID_W)


def _na_kernel(q_ref, k_ref, v_ref, bias_ref, o_ref, *, rows, scale):
    nk = WIN_R * GRID_W
    nq = NA_ROWS_PER_STEP * GRID_W
    hd = q_ref.shape[-1]

    def group(g, carry):
        ks, vs, bias = [], [], []
        for i in range(NA_ROWS_PER_STEP):
            r = g * NA_ROWS_PER_STEP + i
            row_start = jnp.clip(r - WIN_R // 2, 0, rows - WIN_R)
            k0 = pl.multiple_of(row_start * GRID_W, GRID_W)
            ks.append(k_ref[0, pl.ds(k0, nk), :])
            vs.append(v_ref[0, pl.ds(k0, nk), :])
            bias.append(bias_ref[row_start - r + (WIN_R - 1)])
        q_rows = pl.ds(pl.multiple_of(g * nq, nq), nq)
        q = q_ref[0, q_rows, :].reshape(NA_ROWS_PER_STEP, GRID_W, hd)
        s = jnp.einsum("rqd,rkd->rqk", q, jnp.stack(ks), preferred_element_type=F32)
        s = s * scale + jnp.stack(bias)
        m = jnp.max(s, -1, keepdims=True)
        p = jnp.exp(s - m)
        l = jnp.sum(p, -1, keepdims=True)
        o = jnp.einsum("rqk,rkd->rqd", p.astype(BF16), jnp.stack(vs), preferred_element_type=F32) / l
        o_ref[0, q_rows, :] = o.reshape(nq, hd).astype(o_ref.dtype)
        return carry

    lax.fori_loop(0, rows // NA_ROWS_PER_STEP, group, 0, unroll=2)


def _neighbourhood_attention(proj3, b0, bsz, seq_len, bias, nh, hd):
    rows = seq_len // GRID_W
    assert rows % NA_ROWS_PER_STEP == 0 and rows >= WIN_R
    return pl.pallas_call(
        functools.partial(_na_kernel, rows=rows, scale=hd ** -0.5),
        grid=(nh, bsz),
        in_specs=[pl.BlockSpec((1, seq_len, hd), lambda h, b: (b0 + b, 0, h)),
                  pl.BlockSpec((1, seq_len, hd), lambda h, b: (b0 + b, 0, nh + h)),
                  pl.BlockSpec((1, seq_len, hd), lambda h, b: (b0 + b, 0, 2 * nh + h)),
                  pl.BlockSpec((None, WIN_R, GRID_W, WIN_R * GRID_W), lambda h, b: (h, 0, 0, 0))],
        out_specs=pl.BlockSpec((1, seq_len, hd), lambda h, b: (b, 0, h)),
        out_shape=jax.ShapeDtypeStruct((bsz, seq_len, nh * hd), BF16),
        compiler_params=_cparams(("parallel", "parallel")),
        name="neighbourhood_attention",
    )(proj3, proj3, proj3, bias)


def _conv3_kernel(x_ref, w_ref, b_ref, o_ref, *, seq_len, rb):
    w = w_ref[...]
    bias = b_ref[...]
    n_chunks = seq_len // rb

    def body(c, carry):
        r0 = pl.multiple_of(c * rb, rb)
        x = x_ref[0, pl.ds(r0, rb), :].astype(F32)
        row = lax.broadcasted_iota(jnp.int32, x.shape, 0)
        p0 = pl.multiple_of(jnp.maximum(r0 - 16, 0), 16)
        n0 = pl.multiple_of(jnp.minimum(r0 + rb, seq_len - 16), 16)
        prev_row = x_ref[0, pl.ds(p0, 16), :].astype(F32)[15:16]
        next_row = x_ref[0, pl.ds(n0, 16), :].astype(F32)[0:1]
        prev_row = jnp.where(c == 0, 0.0, prev_row)
        next_row = jnp.where(c == n_chunks - 1, 0.0, next_row)
        up = jnp.where(row == 0, prev_row, pltpu.roll(x, 1, 0))
        down = jnp.where(row == rb - 1, next_row, pltpu.roll(x, rb - 1, 0))
        y = up * w[0:1] + x * w[1:2] + down * w[2:3] + bias
        o_ref[0, pl.ds(r0, rb), :] = y.astype(o_ref.dtype)
        return carry

    lax.fori_loop(0, n_chunks, body, 0)


def _short_conv3(proj3, b0, bsz, seq_len, col0, w, b, cb=256, rb=512):
    width = w.shape[1]
    rb = min(rb, seq_len)
    c0 = col0 // cb
    return pl.pallas_call(
        functools.partial(_conv3_kernel, seq_len=seq_len, rb=rb),
        grid=(bsz, width // cb),
        in_specs=[pl.BlockSpec((1, seq_len, cb), lambda bi, c: (b0 + bi, 0, c0 + c)),
                  pl.BlockSpec((3, cb), lambda bi, c: (0, c)),
                  pl.BlockSpec((1, cb), lambda bi, c: (0, c))],
        out_specs=pl.BlockSpec((1, seq_len, cb), lambda bi, c: (bi, 0, c)),
        out_shape=jax.ShapeDtypeStruct((bsz, seq_len, width), BF16),
        compiler_params=_cparams(("parallel", "parallel")),
        name="hyena_short_conv",
    )(proj3, w, b.reshape(1, width))


def _filter_kernel(z_ref, w1_ref, b1_ref, fr_ref, w2_ref, b2_ref, w3_ref, b3_ref, dl_ref, o_ref, *,
                   seq_len, rb):
    i = pl.program_id(0)
    hi = lax.Precision.HIGHEST
    z = z_ref[...]
    h = jnp.sin(fr_ref[0:1] * (jnp.dot(z, w1_ref[...], precision=hi, preferred_element_type=F32)
                               + b1_ref[...]))
    h = jnp.sin(fr_ref[1:2] * (jnp.dot(h, w2_ref[...], precision=hi, preferred_element_type=F32)
                               + b2_ref[...]))
    ch = dl_ref.shape[-1]
    n = i * rb + lax.broadcasted_iota(jnp.int32, (rb, ch), 0)
    sign = jnp.where(n < seq_len, 1.0, jnp.where(n == seq_len, 0.0, -1.0))
    window = jnp.exp(-z[:, 0:1] * dl_ref[...]) * sign
    hb = h.astype(BF16)
    for o in range(w3_ref.shape[0]):
        f = jnp.dot(hb, w3_ref[o].astype(BF16), preferred_element_type=F32) + b3_ref[o]
        o_ref[:, o * ch:(o + 1) * ch] = (f * window).astype(o_ref.dtype)


def _hyena_filter_signal(seq_len, w1, b1, freq, w2, b2, w3, b3, order, ch, rb=512):
    n = 2 * seq_len
    emb, hid = w1.shape
    pad = LANES
    t = jnp.linspace(0.0, 1.0, seq_len, dtype=F32)[:, None]
    bands = (emb - 1) // 2
    fr = jnp.linspace(1e-4, bands - 1, bands, dtype=F32)[None, :]
    wpos = 2.0 * math.pi * jnp.arange(seq_len, dtype=F32)[:, None] / seq_len
    z = jnp.concatenate([t, jnp.cos(fr * wpos), -jnp.sin(fr * wpos)], axis=-1)
    pos = np.arange(n)
    src = np.clip(np.where(pos < seq_len, pos, n - pos), 0, seq_len - 1)
    z2 = jnp.pad(z[src], ((0, 0), (0, pad - emb)))
    w1p = jnp.pad(w1.astype(F32), ((0, pad - emb), (0, pad - hid)))
    b1p = jnp.pad(b1.astype(F32), (0, pad - hid)).reshape(1, pad)
    frp = jnp.pad(freq.astype(F32), ((0, 0), (0, pad - hid)), constant_values=1.0)
    w2p = jnp.pad(w2.astype(F32), ((0, pad - hid), (0, pad - hid)))
    b2p = jnp.pad(b2.astype(F32), (0, pad - hid)).reshape(1, pad)
    w3p = jnp.pad(w3.astype(F32), ((0, pad - hid), (0, 0))).reshape(pad, order, 2, ch).transpose(2, 1, 0, 3)
    b3p = b3.astype(F32).reshape(order, 2, 1, ch).transpose(1, 0, 2, 3)
    deltas = jnp.abs(jnp.linspace(HY_MIN_DECAY, HY_MAX_DECAY, ch, dtype=F32)).reshape(1, ch)
    nb = n // rb
    half = nb // 2
    full = lambda i: (0, 0)
    return pl.pallas_call(
        functools.partial(_filter_kernel, seq_len=seq_len, rb=rb),
        grid=(nb,),
        in_specs=[pl.BlockSpec((rb, pad), lambda i: (i, 0)),
                  pl.BlockSpec((pad, pad), full), pl.BlockSpec((1, pad), full),
                  pl.BlockSpec((2, pad), full),
                  pl.BlockSpec((pad, pad), full), pl.BlockSpec((1, pad), full),
                  pl.BlockSpec((None, order, pad, ch), lambda i: (i // half, 0, 0, 0)),
                  pl.BlockSpec((None, order, 1, ch), lambda i: (i // half, 0, 0, 0)),
                  pl.BlockSpec((1, ch), full)],
        out_specs=pl.BlockSpec((rb, order * ch), lambda i: (i, 0)),
        out_shape=jax.ShapeDtypeStruct((n, order * ch), BF16),
        compiler_params=_cparams(("parallel",)),
        name="hyena_filter_mlp",
    )(z2, w1p, b1p, frp, w2p, b2p, w3p, b3p, deltas)


def _dft_tables(seq_len):
    n = 2 * seq_len
    n2 = DFT_N2
    n1 = n // n2
    hk = n2 // 2
    kg = min(n1, DFT_GROUP)
    ng = n1 // kg
    pi = math.pi
    k1 = jnp.arange(n1, dtype=jnp.int32)
    ph1 = ((2 * k1[:, None] + 1) * k1[None, :]) % (2 * n1)
    th1 = ph1.astype(F32) * (pi / n1)
    c1 = jnp.cos(th1).reshape(ng, kg, n1)
    s1 = jnp.sin(th1).reshape(ng, kg, n1)
    f1 = jnp.concatenate([c1, -s1], axis=1)
    fb = jnp.concatenate([c1, -s1], axis=1).transpose(0, 2, 1)[:, : n1 // 2] * (2.0 / n)
    k = k1[:, None, None] + n1 * jnp.arange(hk, dtype=jnp.int32)[None, :, None]
    ph2 = ((2 * k + 1) * jnp.arange(n2, dtype=jnp.int32)[None, None, :]) % (2 * n)
    th2 = ph2.astype(F32) * (pi / n)
    c2, s2 = jnp.cos(th2), jnp.sin(th2)
    g = jnp.concatenate([jnp.concatenate([c2, s2], axis=2),
                         jnp.concatenate([-s2, c2], axis=2)], axis=1)
    c2t, s2t = c2.transpose(0, 2, 1), s2.transpose(0, 2, 1)
    h = jnp.concatenate([jnp.concatenate([c2t, -s2t], axis=2),
                         jnp.concatenate([s2t, c2t], axis=2)], axis=1)
    nl = 2 if n1 <= DFT_GROUP else 1
    return dict(n1=n1, kg=kg, ng=ng, nl=nl, f1_full=f1.astype(BF16), f1_half=f1[:, :, : n1 // 2].astype(BF16),
                fb=fb.astype(BF16), g=g.astype(BF16), h=h.astype(BF16))


def _load_lanes(ref, rows):
    parts = [ref[s, rows, :] for s in range(ref.shape[0])]
    return parts[0] if len(parts) == 1 else jnp.concatenate(parts, axis=1)


def _store_lanes(ref, rows, val):
    for s in range(ref.shape[0]):
        ref[s, rows, :] = val[:, s * LANES:(s + 1) * LANES]


def _fill_slabs(x_ref, xs, slabs, pitch):
    def body(n1, carry):
        src = pl.ds(pl.multiple_of(n1 * DFT_N2, DFT_N2), DFT_N2)
        _store_lanes(xs, pl.ds(pl.multiple_of(n1 * pitch, SUBLANES), DFT_N2), x_ref[src, :].astype(F32))
        return carry
    lax.fori_loop(0, slabs, body, 0, unroll=4)


def _slab_stage(xs, a_s, f1, slabs, p_x, p_a):
    rows = f1.shape[0]

    def body(i, carry):
        xn = _load_lanes(xs, pl.ds(i, slabs, stride=p_x)).astype(BF16)
        _store_lanes(a_s, pl.ds(pl.multiple_of(i * p_a, SUBLANES), rows),
                     jnp.dot(f1, xn, preferred_element_type=F32))
        return carry
    lax.fori_loop(0, DFT_N2, body, 0, unroll=32)


def _load_slab_freq(a_s, j, kg, p_a):
    are = _load_lanes(a_s, pl.ds(j, DFT_N2, stride=p_a))
    aim = _load_lanes(a_s, pl.ds(kg + j, DFT_N2, stride=p_a))
    return jnp.concatenate([are, aim], axis=0).astype(BF16)


def _filter_spec_kernel(x_ref, f1_ref, g_ref, o_ref, xs, a_s, *, slabs, kg):
    grp = pl.program_id(1)
    p_x, p_a = _pitch(DFT_N2), _pitch(2 * kg)

    @pl.when(grp == 0)
    def _():
        _fill_slabs(x_ref, xs, slabs, p_x)

    _slab_stage(xs, a_s, f1_ref[0], slabs, p_x, p_a)

    def body(j, carry):
        o_ref[j] = jnp.dot(g_ref[j], _load_slab_freq(a_s, j, kg, p_a), preferred_element_type=F32)
        return carry
    lax.fori_loop(0, kg, body, 0, unroll=8)


def _filter_spectrum(f, tables):
    n, cols = f.shape
    n1, kg, ng, nl = tables["n1"], tables["kg"], tables["ng"], tables["nl"]
    p_x, p_a = _pitch(DFT_N2), _pitch(2 * kg)
    wl = nl * LANES
    return pl.pallas_call(
        functools.partial(_filter_spec_kernel, slabs=n1, kg=kg),
        grid=(cols // wl, ng),
        in_specs=[pl.BlockSpec((n, wl), lambda c, g: (0, c)),
                  pl.BlockSpec((1, 2 * kg, n1), lambda c, g: (g, 0, 0)),
                  pl.BlockSpec((kg, DFT_N2, 2 * DFT_N2), lambda c, g: (g, 0, 0))],
        out_specs=pl.BlockSpec((kg, DFT_N2, wl), lambda c, g: (g, 0, c)),
        out_shape=jax.ShapeDtypeStruct((n1, DFT_N2, cols), F32),
        scratch_shapes=[pltpu.VMEM((nl, n1 * p_x, LANES), F32),
                        pltpu.VMEM((nl, DFT_N2 * p_a, LANES), F32)],
        compiler_params=_cparams(("parallel", "arbitrary")),
        name="hyena_filter_spectrum",
    )(f, tables["f1_full"], tables["g"])


def _long_conv_kernel(x_ref, gate_ref, f1_ref, g_ref, h_ref, kf_ref, fb_ref, skip_ref, o_ref,
                      xs, a_s, z_s, y_s, *, slabs, kg, ng):
    grp = pl.program_id(2)
    n2 = DFT_N2
    hk = n2 // 2
    p_x, p_a, p_z, p_y = _pitch(n2), _pitch(2 * kg), _pitch(2 * n2), _pitch(slabs)

    @pl.when(grp == 0)
    def _():
        _fill_slabs(x_ref.at[0], xs, slabs, p_x)
        y_s[...] = jnp.zeros_like(y_s)

    _slab_stage(xs, a_s, f1_ref[0], slabs, p_x, p_a)

    def freq_body(j, carry):
        x = jnp.dot(g_ref[j], _load_slab_freq(a_s, j, kg, p_a), preferred_element_type=F32)
        kf = kf_ref[j]
        xr, xi = x[:hk], x[hk:]
        kr, ki = kf[:hk], kf[hk:]
        y = jnp.concatenate([xr * kr - xi * ki, xr * ki + xi * kr], axis=0)
        _store_lanes(z_s, pl.ds(pl.multiple_of(j * p_z, SUBLANES), n2), y)
        return carry
    lax.fori_loop(0, kg, freq_body, 0, unroll=8)

    def inv_freq_body(j, carry):
        y = _load_lanes(z_s, pl.ds(pl.multiple_of(j * p_z, SUBLANES), n2)).astype(BF16)
        _store_lanes(z_s, pl.ds(pl.multiple_of(j * p_z, SUBLANES), 2 * n2),
                     jnp.dot(h_ref[j], y, preferred_element_type=F32))
        return carry
    lax.fori_loop(0, kg, inv_freq_body, 0, unroll=8)

    fb = fb_ref[0]

    def inv_body(t, carry):
        zre = _load_lanes(z_s, pl.ds(t, kg, stride=p_z))
        zim = _load_lanes(z_s, pl.ds(n2 + t, kg, stride=p_z))
        zz = jnp.concatenate([zre, zim], axis=0).astype(BF16)
        dst = pl.ds(pl.multiple_of(t * p_y, SUBLANES), slabs)
        _store_lanes(y_s, dst, _load_lanes(y_s, dst) + jnp.dot(fb, zz, preferred_element_type=F32))
        return carry
    lax.fori_loop(0, n2, inv_body, 0, unroll=32)

    @pl.when(grp == ng - 1)
    def _():
        skip = skip_ref[...]

        def out_body(t1, carry):
            y = _load_lanes(y_s, pl.ds(t1, n2, stride=p_y))
            u = _load_lanes(xs, pl.ds(pl.multiple_of(t1 * p_x, SUBLANES), n2))
            rows = pl.ds(pl.multiple_of(t1 * n2, n2), n2)
            gate = gate_ref[0, rows, :].astype(F32)
            o_ref[0, rows, :] = ((y + skip * u) * gate).astype(o_ref.dtype)
            return carry
        lax.fori_loop(0, slabs, out_body, 0, unroll=4)


def _long_conv_gate(u, u_blk0, gate, gate_blk0, tables, kf, kf_blk0, skip, ch):
    bsz, seq_len, _ = u.shape
    n1, kg, ng, nl = tables["n1"], tables["kg"], tables["ng"], tables["nl"]
    slabs = n1 // 2
    p_x, p_a, p_z, p_y = _pitch(DFT_N2), _pitch(2 * kg), _pitch(2 * DFT_N2), _pitch(slabs)
    wl = nl * LANES
    return pl.pallas_call(
        functools.partial(_long_conv_kernel, slabs=slabs, kg=kg, ng=ng),
        grid=(bsz, ch // wl, ng),
        in_specs=[pl.BlockSpec((1, seq_len, wl), lambda b, c, g: (b, 0, u_blk0 + c)),
                  pl.BlockSpec((1, seq_len, wl), lambda b, c, g: (b, 0, gate_blk0 + c)),
                  pl.BlockSpec((1, 2 * kg, slabs), lambda b, c, g: (g, 0, 0)),
                  pl.BlockSpec((kg, DFT_N2, 2 * DFT_N2), lambda b, c, g: (g, 0, 0)),
                  pl.BlockSpec((kg, 2 * DFT_N2, DFT_N2), lambda b, c, g: (g, 0, 0)),
                  pl.BlockSpec((kg, DFT_N2, wl), lambda b, c, g: (g, 0, kf_blk0 + c)),
                  pl.BlockSpec((1, slabs, 2 * kg), lambda b, c, g: (g, 0, 0)),
                  pl.BlockSpec((1, wl), lambda b, c, g: (0, c))],
        out_specs=pl.BlockSpec((1, seq_len, wl), lambda b, c, g: (b, 0, c)),
        out_shape=jax.ShapeDtypeStruct((bsz, seq_len, ch), BF16),
        scratch_shapes=[pltpu.VMEM((nl, slabs * p_x, LANES), F32),
                        pltpu.VMEM((nl, DFT_N2 * p_a, LANES), F32),
                        pltpu.VMEM((nl, kg * p_z, LANES), F32),
                        pltpu.VMEM((nl, DFT_N2 * p_y, LANES), F32)],
        compiler_params=_cparams(("parallel", "parallel", "arbitrary")),
        name="hyena_long_conv",
    )(u, gate, tables["f1_half"], tables["g"], tables["h"], kf, tables["fb"],
      skip.astype(F32).reshape(1, ch))


def _hyena(hy, tables, kf, skip, order, ch):
    nblk = ch // (tables["nl"] * LANES)
    z, z_blk0 = hy, 0
    for o in range(order):
        z = _long_conv_gate(z, z_blk0, hy, (o + 1) * nblk, tables, kf, o * nblk, skip[o], ch)
    return z


def _rope(x, cos, sin):
    half = x.shape[-1] // 2
    x1, x2 = x[:, :half], x[:, half:]
    return jnp.concatenate([x1 * cos - x2 * sin, x1 * sin + x2 * cos], axis=-1)


def _ret_kernel(lg_ref, q_ref, k_ref, v_ref, g_ref, cos_ref, sin_ref, o_ref,
                sb_all, kr_all, kz_all, sf_ref, sb_ref, *, n_super, chunks_per_super, k_scale):
    h = pl.program_id(1)
    s = pl.program_id(2)
    cc = RET_CHUNK
    dh = q_ref.shape[-1]
    lgf = lg_ref[0, h]
    lgb = lg_ref[1, h]
    row = lax.broadcasted_iota(jnp.int32, (cc, dh), 0).astype(F32)
    zeta_f = jnp.exp(lgf * (cc - 1.0 - row))

    @pl.when(s < n_super)
    def _backward_sweep():
        @pl.when(s == 0)
        def _():
            sb_ref[...] = jnp.zeros_like(sb_ref)

        sup = n_super - 1 - s
        zeta_b = jnp.exp(lgb * row)
        chunk_decay = jnp.exp(lgb * jnp.full((1, dh), float(cc), F32))

        def body(t, carry):
            c = chunks_per_super - 1 - t
            c0 = pl.multiple_of(c * cc, cc)
            n = sup * chunks_per_super + c
            sb_all[n] = sb_ref[...].astype(BF16)
            k = _rope(k_ref[0, pl.ds(c0, cc), :].astype(F32), cos_ref[pl.ds(c0, cc), :],
                      sin_ref[pl.ds(c0, cc), :]) * k_scale
            kr_all[n] = k.astype(BF16)
            kz_all[n] = (k * zeta_f).astype(BF16)
            kz = (k * zeta_b).astype(BF16)
            v = v_ref[0, pl.ds(c0, cc), :]
            upd = lax.dot_general(kz, v, (((0,), (0,)), ((), ())), preferred_element_type=F32)
            sb_ref[...] = sb_ref[...] * chunk_decay + upd
            return carry

        lax.fori_loop(0, chunks_per_super, body, 0, unroll=4)

    @pl.when(s >= n_super)
    def _forward_sweep():
        @pl.when(s == n_super)
        def _():
            sf_ref[...] = jnp.zeros_like(sf_ref)

        sup = s - n_super
        col = lax.broadcasted_iota(jnp.int32, (cc, cc), 1).astype(F32)
        rowc = lax.broadcasted_iota(jnp.int32, (cc, cc), 0).astype(F32)
        diff = rowc - col
        inner_decay = jnp.where(diff >= 0, jnp.exp(lgf * jnp.maximum(diff, 0.0)),
                                jnp.exp(lgb * jnp.maximum(-diff, 0.0)))
        xi_f = jnp.exp(lgf * (row + 1.0))
        xi_b = jnp.exp(lgb * (cc - row))
        chunk_decay = jnp.exp(lgf * jnp.full((1, dh), float(cc), F32))

        def body(c, carry):
            c0 = pl.multiple_of(c * cc, cc)
            n = sup * chunks_per_super + c
            q = _rope(q_ref[0, pl.ds(c0, cc), :].astype(F32), cos_ref[pl.ds(c0, cc), :],
                      sin_ref[pl.ds(c0, cc), :]).astype(BF16)
            v = v_ref[0, pl.ds(c0, cc), :]
            sc = lax.dot_general(q, kr_all[n], (((1,), (1,)), ((), ())),
                                 preferred_element_type=F32) * inner_decay
            ret = jnp.dot(sc.astype(BF16), v, preferred_element_type=F32)
            ret = ret + xi_f * jnp.dot(q, sf_ref[...].astype(BF16), preferred_element_type=F32)
            ret = ret + xi_b * jnp.dot(q, sb_all[n], preferred_element_type=F32)
            upd = lax.dot_general(kz_all[n], v, (((0,), (0,)), ((), ())), preferred_element_type=F32)
            sf_ref[...] = sf_ref[...] * chunk_decay + upd
            ret = ret * lax.rsqrt(jnp.mean(ret * ret, -1, keepdims=True) + RMS_EPS)
            gate = g_ref[0, pl.ds(c0, cc), :].astype(F32)
            o_ref[0, pl.ds(c0, cc), :] = (ret * (gate / (1.0 + jnp.exp(-gate)))).astype(o_ref.dtype)
            return carry

        lax.fori_loop(0, chunks_per_super, body, 0, unroll=2)


def _retention(proj3, b0, bsz, seq_len, col0, nh, dh, log_gamma, cos, sin):
    sup_len = min(RET_SUPER, seq_len)
    n_super = seq_len // sup_len
    cps = sup_len // RET_CHUNK
    cb0 = col0 // dh

    def fwd_idx(s):
        return jnp.maximum(s - n_super, 0)

    def kv_idx(s):
        return jnp.where(s < n_super, n_super - 1 - s, s - n_super)

    def k_idx(s):
        return jnp.maximum(n_super - 1 - s, 0)

    n_chunks = seq_len // RET_CHUNK
    return pl.pallas_call(
        functools.partial(_ret_kernel, n_super=n_super, chunks_per_super=cps, k_scale=dh ** -0.5),
        grid=(bsz, nh, 2 * n_super),
        in_specs=[pl.BlockSpec(memory_space=pltpu.SMEM),
                  pl.BlockSpec((1, sup_len, dh), lambda b, h, s: (b0 + b, fwd_idx(s), cb0 + h)),
                  pl.BlockSpec((1, sup_len, dh), lambda b, h, s: (b0 + b, k_idx(s), cb0 + nh + h)),
                  pl.BlockSpec((1, sup_len, dh), lambda b, h, s: (b0 + b, kv_idx(s), cb0 + 2 * nh + h)),
                  pl.BlockSpec((1, sup_len, dh), lambda b, h, s: (b0 + b, fwd_idx(s), cb0 + 3 * nh + h)),
                  pl.BlockSpec((sup_len, dh // 2), lambda b, h, s: (kv_idx(s), 0)),
                  pl.BlockSpec((sup_len, dh // 2), lambda b, h, s: (kv_idx(s), 0))],
        out_specs=pl.BlockSpec((1, sup_len, dh), lambda b, h, s: (b, fwd_idx(s), h)),
        out_shape=jax.ShapeDtypeStruct((bsz, seq_len, nh * dh), BF16),
        scratch_shapes=[pltpu.VMEM((n_chunks, dh, dh), BF16),
                        pltpu.VMEM((n_chunks, RET_CHUNK, dh), BF16),
                        pltpu.VMEM((n_chunks, RET_CHUNK, dh), BF16),
                        pltpu.VMEM((dh, dh), F32),
                        pltpu.VMEM((dh, dh), F32)],
        compiler_params=_cparams(("parallel", "parallel", "arbitrary")),
        name="retention",
    )(log_gamma, proj3, proj3, proj3, proj3, cos, sin)


def _assemble_kernel(a1_ref, a2_ref, z1_ref, z2_ref, c1_ref, c2_ref, ga_ref, gb_ref, o_ref, *, nb1):
    first = pl.program_id(0) < nb1
    wa = a1_ref.shape[-1]
    wb = z1_ref.shape[-1]
    a = jnp.where(first, a1_ref[...], a2_ref[...]).astype(F32)
    a = a * lax.rsqrt(jnp.mean(a * a, -1, keepdims=True) + RMS_EPS) * ga_ref[...]
    z = jnp.where(first, z1_ref[...], z2_ref[...]).astype(F32)
    z = z * lax.rsqrt(jnp.mean(z * z, -1, keepdims=True) + RMS_EPS) * gb_ref[...]
    o_ref[:, :wa] = a.astype(o_ref.dtype)
    o_ref[:, wa:wa + wb] = z.astype(o_ref.dtype)
    o_ref[:, wa + wb:] = jnp.where(first, c1_ref[...], c2_ref[...])


def _assemble(a_parts, z_parts, c_parts, gain_a, gain_b, bm=512):
    m1, wa = a_parts[0].shape
    m = m1 + a_parts[1].shape[0]
    wb, wc = z_parts[0].shape[1], c_parts[0].shape[1]
    nb1 = m1 // bm
    part_specs = []
    for w in (wa, wb, wc):
        part_specs += [pl.BlockSpec((bm, w), _first_rows(nb1)), pl.BlockSpec((bm, w), _second_rows(nb1))]
    return pl.pallas_call(
        functools.partial(_assemble_kernel, nb1=nb1),
        grid=(m // bm,),
        in_specs=part_specs + [pl.BlockSpec((1, wa), lambda i: (0, 0)),
                               pl.BlockSpec((1, wb), lambda i: (0, 0))],
        out_specs=pl.BlockSpec((bm, wa + wb + wc), lambda i: (i, 0)),
        out_shape=jax.ShapeDtypeStruct((m, wa + wb + wc), BF16),
        compiler_params=_cparams(("arbitrary",)),
        name="assemble_mixers",
    )(*a_parts, *z_parts, *c_parts, gain_a.astype(F32).reshape(1, wa), gain_b.astype(F32).reshape(1, wb))


def _token_mixers(proj, requests, p, l, dims):
    nh_a, hd_a, ch, order, nh_r, dh_r = dims
    in_cols = proj.shape[1]
    total = proj.shape[0]
    wa = nh_a * hd_a
    hy0 = 3 * wa
    ret0 = hy0 + (order + 1) * ch
    log_gamma = jnp.log1p(-jnp.exp2(-p["ret_decay_exp"][l].astype(F32)))
    a_parts, z_parts, c_parts = [], [], []
    bias = _na_bias_tables(p["na_rpb"][l])
    for (bsz, seq_len, tok0) in requests:
        assert tok0 % seq_len == 0 and total % seq_len == 0
        proj3 = proj.reshape(total // seq_len, seq_len, in_cols)
        b0 = tok0 // seq_len
        a = _neighbourhood_attention(proj3, b0, bsz, seq_len, bias, nh_a, hd_a)
        a_parts.append(a.reshape(bsz * seq_len, wa))
        tables = _dft_tables(seq_len)
        f = _hyena_filter_signal(seq_len, p["hy_f_w1"][l], p["hy_f_b1"][l], p["hy_f_freq"][l],
                                 p["hy_f_w2"][l], p["hy_f_b2"][l], p["hy_f_w3"][l], p["hy_f_b3"][l],
                                 order, ch)
        kf = _filter_spectrum(f, tables)
        hy = _short_conv3(proj3, b0, bsz, seq_len, hy0, p["hy_conv_w"][l].astype(F32),
                          p["hy_conv_b"][l].astype(F32))
        z = _hyena(hy, tables, kf, p["hy_skip"][l], order, ch)
        z_parts.append(z.reshape(bsz * seq_len, ch))
        half = dh_r // 2
        inv_freq = 1.0 / (ROPE_BASE ** jnp.linspace(0.0, 1.0, half, dtype=F32))
        ang = jnp.arange(seq_len, dtype=F32)[:, None] * inv_freq[None, :]
        c = _retention(proj3, b0, bsz, seq_len, ret0, nh_r, dh_r, log_gamma, jnp.cos(ang), jnp.sin(ang))
        c_parts.append(c.reshape(bsz * seq_len, nh_r * dh_r))
    return _assemble(a_parts, z_parts, c_parts, p["grp_gain_a"][l], p["grp_gain_b"][l])


def _pick(n, candidates):
    for c in candidates:
        if n % c == 0:
            return c
    raise ValueError(f"no block size for {n}")


def kernel(x_prompt, x_sample, ln_in_g, ln_in_b, w_in, na_rpb, hy_conv_w, hy_conv_b, hy_f_w1, hy_f_b1,
           hy_f_freq, hy_f_w2, hy_f_b2, hy_f_w3, hy_f_b3, hy_skip, ret_decay_exp, grp_gain_a, grp_gain_b,
           w_out, ln1_g, ln1_b, w_ffn_in, w_ffn_out, ln2_g, ln2_b):
    p = dict(na_rpb=na_rpb, hy_conv_w=hy_conv_w, hy_conv_b=hy_conv_b, hy_f_w1=hy_f_w1, hy_f_b1=hy_f_b1,
             hy_f_freq=hy_f_freq, hy_f_w2=hy_f_w2, hy_f_b2=hy_f_b2, hy_f_w3=hy_f_w3, hy_f_b3=hy_f_b3,
             hy_skip=hy_skip, ret_decay_exp=ret_decay_exp, grp_gain_a=grp_gain_a, grp_gain_b=grp_gain_b)
    depth, d_model, in_cols = w_in.shape
    nh_a = na_rpb.shape[1]
    wa = grp_gain_a.shape[1]
    ch = grp_gain_b.shape[1]
    order = hy_skip.shape[1]
    nh_r = ret_decay_exp.shape[2]
    wr = (in_cols - 3 * wa - (order + 1) * ch) // 4
    dims = (nh_a, wa // nh_a, ch, order, nh_r, wr // nh_r)
    d_ff = w_ffn_out.shape[1]
    alpha = (2.0 * depth) ** 0.25

    bp, lp, _ = x_prompt.shape
    bs, ls, _ = x_sample.shape
    requests = [(bp, lp, 0), (bs, ls, bp * lp)]
    m1 = bp * lp
    m = m1 + bs * ls
    bm = _pick(m, (1024, 512, 256))
    bm_ln = 256
    assert m1 % bm_ln == 0 and m % bm_ln == 0 and m1 % 512 == 0

    xf, xb = _layernorm_join(x_prompt.reshape(m1, d_model), x_sample.reshape(bs * ls, d_model),
                             ln_in_g.astype(F32), ln_in_b.astype(F32), bm_ln)
    w_in_b, w_out_b = w_in.astype(BF16), w_out.astype(BF16)
    w_ffn_in_b, w_ffn_out_b = w_ffn_in.astype(BF16), w_ffn_out.astype(BF16)
    resid = (xf,)
    for l in range(depth):
        proj = _matmul(xb, w_in_b, l, bm, _pick(in_cols, (1024, 512, 256)))
        mixed = _token_mixers(proj, requests, p, l, dims)
        bn_out = _pick(d_model, (1024, 512, 256) if len(resid) == 1 else (512, 256))
        y = _matmul_residual(mixed, w_out_b, l, resid, alpha, bm, bn_out)
        g1, b1 = ln1_g[l].astype(F32), ln1_b[l].astype(F32)
        xb, stats = _layernorm_stats(y, g1, b1, bm_ln)
        hmid = _matmul_swiglu(xb, w_ffn_in_b, l, _pick(m, (2048, 1024, 512, 256)), _pick(d_ff, (512, 256, 128)))
        y = _matmul_residual(hmid, w_ffn_out_b, l, (y, stats, g1, b1), alpha, _pick(m, (512, 256)),
                             _pick(d_model, (512, 256)))
        if l + 1 < depth:
            g2, b2 = ln2_g[l].astype(F32), ln2_b[l].astype(F32)
            xb, stats = _layernorm_stats(y, g2, b2, bm_ln)
            resid = (y, stats, g2, b2)
    o1, o2 = _layernorm_split(y, ln2_g[depth - 1].astype(F32), ln2_b[depth - 1].astype(F32), m1, bm_ln)
    return (o1.reshape(bp, lp, d_model), o2.reshape(bs, ls, d_model))
```

```python
import functools
import math

import numpy as np
import jax
import jax.numpy as jnp
from jax import lax
from jax.experimental import pallas as pl
from jax.experimental.pallas import tpu as pltpu

F32 = jnp.float32
BF16 = jnp.bfloat16

GRID_W = 64
WIN_R = 8
WIN_C = 16
ROPE_BASE = 10000.0
LN_EPS = 1e-5
RMS_EPS = 1e-6
NEG_INF = -1e30
HY_TARGET = 1e-2
HY_FAST_DECAY = 0.3
HY_SLOW_DECAY = 1.5
HY_MIN_DECAY = math.log(HY_TARGET) / HY_SLOW_DECAY
HY_MAX_DECAY = math.log(HY_TARGET) / HY_FAST_DECAY

V7X_VMEM_LIMIT = 56 * 1024 * 1024
LANES = 128
SUBLANES = 8
DFT_N2 = 128
DFT_GROUP = 32
NA_ROWS_PER_STEP = 8
RET_CHUNK = 256
RET_SUPER = 2048


def _cparams(sem, vmem=V7X_VMEM_LIMIT):
    return pltpu.CompilerParams(dimension_semantics=sem, vmem_limit_bytes=vmem)


def _pitch(rows):
    p = -(-rows // SUBLANES)
    return SUBLANES * (p + 1 - p % 2)


def _ln(x, g, b):
    mu = jnp.mean(x, -1, keepdims=True)
    xc = x - mu
    var = jnp.mean(xc * xc, -1, keepdims=True)
    return xc * lax.rsqrt(var + LN_EPS) * g + b


def _ln_stats_kernel(x_ref, g_ref, b_ref, ob_ref, st_ref):
    x = x_ref[...]
    mu = jnp.mean(x, -1, keepdims=True)
    xc = x - mu
    rstd = lax.rsqrt(jnp.mean(xc * xc, -1, keepdims=True) + LN_EPS)
    ob_ref[...] = (xc * rstd * g_ref[...] + b_ref[...]).astype(BF16)
    st_ref[:, :LANES] = jnp.broadcast_to(mu, (x.shape[0], LANES))
    st_ref[:, LANES:] = jnp.broadcast_to(rstd, (x.shape[0], LANES))


def _ln_join_kernel(x1_ref, x2_ref, g_ref, b_ref, of_ref, ob_ref, *, nb1):
    x = jnp.where(pl.program_id(0) < nb1, x1_ref[...], x2_ref[...])
    y = _ln(x, g_ref[...], b_ref[...])
    of_ref[...] = y
    ob_ref[...] = y.astype(BF16)


def _ln_split_kernel(x_ref, g_ref, b_ref, o1_ref, o2_ref, *, nb1):
    y = _ln(x_ref[...], g_ref[...], b_ref[...])

    @pl.when(pl.program_id(0) < nb1)
    def _():
        o1_ref[...] = y

    @pl.when(pl.program_id(0) >= nb1)
    def _():
        o2_ref[...] = y


def _first_rows(nb1):
    return lambda i: (jnp.minimum(i, nb1 - 1), 0)


def _second_rows(nb1):
    return lambda i: (jnp.maximum(i - nb1, 0), 0)


def _layernorm_join(x1, x2, g, b, bm=256):
    m1, d = x1.shape
    m = m1 + x2.shape[0]
    nb1 = m1 // bm
    return pl.pallas_call(
        functools.partial(_ln_join_kernel, nb1=nb1),
        grid=(m // bm,),
        in_specs=[pl.BlockSpec((bm, d), _first_rows(nb1)),
                  pl.BlockSpec((bm, d), _second_rows(nb1)),
                  pl.BlockSpec((1, d), lambda i: (0, 0)),
                  pl.BlockSpec((1, d), lambda i: (0, 0))],
        out_specs=[pl.BlockSpec((bm, d), lambda i: (i, 0)),
                   pl.BlockSpec((bm, d), lambda i: (i, 0))],
        out_shape=[jax.ShapeDtypeStruct((m, d), F32), jax.ShapeDtypeStruct((m, d), BF16)],
        compiler_params=_cparams(("arbitrary",)),
        name="layernorm_join",
    )(x1, x2, g.reshape(1, d), b.reshape(1, d))


def _layernorm_split(x, g, b, m1, bm=256):
    m, d = x.shape
    nb1 = m1 // bm
    return pl.pallas_call(
        functools.partial(_ln_split_kernel, nb1=nb1),
        grid=(m // bm,),
        in_specs=[pl.BlockSpec((bm, d), lambda i: (i, 0)),
                  pl.BlockSpec((1, d), lambda i: (0, 0)),
                  pl.BlockSpec((1, d), lambda i: (0, 0))],
        out_specs=[pl.BlockSpec((bm, d), _first_rows(nb1)),
                   pl.BlockSpec((bm, d), _second_rows(nb1))],
        out_shape=[jax.ShapeDtypeStruct((m1, d), F32), jax.ShapeDtypeStruct((m - m1, d), F32)],
        compiler_params=_cparams(("arbitrary",)),
        name="layernorm_split",
    )(x, g.reshape(1, d), b.reshape(1, d))


def _layernorm_stats(x, g, b, bm=256):
    m, d = x.shape
    return pl.pallas_call(
        _ln_stats_kernel,
        grid=(m // bm,),
        in_specs=[pl.BlockSpec((bm, d), lambda i: (i, 0)),
                  pl.BlockSpec((1, d), lambda i: (0, 0)),
                  pl.BlockSpec((1, d), lambda i: (0, 0))],
        out_specs=[pl.BlockSpec((bm, d), lambda i: (i, 0)),
                   pl.BlockSpec((bm, 2 * LANES), lambda i: (i, 0))],
        out_shape=[jax.ShapeDtypeStruct((m, d), BF16), jax.ShapeDtypeStruct((m, 2 * LANES), F32)],
        compiler_params=_cparams(("parallel",)),
        name="layernorm_stats",
    )(x, g.reshape(1, d), b.reshape(1, d))


def _mm_res_kernel(x_ref, w_ref, r_ref, o_ref, *, alpha):
    o_ref[...] = alpha * r_ref[...] + jnp.dot(x_ref[...], w_ref[...], preferred_element_type=F32)


def _mm_res_ln_kernel(x_ref, w_ref, y_ref, st_ref, g_ref, b_ref, o_ref, *, alpha):
    acc = jnp.dot(x_ref[...], w_ref[...], preferred_element_type=F32)
    mu = st_ref[:, :LANES]
    rstd = st_ref[:, LANES:]
    for c in range(y_ref.shape[1] // LANES):
        cols = slice(c * LANES, (c + 1) * LANES)
        r = (y_ref[:, cols] - mu) * rstd * g_ref[:, cols] + b_ref[:, cols]
        o_ref[:, cols] = alpha * r + acc[:, cols]


def _mm_swiglu_kernel(x_ref, wg_ref, wu_ref, o_ref):
    x = x_ref[...]
    g = jnp.dot(x, wg_ref[...], preferred_element_type=F32)
    u = jnp.dot(x, wu_ref[...], preferred_element_type=F32)
    o_ref[...] = (g / (1.0 + jnp.exp(-g)) * u).astype(o_ref.dtype)


def _mm_cast_kernel(x_ref, w_ref, *refs):
    n_cast = (len(refs) - 1) // 2
    o_ref = refs[n_cast]
    o_ref[...] = jnp.dot(x_ref[...], w_ref[...], preferred_element_type=F32).astype(o_ref.dtype)
    for src, dst in zip(refs[:n_cast], refs[n_cast + 1:]):
        dst[...] = src[...].astype(dst.dtype)


def _cast_rows(rows, n_steps):
    rb = 16
    while rows % rb or rows // rb > n_steps:
        rb += 16
        assert rb <= rows
    return rb


def _matmul(x, w, layer, bm, bn, cast=()):
    m, k = x.shape
    n = w.shape[2]
    nj = n // bn
    n_steps = (m // bm) * nj
    cast_in, cast_out, cast_shapes, cast_args = [], [], [], []
    for stack, lyr in cast:
        _, rows, cols = stack.shape
        rb = _cast_rows(rows, n_steps)
        last = rows // rb - 1
        cast_in.append(pl.BlockSpec((None, rb, cols),
                                    lambda i, j, lyr=lyr, last=last: (lyr, jnp.minimum(i * nj + j, last), 0)))
        cast_out.append(pl.BlockSpec((rb, cols), lambda i, j, last=last: (jnp.minimum(i * nj + j, last), 0)))
        cast_shapes.append(jax.ShapeDtypeStruct((rows, cols), BF16))
        cast_args.append(stack)
    outs = pl.pallas_call(
        _mm_cast_kernel,
        grid=(m // bm, nj),
        in_specs=[pl.BlockSpec((bm, k), lambda i, j: (i, 0)),
                  pl.BlockSpec((None, k, bn), lambda i, j: (layer, 0, j))] + cast_in,
        out_specs=[pl.BlockSpec((bm, bn), lambda i, j: (i, j))] + cast_out,
        out_shape=[jax.ShapeDtypeStruct((m, n), BF16)] + cast_shapes,
        compiler_params=_cparams(("arbitrary", "arbitrary")),
        name="matmul",
    )(x, w, *cast_args)
    return outs[0], outs[1:]


def _matmul_residual(x, w, layer, resid, alpha, bm, bn):
    m, k = x.shape
    n = w.shape[2]
    tile = pl.BlockSpec((bm, bn), lambda i, j: (i, j))
    if len(resid) == 1:
        body, resid_specs = _mm_res_kernel, [tile]
    else:
        body = _mm_res_ln_kernel
        resid = (resid[0], resid[1], resid[2].reshape(1, n), resid[3].reshape(1, n))
        resid_specs = [tile, pl.BlockSpec((bm, 2 * LANES), lambda i, j: (i, 0)),
                       pl.BlockSpec((1, bn), lambda i, j: (0, j)), pl.BlockSpec((1, bn), lambda i, j: (0, j))]
    return pl.pallas_call(
        functools.partial(body, alpha=alpha),
        grid=(m // bm, n // bn),
        in_specs=[pl.BlockSpec((bm, k), lambda i, j: (i, 0)),
                  pl.BlockSpec((None, k, bn), lambda i, j: (layer, 0, j))] + resid_specs,
        out_specs=tile,
        out_shape=jax.ShapeDtypeStruct((m, n), F32),
        compiler_params=_cparams(("parallel", "arbitrary")),
        name="matmul_residual",
    )(x, w, *resid)


def _matmul_swiglu(x, w, layer, bm, bn):
    m, k = x.shape
    f = w.shape[2] // 2
    nb = f // bn
    return pl.pallas_call(
        _mm_swiglu_kernel,
        grid=(m // bm, nb),
        in_specs=[pl.BlockSpec((bm, k), lambda i, j: (i, 0)),
                  pl.BlockSpec((None, k, bn), lambda i, j: (layer, 0, j)),
                  pl.BlockSpec((None, k, bn), lambda i, j: (layer, 0, j + nb))],
        out_specs=pl.BlockSpec((bm, bn), lambda i, j: (i, j)),
        out_shape=jax.ShapeDtypeStruct((m, f), BF16),
        compiler_params=_cparams(("parallel", "arbitrary")),
        name="matmul_swiglu",
    )(x, w, w)


def _na_bias_tables(rpb):
    nh = rpb.shape[0]
    qc = np.arange(GRID_W)[:, None]
    kc = np.arange(GRID_W)[None, :]
    win_start = np.clip(qc - WIN_C // 2, 0, GRID_W - WIN_C)
    col_ok = (kc >= win_start) & (kc < win_start + WIN_C)
    dc_idx = np.clip(kc - qc + (WIN_C - 1), 0, 2 * WIN_C - 2)
    onehot = (dc_idx.reshape(-1)[None, :] == np.arange(2 * WIN_C - 1)[:, None]).astype(np.float32)
    tiles = jnp.einsum("hdc,cq->hdq", rpb.astype(F32), jnp.asarray(onehot),
                       precision=lax.Precision.HIGHEST).reshape(nh, 2 * WIN_R - 1, GRID_W, GRID_W)
    tiles = jnp.where(jnp.asarray(col_ok)[None, None], tiles, NEG_INF)
    dr = np.arange(WIN_R)[:, None] + np.arange(WIN_R)[None, :]
    t = tiles[:, dr]
    return t.transpose(0, 1, 3, 2, 4).reshape(nh, WIN_R, GRID_W, WIN_R * GRID_W)


def _na_kernel(q_ref, k_ref, v_ref, bias_ref, o_ref, *, rows, scale):
    nk = WIN_R * GRID_W
    nq = NA_ROWS_PER_STEP * GRID_W
    hd = q_ref.shape[-1]

    def group(g, carry):
        ks, vs, bias = [], [], []
        for i in range(NA_ROWS_PER_STEP):
            r = g * NA_ROWS_PER_STEP + i
            row_start = jnp.clip(r - WIN_R // 2, 0, rows - WIN_R)
            k0 = pl.multiple_of(row_start * GRID_W, GRID_W)
            ks.append(k_ref[0, pl.ds(k0, nk), :])
            vs.append(v_ref[0, pl.ds(k0, nk), :])
            bias.append(bias_ref[row_start - r + (WIN_R - 1)])
        q_rows = pl.ds(pl.multiple_of(g * nq, nq), nq)
        q = q_ref[0, q_rows, :].reshape(NA_ROWS_PER_STEP, GRID_W, hd)
        s = jnp.einsum("rqd,rkd->rqk", q, jnp.stack(ks), preferred_element_type=F32)
        s = s * scale + jnp.stack(bias)
        m = jnp.max(s, -1, keepdims=True)
        p = jnp.exp(s - m)
        l = jnp.sum(p, -1, keepdims=True)
        o = jnp.einsum("rqk,rkd->rqd", p.astype(BF16), jnp.stack(vs), preferred_element_type=F32) / l
        o_ref[0, q_rows, :] = o.reshape(nq, hd).astype(o_ref.dtype)
        return carry

    lax.fori_loop(0, rows // NA_ROWS_PER_STEP, group, 0, unroll=2)


def _neighbourhood_attention(proj3, b0, bsz, seq_len, bias, nh, hd):
    rows = seq_len // GRID_W
    assert rows % NA_ROWS_PER_STEP == 0 and rows >= WIN_R
    return pl.pallas_call(
        functools.partial(_na_kernel, rows=rows, scale=hd ** -0.5),
        grid=(nh, bsz),
        in_specs=[pl.BlockSpec((1, seq_len, hd), lambda h, b: (b0 + b, 0, h)),
                  pl.BlockSpec((1, seq_len, hd), lambda h, b: (b0 + b, 0, nh + h)),
                  pl.BlockSpec((1, seq_len, hd), lambda h, b: (b0 + b, 0, 2 * nh + h)),
                  pl.BlockSpec((None, WIN_R, GRID_W, WIN_R * GRID_W), lambda h, b: (h, 0, 0, 0))],
        out_specs=pl.BlockSpec((1, seq_len, hd), lambda h, b: (b, 0, h)),
        out_shape=jax.ShapeDtypeStruct((bsz, seq_len, nh * hd), BF16),
        compiler_params=_cparams(("parallel", "parallel")),
        name="neighbourhood_attention",
    )(proj3, proj3, proj3, bias)


def _conv3_kernel(x_ref, w_ref, b_ref, o_ref, *, seq_len, rb):
    w = w_ref[...]
    bias = b_ref[...]
    n_chunks = seq_len // rb

    def body(c, carry):
        r0 = pl.multiple_of(c * rb, rb)
        x = x_ref[0, pl.ds(r0, rb), :].astype(F32)
        row = lax.broadcasted_iota(jnp.int32, x.shape, 0)
        p0 = pl.multiple_of(jnp.maximum(r0 - 16, 0), 16)
        n0 = pl.multiple_of(jnp.minimum(r0 + rb, seq_len - 16), 16)
        prev_row = x_ref[0, pl.ds(p0, 16), :].astype(F32)[15:16]
        next_row = x_ref[0, pl.ds(n0, 16), :].astype(F32)[0:1]
        prev_row = jnp.where(c == 0, 0.0, prev_row)
        next_row = jnp.where(c == n_chunks - 1, 0.0, next_row)
        up = jnp.where(row == 0, prev_row, pltpu.roll(x, 1, 0))
        down = jnp.where(row == rb - 1, next_row, pltpu.roll(x, rb - 1, 0))
        y = up * w[0:1] + x * w[1:2] + down * w[2:3] + bias
        o_ref[0, pl.ds(r0, rb), :] = y.astype(o_ref.dtype)
        return carry

    lax.fori_loop(0, n_chunks, body, 0)


def _short_conv3(proj3, b0, bsz, seq_len, col0, w, b, cb=256, rb=512):
    width = w.shape[1]
    rb = min(rb, seq_len)
    c0 = col0 // cb
    return pl.pallas_call(
        functools.partial(_conv3_kernel, seq_len=seq_len, rb=rb),
        grid=(bsz, width // cb),
        in_specs=[pl.BlockSpec((1, seq_len, cb), lambda bi, c: (b0 + bi, 0, c0 + c)),
                  pl.BlockSpec((3, cb), lambda bi, c: (0, c)),
                  pl.BlockSpec((1, cb), lambda bi, c: (0, c))],
        out_specs=pl.BlockSpec((1, seq_len, cb), lambda bi, c: (bi, 0, c)),
        out_shape=jax.ShapeDtypeStruct((bsz, seq_len, width), BF16),
        compiler_params=_cparams(("parallel", "parallel")),
        name="hyena_short_conv",
    )(proj3, w, b.reshape(1, width))


def _filter_kernel(z_ref, w1_ref, b1_ref, fr_ref, w2_ref, b2_ref, w3_ref, b3_ref, dl_ref, o_ref, *,
                   seq_len, rb):
    i = pl.program_id(0)
    hi = lax.Precision.HIGHEST
    z = z_ref[...]
    h = jnp.sin(fr_ref[0:1] * (jnp.dot(z, w1_ref[...], precision=hi, preferred_element_type=F32)
                               + b1_ref[...]))
    h = jnp.sin(fr_ref[1:2] * (jnp.dot(h, w2_ref[...], precision=hi, preferred_element_type=F32)
                               + b2_ref[...]))
    ch = dl_ref.shape[-1]
    n = i * rb + lax.broadcasted_iota(jnp.int32, (rb, ch), 0)
    sign = jnp.where(n < seq_len, 1.0, jnp.where(n == seq_len, 0.0, -1.0))
    window = jnp.exp(-z[:, 0:1] * dl_ref[...]) * sign
    hb = h.astype(BF16)
    for o in range(w3_ref.shape[0]):
        f = jnp.dot(hb, w3_ref[o].astype(BF16), preferred_element_type=F32) + b3_ref[o]
        o_ref[:, o * ch:(o + 1) * ch] = (f * window).astype(o_ref.dtype)


def _hyena_filter_signal(seq_len, w1, b1, freq, w2, b2, w3, b3, order, ch, rb=512):
    n = 2 * seq_len
    emb, hid = w1.shape
    pad = LANES
    t = jnp.linspace(0.0, 1.0, seq_len, dtype=F32)[:, None]
    bands = (emb - 1) // 2
    fr = jnp.linspace(1e-4, bands - 1, bands, dtype=F32)[None, :]
    wpos = 2.0 * math.pi * jnp.arange(seq_len, dtype=F32)[:, None] / seq_len
    z = jnp.concatenate([t, jnp.cos(fr * wpos), -jnp.sin(fr * wpos)], axis=-1)
    pos = np.arange(n)
    src = np.clip(np.where(pos < seq_len, pos, n - pos), 0, seq_len - 1)
    z2 = jnp.pad(z[src], ((0, 0), (0, pad - emb)))
    w1p = jnp.pad(w1.astype(F32), ((0, pad - emb), (0, pad - hid)))
    b1p = jnp.pad(b1.astype(F32), (0, pad - hid)).reshape(1, pad)
    frp = jnp.pad(freq.astype(F32), ((0, 0), (0, pad - hid)), constant_values=1.0)
    w2p = jnp.pad(w2.astype(F32), ((0, pad - hid), (0, pad - hid)))
    b2p = jnp.pad(b2.astype(F32), (0, pad - hid)).reshape(1, pad)
    w3p = jnp.pad(w3.astype(F32), ((0, pad - hid), (0, 0))).reshape(pad, order, 2, ch).transpose(2, 1, 0, 3)
    b3p = b3.astype(F32).reshape(order, 2, 1, ch).transpose(1, 0, 2, 3)
    deltas = jnp.abs(jnp.linspace(HY_MIN_DECAY, HY_MAX_DECAY, ch, dtype=F32)).reshape(1, ch)
    nb = n // rb
    half = nb // 2
    full = lambda i: (0, 0)
    return pl.pallas_call(
        functools.partial(_filter_kernel, seq_len=seq_len, rb=rb),
        grid=(nb,),
        in_specs=[pl.BlockSpec((rb, pad), lambda i: (i, 0)),
                  pl.BlockSpec((pad, pad), full), pl.BlockSpec((1, pad), full),
                  pl.BlockSpec((2, pad), full),
                  pl.BlockSpec((pad, pad), full), pl.BlockSpec((1, pad), full),
                  pl.BlockSpec((None, order, pad, ch), lambda i: (i // half, 0, 0, 0)),
                  pl.BlockSpec((None, order, 1, ch), lambda i: (i // half, 0, 0, 0)),
                  pl.BlockSpec((1, ch), full)],
        out_specs=pl.BlockSpec((rb, order * ch), lambda i: (i, 0)),
        out_shape=jax.ShapeDtypeStruct((n, order * ch), BF16),
        compiler_params=_cparams(("parallel",)),
        name="hyena_filter_mlp",
    )(z2, w1p, b1p, frp, w2p, b2p, w3p, b3p, deltas)


def _dft_tables(seq_len):
    n = 2 * seq_len
    n2 = DFT_N2
    n1 = n // n2
    hk = n2 // 2
    kg = min(n1, DFT_GROUP)
    ng = n1 // kg
    pi = math.pi
    k1 = jnp.arange(n1, dtype=jnp.int32)
    ph1 = ((2 * k1[:, None] + 1) * k1[None, :]) % (2 * n1)
    th1 = ph1.astype(F32) * (pi / n1)
    c1 = jnp.cos(th1).reshape(ng, kg, n1)
    s1 = jnp.sin(th1).reshape(ng, kg, n1)
    f1 = jnp.concatenate([c1, -s1], axis=1)
    fb = jnp.concatenate([c1, -s1], axis=1).transpose(0, 2, 1)[:, : n1 // 2] * (2.0 / n)
    k = k1[:, None, None] + n1 * jnp.arange(hk, dtype=jnp.int32)[None, :, None]
    ph2 = ((2 * k + 1) * jnp.arange(n2, dtype=jnp.int32)[None, None, :]) % (2 * n)
    th2 = ph2.astype(F32) * (pi / n)
    c2, s2 = jnp.cos(th2), jnp.sin(th2)
    g = jnp.concatenate([jnp.concatenate([c2, s2], axis=2),
                         jnp.concatenate([-s2, c2], axis=2)], axis=1)
    c2t, s2t = c2.transpose(0, 2, 1), s2.transpose(0, 2, 1)
    h = jnp.concatenate([jnp.concatenate([c2t, -s2t], axis=2),
                         jnp.concatenate([s2t, c2t], axis=2)], axis=1)
    nl = 2 if n1 <= DFT_GROUP else 1
    return dict(n1=n1, kg=kg, ng=ng, nl=nl, f1_full=f1.astype(BF16), f1_half=f1[:, :, : n1 // 2].astype(BF16),
                fb=fb.astype(BF16), g=g.astype(BF16), h=h.astype(BF16))


def _load_lanes(ref, rows):
    parts = [ref[s, rows, :] for s in range(ref.shape[0])]
    return parts[0] if len(parts) == 1 else jnp.concatenate(parts, axis=1)


def _store_lanes(ref, rows, val):
    for s in range(ref.shape[0]):
        ref[s, rows, :] = val[:, s * LANES:(s + 1) * LANES]


def _fill_slabs(x_ref, xs, slabs, pitch):
    def body(n1, carry):
        src = pl.ds(pl.multiple_of(n1 * DFT_N2, DFT_N2), DFT_N2)
        _store_lanes(xs, pl.ds(pl.multiple_of(n1 * pitch, SUBLANES), DFT_N2), x_ref[src, :].astype(F32))
        return carry
    lax.fori_loop(0, slabs, body, 0, unroll=4)


def _slab_stage(xs, a_s, f1, slabs, p_x, p_a):
    rows = f1.shape[0]

    def body(i, carry):
        xn = _load_lanes(xs, pl.ds(i, slabs, stride=p_x)).astype(BF16)
        _store_lanes(a_s, pl.ds(pl.multiple_of(i * p_a, SUBLANES), rows),
                     jnp.dot(f1, xn, preferred_element_type=F32))
        return carry
    lax.fori_loop(0, DFT_N2, body, 0, unroll=32)


def _load_slab_freq(a_s, j, kg, p_a):
    are = _load_lanes(a_s, pl.ds(j, DFT_N2, stride=p_a))
    aim = _load_lanes(a_s, pl.ds(kg + j, DFT_N2, stride=p_a))
    return jnp.concatenate([are, aim], axis=0).astype(BF16)


def _filter_spec_kernel(x_ref, f1_ref, g_ref, o_ref, xs, a_s, *, slabs, kg):
    grp = pl.program_id(1)
    p_x, p_a = _pitch(DFT_N2), _pitch(2 * kg)

    @pl.when(grp == 0)
    def _():
        _fill_slabs(x_ref, xs, slabs, p_x)

    _slab_stage(xs, a_s, f1_ref[0], slabs, p_x, p_a)

    def body(j, carry):
        o_ref[j] = jnp.dot(g_ref[j], _load_slab_freq(a_s, j, kg, p_a), preferred_element_type=F32)
        return carry
    lax.fori_loop(0, kg, body, 0, unroll=8)


def _filter_spectrum(f, tables):
    n, cols = f.shape
    n1, kg, ng, nl = tables["n1"], tables["kg"], tables["ng"], tables["nl"]
    p_x, p_a = _pitch(DFT_N2), _pitch(2 * kg)
    wl = nl * LANES
    return pl.pallas_call(
        functools.partial(_filter_spec_kernel, slabs=n1, kg=kg),
        grid=(cols // wl, ng),
        in_specs=[pl.BlockSpec((n, wl), lambda c, g: (0, c)),
                  pl.BlockSpec((1, 2 * kg, n1), lambda c, g: (g, 0, 0)),
                  pl.BlockSpec((kg, DFT_N2, 2 * DFT_N2), lambda c, g: (g, 0, 0))],
        out_specs=pl.BlockSpec((kg, DFT_N2, wl), lambda c, g: (g, 0, c)),
        out_shape=jax.ShapeDtypeStruct((n1, DFT_N2, cols), F32),
        scratch_shapes=[pltpu.VMEM((nl, n1 * p_x, LANES), F32),
                        pltpu.VMEM((nl, DFT_N2 * p_a, LANES), F32)],
        compiler_params=_cparams(("parallel", "arbitrary")),
        name="hyena_filter_spectrum",
    )(f, tables["f1_full"], tables["g"])


def _long_conv_kernel(x_ref, gate_ref, f1_ref, g_ref, h_ref, kf_ref, fb_ref, skip_ref, o_ref,
                      xs, a_s, z_s, y_s, *, slabs, kg, ng):
    grp = pl.program_id(2)
    n2 = DFT_N2
    hk = n2 // 2
    p_x, p_a, p_z, p_y = _pitch(n2), _pitch(2 * kg), _pitch(2 * n2), _pitch(slabs)

    @pl.when(grp == 0)
    def _():
        _fill_slabs(x_ref.at[0], xs, slabs, p_x)
        y_s[...] = jnp.zeros_like(y_s)

    _slab_stage(xs, a_s, f1_ref[0], slabs, p_x, p_a)

    def freq_body(j, carry):
        x = jnp.dot(g_ref[j], _load_slab_freq(a_s, j, kg, p_a), preferred_element_type=F32)
        kf = kf_ref[j]
        xr, xi = x[:hk], x[hk:]
        kr, ki = kf[:hk], kf[hk:]
        y = jnp.concatenate([xr * kr - xi * ki, xr * ki + xi * kr], axis=0)
        _store_lanes(z_s, pl.ds(pl.multiple_of(j * p_z, SUBLANES), n2), y)
        return carry
    lax.fori_loop(0, kg, freq_body, 0, unroll=8)

    def inv_freq_body(j, carry):
        y = _load_lanes(z_s, pl.ds(pl.multiple_of(j * p_z, SUBLANES), n2)).astype(BF16)
        _store_lanes(z_s, pl.ds(pl.multiple_of(j * p_z, SUBLANES), 2 * n2),
                     jnp.dot(h_ref[j], y, preferred_element_type=F32))
        return carry
    lax.fori_loop(0, kg, inv_freq_body, 0, unroll=8)

    fb = fb_ref[0]

    def inv_body(t, carry):
        zre = _load_lanes(z_s, pl.ds(t, kg, stride=p_z))
        zim = _load_lanes(z_s, pl.ds(n2 + t, kg, stride=p_z))
        zz = jnp.concatenate([zre, zim], axis=0).astype(BF16)
        dst = pl.ds(pl.multiple_of(t * p_y, SUBLANES), slabs)
        _store_lanes(y_s, dst, _load_lanes(y_s, dst) + jnp.dot(fb, zz, preferred_element_type=F32))
        return carry
    lax.fori_loop(0, n2, inv_body, 0, unroll=32)

    @pl.when(grp == ng - 1)
    def _():
        skip = skip_ref[...]

        def out_body(t1, carry):
            y = _load_lanes(y_s, pl.ds(t1, n2, stride=p_y))
            u = _load_lanes(xs, pl.ds(pl.multiple_of(t1 * p_x, SUBLANES), n2))
            rows = pl.ds(pl.multiple_of(t1 * n2, n2), n2)
            gate = gate_ref[0, rows, :].astype(F32)
            o_ref[0, rows, :] = ((y + skip * u) * gate).astype(o_ref.dtype)
            return carry
        lax.fori_loop(0, slabs, out_body, 0, unroll=4)


def _long_conv_gate(u, u_blk0, gate, gate_blk0, tables, kf, kf_blk0, skip, ch):
    bsz, seq_len, _ = u.shape
    n1, kg, ng, nl = tables["n1"], tables["kg"], tables["ng"], tables["nl"]
    slabs = n1 // 2
    p_x, p_a, p_z, p_y = _pitch(DFT_N2), _pitch(2 * kg), _pitch(2 * DFT_N2), _pitch(slabs)
    wl = nl * LANES
    return pl.pallas_call(
        functools.partial(_long_conv_kernel, slabs=slabs, kg=kg, ng=ng),
        grid=(bsz, ch // wl, ng),
        in_specs=[pl.BlockSpec((1, seq_len, wl), lambda b, c, g: (b, 0, u_blk0 + c)),
                  pl.BlockSpec((1, seq_len, wl), lambda b, c, g: (b, 0, gate_blk0 + c)),
                  pl.BlockSpec((1, 2 * kg, slabs), lambda b, c, g: (g, 0, 0)),
                  pl.BlockSpec((kg, DFT_N2, 2 * DFT_N2), lambda b, c, g: (g, 0, 0)),
                  pl.BlockSpec((kg, 2 * DFT_N2, DFT_N2), lambda b, c, g: (g, 0, 0)),
                  pl.BlockSpec((kg, DFT_N2, wl), lambda b, c, g: (g, 0, kf_blk0 + c)),
                  pl.BlockSpec((1, slabs, 2 * kg), lambda b, c, g: (g, 0, 0)),
                  pl.BlockSpec((1, wl), lambda b, c, g: (0, c))],
        out_specs=pl.BlockSpec((1, seq_len, wl), lambda b, c, g: (b, 0, c)),
        out_shape=jax.ShapeDtypeStruct((bsz, seq_len, ch), BF16),
        scratch_shapes=[pltpu.VMEM((nl, slabs * p_x, LANES), F32),
                        pltpu.VMEM((nl, DFT_N2 * p_a, LANES), F32),
                        pltpu.VMEM((nl, kg * p_z, LANES), F32),
                        pltpu.VMEM((nl, DFT_N2 * p_y, LANES), F32)],
        compiler_params=_cparams(("parallel", "parallel", "arbitrary")),
        name="hyena_long_conv",
    )(u, gate, tables["f1_half"], tables["g"], tables["h"], kf, tables["fb"],
      skip.astype(F32).reshape(1, ch))


def _hyena(hy, tables, kf, skip, order, ch):
    nblk = ch // (tables["nl"] * LANES)
    z, z_blk0 = hy, 0
    for o in range(order):
        z = _long_conv_gate(z, z_blk0, hy, (o + 1) * nblk, tables, kf, o * nblk, skip[o], ch)
    return z


def _rope(x, cos, sin):
    half = x.shape[-1] // 2
    x1, x2 = x[:, :half], x[:, half:]
    return jnp.concatenate([x1 * cos - x2 * sin, x1 * sin + x2 * cos], axis=-1)


def _ret_kernel(lg_ref, q_ref, k_ref, v_ref, g_ref, cos_ref, sin_ref, o_ref,
                sb_all, kr_all, kz_all, sf_ref, sb_ref, *, n_super, chunks_per_super, k_scale):
    h = pl.program_id(1)
    s = pl.program_id(2)
    cc = RET_CHUNK
    dh = q_ref.shape[-1]
    lgf = lg_ref[0, h]
    lgb = lg_ref[1, h]
    row = lax.broadcasted_iota(jnp.int32, (cc, dh), 0).astype(F32)
    zeta_f = jnp.exp(lgf * (cc - 1.0 - row))

    @pl.when(s < n_super)
    def _backward_sweep():
        @pl.when(s == 0)
        def _():
            sb_ref[...] = jnp.zeros_like(sb_ref)

        sup = n_super - 1 - s
        zeta_b = jnp.exp(lgb * row)
        chunk_decay = jnp.exp(lgb * jnp.full((1, dh), float(cc), F32))

        def body(t, carry):
            c = chunks_per_super - 1 - t
            c0 = pl.multiple_of(c * cc, cc)
            n = sup * chunks_per_super + c
            sb_all[n] = sb_ref[...].astype(BF16)
            k = _rope(k_ref[0, pl.ds(c0, cc), :].astype(F32), cos_ref[pl.ds(c0, cc), :],
                      sin_ref[pl.ds(c0, cc), :]) * k_scale
            kr_all[n] = k.astype(BF16)
            kz_all[n] = (k * zeta_f).astype(BF16)
            kz = (k * zeta_b).astype(BF16)
            v = v_ref[0, pl.ds(c0, cc), :]
            upd = lax.dot_general(kz, v, (((0,), (0,)), ((), ())), preferred_element_type=F32)
            sb_ref[...] = sb_ref[...] * chunk_decay + upd
            return carry

        lax.fori_loop(0, chunks_per_super, body, 0, unroll=4)

    @pl.when(s >= n_super)
    def _forward_sweep():
        @pl.when(s == n_super)
        def _():
            sf_ref[...] = jnp.zeros_like(sf_ref)

        sup = s - n_super
        col = lax.broadcasted_iota(jnp.int32, (cc, cc), 1).astype(F32)
        rowc = lax.broadcasted_iota(jnp.int32, (cc, cc), 0).astype(F32)
        diff = rowc - col
        inner_decay = jnp.where(diff >= 0, jnp.exp(lgf * jnp.maximum(diff, 0.0)),
                                jnp.exp(lgb * jnp.maximum(-diff, 0.0)))
        xi_f = jnp.exp(lgf * (row + 1.0))
        xi_b = jnp.exp(lgb * (cc - row))
        chunk_decay = jnp.exp(lgf * jnp.full((1, dh), float(cc), F32))

        def body(c, carry):
            c0 = pl.multiple_of(c * cc, cc)
            n = sup * chunks_per_super + c
            q = _rope(q_ref[0, pl.ds(c0, cc), :].astype(F32), cos_ref[pl.ds(c0, cc), :],
                      sin_ref[pl.ds(c0, cc), :]).astype(BF16)
            v = v_ref[0, pl.ds(c0, cc), :]
            sc = lax.dot_general(q, kr_all[n], (((1,), (1,)), ((), ())),
                                 preferred_element_type=F32) * inner_decay
            ret = jnp.dot(sc.astype(BF16), v, preferred_element_type=F32)
            ret = ret + xi_f * jnp.dot(q, sf_ref[...].astype(BF16), preferred_element_type=F32)
            ret = ret + xi_b * jnp.dot(q, sb_all[n], preferred_element_type=F32)
            upd = lax.dot_general(kz_all[n], v, (((0,), (0,)), ((), ())), preferred_element_type=F32)
            sf_ref[...] = sf_ref[...] * chunk_decay + upd
            ret = ret * lax.rsqrt(jnp.mean(ret * ret, -1, keepdims=True) + RMS_EPS)
            gate = g_ref[0, pl.ds(c0, cc), :].astype(F32)
            o_ref[0, pl.ds(c0, cc), :] = (ret * (gate / (1.0 + jnp.exp(-gate)))).astype(o_ref.dtype)
            return carry

        lax.fori_loop(0, chunks_per_super, body, 0, unroll=2)


def _retention(proj3, b0, bsz, seq_len, col0, nh, dh, log_gamma, cos, sin):
    sup_len = min(RET_SUPER, seq_len)
    n_super = seq_len // sup_len
    cps = sup_len // RET_CHUNK
    cb0 = col0 // dh

    def fwd_idx(s):
        return jnp.maximum(s - n_super, 0)

    def kv_idx(s):
        return jnp.where(s < n_super, n_super - 1 - s, s - n_super)

    def k_idx(s):
        return jnp.maximum(n_super - 1 - s, 0)

    n_chunks = seq_len // RET_CHUNK
    return pl.pallas_call(
        functools.partial(_ret_kernel, n_super=n_super, chunks_per_super=cps, k_scale=dh ** -0.5),
        grid=(bsz, nh, 2 * n_super),
        in_specs=[pl.BlockSpec(memory_space=pltpu.SMEM),
                  pl.BlockSpec((1, sup_len, dh), lambda b, h, s: (b0 + b, fwd_idx(s), cb0 + h)),
                  pl.BlockSpec((1, sup_len, dh), lambda b, h, s: (b0 + b, k_idx(s), cb0 + nh + h)),
                  pl.BlockSpec((1, sup_len, dh), lambda b, h, s: (b0 + b, kv_idx(s), cb0 + 2 * nh + h)),
                  pl.BlockSpec((1, sup_len, dh), lambda b, h, s: (b0 + b, fwd_idx(s), cb0 + 3 * nh + h)),
                  pl.BlockSpec((sup_len, dh // 2), lambda b, h, s: (kv_idx(s), 0)),
                  pl.BlockSpec((sup_len, dh // 2), lambda b, h, s: (kv_idx(s), 0))],
        out_specs=pl.BlockSpec((1, sup_len, dh), lambda b, h, s: (b, fwd_idx(s), h)),
        out_shape=jax.ShapeDtypeStruct((bsz, seq_len, nh * dh), BF16),
        scratch_shapes=[pltpu.VMEM((n_chunks, dh, dh), BF16),
                        pltpu.VMEM((n_chunks, RET_CHUNK, dh), BF16),
                        pltpu.VMEM((n_chunks, RET_CHUNK, dh), BF16),
                        pltpu.VMEM((dh, dh), F32),
                        pltpu.VMEM((dh, dh), F32)],
        compiler_params=_cparams(("parallel", "parallel", "arbitrary")),
        name="retention",
    )(log_gamma, proj3, proj3, proj3, proj3, cos, sin)


def _assemble_kernel(a1_ref, a2_ref, z1_ref, z2_ref, c1_ref, c2_ref, ga_ref, gb_ref, o_ref, *, nb1):
    first = pl.program_id(0) < nb1
    wa = a1_ref.shape[-1]
    wb = z1_ref.shape[-1]
    a = jnp.where(first, a1_ref[...], a2_ref[...]).astype(F32)
    a = a * lax.rsqrt(jnp.mean(a * a, -1, keepdims=True) + RMS_EPS) * ga_ref[...]
    z = jnp.where(first, z1_ref[...], z2_ref[...]).astype(F32)
    z = z * lax.rsqrt(jnp.mean(z * z, -1, keepdims=True) + RMS_EPS) * gb_ref[...]
    o_ref[:, :wa] = a.astype(o_ref.dtype)
    o_ref[:, wa:wa + wb] = z.astype(o_ref.dtype)
    o_ref[:, wa + wb:] = jnp.where(first, c1_ref[...], c2_ref[...])


def _assemble(a_parts, z_parts, c_parts, gain_a, gain_b, bm=512):
    m1, wa = a_parts[0].shape
    m = m1 + a_parts[1].shape[0]
    wb, wc = z_parts[0].shape[1], c_parts[0].shape[1]
    nb1 = m1 // bm
    part_specs = []
    for w in (wa, wb, wc):
        part_specs += [pl.BlockSpec((bm, w), _first_rows(nb1)), pl.BlockSpec((bm, w), _second_rows(nb1))]
    return pl.pallas_call(
        functools.partial(_assemble_kernel, nb1=nb1),
        grid=(m // bm,),
        in_specs=part_specs + [pl.BlockSpec((1, wa), lambda i: (0, 0)),
                               pl.BlockSpec((1, wb), lambda i: (0, 0))],
        out_specs=pl.BlockSpec((bm, wa + wb + wc), lambda i: (i, 0)),
        out_shape=jax.ShapeDtypeStruct((m, wa + wb + wc), BF16),
        compiler_params=_cparams(("arbitrary",)),
        name="assemble_mixers",
    )(*a_parts, *z_parts, *c_parts, gain_a.astype(F32).reshape(1, wa), gain_b.astype(F32).reshape(1, wb))


def _token_mixers(proj, requests, p, l, dims):
    nh_a, hd_a, ch, order, nh_r, dh_r = dims
    in_cols = proj.shape[1]
    total = proj.shape[0]
    wa = nh_a * hd_a
    hy0 = 3 * wa
    ret0 = hy0 + (order + 1) * ch
    log_gamma = jnp.log1p(-jnp.exp2(-p["ret_decay_exp"][l].astype(F32)))
    a_parts, z_parts, c_parts = [], [], []
    bias = _na_bias_tables(p["na_rpb"][l])
    for (bsz, seq_len, tok0) in requests:
        assert tok0 % seq_len == 0 and total % seq_len == 0
        proj3 = proj.reshape(total // seq_len, seq_len, in_cols)
        b0 = tok0 // seq_len
        a = _neighbourhood_attention(proj3, b0, bsz, seq_len, bias, nh_a, hd_a)
        a_parts.append(a.reshape(bsz * seq_len, wa))
        tables = _dft_tables(seq_len)
        f = _hyena_filter_signal(seq_len, p["hy_f_w1"][l], p["hy_f_b1"][l], p["hy_f_freq"][l],
                                 p["hy_f_w2"][l], p["hy_f_b2"][l], p["hy_f_w3"][l], p["hy_f_b3"][l],
                                 order, ch)
        kf = _filter_spectrum(f, tables)
        hy = _short_conv3(proj3, b0, bsz, seq_len, hy0, p["hy_conv_w"][l].astype(F32),
                          p["hy_conv_b"][l].astype(F32))
        z = _hyena(hy, tables, kf, p["hy_skip"][l], order, ch)
        z_parts.append(z.reshape(bsz * seq_len, ch))
        half = dh_r // 2
        inv_freq = 1.0 / (ROPE_BASE ** jnp.linspace(0.0, 1.0, half, dtype=F32))
        ang = jnp.arange(seq_len, dtype=F32)[:, None] * inv_freq[None, :]
        c = _retention(proj3, b0, bsz, seq_len, ret0, nh_r, dh_r, log_gamma, jnp.cos(ang), jnp.sin(ang))
        c_parts.append(c.reshape(bsz * seq_len, nh_r * dh_r))
    return _assemble(a_parts, z_parts, c_parts, p["grp_gain_a"][l], p["grp_gain_b"][l])


def _pick(n, candidates):
    for c in candidates:
        if n % c == 0:
            return c
    raise ValueError(f"no block size for {n}")


def kernel(x_prompt, x_sample, ln_in_g, ln_in_b, w_in, na_rpb, hy_conv_w, hy_conv_b, hy_f_w1, hy_f_b1,
           hy_f_freq, hy_f_w2, hy_f_b2, hy_f_w3, hy_f_b3, hy_skip, ret_decay_exp, grp_gain_a, grp_gain_b,
           w_out, ln1_g, ln1_b, w_ffn_in, w_ffn_out, ln2_g, ln2_b):
    p = dict(na_rpb=na_rpb, hy_conv_w=hy_conv_w, hy_conv_b=hy_conv_b, hy_f_w1=hy_f_w1, hy_f_b1=hy_f_b1,
             hy_f_freq=hy_f_freq, hy_f_w2=hy_f_w2, hy_f_b2=hy_f_b2, hy_f_w3=hy_f_w3, hy_f_b3=hy_f_b3,
             hy_skip=hy_skip, ret_decay_exp=ret_decay_exp, grp_gain_a=grp_gain_a, grp_gain_b=grp_gain_b)
    depth, d_model, in_cols = w_in.shape
    nh_a = na_rpb.shape[1]
    wa = grp_gain_a.shape[1]
    ch = grp_gain_b.shape[1]
    order = hy_skip.shape[1]
    nh_r = ret_decay_exp.shape[2]
    wr = (in_cols - 3 * wa - (order + 1) * ch) // 4
    dims = (nh_a, wa // nh_a, ch, order, nh_r, wr // nh_r)
    d_ff = w_ffn_out.shape[1]
    alpha = (2.0 * depth) ** 0.25

    bp, lp, _ = x_prompt.shape
    bs, ls, _ = x_sample.shape
    requests = [(bp, lp, 0), (bs, ls, bp * lp)]
    m1 = bp * lp
    m = m1 + bs * ls
    bm = _pick(m, (1024, 512, 256))
    bm_ln = 256
    assert m1 % bm_ln == 0 and m % bm_ln == 0 and m1 % 512 == 0

    xf, xb = _layernorm_join(x_prompt.reshape(m1, d_model), x_sample.reshape(bs * ls, d_model),
                             ln_in_g.astype(F32), ln_in_b.astype(F32), bm_ln)
    w_in_b = w_in[0:1].astype(BF16)
    resid = (xf,)
    for l in range(depth):
        cast = [(w_out, l), (w_ffn_in, l), (w_ffn_out, l)] + ([(w_in, l + 1)] if l + 1 < depth else [])
        proj, cast_w = _matmul(xb, w_in_b, 0, bm, _pick(in_cols, (1024, 512, 256)), cast)
        w_out_b, w_ffn_in_b, w_ffn_out_b = (c[None] for c in cast_w[:3])
        if l + 1 < depth:
            w_in_b = cast_w[3][None]
        mixed = _token_mixers(proj, requests, p, l, dims)
        bn_out = _pick(d_model, (1024, 512, 256) if len(resid) == 1 else (512, 256))
        y = _matmul_residual(mixed, w_out_b, 0, resid, alpha, bm, bn_out)
        g1, b1 = ln1_g[l].astype(F32), ln1_b[l].astype(F32)
        xb, stats = _layernorm_stats(y, g1, b1, bm_ln)
        hmid = _matmul_swiglu(xb, w_ffn_in_b, 0, _pick(m, (2048, 1024, 512, 256)), _pick(d_ff, (512, 256, 128)))
        y = _matmul_residual(hmid, w_ffn_out_b, 0, (y, stats, g1, b1), alpha, _pick(m, (512, 256)),
                             _pick(d_model, (512, 256)))
        if l + 1 < depth:
            g2, b2 = ln2_g[l].astype(F32), ln2_b[l].astype(F32)
            xb, stats = _layernorm_stats(y, g2, b2, bm_ln)
            resid = (y, stats, g2, b2)
    o1, o2 = _layernorm_split(y, ln2_g[depth - 1].astype(F32), ln2_b[depth - 1].astype(F32), m1, bm_ln)
    return (o1.reshape(bp, lp, d_model), o2.reshape(bs, ls, d_model))
```

```python
import functools
import math

import numpy as np
import jax
import jax.numpy as jnp
from jax import lax
from jax.experimental import pallas as pl
from jax.experimental.pallas import tpu as pltpu

F32 = jnp.float32
BF16 = jnp.bfloat16

GRID_W = 64
WIN_R = 8
WIN_C = 16
ROPE_BASE = 10000.0
LN_EPS = 1e-5
RMS_EPS = 1e-6
NEG_INF = -1e30
HY_TARGET = 1e-2
HY_FAST_DECAY = 0.3
HY_SLOW_DECAY = 1.5
HY_MIN_DECAY = math.log(HY_TARGET) / HY_SLOW_DECAY
HY_MAX_DECAY = math.log(HY_TARGET) / HY_FAST_DECAY

V7X_VMEM_LIMIT = 56 * 1024 * 1024
LANES = 128
SUBLANES = 8
DFT_N2 = 128
DFT_GROUP = 32
NA_ROWS_PER_STEP = 8
RET_CHUNK = 256
RET_SUPER = 2048


def _cparams(sem, vmem=V7X_VMEM_LIMIT):
    return pltpu.CompilerParams(dimension_semantics=sem, vmem_limit_bytes=vmem)


def _pitch(rows):
    p = -(-rows // SUBLANES)
    return SUBLANES * (p + 1 - p % 2)


def _ln(x, g, b):
    mu = jnp.mean(x, -1, keepdims=True)
    xc = x - mu
    var = jnp.mean(xc * xc, -1, keepdims=True)
    return xc * lax.rsqrt(var + LN_EPS) * g + b


def _ln_stats_kernel(x_ref, g_ref, b_ref, ob_ref, st_ref):
    x = x_ref[...]
    mu = jnp.mean(x, -1, keepdims=True)
    xc = x - mu
    rstd = lax.rsqrt(jnp.mean(xc * xc, -1, keepdims=True) + LN_EPS)
    ob_ref[...] = (xc * rstd * g_ref[...] + b_ref[...]).astype(BF16)
    st_ref[:, :LANES] = jnp.broadcast_to(mu, (x.shape[0], LANES))
    st_ref[:, LANES:] = jnp.broadcast_to(rstd, (x.shape[0], LANES))


def _ln_join_kernel(x1_ref, x2_ref, g_ref, b_ref, of_ref, ob_ref, *, nb1):
    x = jnp.where(pl.program_id(0) < nb1, x1_ref[...], x2_ref[...])
    y = _ln(x, g_ref[...], b_ref[...])
    of_ref[...] = y
    ob_ref[...] = y.astype(BF16)


def _ln_split_kernel(x_ref, g_ref, b_ref, o1_ref, o2_ref, *, nb1):
    y = _ln(x_ref[...], g_ref[...], b_ref[...])

    @pl.when(pl.program_id(0) < nb1)
    def _():
        o1_ref[...] = y

    @pl.when(pl.program_id(0) >= nb1)
    def _():
        o2_ref[...] = y


def _first_rows(nb1):
    return lambda i: (jnp.minimum(i, nb1 - 1), 0)


def _second_rows(nb1):
    return lambda i: (jnp.maximum(i - nb1, 0), 0)


def _layernorm_join(x1, x2, g, b, bm=256):
    m1, d = x1.shape
    m = m1 + x2.shape[0]
    nb1 = m1 // bm
    return pl.pallas_call(
        functools.partial(_ln_join_kernel, nb1=nb1),
        grid=(m // bm,),
        in_specs=[pl.BlockSpec((bm, d), _first_rows(nb1)),
                  pl.BlockSpec((bm, d), _second_rows(nb1)),
                  pl.BlockSpec((1, d), lambda i: (0, 0)),
                  pl.BlockSpec((1, d), lambda i: (0, 0))],
        out_specs=[pl.BlockSpec((bm, d), lambda i: (i, 0)),
                   pl.BlockSpec((bm, d), lambda i: (i, 0))],
        out_shape=[jax.ShapeDtypeStruct((m, d), F32), jax.ShapeDtypeStruct((m, d), BF16)],
        compiler_params=_cparams(("arbitrary",)),
        name="layernorm_join",
    )(x1, x2, g.reshape(1, d), b.reshape(1, d))


def _layernorm_split(x, g, b, m1, bm=256):
    m, d = x.shape
    nb1 = m1 // bm
    return pl.pallas_call(
        functools.partial(_ln_split_kernel, nb1=nb1),
        grid=(m // bm,),
        in_specs=[pl.BlockSpec((bm, d), lambda i: (i, 0)),
                  pl.BlockSpec((1, d), lambda i: (0, 0)),
                  pl.BlockSpec((1, d), lambda i: (0, 0))],
        out_specs=[pl.BlockSpec((bm, d), _first_rows(nb1)),
                   pl.BlockSpec((bm, d), _second_rows(nb1))],
        out_shape=[jax.ShapeDtypeStruct((m1, d), F32), jax.ShapeDtypeStruct((m - m1, d), F32)],
        compiler_params=_cparams(("arbitrary",)),
        name="layernorm_split",
    )(x, g.reshape(1, d), b.reshape(1, d))


def _layernorm_stats(x, g, b, bm=256):
    m, d = x.shape
    return pl.pallas_call(
        _ln_stats_kernel,
        grid=(m // bm,),
        in_specs=[pl.BlockSpec((bm, d), lambda i: (i, 0)),
                  pl.BlockSpec((1, d), lambda i: (0, 0)),
                  pl.BlockSpec((1, d), lambda i: (0, 0))],
        out_specs=[pl.BlockSpec((bm, d), lambda i: (i, 0)),
                   pl.BlockSpec((bm, 2 * LANES), lambda i: (i, 0))],
        out_shape=[jax.ShapeDtypeStruct((m, d), BF16), jax.ShapeDtypeStruct((m, 2 * LANES), F32)],
        compiler_params=_cparams(("parallel",)),
        name="layernorm_stats",
    )(x, g.reshape(1, d), b.reshape(1, d))


def _mm_res_kernel(x_ref, w_ref, r_ref, o_ref, *, alpha):
    o_ref[...] = alpha * r_ref[...] + jnp.dot(x_ref[...], w_ref[...], preferred_element_type=F32)


def _mm_res_ln_kernel(x_ref, w_ref, y_ref, st_ref, g_ref, b_ref, o_ref, *, alpha):
    acc = jnp.dot(x_ref[...], w_ref[...], preferred_element_type=F32)
    mu = st_ref[:, :LANES]
    rstd = st_ref[:, LANES:]
    for c in range(y_ref.shape[1] // LANES):
        cols = slice(c * LANES, (c + 1) * LANES)
        r = (y_ref[:, cols] - mu) * rstd * g_ref[:, cols] + b_ref[:, cols]
        o_ref[:, cols] = alpha * r + acc[:, cols]


def _mm_swiglu_kernel(x_ref, wg_ref, wu_ref, o_ref):
    x = x_ref[...]
    g = jnp.dot(x, wg_ref[...], preferred_element_type=F32)
    u = jnp.dot(x, wu_ref[...], preferred_element_type=F32)
    o_ref[...] = (g / (1.0 + jnp.exp(-g)) * u).astype(o_ref.dtype)


def _mm_cast_kernel(x_ref, w_ref, *refs):
    n_cast = (len(refs) - 1) // 2
    o_ref = refs[n_cast]
    o_ref[...] = jnp.dot(x_ref[...], w_ref[...], preferred_element_type=F32).astype(o_ref.dtype)
    for src, dst in zip(refs[:n_cast], refs[n_cast + 1:]):
        tile = dst.shape[-1]
        for t in range(dst.shape[0]):
            dst[t] = src[:, t * tile:(t + 1) * tile].astype(dst.dtype)


def _cast_rows(rows, n_steps):
    rb = 16
    while rows % rb or rows // rb > n_steps:
        rb += 16
        assert rb <= rows
    return rb


def _matmul(x, w, layer, bm, bn, cast=()):
    m, k = x.shape
    n = w.shape[2]
    nj = n // bn
    n_steps = (m // bm) * nj
    cast_in, cast_out, cast_shapes, cast_args = [], [], [], []
    for stack, lyr, tile in cast:
        _, rows, cols = stack.shape
        rb = _cast_rows(rows, n_steps)
        last = rows // rb - 1
        cast_in.append(pl.BlockSpec((None, rb, cols),
                                    lambda i, j, lyr=lyr, last=last: (lyr, jnp.minimum(i * nj + j, last), 0)))
        cast_out.append(pl.BlockSpec((cols // tile, rb, tile),
                                     lambda i, j, last=last: (0, jnp.minimum(i * nj + j, last), 0)))
        cast_shapes.append(jax.ShapeDtypeStruct((cols // tile, rows, tile), BF16))
        cast_args.append(stack)
    outs = pl.pallas_call(
        _mm_cast_kernel,
        grid=(m // bm, nj),
        in_specs=[pl.BlockSpec((bm, k), lambda i, j: (i, 0)),
                  pl.BlockSpec((None, k, bn), lambda i, j: (layer, 0, j))] + cast_in,
        out_specs=[pl.BlockSpec((bm, bn), lambda i, j: (i, j))] + cast_out,
        out_shape=[jax.ShapeDtypeStruct((m, n), BF16)] + cast_shapes,
        compiler_params=_cparams(("arbitrary", "arbitrary")),
        name="matmul",
    )(x, w, *cast_args)
    return outs[0], outs[1:]


def _matmul_residual(x, w, resid, alpha, bm):
    m, k = x.shape
    bn = w.shape[2]
    n = w.shape[0] * bn
    tile = pl.BlockSpec((bm, bn), lambda i, j: (i, j))
    if len(resid) == 1:
        body, resid_specs = _mm_res_kernel, [tile]
    else:
        body = _mm_res_ln_kernel
        resid = (resid[0], resid[1], resid[2].reshape(1, n), resid[3].reshape(1, n))
        resid_specs = [tile, pl.BlockSpec((bm, 2 * LANES), lambda i, j: (i, 0)),
                       pl.BlockSpec((1, bn), lambda i, j: (0, j)), pl.BlockSpec((1, bn), lambda i, j: (0, j))]
    return pl.pallas_call(
        functools.partial(body, alpha=alpha),
        grid=(m // bm, n // bn),
        in_specs=[pl.BlockSpec((bm, k), lambda i, j: (i, 0)),
                  pl.BlockSpec((None, k, bn), lambda i, j: (j, 0, 0))] + resid_specs,
        out_specs=tile,
        out_shape=jax.ShapeDtypeStruct((m, n), F32),
        compiler_params=_cparams(("parallel", "arbitrary")),
        name="matmul_residual",
    )(x, w, *resid)


def _matmul_swiglu(x, w, bm):
    m, k = x.shape
    bn = w.shape[2]
    nb = w.shape[0] // 2
    f = nb * bn
    return pl.pallas_call(
        _mm_swiglu_kernel,
        grid=(m // bm, nb),
        in_specs=[pl.BlockSpec((bm, k), lambda i, j: (i, 0)),
                  pl.BlockSpec((None, k, bn), lambda i, j: (j, 0, 0)),
                  pl.BlockSpec((None, k, bn), lambda i, j: (j + nb, 0, 0))],
        out_specs=pl.BlockSpec((bm, bn), lambda i, j: (i, j)),
        out_shape=jax.ShapeDtypeStruct((m, f), BF16),
        compiler_params=_cparams(("parallel", "arbitrary")),
        name="matmul_swiglu",
    )(x, w, w)


def _na_bias_tables(rpb):
    nh = rpb.shape[0]
    qc = np.arange(GRID_W)[:, None]
    kc = np.arange(GRID_W)[None, :]
    win_start = np.clip(qc - WIN_C // 2, 0, GRID_W - WIN_C)
    col_ok = (kc >= win_start) & (kc < win_start + WIN_C)
    dc_idx = np.clip(kc - qc + (WIN_C - 1), 0, 2 * WIN_C - 2)
    onehot = (dc_idx.reshape(-1)[None, :] == np.arange(2 * WIN_C - 1)[:, None]).astype(np.float32)
    tiles = jnp.einsum("hdc,cq->hdq", rpb.astype(F32), jnp.asarray(onehot),
                       precision=lax.Precision.HIGHEST).reshape(nh, 2 * WIN_R - 1, GRID_W, GRID_W)
    tiles = jnp.where(jnp.asarray(col_ok)[None, None], tiles, NEG_INF)
    dr = np.arange(WIN_R)[:, None] + np.arange(WIN_R)[None, :]
    t = tiles[:, dr]
    return t.transpose(0, 1, 3, 2, 4).reshape(nh, WIN_R, GRID_W, WIN_R * GRID_W)


def _na_kernel(q_ref, k_ref, v_ref, bias_ref, o_ref, *, rows, scale):
    nk = WIN_R * GRID_W
    nq = NA_ROWS_PER_STEP * GRID_W
    hd = q_ref.shape[-1]

    def group(g, carry):
        ks, vs, bias = [], [], []
        for i in range(NA_ROWS_PER_STEP):
            r = g * NA_ROWS_PER_STEP + i
            row_start = jnp.clip(r - WIN_R // 2, 0, rows - WIN_R)
            k0 = pl.multiple_of(row_start * GRID_W, GRID_W)
            ks.append(k_ref[0, pl.ds(k0, nk), :])
            vs.append(v_ref[0, pl.ds(k0, nk), :])
            bias.append(bias_ref[row_start - r + (WIN_R - 1)])
        q_rows = pl.ds(pl.multiple_of(g * nq, nq), nq)
        q = q_ref[0, q_rows, :].reshape(NA_ROWS_PER_STEP, GRID_W, hd)
        s = jnp.einsum("rqd,rkd->rqk", q, jnp.stack(ks), preferred_element_type=F32)
        s = s * scale + jnp.stack(bias)
        m = jnp.max(s, -1, keepdims=True)
        p = jnp.exp(s - m)
        l = jnp.sum(p, -1, keepdims=True)
        o = jnp.einsum("rqk,rkd->rqd", p.astype(BF16), jnp.stack(vs), preferred_element_type=F32) / l
        o_ref[0, q_rows, :] = o.reshape(nq, hd).astype(o_ref.dtype)
        return carry

    lax.fori_loop(0, rows // NA_ROWS_PER_STEP, group, 0, unroll=2)


def _neighbourhood_attention(proj3, b0, bsz, seq_len, bias, nh, hd):
    rows = seq_len // GRID_W
    assert rows % NA_ROWS_PER_STEP == 0 and rows >= WIN_R
    return pl.pallas_call(
        functools.partial(_na_kernel, rows=rows, scale=hd ** -0.5),
        grid=(nh, bsz),
        in_specs=[pl.BlockSpec((1, seq_len, hd), lambda h, b: (b0 + b, 0, h)),
                  pl.BlockSpec((1, seq_len, hd), lambda h, b: (b0 + b, 0, nh + h)),
                  pl.BlockSpec((1, seq_len, hd), lambda h, b: (b0 + b, 0, 2 * nh + h)),
                  pl.BlockSpec((None, WIN_R, GRID_W, WIN_R * GRID_W), lambda h, b: (h, 0, 0, 0))],
        out_specs=pl.BlockSpec((1, seq_len, hd), lambda h, b: (b, 0, h)),
        out_shape=jax.ShapeDtypeStruct((bsz, seq_len, nh * hd), BF16),
        compiler_params=_cparams(("parallel", "parallel")),
        name="neighbourhood_attention",
    )(proj3, proj3, proj3, bias)


def _conv3_kernel(x_ref, w_ref, b_ref, o_ref, *, seq_len, rb):
    w = w_ref[...]
    bias = b_ref[...]
    n_chunks = seq_len // rb

    def body(c, carry):
        r0 = pl.multiple_of(c * rb, rb)
        x = x_ref[0, pl.ds(r0, rb), :].astype(F32)
        row = lax.broadcasted_iota(jnp.int32, x.shape, 0)
        p0 = pl.multiple_of(jnp.maximum(r0 - 16, 0), 16)
        n0 = pl.multiple_of(jnp.minimum(r0 + rb, seq_len - 16), 16)
        prev_row = x_ref[0, pl.ds(p0, 16), :].astype(F32)[15:16]
        next_row = x_ref[0, pl.ds(n0, 16), :].astype(F32)[0:1]
        prev_row = jnp.where(c == 0, 0.0, prev_row)
        next_row = jnp.where(c == n_chunks - 1, 0.0, next_row)
        up = jnp.where(row == 0, prev_row, pltpu.roll(x, 1, 0))
        down = jnp.where(row == rb - 1, next_row, pltpu.roll(x, rb - 1, 0))
        y = up * w[0:1] + x * w[1:2] + down * w[2:3] + bias
        o_ref[0, pl.ds(r0, rb), :] = y.astype(o_ref.dtype)
        return carry

    lax.fori_loop(0, n_chunks, body, 0)


def _short_conv3(proj3, b0, bsz, seq_len, col0, w, b, cb=256, rb=512):
    width = w.shape[1]
    rb = min(rb, seq_len)
    c0 = col0 // cb
    return pl.pallas_call(
        functools.partial(_conv3_kernel, seq_len=seq_len, rb=rb),
        grid=(bsz, width // cb),
        in_specs=[pl.BlockSpec((1, seq_len, cb), lambda bi, c: (b0 + bi, 0, c0 + c)),
                  pl.BlockSpec((3, cb), lambda bi, c: (0, c)),
                  pl.BlockSpec((1, cb), lambda bi, c: (0, c))],
        out_specs=pl.BlockSpec((1, seq_len, cb), lambda bi, c: (bi, 0, c)),
        out_shape=jax.ShapeDtypeStruct((bsz, seq_len, width), BF16),
        compiler_params=_cparams(("parallel", "parallel")),
        name="hyena_short_conv",
    )(proj3, w, b.reshape(1, width))


def _filter_kernel(z_ref, w1_ref, b1_ref, fr_ref, w2_ref, b2_ref, w3_ref, b3_ref, dl_ref, o_ref, *,
                   seq_len, rb):
    i = pl.program_id(0)
    hi = lax.Precision.HIGHEST
    z = z_ref[...]
    h = jnp.sin(fr_ref[0:1] * (jnp.dot(z, w1_ref[...], precision=hi, preferred_element_type=F32)
                               + b1_ref[...]))
    h = jnp.sin(fr_ref[1:2] * (jnp.dot(h, w2_ref[...], precision=hi, preferred_element_type=F32)
                               + b2_ref[...]))
    ch = dl_ref.shape[-1]
    n = i * rb + lax.broadcasted_iota(jnp.int32, (rb, ch), 0)
    sign = jnp.where(n < seq_len, 1.0, jnp.where(n == seq_len, 0.0, -1.0))
    window = jnp.exp(-z[:, 0:1] * dl_ref[...]) * sign
    hb = h.astype(BF16)
    for o in range(w3_ref.shape[0]):
        f = jnp.dot(hb, w3_ref[o].astype(BF16), preferred_element_type=F32) + b3_ref[o]
        o_ref[:, o * ch:(o + 1) * ch] = (f * window).astype(o_ref.dtype)


def _hyena_filter_signal(seq_len, w1, b1, freq, w2, b2, w3, b3, order, ch, rb=512):
    n = 2 * seq_len
    emb, hid = w1.shape
    pad = LANES
    t = jnp.linspace(0.0, 1.0, seq_len, dtype=F32)[:, None]
    bands = (emb - 1) // 2
    fr = jnp.linspace(1e-4, bands - 1, bands, dtype=F32)[None, :]
    wpos = 2.0 * math.pi * jnp.arange(seq_len, dtype=F32)[:, None] / seq_len
    z = jnp.concatenate([t, jnp.cos(fr * wpos), -jnp.sin(fr * wpos)], axis=-1)
    pos = np.arange(n)
    src = np.clip(np.where(pos < seq_len, pos, n - pos), 0, seq_len - 1)
    z2 = jnp.pad(z[src], ((0, 0), (0, pad - emb)))
    w1p = jnp.pad(w1.astype(F32), ((0, pad - emb), (0, pad - hid)))
    b1p = jnp.pad(b1.astype(F32), (0, pad - hid)).reshape(1, pad)
    frp = jnp.pad(freq.astype(F32), ((0, 0), (0, pad - hid)), constant_values=1.0)
    w2p = jnp.pad(w2.astype(F32), ((0, pad - hid), (0, pad - hid)))
    b2p = jnp.pad(b2.astype(F32), (0, pad - hid)).reshape(1, pad)
    w3p = jnp.pad(w3.astype(F32), ((0, pad - hid), (0, 0))).reshape(pad, order, 2, ch).transpose(2, 1, 0, 3)
    b3p = b3.astype(F32).reshape(order, 2, 1, ch).transpose(1, 0, 2, 3)
    deltas = jnp.abs(jnp.linspace(HY_MIN_DECAY, HY_MAX_DECAY, ch, dtype=F32)).reshape(1, ch)
    nb = n // rb
    half = nb // 2
    full = lambda i: (0, 0)
    return pl.pallas_call(
        functools.partial(_filter_kernel, seq_len=seq_len, rb=rb),
        grid=(nb,),
        in_specs=[pl.BlockSpec((rb, pad), lambda i: (i, 0)),
                  pl.BlockSpec((pad, pad), full), pl.BlockSpec((1, pad), full),
                  pl.BlockSpec((2, pad), full),
                  pl.BlockSpec((pad, pad), full), pl.BlockSpec((1, pad), full),
                  pl.BlockSpec((None, order, pad, ch), lambda i: (i // half, 0, 0, 0)),
                  pl.BlockSpec((None, order, 1, ch), lambda i: (i // half, 0, 0, 0)),
                  pl.BlockSpec((1, ch), full)],
        out_specs=pl.BlockSpec((rb, order * ch), lambda i: (i, 0)),
        out_shape=jax.ShapeDtypeStruct((n, order * ch), BF16),
        compiler_params=_cparams(("parallel",)),
        name="hyena_filter_mlp",
    )(z2, w1p, b1p, frp, w2p, b2p, w3p, b3p, deltas)


def _dft_tables(seq_len):
    n = 2 * seq_len
    n2 = DFT_N2
    n1 = n // n2
    hk = n2 // 2
    kg = min(n1, DFT_GROUP)
    ng = n1 // kg
    pi = math.pi
    k1 = jnp.arange(n1, dtype=jnp.int32)
    ph1 = ((2 * k1[:, None] + 1) * k1[None, :]) % (2 * n1)
    th1 = ph1.astype(F32) * (pi / n1)
    c1 = jnp.cos(th1).reshape(ng, kg, n1)
    s1 = jnp.sin(th1).reshape(ng, kg, n1)
    f1 = jnp.concatenate([c1, -s1], axis=1)
    fb = jnp.concatenate([c1, -s1], axis=1).transpose(0, 2, 1)[:, : n1 // 2] * (2.0 / n)
    k = k1[:, None, None] + n1 * jnp.arange(hk, dtype=jnp.int32)[None, :, None]
    ph2 = ((2 * k + 1) * jnp.arange(n2, dtype=jnp.int32)[None, None, :]) % (2 * n)
    th2 = ph2.astype(F32) * (pi / n)
    c2, s2 = jnp.cos(th2), jnp.sin(th2)
    g = jnp.concatenate([jnp.concatenate([c2, s2], axis=2),
                         jnp.concatenate([-s2, c2], axis=2)], axis=1)
    c2t, s2t = c2.transpose(0, 2, 1), s2.transpose(0, 2, 1)
    h = jnp.concatenate([jnp.concatenate([c2t, -s2t], axis=2),
                         jnp.concatenate([s2t, c2t], axis=2)], axis=1)
    nl = 2 if n1 <= DFT_GROUP else 1
    return dict(n1=n1, kg=kg, ng=ng, nl=nl, f1_full=f1.astype(BF16), f1_half=f1[:, :, : n1 // 2].astype(BF16),
                fb=fb.astype(BF16), g=g.astype(BF16), h=h.astype(BF16))


def _load_lanes(ref, rows):
    parts = [ref[s, rows, :] for s in range(ref.shape[0])]
    return parts[0] if len(parts) == 1 else jnp.concatenate(parts, axis=1)


def _store_lanes(ref, rows, val):
    for s in range(ref.shape[0]):
        ref[s, rows, :] = val[:, s * LANES:(s + 1) * LANES]


def _fill_slabs(x_ref, xs, slabs, pitch):
    def body(n1, carry):
        src = pl.ds(pl.multiple_of(n1 * DFT_N2, DFT_N2), DFT_N2)
        _store_lanes(xs, pl.ds(pl.multiple_of(n1 * pitch, SUBLANES), DFT_N2), x_ref[src, :].astype(F32))
        return carry
    lax.fori_loop(0, slabs, body, 0, unroll=4)


def _slab_stage(xs, a_s, f1, slabs, p_x, p_a):
    rows = f1.shape[0]

    def body(i, carry):
        xn = _load_lanes(xs, pl.ds(i, slabs, stride=p_x)).astype(BF16)
        _store_lanes(a_s, pl.ds(pl.multiple_of(i * p_a, SUBLANES), rows),
                     jnp.dot(f1, xn, preferred_element_type=F32))
        return carry
    lax.fori_loop(0, DFT_N2, body, 0, unroll=32)


def _load_slab_freq(a_s, j, kg, p_a):
    are = _load_lanes(a_s, pl.ds(j, DFT_N2, stride=p_a))
    aim = _load_lanes(a_s, pl.ds(kg + j, DFT_N2, stride=p_a))
    return jnp.concatenate([are, aim], axis=0).astype(BF16)


def _filter_spec_kernel(x_ref, f1_ref, g_ref, o_ref, xs, a_s, *, slabs, kg):
    grp = pl.program_id(1)
    p_x, p_a = _pitch(DFT_N2), _pitch(2 * kg)

    @pl.when(grp == 0)
    def _():
        _fill_slabs(x_ref, xs, slabs, p_x)

    _slab_stage(xs, a_s, f1_ref[0], slabs, p_x, p_a)

    def body(j, carry):
        o_ref[j] = jnp.dot(g_ref[j], _load_slab_freq(a_s, j, kg, p_a), preferred_element_type=F32)
        return carry
    lax.fori_loop(0, kg, body, 0, unroll=8)


def _filter_spectrum(f, tables):
    n, cols = f.shape
    n1, kg, ng, nl = tables["n1"], tables["kg"], tables["ng"], tables["nl"]
    p_x, p_a = _pitch(DFT_N2), _pitch(2 * kg)
    wl = nl * LANES
    return pl.pallas_call(
        functools.partial(_filter_spec_kernel, slabs=n1, kg=kg),
        grid=(cols // wl, ng),
        in_specs=[pl.BlockSpec((n, wl), lambda c, g: (0, c)),
                  pl.BlockSpec((1, 2 * kg, n1), lambda c, g: (g, 0, 0)),
                  pl.BlockSpec((kg, DFT_N2, 2 * DFT_N2), lambda c, g: (g, 0, 0))],
        out_specs=pl.BlockSpec((kg, DFT_N2, wl), lambda c, g: (g, 0, c)),
        out_shape=jax.ShapeDtypeStruct((n1, DFT_N2, cols), F32),
        scratch_shapes=[pltpu.VMEM((nl, n1 * p_x, LANES), F32),
                        pltpu.VMEM((nl, DFT_N2 * p_a, LANES), F32)],
        compiler_params=_cparams(("parallel", "arbitrary")),
        name="hyena_filter_spectrum",
    )(f, tables["f1_full"], tables["g"])


def _long_conv_kernel(x_ref, gate_ref, f1_ref, g_ref, h_ref, kf_ref, fb_ref, skip_ref, o_ref,
                      xs, a_s, z_s, y_s, *, slabs, kg, ng):
    grp = pl.program_id(2)
    n2 = DFT_N2
    hk = n2 // 2
    p_x, p_a, p_z, p_y = _pitch(n2), _pitch(2 * kg), _pitch(2 * n2), _pitch(slabs)

    @pl.when(grp == 0)
    def _():
        _fill_slabs(x_ref.at[0], xs, slabs, p_x)
        y_s[...] = jnp.zeros_like(y_s)

    _slab_stage(xs, a_s, f1_ref[0], slabs, p_x, p_a)

    def freq_body(j, carry):
        x = jnp.dot(g_ref[j], _load_slab_freq(a_s, j, kg, p_a), preferred_element_type=F32)
        kf = kf_ref[j]
        xr, xi = x[:hk], x[hk:]
        kr, ki = kf[:hk], kf[hk:]
        y = jnp.concatenate([xr * kr - xi * ki, xr * ki + xi * kr], axis=0)
        _store_lanes(z_s, pl.ds(pl.multiple_of(j * p_z, SUBLANES), n2), y)
        return carry
    lax.fori_loop(0, kg, freq_body, 0, unroll=8)

    def inv_freq_body(j, carry):
        y = _load_lanes(z_s, pl.ds(pl.multiple_of(j * p_z, SUBLANES), n2)).astype(BF16)
        _store_lanes(z_s, pl.ds(pl.multiple_of(j * p_z, SUBLANES), 2 * n2),
                     jnp.dot(h_ref[j], y, preferred_element_type=F32))
        return carry
    lax.fori_loop(0, kg, inv_freq_body, 0, unroll=8)

    fb = fb_ref[0]

    def inv_body(t, carry):
        zre = _load_lanes(z_s, pl.ds(t, kg, stride=p_z))
        zim = _load_lanes(z_s, pl.ds(n2 + t, kg, stride=p_z))
        zz = jnp.concatenate([zre, zim], axis=0).astype(BF16)
        dst = pl.ds(pl.multiple_of(t * p_y, SUBLANES), slabs)
        _store_lanes(y_s, dst, _load_lanes(y_s, dst) + jnp.dot(fb, zz, preferred_element_type=F32))
        return carry
    lax.fori_loop(0, n2, inv_body, 0, unroll=32)

    @pl.when(grp == ng - 1)
    def _():
        skip = skip_ref[...]

        def out_body(t1, carry):
            y = _load_lanes(y_s, pl.ds(t1, n2, stride=p_y))
            u = _load_lanes(xs, pl.ds(pl.multiple_of(t1 * p_x, SUBLANES), n2))
            rows = pl.ds(pl.multiple_of(t1 * n2, n2), n2)
            gate = gate_ref[0, rows, :].astype(F32)
            o_ref[0, rows, :] = ((y + skip * u) * gate).astype(o_ref.dtype)
            return carry
        lax.fori_loop(0, slabs, out_body, 0, unroll=4)


def _long_conv_gate(u, u_blk0, gate, gate_blk0, tables, kf, kf_blk0, skip, ch):
    bsz, seq_len, _ = u.shape
    n1, kg, ng, nl = tables["n1"], tables["kg"], tables["ng"], tables["nl"]
    slabs = n1 // 2
    p_x, p_a, p_z, p_y = _pitch(DFT_N2), _pitch(2 * kg), _pitch(2 * DFT_N2), _pitch(slabs)
    wl = nl * LANES
    return pl.pallas_call(
        functools.partial(_long_conv_kernel, slabs=slabs, kg=kg, ng=ng),
        grid=(bsz, ch // wl, ng),
        in_specs=[pl.BlockSpec((1, seq_len, wl), lambda b, c, g: (b, 0, u_blk0 + c)),
                  pl.BlockSpec((1, seq_len, wl), lambda b, c, g: (b, 0, gate_blk0 + c)),
                  pl.BlockSpec((1, 2 * kg, slabs), lambda b, c, g: (g, 0, 0)),
                  pl.BlockSpec((kg, DFT_N2, 2 * DFT_N2), lambda b, c, g: (g, 0, 0)),
                  pl.BlockSpec((kg, 2 * DFT_N2, DFT_N2), lambda b, c, g: (g, 0, 0)),
                  pl.BlockSpec((kg, DFT_N2, wl), lambda b, c, g: (g, 0, kf_blk0 + c)),
                  pl.BlockSpec((1, slabs, 2 * kg), lambda b, c, g: (g, 0, 0)),
                  pl.BlockSpec((1, wl), lambda b, c, g: (0, c))],
        out_specs=pl.BlockSpec((1, seq_len, wl), lambda b, c, g: (b, 0, c)),
        out_shape=jax.ShapeDtypeStruct((bsz, seq_len, ch), BF16),
        scratch_shapes=[pltpu.VMEM((nl, slabs * p_x, LANES), F32),
                        pltpu.VMEM((nl, DFT_N2 * p_a, LANES), F32),
                        pltpu.VMEM((nl, kg * p_z, LANES), F32),
                        pltpu.VMEM((nl, DFT_N2 * p_y, LANES), F32)],
        compiler_params=_cparams(("parallel", "parallel", "arbitrary")),
        name="hyena_long_conv",
    )(u, gate, tables["f1_half"], tables["g"], tables["h"], kf, tables["fb"],
      skip.astype(F32).reshape(1, ch))


def _hyena(hy, tables, kf, skip, order, ch):
    nblk = ch // (tables["nl"] * LANES)
    z, z_blk0 = hy, 0
    for o in range(order):
        z = _long_conv_gate(z, z_blk0, hy, (o + 1) * nblk, tables, kf, o * nblk, skip[o], ch)
    return z


def _rope(x, cos, sin):
    half = x.shape[-1] // 2
    x1, x2 = x[:, :half], x[:, half:]
    return jnp.concatenate([x1 * cos - x2 * sin, x1 * sin + x2 * cos], axis=-1)


def _ret_kernel(lg_ref, q_ref, k_ref, v_ref, g_ref, cos_ref, sin_ref, o_ref,
                sb_all, kr_all, kz_all, sf_ref, sb_ref, *, n_super, chunks_per_super, k_scale):
    h = pl.program_id(1)
    s = pl.program_id(2)
    cc = RET_CHUNK
    dh = q_ref.shape[-1]
    lgf = lg_ref[0, h]
    lgb = lg_ref[1, h]
    row = lax.broadcasted_iota(jnp.int32, (cc, dh), 0).astype(F32)
    zeta_f = jnp.exp(lgf * (cc - 1.0 - row))

    @pl.when(s < n_super)
    def _backward_sweep():
        @pl.when(s == 0)
        def _():
            sb_ref[...] = jnp.zeros_like(sb_ref)

        sup = n_super - 1 - s
        zeta_b = jnp.exp(lgb * row)
        chunk_decay = jnp.exp(lgb * jnp.full((1, dh), float(cc), F32))

        def body(t, carry):
            c = chunks_per_super - 1 - t
            c0 = pl.multiple_of(c * cc, cc)
            n = sup * chunks_per_super + c
            sb_all[n] = sb_ref[...].astype(BF16)
            k = _rope(k_ref[0, pl.ds(c0, cc), :].astype(F32), cos_ref[pl.ds(c0, cc), :],
                      sin_ref[pl.ds(c0, cc), :]) * k_scale
            kr_all[n] = k.astype(BF16)
            kz_all[n] = (k * zeta_f).astype(BF16)
            kz = (k * zeta_b).astype(BF16)
            v = v_ref[0, pl.ds(c0, cc), :]
            upd = lax.dot_general(kz, v, (((0,), (0,)), ((), ())), preferred_element_type=F32)
            sb_ref[...] = sb_ref[...] * chunk_decay + upd
            return carry

        lax.fori_loop(0, chunks_per_super, body, 0, unroll=4)

    @pl.when(s >= n_super)
    def _forward_sweep():
        @pl.when(s == n_super)
        def _():
            sf_ref[...] = jnp.zeros_like(sf_ref)

        sup = s - n_super
        col = lax.broadcasted_iota(jnp.int32, (cc, cc), 1).astype(F32)
        rowc = lax.broadcasted_iota(jnp.int32, (cc, cc), 0).astype(F32)
        diff = rowc - col
        inner_decay = jnp.where(diff >= 0, jnp.exp(lgf * jnp.maximum(diff, 0.0)),
                                jnp.exp(lgb * jnp.maximum(-diff, 0.0)))
        xi_f = jnp.exp(lgf * (row + 1.0))
        xi_b = jnp.exp(lgb * (cc - row))
        chunk_decay = jnp.exp(lgf * jnp.full((1, dh), float(cc), F32))

        def body(c, carry):
            c0 = pl.multiple_of(c * cc, cc)
            n = sup * chunks_per_super + c
            q = _rope(q_ref[0, pl.ds(c0, cc), :].astype(F32), cos_ref[pl.ds(c0, cc), :],
                      sin_ref[pl.ds(c0, cc), :]).astype(BF16)
            v = v_ref[0, pl.ds(c0, cc), :]
            sc = lax.dot_general(q, kr_all[n], (((1,), (1,)), ((), ())),
                                 preferred_element_type=F32) * inner_decay
            ret = jnp.dot(sc.astype(BF16), v, preferred_element_type=F32)
            ret = ret + xi_f * jnp.dot(q, sf_ref[...].astype(BF16), preferred_element_type=F32)
            ret = ret + xi_b * jnp.dot(q, sb_all[n], preferred_element_type=F32)
            upd = lax.dot_general(kz_all[n], v, (((0,), (0,)), ((), ())), preferred_element_type=F32)
            sf_ref[...] = sf_ref[...] * chunk_decay + upd
            ret = ret * lax.rsqrt(jnp.mean(ret * ret, -1, keepdims=True) + RMS_EPS)
            gate = g_ref[0, pl.ds(c0, cc), :].astype(F32)
            o_ref[0, pl.ds(c0, cc), :] = (ret * (gate / (1.0 + jnp.exp(-gate)))).astype(o_ref.dtype)
            return carry

        lax.fori_loop(0, chunks_per_super, body, 0, unroll=4)


def _retention(proj3, b0, bsz, seq_len, col0, nh, dh, log_gamma, cos, sin):
    sup_len = min(RET_SUPER, seq_len)
    n_super = seq_len // sup_len
    cps = sup_len // RET_CHUNK
    cb0 = col0 // dh

    def fwd_idx(s):
        return jnp.maximum(s - n_super, 0)

    def kv_idx(s):
        return jnp.where(s < n_super, n_super - 1 - s, s - n_super)

    def k_idx(s):
        return jnp.maximum(n_super - 1 - s, 0)

    n_chunks = seq_len // RET_CHUNK
    return pl.pallas_call(
        functools.partial(_ret_kernel, n_super=n_super, chunks_per_super=cps, k_scale=dh ** -0.5),
        grid=(bsz, nh, 2 * n_super),
        in_specs=[pl.BlockSpec(memory_space=pltpu.SMEM),
                  pl.BlockSpec((1, sup_len, dh), lambda b, h, s: (b0 + b, fwd_idx(s), cb0 + h)),
                  pl.BlockSpec((1, sup_len, dh), lambda b, h, s: (b0 + b, k_idx(s), cb0 + nh + h)),
                  pl.BlockSpec((1, sup_len, dh), lambda b, h, s: (b0 + b, kv_idx(s), cb0 + 2 * nh + h)),
                  pl.BlockSpec((1, sup_len, dh), lambda b, h, s: (b0 + b, fwd_idx(s), cb0 + 3 * nh + h)),
                  pl.BlockSpec((sup_len, dh // 2), lambda b, h, s: (kv_idx(s), 0)),
                  pl.BlockSpec((sup_len, dh // 2), lambda b, h, s: (kv_idx(s), 0))],
        out_specs=pl.BlockSpec((1, sup_len, dh), lambda b, h, s: (b, fwd_idx(s), h)),
        out_shape=jax.ShapeDtypeStruct((bsz, seq_len, nh * dh), BF16),
        scratch_shapes=[pltpu.VMEM((n_chunks, dh, dh), BF16),
                        pltpu.VMEM((n_chunks, RET_CHUNK, dh), BF16),
                        pltpu.VMEM((n_chunks, RET_CHUNK, dh), BF16),
                        pltpu.VMEM((dh, dh), F32),
                        pltpu.VMEM((dh, dh), F32)],
        compiler_params=_cparams(("parallel", "parallel", "arbitrary")),
        name="retention",
    )(log_gamma, proj3, proj3, proj3, proj3, cos, sin)


def _assemble_kernel(a1_ref, a2_ref, z1_ref, z2_ref, c1_ref, c2_ref, ga_ref, gb_ref, o_ref, *, nb1):
    first = pl.program_id(0) < nb1
    wa = a1_ref.shape[-1]
    wb = z1_ref.shape[-1]
    a = jnp.where(first, a1_ref[...], a2_ref[...]).astype(F32)
    a = a * lax.rsqrt(jnp.mean(a * a, -1, keepdims=True) + RMS_EPS) * ga_ref[...]
    z = jnp.where(first, z1_ref[...], z2_ref[...]).astype(F32)
    z = z * lax.rsqrt(jnp.mean(z * z, -1, keepdims=True) + RMS_EPS) * gb_ref[...]
    o_ref[:, :wa] = a.astype(o_ref.dtype)
    o_ref[:, wa:wa + wb] = z.astype(o_ref.dtype)
    o_ref[:, wa + wb:] = jnp.where(first, c1_ref[...], c2_ref[...])


def _assemble(a_parts, z_parts, c_parts, gain_a, gain_b, bm=512):
    m1, wa = a_parts[0].shape
    m = m1 + a_parts[1].shape[0]
    wb, wc = z_parts[0].shape[1], c_parts[0].shape[1]
    nb1 = m1 // bm
    part_specs = []
    for w in (wa, wb, wc):
        part_specs += [pl.BlockSpec((bm, w), _first_rows(nb1)), pl.BlockSpec((bm, w), _second_rows(nb1))]
    return pl.pallas_call(
        functools.partial(_assemble_kernel, nb1=nb1),
        grid=(m // bm,),
        in_specs=part_specs + [pl.BlockSpec((1, wa), lambda i: (0, 0)),
                               pl.BlockSpec((1, wb), lambda i: (0, 0))],
        out_specs=pl.BlockSpec((bm, wa + wb + wc), lambda i: (i, 0)),
        out_shape=jax.ShapeDtypeStruct((m, wa + wb + wc), BF16),
        compiler_params=_cparams(("arbitrary",)),
        name="assemble_mixers",
    )(*a_parts, *z_parts, *c_parts, gain_a.astype(F32).reshape(1, wa), gain_b.astype(F32).reshape(1, wb))


def _token_mixers(proj, requests, p, l, dims):
    nh_a, hd_a, ch, order, nh_r, dh_r = dims
    in_cols = proj.shape[1]
    total = proj.shape[0]
    wa = nh_a * hd_a
    hy0 = 3 * wa
    ret0 = hy0 + (order + 1) * ch
    log_gamma = jnp.log1p(-jnp.exp2(-p["ret_decay_exp"][l].astype(F32)))
    a_parts, z_parts, c_parts = [], [], []
    bias = _na_bias_tables(p["na_rpb"][l])
    for (bsz, seq_len, tok0) in requests:
        assert tok0 % seq_len == 0 and total % seq_len == 0
        proj3 = proj.reshape(total // seq_len, seq_len, in_cols)
        b0 = tok0 // seq_len
        a = _neighbourhood_attention(proj3, b0, bsz, seq_len, bias, nh_a, hd_a)
        a_parts.append(a.reshape(bsz * seq_len, wa))
        tables = _dft_tables(seq_len)
        f = _hyena_filter_signal(seq_len, p["hy_f_w1"][l], p["hy_f_b1"][l], p["hy_f_freq"][l],
                                 p["hy_f_w2"][l], p["hy_f_b2"][l], p["hy_f_w3"][l], p["hy_f_b3"][l],
                                 order, ch)
        kf = _filter_spectrum(f, tables)
        hy = _short_conv3(proj3, b0, bsz, seq_len, hy0, p["hy_conv_w"][l].astype(F32),
                          p["hy_conv_b"][l].astype(F32))
        z = _hyena(hy, tables, kf, p["hy_skip"][l], order, ch)
        z_parts.append(z.reshape(bsz * seq_len, ch))
        half = dh_r // 2
        inv_freq = 1.0 / (ROPE_BASE ** jnp.linspace(0.0, 1.0, half, dtype=F32))
        ang = jnp.arange(seq_len, dtype=F32)[:, None] * inv_freq[None, :]
        c = _retention(proj3, b0, bsz, seq_len, ret0, nh_r, dh_r, log_gamma, jnp.cos(ang), jnp.sin(ang))
        c_parts.append(c.reshape(bsz * seq_len, nh_r * dh_r))
    return _assemble(a_parts, z_parts, c_parts, p["grp_gain_a"][l], p["grp_gain_b"][l])


def _pick(n, candidates):
    for c in candidates:
        if n % c == 0:
            return c
    raise ValueError(f"no block size for {n}")


def kernel(x_prompt, x_sample, ln_in_g, ln_in_b, w_in, na_rpb, hy_conv_w, hy_conv_b, hy_f_w1, hy_f_b1,
           hy_f_freq, hy_f_w2, hy_f_b2, hy_f_w3, hy_f_b3, hy_skip, ret_decay_exp, grp_gain_a, grp_gain_b,
           w_out, ln1_g, ln1_b, w_ffn_in, w_ffn_out, ln2_g, ln2_b):
    p = dict(na_rpb=na_rpb, hy_conv_w=hy_conv_w, hy_conv_b=hy_conv_b, hy_f_w1=hy_f_w1, hy_f_b1=hy_f_b1,
             hy_f_freq=hy_f_freq, hy_f_w2=hy_f_w2, hy_f_b2=hy_f_b2, hy_f_w3=hy_f_w3, hy_f_b3=hy_f_b3,
             hy_skip=hy_skip, ret_decay_exp=ret_decay_exp, grp_gain_a=grp_gain_a, grp_gain_b=grp_gain_b)
    depth, d_model, in_cols = w_in.shape
    nh_a = na_rpb.shape[1]
    wa = grp_gain_a.shape[1]
    ch = grp_gain_b.shape[1]
    order = hy_skip.shape[1]
    nh_r = ret_decay_exp.shape[2]
    wr = (in_cols - 3 * wa - (order + 1) * ch) // 4
    dims = (nh_a, wa // nh_a, ch, order, nh_r, wr // nh_r)
    d_ff = w_ffn_out.shape[1]
    alpha = (2.0 * depth) ** 0.25

    bp, lp, _ = x_prompt.shape
    bs, ls, _ = x_sample.shape
    requests = [(bp, lp, 0), (bs, ls, bp * lp)]
    m1 = bp * lp
    m = m1 + bs * ls
    bm = _pick(m, (1024, 512, 256))
    bm_ln = 256
    assert m1 % bm_ln == 0 and m % bm_ln == 0 and m1 % 512 == 0

    xf, xb = _layernorm_join(x_prompt.reshape(m1, d_model), x_sample.reshape(bs * ls, d_model),
                             ln_in_g.astype(F32), ln_in_b.astype(F32), bm_ln)
    w_in_b = w_in[0:1].astype(BF16)
    resid = (xf,)
    for l in range(depth):
        bn_out = _pick(d_model, (1024, 512, 256) if len(resid) == 1 else (512, 256))
        cast = [(w_out, l, bn_out), (w_ffn_in, l, _pick(d_ff, (512, 256, 128))),
                (w_ffn_out, l, _pick(d_model, (512, 256)))]
        if l + 1 < depth:
            cast.append((w_in, l + 1, in_cols))
        proj, cast_w = _matmul(xb, w_in_b, 0, bm, _pick(in_cols, (1024, 512, 256)), cast)
        w_out_b, w_ffn_in_b, w_ffn_out_b = cast_w[:3]
        if l + 1 < depth:
            w_in_b = cast_w[3]
        mixed = _token_mixers(proj, requests, p, l, dims)
        y = _matmul_residual(mixed, w_out_b, resid, alpha, bm)
        g1, b1 = ln1_g[l].astype(F32), ln1_b[l].astype(F32)
        xb, stats = _layernorm_stats(y, g1, b1, bm_ln)
        hmid = _matmul_swiglu(xb, w_ffn_in_b, _pick(m, (2048, 1024, 512, 256)))
        y = _matmul_residual(hmid, w_ffn_out_b, (y, stats, g1, b1), alpha, _pick(m, (512, 256)))
        if l + 1 < depth:
            g2, b2 = ln2_g[l].astype(F32), ln2_b[l].astype(F32)
            xb, stats = _layernorm_stats(y, g2, b2, bm_ln)
            resid = (y, stats, g2, b2)
    o1, o2 = _layernorm_split(y, ln2_g[depth - 1].astype(F32), ln2_b[depth - 1].astype(F32), m1, bm_ln)
    return (o1.reshape(bp, lp, d_model), o2.reshape(bs, ls, d_model))
```

```python
import functools
import math

import numpy as np
import jax
import jax.numpy as jnp
from jax import lax
from jax.experimental import pallas as pl
from jax.experimental.pallas import tpu as pltpu

F32 = jnp.float32
BF16 = jnp.bfloat16

GRID_W = 64
WIN_R = 8
WIN_C = 16
ROPE_BASE = 10000.0
LN_EPS = 1e-5
RMS_EPS = 1e-6
NEG_INF = -1e30
HY_TARGET = 1e-2
HY_FAST_DECAY = 0.3
HY_SLOW_DECAY = 1.5
HY_MIN_DECAY = math.log(HY_TARGET) / HY_SLOW_DECAY
HY_MAX_DECAY = math.log(HY_TARGET) / HY_FAST_DECAY

V7X_VMEM_LIMIT = 56 * 1024 * 1024
LANES = 128
SUBLANES = 8
DFT_N2 = 128
DFT_GROUP = 32
NA_ROWS_PER_STEP = 8
RET_CHUNK = 256
RET_SUPER = 2048


def _cparams(sem, vmem=V7X_VMEM_LIMIT):
    return pltpu.CompilerParams(dimension_semantics=sem, vmem_limit_bytes=vmem)


def _pitch(rows):
    p = -(-rows // SUBLANES)
    return SUBLANES * (p + 1 - p % 2)


def _ln(x, g, b):
    mu = jnp.mean(x, -1, keepdims=True)
    xc = x - mu
    var = jnp.mean(xc * xc, -1, keepdims=True)
    return xc * lax.rsqrt(var + LN_EPS) * g + b


def _ln_stats_kernel(x_ref, g_ref, b_ref, ob_ref, st_ref):
    x = x_ref[...]
    mu = jnp.mean(x, -1, keepdims=True)
    xc = x - mu
    rstd = lax.rsqrt(jnp.mean(xc * xc, -1, keepdims=True) + LN_EPS)
    ob_ref[...] = (xc * rstd * g_ref[...] + b_ref[...]).astype(BF16)
    st_ref[:, :LANES] = jnp.broadcast_to(mu, (x.shape[0], LANES))
    st_ref[:, LANES:] = jnp.broadcast_to(rstd, (x.shape[0], LANES))


def _ln_join_kernel(x1_ref, x2_ref, g_ref, b_ref, of_ref, ob_ref, *, nb1):
    x = jnp.where(pl.program_id(0) < nb1, x1_ref[...], x2_ref[...])
    y = _ln(x, g_ref[...], b_ref[...])
    of_ref[...] = y
    ob_ref[...] = y.astype(BF16)


def _ln_split_kernel(x_ref, g_ref, b_ref, o1_ref, o2_ref, *, nb1):
    y = _ln(x_ref[...], g_ref[...], b_ref[...])

    @pl.when(pl.program_id(0) < nb1)
    def _():
        o1_ref[...] = y

    @pl.when(pl.program_id(0) >= nb1)
    def _():
        o2_ref[...] = y


def _first_rows(nb1):
    return lambda i: (jnp.minimum(i, nb1 - 1), 0)


def _second_rows(nb1):
    return lambda i: (jnp.maximum(i - nb1, 0), 0)


def _layernorm_join(x1, x2, g, b, bm=256):
    m1, d = x1.shape
    m = m1 + x2.shape[0]
    nb1 = m1 // bm
    return pl.pallas_call(
        functools.partial(_ln_join_kernel, nb1=nb1),
        grid=(m // bm,),
        in_specs=[pl.BlockSpec((bm, d), _first_rows(nb1)),
                  pl.BlockSpec((bm, d), _second_rows(nb1)),
                  pl.BlockSpec((1, d), lambda i: (0, 0)),
                  pl.BlockSpec((1, d), lambda i: (0, 0))],
        out_specs=[pl.BlockSpec((bm, d), lambda i: (i, 0)),
                   pl.BlockSpec((bm, d), lambda i: (i, 0))],
        out_shape=[jax.ShapeDtypeStruct((m, d), F32), jax.ShapeDtypeStruct((m, d), BF16)],
        compiler_params=_cparams(("arbitrary",)),
        name="layernorm_join",
    )(x1, x2, g.reshape(1, d), b.reshape(1, d))


def _layernorm_split(x, g, b, m1, bm=256):
    m, d = x.shape
    nb1 = m1 // bm
    return pl.pallas_call(
        functools.partial(_ln_split_kernel, nb1=nb1),
        grid=(m // bm,),
        in_specs=[pl.BlockSpec((bm, d), lambda i: (i, 0)),
                  pl.BlockSpec((1, d), lambda i: (0, 0)),
                  pl.BlockSpec((1, d), lambda i: (0, 0))],
        out_specs=[pl.BlockSpec((bm, d), _first_rows(nb1)),
                   pl.BlockSpec((bm, d), _second_rows(nb1))],
        out_shape=[jax.ShapeDtypeStruct((m1, d), F32), jax.ShapeDtypeStruct((m - m1, d), F32)],
        compiler_params=_cparams(("arbitrary",)),
        name="layernorm_split",
    )(x, g.reshape(1, d), b.reshape(1, d))


def _layernorm_stats(x, g, b, bm=256):
    m, d = x.shape
    return pl.pallas_call(
        _ln_stats_kernel,
        grid=(m // bm,),
        in_specs=[pl.BlockSpec((bm, d), lambda i: (i, 0)),
                  pl.BlockSpec((1, d), lambda i: (0, 0)),
                  pl.BlockSpec((1, d), lambda i: (0, 0))],
        out_specs=[pl.BlockSpec((bm, d), lambda i: (i, 0)),
                   pl.BlockSpec((bm, 2 * LANES), lambda i: (i, 0))],
        out_shape=[jax.ShapeDtypeStruct((m, d), BF16), jax.ShapeDtypeStruct((m, 2 * LANES), F32)],
        compiler_params=_cparams(("parallel",)),
        name="layernorm_stats",
    )(x, g.reshape(1, d), b.reshape(1, d))


def _mm_res_kernel(x_ref, w_ref, r_ref, o_ref, *, alpha):
    o_ref[...] = alpha * r_ref[...] + jnp.dot(x_ref[...], w_ref[...], preferred_element_type=F32)


def _mm_res_ln_kernel(x_ref, w_ref, y_ref, st_ref, g_ref, b_ref, o_ref, *, alpha):
    acc = jnp.dot(x_ref[...], w_ref[...], preferred_element_type=F32)
    mu = st_ref[:, :LANES]
    rstd = st_ref[:, LANES:]
    for c in range(y_ref.shape[1] // LANES):
        cols = slice(c * LANES, (c + 1) * LANES)
        r = (y_ref[:, cols] - mu) * rstd * g_ref[:, cols] + b_ref[:, cols]
        o_ref[:, cols] = alpha * r + acc[:, cols]


def _mm_swiglu_kernel(x_ref, wg_ref, wu_ref, o_ref):
    x = x_ref[...]
    g = jnp.dot(x, wg_ref[...], preferred_element_type=F32)
    u = jnp.dot(x, wu_ref[...], preferred_element_type=F32)
    o_ref[...] = (g / (1.0 + jnp.exp(-g)) * u).astype(o_ref.dtype)


def _mm_cast_kernel(x_ref, w_ref, *refs):
    n_cast = (len(refs) - 1) // 2
    o_ref = refs[n_cast]
    o_ref[...] = jnp.dot(x_ref[...], w_ref[...], preferred_element_type=F32).astype(o_ref.dtype)
    for src, dst in zip(refs[:n_cast], refs[n_cast + 1:]):
        tile = dst.shape[-1]
        for t in range(dst.shape[0]):
            dst[t] = src[:, t * tile:(t + 1) * tile].astype(dst.dtype)


def _cast_rows(rows, n_steps):
    rb = 16
    while rows % rb or rows // rb > n_steps:
        rb += 16
        assert rb <= rows
    return rb


def _matmul(x, w, layer, bm, bn, cast=()):
    m, k = x.shape
    n = w.shape[2]
    nj = n // bn
    n_steps = (m // bm) * nj
    cast_in, cast_out, cast_shapes, cast_args = [], [], [], []
    for stack, lyr, tile in cast:
        _, rows, cols = stack.shape
        rb = _cast_rows(rows, n_steps)
        last = rows // rb - 1
        cast_in.append(pl.BlockSpec((None, rb, cols),
                                    lambda i, j, lyr=lyr, last=last: (lyr, jnp.minimum(i * nj + j, last), 0)))
        cast_out.append(pl.BlockSpec((cols // tile, rb, tile),
                                     lambda i, j, last=last: (0, jnp.minimum(i * nj + j, last), 0)))
        cast_shapes.append(jax.ShapeDtypeStruct((cols // tile, rows, tile), BF16))
        cast_args.append(stack)
    outs = pl.pallas_call(
        _mm_cast_kernel,
        grid=(m // bm, nj),
        in_specs=[pl.BlockSpec((bm, k), lambda i, j: (i, 0)),
                  pl.BlockSpec((None, k, bn), lambda i, j: (layer, 0, j))] + cast_in,
        out_specs=[pl.BlockSpec((bm, bn), lambda i, j: (i, j))] + cast_out,
        out_shape=[jax.ShapeDtypeStruct((m, n), BF16)] + cast_shapes,
        compiler_params=_cparams(("arbitrary", "arbitrary")),
        name="matmul",
    )(x, w, *cast_args)
    return outs[0], outs[1:]


def _matmul_residual(x, w, resid, alpha, bm):
    m, k = x.shape
    bn = w.shape[2]
    n = w.shape[0] * bn
    tile = pl.BlockSpec((bm, bn), lambda i, j: (i, j))
    if len(resid) == 1:
        body, resid_specs = _mm_res_kernel, [tile]
    else:
        body = _mm_res_ln_kernel
        resid = (resid[0], resid[1], resid[2].reshape(1, n), resid[3].reshape(1, n))
        resid_specs = [tile, pl.BlockSpec((bm, 2 * LANES), lambda i, j: (i, 0)),
                       pl.BlockSpec((1, bn), lambda i, j: (0, j)), pl.BlockSpec((1, bn), lambda i, j: (0, j))]
    return pl.pallas_call(
        functools.partial(body, alpha=alpha),
        grid=(m // bm, n // bn),
        in_specs=[pl.BlockSpec((bm, k), lambda i, j: (i, 0)),
                  pl.BlockSpec((None, k, bn), lambda i, j: (j, 0, 0))] + resid_specs,
        out_specs=tile,
        out_shape=jax.ShapeDtypeStruct((m, n), F32),
        compiler_params=_cparams(("parallel", "arbitrary")),
        name="matmul_residual",
    )(x, w, *resid)


def _matmul_swiglu(x, w, bm):
    m, k = x.shape
    bn = w.shape[2]
    nb = w.shape[0] // 2
    f = nb * bn
    return pl.pallas_call(
        _mm_swiglu_kernel,
        grid=(m // bm, nb),
        in_specs=[pl.BlockSpec((bm, k), lambda i, j: (i, 0)),
                  pl.BlockSpec((None, k, bn), lambda i, j: (j, 0, 0)),
                  pl.BlockSpec((None, k, bn), lambda i, j: (j + nb, 0, 0))],
        out_specs=pl.BlockSpec((bm, bn), lambda i, j: (i, j)),
        out_shape=jax.ShapeDtypeStruct((m, f), BF16),
        compiler_params=_cparams(("parallel", "arbitrary")),
        name="matmul_swiglu",
    )(x, w, w)


def _na_bias_tables(rpb):
    nh = rpb.shape[0]
    qc = np.arange(GRID_W)[:, None]
    kc = np.arange(GRID_W)[None, :]
    win_start = np.clip(qc - WIN_C // 2, 0, GRID_W - WIN_C)
    col_ok = (kc >= win_start) & (kc < win_start + WIN_C)
    dc_idx = np.clip(kc - qc + (WIN_C - 1), 0, 2 * WIN_C - 2)
    onehot = (dc_idx.reshape(-1)[None, :] == np.arange(2 * WIN_C - 1)[:, None]).astype(np.float32)
    tiles = jnp.einsum("hdc,cq->hdq", rpb.astype(F32), jnp.asarray(onehot),
                       precision=lax.Precision.HIGHEST).reshape(nh, 2 * WIN_R - 1, GRID_W, GRID_W)
    tiles = jnp.where(jnp.asarray(col_ok)[None, None], tiles, NEG_INF)
    dr = np.arange(WIN_R)[:, None] + np.arange(WIN_R)[None, :]
    t = tiles[:, dr]
    return t.transpose(0, 1, 3, 2, 4).reshape(nh, WIN_R, GRID_W, WIN_R * GRID_W)


def _na_kernel(q_ref, k_ref, v_ref, bias_ref, o_ref, *, rows, scale):
    nk = WIN_R * GRID_W
    nq = NA_ROWS_PER_STEP * GRID_W
    hd = q_ref.shape[-1]

    def group(g, carry):
        ks, vs, bias = [], [], []
        for i in range(NA_ROWS_PER_STEP):
            r = g * NA_ROWS_PER_STEP + i
            row_start = jnp.clip(r - WIN_R // 2, 0, rows - WIN_R)
            k0 = pl.multiple_of(row_start * GRID_W, GRID_W)
            ks.append(k_ref[0, pl.ds(k0, nk), :])
            vs.append(v_ref[0, pl.ds(k0, nk), :])
            bias.append(bias_ref[row_start - r + (WIN_R - 1)])
        q_rows = pl.ds(pl.multiple_of(g * nq, nq), nq)
        q = q_ref[0, q_rows, :].reshape(NA_ROWS_PER_STEP, GRID_W, hd)
        s = jnp.einsum("rqd,rkd->rqk", q, jnp.stack(ks), preferred_element_type=F32)
        s = s * scale + jnp.stack(bias)
        m = jnp.max(s, -1, keepdims=True)
        p = jnp.exp(s - m)
        l = jnp.sum(p, -1, keepdims=True)
        o = jnp.einsum("rqk,rkd->rqd", p.astype(BF16), jnp.stack(vs), preferred_element_type=F32) / l
        o_ref[0, q_rows, :] = o.reshape(nq, hd).astype(o_ref.dtype)
        return carry

    lax.fori_loop(0, rows // NA_ROWS_PER_STEP, group, 0, unroll=2)


def _neighbourhood_attention(proj3, b0, bsz, seq_len, bias, nh, hd):
    rows = seq_len // GRID_W
    assert rows % NA_ROWS_PER_STEP == 0 and rows >= WIN_R
    return pl.pallas_call(
        functools.partial(_na_kernel, rows=rows, scale=hd ** -0.5),
        grid=(nh, bsz),
        in_specs=[pl.BlockSpec((1, seq_len, hd), lambda h, b: (b0 + b, 0, h)),
                  pl.BlockSpec((1, seq_len, hd), lambda h, b: (b0 + b, 0, nh + h)),
                  pl.BlockSpec((1, seq_len, hd), lambda h, b: (b0 + b, 0, 2 * nh + h)),
                  pl.BlockSpec((None, WIN_R, GRID_W, WIN_R * GRID_W), lambda h, b: (h, 0, 0, 0))],
        out_specs=pl.BlockSpec((1, seq_len, hd), lambda h, b: (b, 0, h)),
        out_shape=jax.ShapeDtypeStruct((bsz, seq_len, nh * hd), BF16),
        compiler_params=_cparams(("parallel", "parallel")),
        name="neighbourhood_attention",
    )(proj3, proj3, proj3, bias)


def _conv3_kernel(x_ref, w_ref, b_ref, o_ref, *, seq_len, rb):
    w = w_ref[...]
    bias = b_ref[...]
    n_chunks = seq_len // rb

    def body(c, carry):
        r0 = pl.multiple_of(c * rb, rb)
        x = x_ref[0, pl.ds(r0, rb), :].astype(F32)
        row = lax.broadcasted_iota(jnp.int32, x.shape, 0)
        p0 = pl.multiple_of(jnp.maximum(r0 - 16, 0), 16)
        n0 = pl.multiple_of(jnp.minimum(r0 + rb, seq_len - 16), 16)
        prev_row = x_ref[0, pl.ds(p0, 16), :].astype(F32)[15:16]
        next_row = x_ref[0, pl.ds(n0, 16), :].astype(F32)[0:1]
        prev_row = jnp.where(c == 0, 0.0, prev_row)
        next_row = jnp.where(c == n_chunks - 1, 0.0, next_row)
        up = jnp.where(row == 0, prev_row, pltpu.roll(x, 1, 0))
        down = jnp.where(row == rb - 1, next_row, pltpu.roll(x, rb - 1, 0))
        y = up * w[0:1] + x * w[1:2] + down * w[2:3] + bias
        o_ref[0, pl.ds(r0, rb), :] = y.astype(o_ref.dtype)
        return carry

    lax.fori_loop(0, n_chunks, body, 0)


def _short_conv3(proj3, b0, bsz, seq_len, col0, w, b, cb=256, rb=512):
    width = w.shape[1]
    rb = min(rb, seq_len)
    c0 = col0 // cb
    return pl.pallas_call(
        functools.partial(_conv3_kernel, seq_len=seq_len, rb=rb),
        grid=(bsz, width // cb),
        in_specs=[pl.BlockSpec((1, seq_len, cb), lambda bi, c: (b0 + bi, 0, c0 + c)),
                  pl.BlockSpec((3, cb), lambda bi, c: (0, c)),
                  pl.BlockSpec((1, cb), lambda bi, c: (0, c))],
        out_specs=pl.BlockSpec((1, seq_len, cb), lambda bi, c: (bi, 0, c)),
        out_shape=jax.ShapeDtypeStruct((bsz, seq_len, width), BF16),
        compiler_params=_cparams(("parallel", "parallel")),
        name="hyena_short_conv",
    )(proj3, w, b.reshape(1, width))


def _filter_kernel(z_ref, w1_ref, b1_ref, fr_ref, w2_ref, b2_ref, w3_ref, b3_ref, dl_ref, o_ref, *,
                   seq_len, rb):
    i = pl.program_id(0)
    hi = lax.Precision.HIGHEST
    z = z_ref[...]
    h = jnp.sin(fr_ref[0:1] * (jnp.dot(z, w1_ref[...], precision=hi, preferred_element_type=F32)
                               + b1_ref[...]))
    h = jnp.sin(fr_ref[1:2] * (jnp.dot(h, w2_ref[...], precision=hi, preferred_element_type=F32)
                               + b2_ref[...]))
    ch = dl_ref.shape[-1]
    n = i * rb + lax.broadcasted_iota(jnp.int32, (rb, ch), 0)
    sign = jnp.where(n < seq_len, 1.0, jnp.where(n == seq_len, 0.0, -1.0))
    window = jnp.exp(-z[:, 0:1] * dl_ref[...]) * sign
    hb = h.astype(BF16)
    for o in range(w3_ref.shape[0]):
        f = jnp.dot(hb, w3_ref[o].astype(BF16), preferred_element_type=F32) + b3_ref[o]
        o_ref[:, o * ch:(o + 1) * ch] = (f * window).astype(o_ref.dtype)


def _hyena_filter_signal(seq_len, w1, b1, freq, w2, b2, w3, b3, order, ch, rb=512):
    n = 2 * seq_len
    emb, hid = w1.shape
    pad = LANES
    t = jnp.linspace(0.0, 1.0, seq_len, dtype=F32)[:, None]
    bands = (emb - 1) // 2
    fr = jnp.linspace(1e-4, bands - 1, bands, dtype=F32)[None, :]
    wpos = 2.0 * math.pi * jnp.arange(seq_len, dtype=F32)[:, None] / seq_len
    z = jnp.concatenate([t, jnp.cos(fr * wpos), -jnp.sin(fr * wpos)], axis=-1)
    pos = np.arange(n)
    src = np.clip(np.where(pos < seq_len, pos, n - pos), 0, seq_len - 1)
    z2 = jnp.pad(z[src], ((0, 0), (0, pad - emb)))
    w1p = jnp.pad(w1.astype(F32), ((0, pad - emb), (0, pad - hid)))
    b1p = jnp.pad(b1.astype(F32), (0, pad - hid)).reshape(1, pad)
    frp = jnp.pad(freq.astype(F32), ((0, 0), (0, pad - hid)), constant_values=1.0)
    w2p = jnp.pad(w2.astype(F32), ((0, pad - hid), (0, pad - hid)))
    b2p = jnp.pad(b2.astype(F32), (0, pad - hid)).reshape(1, pad)
    w3p = jnp.pad(w3.astype(F32), ((0, pad - hid), (0, 0))).reshape(pad, order, 2, ch).transpose(2, 1, 0, 3)
    b3p = b3.astype(F32).reshape(order, 2, 1, ch).transpose(1, 0, 2, 3)
    deltas = jnp.abs(jnp.linspace(HY_MIN_DECAY, HY_MAX_DECAY, ch, dtype=F32)).reshape(1, ch)
    nb = n // rb
    half = nb // 2
    full = lambda i: (0, 0)
    return pl.pallas_call(
        functools.partial(_filter_kernel, seq_len=seq_len, rb=rb),
        grid=(nb,),
        in_specs=[pl.BlockSpec((rb, pad), lambda i: (i, 0)),
                  pl.BlockSpec((pad, pad), full), pl.BlockSpec((1, pad), full),
                  pl.BlockSpec((2, pad), full),
                  pl.BlockSpec((pad, pad), full), pl.BlockSpec((1, pad), full),
                  pl.BlockSpec((None, order, pad, ch), lambda i: (i // half, 0, 0, 0)),
                  pl.BlockSpec((None, order, 1, ch), lambda i: (i // half, 0, 0, 0)),
                  pl.BlockSpec((1, ch), full)],
        out_specs=pl.BlockSpec((rb, order * ch), lambda i: (i, 0)),
        out_shape=jax.ShapeDtypeStruct((n, order * ch), BF16),
        compiler_params=_cparams(("parallel",)),
        name="hyena_filter_mlp",
    )(z2, w1p, b1p, frp, w2p, b2p, w3p, b3p, deltas)


def _dft_tables(seq_len):
    n = 2 * seq_len
    n2 = DFT_N2
    n1 = n // n2
    hk = n2 // 2
    kg = min(n1, DFT_GROUP)
    ng = n1 // kg
    pi = math.pi
    lows = np.arange(n1 // 2).reshape(ng, kg // 2)
    k1 = jnp.asarray(np.concatenate([lows, n1 - 1 - lows], axis=1).reshape(-1), jnp.int32)
    mirror = jnp.asarray(np.tile(np.arange(kg) >= kg // 2, ng))
    slab = jnp.arange(n1, dtype=jnp.int32)
    ph1 = ((2 * k1[:, None] + 1) * slab[None, :]) % (2 * n1)
    th1 = ph1.astype(F32) * (pi / n1)
    c1 = jnp.cos(th1).reshape(ng, kg, n1)
    s1 = jnp.sin(th1).reshape(ng, kg, n1)
    f1 = jnp.concatenate([c1[:, : kg // 2], -s1[:, : kg // 2]], axis=1)
    fb = jnp.concatenate([c1, -s1], axis=1).transpose(0, 2, 1)[:, : n1 // 2] * (2.0 / n)
    k = k1[:, None, None] + n1 * jnp.arange(hk, dtype=jnp.int32)[None, :, None]
    ph2 = ((2 * k + 1) * jnp.arange(n2, dtype=jnp.int32)[None, None, :]) % (2 * n)
    th2 = ph2.astype(F32) * (pi / n)
    c2, s2 = jnp.cos(th2), jnp.sin(th2)
    sgn = jnp.where(mirror, -1.0, 1.0)[:, None, None]
    g = jnp.concatenate([jnp.concatenate([c2, sgn * s2], axis=2),
                         jnp.concatenate([-s2, sgn * c2], axis=2)], axis=1)
    c2t, s2t = c2.transpose(0, 2, 1), s2.transpose(0, 2, 1)
    h = jnp.concatenate([jnp.concatenate([c2t, -s2t], axis=2),
                         jnp.concatenate([s2t, c2t], axis=2)], axis=1)
    nl = 2 if n1 <= DFT_GROUP else 1
    return dict(n1=n1, kg=kg, ng=ng, nl=nl, f1_full=f1.astype(BF16), f1_half=f1[:, :, : n1 // 2].astype(BF16),
                fb=fb.astype(BF16), g=g.astype(BF16), h=h.astype(BF16))


def _load_lanes(ref, rows):
    parts = [ref[s, rows, :] for s in range(ref.shape[0])]
    return parts[0] if len(parts) == 1 else jnp.concatenate(parts, axis=1)


def _store_lanes(ref, rows, val):
    for s in range(ref.shape[0]):
        ref[s, rows, :] = val[:, s * LANES:(s + 1) * LANES]


def _fill_slabs(x_ref, xs, slabs, pitch):
    def body(n1, carry):
        src = pl.ds(pl.multiple_of(n1 * DFT_N2, DFT_N2), DFT_N2)
        _store_lanes(xs, pl.ds(pl.multiple_of(n1 * pitch, SUBLANES), DFT_N2), x_ref[src, :].astype(F32))
        return carry
    lax.fori_loop(0, slabs, body, 0, unroll=4)


def _slab_stage(xs, a_s, f1, slabs, p_x, p_a):
    rows = f1.shape[0]

    def body(i, carry):
        xn = _load_lanes(xs, pl.ds(i, slabs, stride=p_x)).astype(BF16)
        _store_lanes(a_s, pl.ds(pl.multiple_of(i * p_a, SUBLANES), rows),
                     jnp.dot(f1, xn, preferred_element_type=F32))
        return carry
    lax.fori_loop(0, DFT_N2, body, 0, unroll=32)


def _load_slab_freq(a_s, low, kg, p_a):
    are = _load_lanes(a_s, pl.ds(low, DFT_N2, stride=p_a))
    aim = _load_lanes(a_s, pl.ds(kg // 2 + low, DFT_N2, stride=p_a))
    return jnp.concatenate([are, aim], axis=0).astype(BF16)


def _for_each_slot(kg, body):
    for first in (0, kg // 2):
        def step(low, carry, first=first):
            body(first + low, low)
            return carry
        lax.fori_loop(0, kg // 2, step, 0, unroll=8)


def _filter_spec_kernel(x_ref, f1_ref, g_ref, o_ref, xs, a_s, *, slabs, kg):
    grp = pl.program_id(1)
    p_x, p_a = _pitch(DFT_N2), _pitch(kg)

    @pl.when(grp == 0)
    def _():
        _fill_slabs(x_ref, xs, slabs, p_x)

    _slab_stage(xs, a_s, f1_ref[0], slabs, p_x, p_a)

    def body(j, low):
        o_ref[j] = jnp.dot(g_ref[j], _load_slab_freq(a_s, low, kg, p_a), preferred_element_type=F32)
    _for_each_slot(kg, body)


def _filter_spectrum(f, tables):
    n, cols = f.shape
    n1, kg, ng, nl = tables["n1"], tables["kg"], tables["ng"], tables["nl"]
    p_x, p_a = _pitch(DFT_N2), _pitch(kg)
    wl = nl * LANES
    return pl.pallas_call(
        functools.partial(_filter_spec_kernel, slabs=n1, kg=kg),
        grid=(cols // wl, ng),
        in_specs=[pl.BlockSpec((n, wl), lambda c, g: (0, c)),
                  pl.BlockSpec((1, kg, n1), lambda c, g: (g, 0, 0)),
                  pl.BlockSpec((kg, DFT_N2, 2 * DFT_N2), lambda c, g: (g, 0, 0))],
        out_specs=pl.BlockSpec((kg, DFT_N2, wl), lambda c, g: (g, 0, c)),
        out_shape=jax.ShapeDtypeStruct((n1, DFT_N2, cols), F32),
        scratch_shapes=[pltpu.VMEM((nl, n1 * p_x, LANES), F32),
                        pltpu.VMEM((nl, DFT_N2 * p_a, LANES), F32)],
        compiler_params=_cparams(("parallel", "arbitrary")),
        name="hyena_filter_spectrum",
    )(f, tables["f1_full"], tables["g"])


def _long_conv_kernel(x_ref, gate_ref, f1_ref, g_ref, h_ref, kf_ref, fb_ref, skip_ref, o_ref,
                      xs, a_s, z_s, y_s, *, slabs, kg, ng):
    grp = pl.program_id(2)
    n2 = DFT_N2
    hk = n2 // 2
    p_x, p_a, p_z, p_y = _pitch(n2), _pitch(kg), _pitch(2 * n2), _pitch(slabs)

    @pl.when(grp == 0)
    def _():
        _fill_slabs(x_ref.at[0], xs, slabs, p_x)
        y_s[...] = jnp.zeros_like(y_s)

    _slab_stage(xs, a_s, f1_ref[0], slabs, p_x, p_a)

    def freq_body(j, low):
        x = jnp.dot(g_ref[j], _load_slab_freq(a_s, low, kg, p_a), preferred_element_type=F32)
        kf = kf_ref[j]
        xr, xi = x[:hk], x[hk:]
        kr, ki = kf[:hk], kf[hk:]
        y = jnp.concatenate([xr * kr - xi * ki, xr * ki + xi * kr], axis=0)
        _store_lanes(z_s, pl.ds(pl.multiple_of(j * p_z, SUBLANES), n2), y)
    _for_each_slot(kg, freq_body)

    def inv_freq_body(j, carry):
        y = _load_lanes(z_s, pl.ds(pl.multiple_of(j * p_z, SUBLANES), n2)).astype(BF16)
        _store_lanes(z_s, pl.ds(pl.multiple_of(j * p_z, SUBLANES), 2 * n2),
                     jnp.dot(h_ref[j], y, preferred_element_type=F32))
        return carry
    lax.fori_loop(0, kg, inv_freq_body, 0, unroll=8)

    fb = fb_ref[0]

    def inv_body(t, carry):
        zre = _load_lanes(z_s, pl.ds(t, kg, stride=p_z))
        zim = _load_lanes(z_s, pl.ds(n2 + t, kg, stride=p_z))
        zz = jnp.concatenate([zre, zim], axis=0).astype(BF16)
        dst = pl.ds(pl.multiple_of(t * p_y, SUBLANES), slabs)
        _store_lanes(y_s, dst, _load_lanes(y_s, dst) + jnp.dot(fb, zz, preferred_element_type=F32))
        return carry
    lax.fori_loop(0, n2, inv_body, 0, unroll=32)

    @pl.when(grp == ng - 1)
    def _():
        skip = skip_ref[...]

        def out_body(t1, carry):
            y = _load_lanes(y_s, pl.ds(t1, n2, stride=p_y))
            u = _load_lanes(xs, pl.ds(pl.multiple_of(t1 * p_x, SUBLANES), n2))
            rows = pl.ds(pl.multiple_of(t1 * n2, n2), n2)
            gate = gate_ref[0, rows, :].astype(F32)
            o_ref[0, rows, :] = ((y + skip * u) * gate).astype(o_ref.dtype)
            return carry
        lax.fori_loop(0, slabs, out_body, 0, unroll=4)


def _long_conv_gate(u, u_blk0, gate, gate_blk0, tables, kf, kf_blk0, skip, ch):
    bsz, seq_len, _ = u.shape
    n1, kg, ng, nl = tables["n1"], tables["kg"], tables["ng"], tables["nl"]
    slabs = n1 // 2
    p_x, p_a, p_z, p_y = _pitch(DFT_N2), _pitch(kg), _pitch(2 * DFT_N2), _pitch(slabs)
    wl = nl * LANES
    return pl.pallas_call(
        functools.partial(_long_conv_kernel, slabs=slabs, kg=kg, ng=ng),
        grid=(bsz, ch // wl, ng),
        in_specs=[pl.BlockSpec((1, seq_len, wl), lambda b, c, g: (b, 0, u_blk0 + c)),
                  pl.BlockSpec((1, seq_len, wl), lambda b, c, g: (b, 0, gate_blk0 + c)),
                  pl.BlockSpec((1, kg, slabs), lambda b, c, g: (g, 0, 0)),
                  pl.BlockSpec((kg, DFT_N2, 2 * DFT_N2), lambda b, c, g: (g, 0, 0)),
                  pl.BlockSpec((kg, 2 * DFT_N2, DFT_N2), lambda b, c, g: (g, 0, 0)),
                  pl.BlockSpec((kg, DFT_N2, wl), lambda b, c, g: (g, 0, kf_blk0 + c)),
                  pl.BlockSpec((1, slabs, 2 * kg), lambda b, c, g: (g, 0, 0)),
                  pl.BlockSpec((1, wl), lambda b, c, g: (0, c))],
        out_specs=pl.BlockSpec((1, seq_len, wl), lambda b, c, g: (b, 0, c)),
        out_shape=jax.ShapeDtypeStruct((bsz, seq_len, ch), BF16),
        scratch_shapes=[pltpu.VMEM((nl, slabs * p_x, LANES), F32),
                        pltpu.VMEM((nl, DFT_N2 * p_a, LANES), F32),
                        pltpu.VMEM((nl, kg * p_z, LANES), F32),
                        pltpu.VMEM((nl, DFT_N2 * p_y, LANES), F32)],
        compiler_params=_cparams(("parallel", "parallel", "arbitrary")),
        name="hyena_long_conv",
    )(u, gate, tables["f1_half"], tables["g"], tables["h"], kf, tables["fb"],
      skip.astype(F32).reshape(1, ch))


def _hyena(hy, tables, kf, skip, order, ch):
    nblk = ch // (tables["nl"] * LANES)
    z, z_blk0 = hy, 0
    for o in range(order):
        z = _long_conv_gate(z, z_blk0, hy, (o + 1) * nblk, tables, kf, o * nblk, skip[o], ch)
    return z


def _rope(x, cos, sin):
    half = x.shape[-1] // 2
    x1, x2 = x[:, :half], x[:, half:]
    return jnp.concatenate([x1 * cos - x2 * sin, x1 * sin + x2 * cos], axis=-1)


def _ret_kernel(lg_ref, q_ref, k_ref, v_ref, g_ref, cos_ref, sin_ref, o_ref,
                sb_all, kr_all, kz_all, sf_ref, sb_ref, *, n_super, chunks_per_super, k_scale):
    h = pl.program_id(1)
    s = pl.program_id(2)
    cc = RET_CHUNK
    dh = q_ref.shape[-1]
    lgf = lg_ref[0, h]
    lgb = lg_ref[1, h]
    row = lax.broadcasted_iota(jnp.int32, (cc, dh), 0).astype(F32)
    zeta_f = jnp.exp(lgf * (cc - 1.0 - row))

    @pl.when(s < n_super)
    def _backward_sweep():
        @pl.when(s == 0)
        def _():
            sb_ref[...] = jnp.zeros_like(sb_ref)

        sup = n_super - 1 - s
        zeta_b = jnp.exp(lgb * row)
        chunk_decay = jnp.exp(lgb * jnp.full((1, dh), float(cc), F32))

        def body(t, carry):
            c = chunks_per_super - 1 - t
            c0 = pl.multiple_of(c * cc, cc)
            n = sup * chunks_per_super + c
            sb_all[n] = sb_ref[...].astype(BF16)
            k = _rope(k_ref[0, pl.ds(c0, cc), :].astype(F32), cos_ref[pl.ds(c0, cc), :],
                      sin_ref[pl.ds(c0, cc), :]) * k_scale
            kr_all[n] = k.astype(BF16)
            kz_all[n] = (k * zeta_f).astype(BF16)
            kz = (k * zeta_b).astype(BF16)
            v = v_ref[0, pl.ds(c0, cc), :]
            upd = lax.dot_general(kz, v, (((0,), (0,)), ((), ())), preferred_element_type=F32)
            sb_ref[...] = sb_ref[...] * chunk_decay + upd
            return carry

        lax.fori_loop(0, chunks_per_super, body, 0, unroll=4)

    @pl.when(s >= n_super)
    def _forward_sweep():
        @pl.when(s == n_super)
        def _():
            sf_ref[...] = jnp.zeros_like(sf_ref)

        sup = s - n_super
        col = lax.broadcasted_iota(jnp.int32, (cc, cc), 1).astype(F32)
        rowc = lax.broadcasted_iota(jnp.int32, (cc, cc), 0).astype(F32)
        diff = rowc - col
        inner_decay = jnp.where(diff >= 0, jnp.exp(lgf * jnp.maximum(diff, 0.0)),
                                jnp.exp(lgb * jnp.maximum(-diff, 0.0)))
        xi_f = jnp.exp(lgf * (row + 1.0))
        xi_b = jnp.exp(lgb * (cc - row))
        chunk_decay = jnp.exp(lgf * jnp.full((1, dh), float(cc), F32))

        def body(c, carry):
            c0 = pl.multiple_of(c * cc, cc)
            n = sup * chunks_per_super + c
            q = _rope(q_ref[0, pl.ds(c0, cc), :].astype(F32), cos_ref[pl.ds(c0, cc), :],
                      sin_ref[pl.ds(c0, cc), :]).astype(BF16)
            v = v_ref[0, pl.ds(c0, cc), :]
            sc = lax.dot_general(q, kr_all[n], (((1,), (1,)), ((), ())),
                                 preferred_element_type=F32) * inner_decay
            ret = jnp.dot(sc.astype(BF16), v, preferred_element_type=F32)
            ret = ret + xi_f * jnp.dot(q, sf_ref[...].astype(BF16), preferred_element_type=F32)
            ret = ret + xi_b * jnp.dot(q, sb_all[n], preferred_element_type=F32)
            upd = lax.dot_general(kz_all[n], v, (((0,), (0,)), ((), ())), preferred_element_type=F32)
            sf_ref[...] = sf_ref[...] * chunk_decay + upd
            ret = ret * lax.rsqrt(jnp.mean(ret * ret, -1, keepdims=True) + RMS_EPS)
            gate = g_ref[0, pl.ds(c0, cc), :].astype(F32)
            o_ref[0, pl.ds(c0, cc), :] = (ret * (gate / (1.0 + jnp.exp(-gate)))).astype(o_ref.dtype)
            return carry

        lax.fori_loop(0, chunks_per_super, body, 0, unroll=4)


def _retention(proj3, b0, bsz, seq_len, col0, nh, dh, log_gamma, cos, sin):
    sup_len = min(RET_SUPER, seq_len)
    n_super = seq_len // sup_len
    cps = sup_len // RET_CHUNK
    cb0 = col0 // dh

    def fwd_idx(s):
        return jnp.maximum(s - n_super, 0)

    def kv_idx(s):
        return jnp.where(s < n_super, n_super - 1 - s, s - n_super)

    def k_idx(s):
        return jnp.maximum(n_super - 1 - s, 0)

    n_chunks = seq_len // RET_CHUNK
    return pl.pallas_call(
        functools.partial(_ret_kernel, n_super=n_super, chunks_per_super=cps, k_scale=dh ** -0.5),
        grid=(bsz, nh, 2 * n_super),
        in_specs=[pl.BlockSpec(memory_space=pltpu.SMEM),
                  pl.BlockSpec((1, sup_len, dh), lambda b, h, s: (b0 + b, fwd_idx(s), cb0 + h)),
                  pl.BlockSpec((1, sup_len, dh), lambda b, h, s: (b0 + b, k_idx(s), cb0 + nh + h)),
                  pl.BlockSpec((1, sup_len, dh), lambda b, h, s: (b0 + b, kv_idx(s), cb0 + 2 * nh + h)),
                  pl.BlockSpec((1, sup_len, dh), lambda b, h, s: (b0 + b, fwd_idx(s), cb0 + 3 * nh + h)),
                  pl.BlockSpec((sup_len, dh // 2), lambda b, h, s: (kv_idx(s), 0)),
                  pl.BlockSpec((sup_len, dh // 2), lambda b, h, s: (kv_idx(s), 0))],
        out_specs=pl.BlockSpec((1, sup_len, dh), lambda b, h, s: (b, fwd_idx(s), h)),
        out_shape=jax.ShapeDtypeStruct((bsz, seq_len, nh * dh), BF16),
        scratch_shapes=[pltpu.VMEM((n_chunks, dh, dh), BF16),
                        pltpu.VMEM((n_chunks, RET_CHUNK, dh), BF16),
                        pltpu.VMEM((n_chunks, RET_CHUNK, dh), BF16),
                        pltpu.VMEM((dh, dh), F32),
                        pltpu.VMEM((dh, dh), F32)],
        compiler_params=_cparams(("parallel", "parallel", "arbitrary")),
        name="retention",
    )(log_gamma, proj3, proj3, proj3, proj3, cos, sin)


def _assemble_kernel(a1_ref, a2_ref, z1_ref, z2_ref, c1_ref, c2_ref, ga_ref, gb_ref, o_ref, *, nb1):
    first = pl.program_id(0) < nb1
    wa = a1_ref.shape[-1]
    wb = z1_ref.shape[-1]
    a = jnp.where(first, a1_ref[...], a2_ref[...]).astype(F32)
    a = a * lax.rsqrt(jnp.mean(a * a, -1, keepdims=True) + RMS_EPS) * ga_ref[...]
    z = jnp.where(first, z1_ref[...], z2_ref[...]).astype(F32)
    z = z * lax.rsqrt(jnp.mean(z * z, -1, keepdims=True) + RMS_EPS) * gb_ref[...]
    o_ref[:, :wa] = a.astype(o_ref.dtype)
    o_ref[:, wa:wa + wb] = z.astype(o_ref.dtype)
    o_ref[:, wa + wb:] = jnp.where(first, c1_ref[...], c2_ref[...])


def _assemble(a_parts, z_parts, c_parts, gain_a, gain_b, bm=512):
    m1, wa = a_parts[0].shape
    m = m1 + a_parts[1].shape[0]
    wb, wc = z_parts[0].shape[1], c_parts[0].shape[1]
    nb1 = m1 // bm
    part_specs = []
    for w in (wa, wb, wc):
        part_specs += [pl.BlockSpec((bm, w), _first_rows(nb1)), pl.BlockSpec((bm, w), _second_rows(nb1))]
    return pl.pallas_call(
        functools.partial(_assemble_kernel, nb1=nb1),
        grid=(m // bm,),
        in_specs=part_specs + [pl.BlockSpec((1, wa), lambda i: (0, 0)),
                               pl.BlockSpec((1, wb), lambda i: (0, 0))],
        out_specs=pl.BlockSpec((bm, wa + wb + wc), lambda i: (i, 0)),
        out_shape=jax.ShapeDtypeStruct((m, wa + wb + wc), BF16),
        compiler_params=_cparams(("arbitrary",)),
        name="assemble_mixers",
    )(*a_parts, *z_parts, *c_parts, gain_a.astype(F32).reshape(1, wa), gain_b.astype(F32).reshape(1, wb))


def _token_mixers(proj, requests, p, l, dims):
    nh_a, hd_a, ch, order, nh_r, dh_r = dims
    in_cols = proj.shape[1]
    total = proj.shape[0]
    wa = nh_a * hd_a
    hy0 = 3 * wa
    ret0 = hy0 + (order + 1) * ch
    log_gamma = jnp.log1p(-jnp.exp2(-p["ret_decay_exp"][l].astype(F32)))
    a_parts, z_parts, c_parts = [], [], []
    bias = _na_bias_tables(p["na_rpb"][l])
    for (bsz, seq_len, tok0) in requests:
        assert tok0 % seq_len == 0 and total % seq_len == 0
        proj3 = proj.reshape(total // seq_len, seq_len, in_cols)
        b0 = tok0 // seq_len
        a = _neighbourhood_attention(proj3, b0, bsz, seq_len, bias, nh_a, hd_a)
        a_parts.append(a.reshape(bsz * seq_len, wa))
        tables = _dft_tables(seq_len)
        f = _hyena_filter_signal(seq_len, p["hy_f_w1"][l], p["hy_f_b1"][l], p["hy_f_freq"][l],
                                 p["hy_f_w2"][l], p["hy_f_b2"][l], p["hy_f_w3"][l], p["hy_f_b3"][l],
                                 order, ch)
        kf = _filter_spectrum(f, tables)
        hy = _short_conv3(proj3, b0, bsz, seq_len, hy0, p["hy_conv_w"][l].astype(F32),
                          p["hy_conv_b"][l].astype(F32))
        z = _hyena(hy, tables, kf, p["hy_skip"][l], order, ch)
        z_parts.append(z.reshape(bsz * seq_len, ch))
        half = dh_r // 2
        inv_freq = 1.0 / (ROPE_BASE ** jnp.linspace(0.0, 1.0, half, dtype=F32))
        ang = jnp.arange(seq_len, dtype=F32)[:, None] * inv_freq[None, :]
        c = _retention(proj3, b0, bsz, seq_len, ret0, nh_r, dh_r, log_gamma, jnp.cos(ang), jnp.sin(ang))
        c_parts.append(c.reshape(bsz * seq_len, nh_r * dh_r))
    return _assemble(a_parts, z_parts, c_parts, p["grp_gain_a"][l], p["grp_gain_b"][l])


def _pick(n, candidates):
    for c in candidates:
        if n % c == 0:
            return c
    raise ValueError(f"no block size for {n}")


def kernel(x_prompt, x_sample, ln_in_g, ln_in_b, w_in, na_rpb, hy_conv_w, hy_conv_b, hy_f_w1, hy_f_b1,
           hy_f_freq, hy_f_w2, hy_f_b2, hy_f_w3, hy_f_b3, hy_skip, ret_decay_exp, grp_gain_a, grp_gain_b,
           w_out, ln1_g, ln1_b, w_ffn_in, w_ffn_out, ln2_g, ln2_b):
    p = dict(na_rpb=na_rpb, hy_conv_w=hy_conv_w, hy_conv_b=hy_conv_b, hy_f_w1=hy_f_w1, hy_f_b1=hy_f_b1,
             hy_f_freq=hy_f_freq, hy_f_w2=hy_f_w2, hy_f_b2=hy_f_b2, hy_f_w3=hy_f_w3, hy_f_b3=hy_f_b3,
             hy_skip=hy_skip, ret_decay_exp=ret_decay_exp, grp_gain_a=grp_gain_a, grp_gain_b=grp_gain_b)
    depth, d_model, in_cols = w_in.shape
    nh_a = na_rpb.shape[1]
    wa = grp_gain_a.shape[1]
    ch = grp_gain_b.shape[1]
    order = hy_skip.shape[1]
    nh_r = ret_decay_exp.shape[2]
    wr = (in_cols - 3 * wa - (order + 1) * ch) // 4
    dims = (nh_a, wa // nh_a, ch, order, nh_r, wr // nh_r)
    d_ff = w_ffn_out.shape[1]
    alpha = (2.0 * depth) ** 0.25

    bp, lp, _ = x_prompt.shape
    bs, ls, _ = x_sample.shape
    requests = [(bp, lp, 0), (bs, ls, bp * lp)]
    m1 = bp * lp
    m = m1 + bs * ls
    bm = _pick(m, (1024, 512, 256))
    bm_ln = 256
    assert m1 % bm_ln == 0 and m % bm_ln == 0 and m1 % 512 == 0

    xf, xb = _layernorm_join(x_prompt.reshape(m1, d_model), x_sample.reshape(bs * ls, d_model),
                             ln_in_g.astype(F32), ln_in_b.astype(F32), bm_ln)
    w_in_b = w_in[0:1].astype(BF16)
    resid = (xf,)
    for l in range(depth):
        bn_out = _pick(d_model, (1024, 512, 256) if len(resid) == 1 else (512, 256))
        cast = [(w_out, l, bn_out), (w_ffn_in, l, _pick(d_ff, (512, 256, 128))),
                (w_ffn_out, l, _pick(d_model, (512, 256)))]
        if l + 1 < depth:
            cast.append((w_in, l + 1, in_cols))
        proj, cast_w = _matmul(xb, w_in_b, 0, bm, _pick(in_cols, (1024, 512, 256)), cast)
        w_out_b, w_ffn_in_b, w_ffn_out_b = cast_w[:3]
        if l + 1 < depth:
            w_in_b = cast_w[3]
        mixed = _token_mixers(proj, requests, p, l, dims)
        y = _matmul_residual(mixed, w_out_b, resid, alpha, bm)
        g1, b1 = ln1_g[l].astype(F32), ln1_b[l].astype(F32)
        xb, stats = _layernorm_stats(y, g1, b1, bm_ln)
        hmid = _matmul_swiglu(xb, w_ffn_in_b, _pick(m, (2048, 1024, 512, 256)))
        y = _matmul_residual(hmid, w_ffn_out_b, (y, stats, g1, b1), alpha, _pick(m, (512, 256)))
        if l + 1 < depth:
            g2, b2 = ln2_g[l].astype(F32), ln2_b[l].astype(F32)
            xb, stats = _layernorm_stats(y, g2, b2, bm_ln)
            resid = (y, stats, g2, b2)
    o1, o2 = _layernorm_split(y, ln2_g[depth - 1].astype(F32), ln2_b[depth - 1].astype(F32), m1, bm_ln)
    return (o1.reshape(bp, lp, d_model), o2.reshape(bs, ls, d_model))
```

```python
import functools
import math

import numpy as np
import jax
import jax.numpy as jnp
from jax import lax
from jax.experimental import pallas as pl
from jax.experimental.pallas import tpu as pltpu

F32 = jnp.float32
BF16 = jnp.bfloat16

GRID_W = 64
WIN_R = 8
WIN_C = 16
ROPE_BASE = 10000.0
LN_EPS = 1e-5
RMS_EPS = 1e-6
NEG_INF = -1e30
HY_TARGET = 1e-2
HY_FAST_DECAY = 0.3
HY_SLOW_DECAY = 1.5
HY_MIN_DECAY = math.log(HY_TARGET) / HY_SLOW_DECAY
HY_MAX_DECAY = math.log(HY_TARGET) / HY_FAST_DECAY

V7X_VMEM_LIMIT = 56 * 1024 * 1024
LANES = 128
SUBLANES = 8
DFT_N2 = 128
DFT_GROUP = 32
NA_ROWS_PER_STEP = 8
RET_CHUNK = 256
RET_SUPER = 2048


def _cparams(sem, vmem=V7X_VMEM_LIMIT):
    return pltpu.CompilerParams(dimension_semantics=sem, vmem_limit_bytes=vmem)


def _pitch(rows):
    p = -(-rows // SUBLANES)
    return SUBLANES * (p + 1 - p % 2)


def _ln(x, g, b):
    mu = jnp.mean(x, -1, keepdims=True)
    xc = x - mu
    var = jnp.mean(xc * xc, -1, keepdims=True)
    return xc * lax.rsqrt(var + LN_EPS) * g + b


def _ln_stats_kernel(x_ref, g_ref, b_ref, ob_ref, st_ref):
    x = x_ref[...]
    mu = jnp.mean(x, -1, keepdims=True)
    xc = x - mu
    rstd = lax.rsqrt(jnp.mean(xc * xc, -1, keepdims=True) + LN_EPS)
    ob_ref[...] = (xc * rstd * g_ref[...] + b_ref[...]).astype(BF16)
    st_ref[:, :LANES] = jnp.broadcast_to(mu, (x.shape[0], LANES))
    st_ref[:, LANES:] = jnp.broadcast_to(rstd, (x.shape[0], LANES))


def _ln_join_kernel(x1_ref, x2_ref, g_ref, b_ref, of_ref, ob_ref, *, nb1):
    x = jnp.where(pl.program_id(0) < nb1, x1_ref[...], x2_ref[...])
    y = _ln(x, g_ref[...], b_ref[...])
    of_ref[...] = y
    ob_ref[...] = y.astype(BF16)


def _ln_split_kernel(x_ref, g_ref, b_ref, o1_ref, o2_ref, *, nb1):
    y = _ln(x_ref[...], g_ref[...], b_ref[...])

    @pl.when(pl.program_id(0) < nb1)
    def _():
        o1_ref[...] = y

    @pl.when(pl.program_id(0) >= nb1)
    def _():
        o2_ref[...] = y


def _first_rows(nb1):
    return lambda i: (jnp.minimum(i, nb1 - 1), 0)


def _second_rows(nb1):
    return lambda i: (jnp.maximum(i - nb1, 0), 0)


def _layernorm_join(x1, x2, g, b, bm=256):
    m1, d = x1.shape
    m = m1 + x2.shape[0]
    nb1 = m1 // bm
    return pl.pallas_call(
        functools.partial(_ln_join_kernel, nb1=nb1),
        grid=(m // bm,),
        in_specs=[pl.BlockSpec((bm, d), _first_rows(nb1)),
                  pl.BlockSpec((bm, d), _second_rows(nb1)),
                  pl.BlockSpec((1, d), lambda i: (0, 0)),
                  pl.BlockSpec((1, d), lambda i: (0, 0))],
        out_specs=[pl.BlockSpec((bm, d), lambda i: (i, 0)),
                   pl.BlockSpec((bm, d), lambda i: (i, 0))],
        out_shape=[jax.ShapeDtypeStruct((m, d), F32), jax.ShapeDtypeStruct((m, d), BF16)],
        compiler_params=_cparams(("arbitrary",)),
        name="layernorm_join",
    )(x1, x2, g.reshape(1, d), b.reshape(1, d))


def _layernorm_split(x, g, b, m1, bm=256):
    m, d = x.shape
    nb1 = m1 // bm
    return pl.pallas_call(
        functools.partial(_ln_split_kernel, nb1=nb1),
        grid=(m // bm,),
        in_specs=[pl.BlockSpec((bm, d), lambda i: (i, 0)),
                  pl.BlockSpec((1, d), lambda i: (0, 0)),
                  pl.BlockSpec((1, d), lambda i: (0, 0))],
        out_specs=[pl.BlockSpec((bm, d), _first_rows(nb1)),
                   pl.BlockSpec((bm, d), _second_rows(nb1))],
        out_shape=[jax.ShapeDtypeStruct((m1, d), F32), jax.ShapeDtypeStruct((m - m1, d), F32)],
        compiler_params=_cparams(("arbitrary",)),
        name="layernorm_split",
    )(x, g.reshape(1, d), b.reshape(1, d))


def _layernorm_stats(x, g, b, bm=256):
    m, d = x.shape
    return pl.pallas_call(
        _ln_stats_kernel,
        grid=(m // bm,),
        in_specs=[pl.BlockSpec((bm, d), lambda i: (i, 0)),
                  pl.BlockSpec((1, d), lambda i: (0, 0)),
                  pl.BlockSpec((1, d), lambda i: (0, 0))],
        out_specs=[pl.BlockSpec((bm, d), lambda i: (i, 0)),
                   pl.BlockSpec((bm, 2 * LANES), lambda i: (i, 0))],
        out_shape=[jax.ShapeDtypeStruct((m, d), BF16), jax.ShapeDtypeStruct((m, 2 * LANES), F32)],
        compiler_params=_cparams(("parallel",)),
        name="layernorm_stats",
    )(x, g.reshape(1, d), b.reshape(1, d))


def _mm_res_kernel(x_ref, w_ref, r_ref, o_ref, *, alpha):
    o_ref[...] = alpha * r_ref[...] + jnp.dot(x_ref[...], w_ref[...], preferred_element_type=F32)


def _mm_res_ln_kernel(x_ref, w_ref, y_ref, st_ref, g_ref, b_ref, o_ref, *, alpha):
    acc = jnp.dot(x_ref[...], w_ref[...], preferred_element_type=F32)
    mu = st_ref[:, :LANES]
    rstd = st_ref[:, LANES:]
    for c in range(y_ref.shape[1] // LANES):
        cols = slice(c * LANES, (c + 1) * LANES)
        r = (y_ref[:, cols] - mu) * rstd * g_ref[:, cols] + b_ref[:, cols]
        o_ref[:, cols] = alpha * r + acc[:, cols]


def _mm_swiglu_kernel(x_ref, wg_ref, wu_ref, o_ref):
    x = x_ref[...]
    g = jnp.dot(x, wg_ref[...], preferred_element_type=F32)
    u = jnp.dot(x, wu_ref[...], preferred_element_type=F32)
    o_ref[...] = (g / (1.0 + jnp.exp(-g)) * u).astype(o_ref.dtype)


def _mm_cast_kernel(x_ref, w_ref, *refs):
    n_cast = (len(refs) - 1) // 2
    o_ref = refs[n_cast]
    o_ref[...] = jnp.dot(x_ref[...], w_ref[...], preferred_element_type=F32).astype(o_ref.dtype)
    for src, dst in zip(refs[:n_cast], refs[n_cast + 1:]):
        tile = dst.shape[-1]
        for t in range(dst.shape[0]):
            dst[t] = src[:, t * tile:(t + 1) * tile].astype(dst.dtype)


def _cast_rows(rows, n_steps):
    rb = 16
    while rows % rb or rows // rb > n_steps:
        rb += 16
        assert rb <= rows
    return rb


def _matmul(x, w, layer, bm, bn, cast=()):
    m, k = x.shape
    n = w.shape[2]
    nj = n // bn
    n_steps = (m // bm) * nj
    cast_in, cast_out, cast_shapes, cast_args = [], [], [], []
    for stack, lyr, tile in cast:
        _, rows, cols = stack.shape
        rb = _cast_rows(rows, n_steps)
        last = rows // rb - 1
        cast_in.append(pl.BlockSpec((None, rb, cols),
                                    lambda i, j, lyr=lyr, last=last: (lyr, jnp.minimum(i * nj + j, last), 0)))
        cast_out.append(pl.BlockSpec((cols // tile, rb, tile),
                                     lambda i, j, last=last: (0, jnp.minimum(i * nj + j, last), 0)))
        cast_shapes.append(jax.ShapeDtypeStruct((cols // tile, rows, tile), BF16))
        cast_args.append(stack)
    outs = pl.pallas_call(
        _mm_cast_kernel,
        grid=(m // bm, nj),
        in_specs=[pl.BlockSpec((bm, k), lambda i, j: (i, 0)),
                  pl.BlockSpec((None, k, bn), lambda i, j: (layer, 0, j))] + cast_in,
        out_specs=[pl.BlockSpec((bm, bn), lambda i, j: (i, j))] + cast_out,
        out_shape=[jax.ShapeDtypeStruct((m, n), BF16)] + cast_shapes,
        compiler_params=_cparams(("arbitrary", "arbitrary")),
        name="matmul",
    )(x, w, *cast_args)
    return outs[0], outs[1:]


def _matmul_residual(x, w, resid, alpha, bm):
    m, k = x.shape
    bn = w.shape[2]
    n = w.shape[0] * bn
    tile = pl.BlockSpec((bm, bn), lambda i, j: (i, j))
    if len(resid) == 1:
        body, resid_specs = _mm_res_kernel, [tile]
    else:
        body = _mm_res_ln_kernel
        resid = (resid[0], resid[1], resid[2].reshape(1, n), resid[3].reshape(1, n))
        resid_specs = [tile, pl.BlockSpec((bm, 2 * LANES), lambda i, j: (i, 0)),
                       pl.BlockSpec((1, bn), lambda i, j: (0, j)), pl.BlockSpec((1, bn), lambda i, j: (0, j))]
    return pl.pallas_call(
        functools.partial(body, alpha=alpha),
        grid=(m // bm, n // bn),
        in_specs=[pl.BlockSpec((bm, k), lambda i, j: (i, 0)),
                  pl.BlockSpec((None, k, bn), lambda i, j: (j, 0, 0))] + resid_specs,
        out_specs=tile,
        out_shape=jax.ShapeDtypeStruct((m, n), F32),
        compiler_params=_cparams(("parallel", "arbitrary")),
        name="matmul_residual",
    )(x, w, *resid)


def _matmul_swiglu(x, w, bm):
    m, k = x.shape
    bn = w.shape[2]
    nb = w.shape[0] // 2
    f = nb * bn
    return pl.pallas_call(
        _mm_swiglu_kernel,
        grid=(m // bm, nb),
        in_specs=[pl.BlockSpec((bm, k), lambda i, j: (i, 0)),
                  pl.BlockSpec((None, k, bn), lambda i, j: (j, 0, 0)),
                  pl.BlockSpec((None, k, bn), lambda i, j: (j + nb, 0, 0))],
        out_specs=pl.BlockSpec((bm, bn), lambda i, j: (i, j)),
        out_shape=jax.ShapeDtypeStruct((m, f), BF16),
        compiler_params=_cparams(("parallel", "arbitrary")),
        name="matmul_swiglu",
    )(x, w, w)


def _na_bias_tables(rpb):
    nh = rpb.shape[0]
    qc = np.arange(GRID_W)[:, None]
    kc = np.arange(GRID_W)[None, :]
    win_start = np.clip(qc - WIN_C // 2, 0, GRID_W - WIN_C)
    col_ok = (kc >= win_start) & (kc < win_start + WIN_C)
    dc_idx = np.clip(kc - qc + (WIN_C - 1), 0, 2 * WIN_C - 2)
    onehot = (dc_idx.reshape(-1)[None, :] == np.arange(2 * WIN_C - 1)[:, None]).astype(np.float32)
    tiles = jnp.einsum("hdc,cq->hdq", rpb.astype(F32), jnp.asarray(onehot),
                       precision=lax.Precision.HIGHEST).reshape(nh, 2 * WIN_R - 1, GRID_W, GRID_W)
    tiles = jnp.where(jnp.asarray(col_ok)[None, None], tiles, NEG_INF)
    dr = np.arange(WIN_R)[:, None] + np.arange(WIN_R)[None, :]
    t = tiles[:, dr]
    return t.transpose(0, 1, 3, 2, 4).reshape(nh, WIN_R, GRID_W, WIN_R * GRID_W)


def _na_kernel(q_ref, k_ref, v_ref, bias_ref, o_ref, *, rows, scale):
    nk = WIN_R * GRID_W
    nq = NA_ROWS_PER_STEP * GRID_W
    hd = q_ref.shape[-1]

    def group(g, carry):
        ks, vs, bias = [], [], []
        for i in range(NA_ROWS_PER_STEP):
            r = g * NA_ROWS_PER_STEP + i
            row_start = jnp.clip(r - WIN_R // 2, 0, rows - WIN_R)
            k0 = pl.multiple_of(row_start * GRID_W, GRID_W)
            ks.append(k_ref[0, pl.ds(k0, nk), :])
            vs.append(v_ref[0, pl.ds(k0, nk), :])
            bias.append(bias_ref[row_start - r + (WIN_R - 1)])
        q_rows = pl.ds(pl.multiple_of(g * nq, nq), nq)
        q = q_ref[0, q_rows, :].reshape(NA_ROWS_PER_STEP, GRID_W, hd)
        s = jnp.einsum("rqd,rkd->rqk", q, jnp.stack(ks), preferred_element_type=F32)
        s = s * scale + jnp.stack(bias)
        m = jnp.max(s, -1, keepdims=True)
        p = jnp.exp(s - m)
        l = jnp.sum(p, -1, keepdims=True)
        o = jnp.einsum("rqk,rkd->rqd", p.astype(BF16), jnp.stack(vs), preferred_element_type=F32) / l
        o_ref[0, q_rows, :] = o.reshape(nq, hd).astype(o_ref.dtype)
        return carry

    lax.fori_loop(0, rows // NA_ROWS_PER_STEP, group, 0, unroll=2)


def _neighbourhood_attention(proj3, b0, bsz, seq_len, bias, nh, hd):
    rows = seq_len // GRID_W
    assert rows % NA_ROWS_PER_STEP == 0 and rows >= WIN_R
    return pl.pallas_call(
        functools.partial(_na_kernel, rows=rows, scale=hd ** -0.5),
        grid=(nh, bsz),
        in_specs=[pl.BlockSpec((1, seq_len, hd), lambda h, b: (b0 + b, 0, h)),
                  pl.BlockSpec((1, seq_len, hd), lambda h, b: (b0 + b, 0, nh + h)),
                  pl.BlockSpec((1, seq_len, hd), lambda h, b: (b0 + b, 0, 2 * nh + h)),
                  pl.BlockSpec((None, WIN_R, GRID_W, WIN_R * GRID_W), lambda h, b: (h, 0, 0, 0))],
        out_specs=pl.BlockSpec((1, seq_len, hd), lambda h, b: (b, 0, h)),
        out_shape=jax.ShapeDtypeStruct((bsz, seq_len, nh * hd), BF16),
        compiler_params=_cparams(("parallel", "parallel")),
        name="neighbourhood_attention",
    )(proj3, proj3, proj3, bias)


def _conv3_kernel(x_ref, w_ref, b_ref, o_ref, *, seq_len, rb):
    w = w_ref[...]
    bias = b_ref[...]
    n_chunks = seq_len // rb

    def body(c, carry):
        r0 = pl.multiple_of(c * rb, rb)
        x = x_ref[0, pl.ds(r0, rb), :].astype(F32)
        row = lax.broadcasted_iota(jnp.int32, x.shape, 0)
        p0 = pl.multiple_of(jnp.maximum(r0 - 16, 0), 16)
        n0 = pl.multiple_of(jnp.minimum(r0 + rb, seq_len - 16), 16)
        prev_row = x_ref[0, pl.ds(p0, 16), :].astype(F32)[15:16]
        next_row = x_ref[0, pl.ds(n0, 16), :].astype(F32)[0:1]
        prev_row = jnp.where(c == 0, 0.0, prev_row)
        next_row = jnp.where(c == n_chunks - 1, 0.0, next_row)
        up = jnp.where(row == 0, prev_row, pltpu.roll(x, 1, 0))
        down = jnp.where(row == rb - 1, next_row, pltpu.roll(x, rb - 1, 0))
        y = up * w[0:1] + x * w[1:2] + down * w[2:3] + bias
        o_ref[0, pl.ds(r0, rb), :] = y.astype(o_ref.dtype)
        return carry

    lax.fori_loop(0, n_chunks, body, 0)


def _short_conv3(proj3, b0, bsz, seq_len, col0, w, b, cb=256, rb=512):
    width = w.shape[1]
    rb = min(rb, seq_len)
    c0 = col0 // cb
    return pl.pallas_call(
        functools.partial(_conv3_kernel, seq_len=seq_len, rb=rb),
        grid=(bsz, width // cb),
        in_specs=[pl.BlockSpec((1, seq_len, cb), lambda bi, c: (b0 + bi, 0, c0 + c)),
                  pl.BlockSpec((3, cb), lambda bi, c: (0, c)),
                  pl.BlockSpec((1, cb), lambda bi, c: (0, c))],
        out_specs=pl.BlockSpec((1, seq_len, cb), lambda bi, c: (bi, 0, c)),
        out_shape=jax.ShapeDtypeStruct((bsz, seq_len, width), BF16),
        compiler_params=_cparams(("parallel", "parallel")),
        name="hyena_short_conv",
    )(proj3, w, b.reshape(1, width))


def _filter_kernel(z_ref, w1_ref, b1_ref, fr_ref, w2_ref, b2_ref, w3_ref, b3_ref, dl_ref, o_ref, *,
                   seq_len, rb):
    i = pl.program_id(0)
    hi = lax.Precision.HIGHEST
    z = z_ref[...]
    h = jnp.sin(fr_ref[0:1] * (jnp.dot(z, w1_ref[...], precision=hi, preferred_element_type=F32)
                               + b1_ref[...]))
    h = jnp.sin(fr_ref[1:2] * (jnp.dot(h, w2_ref[...], precision=hi, preferred_element_type=F32)
                               + b2_ref[...]))
    ch = dl_ref.shape[-1]
    n = i * rb + lax.broadcasted_iota(jnp.int32, (rb, ch), 0)
    sign = jnp.where(n < seq_len, 1.0, jnp.where(n == seq_len, 0.0, -1.0))
    window = jnp.exp(-z[:, 0:1] * dl_ref[...]) * sign
    hb = h.astype(BF16)
    for o in range(w3_ref.shape[0]):
        f = jnp.dot(hb, w3_ref[o].astype(BF16), preferred_element_type=F32) + b3_ref[o]
        o_ref[:, o * ch:(o + 1) * ch] = (f * window).astype(o_ref.dtype)


def _hyena_filter_signal(seq_len, w1, b1, freq, w2, b2, w3, b3, order, ch, rb=512):
    n = 2 * seq_len
    emb, hid = w1.shape
    pad = LANES
    t = jnp.linspace(0.0, 1.0, seq_len, dtype=F32)[:, None]
    bands = (emb - 1) // 2
    fr = jnp.linspace(1e-4, bands - 1, bands, dtype=F32)[None, :]
    wpos = 2.0 * math.pi * jnp.arange(seq_len, dtype=F32)[:, None] / seq_len
    z = jnp.concatenate([t, jnp.cos(fr * wpos), -jnp.sin(fr * wpos)], axis=-1)
    pos = np.arange(n)
    src = np.clip(np.where(pos < seq_len, pos, n - pos), 0, seq_len - 1)
    z2 = jnp.pad(z[src], ((0, 0), (0, pad - emb)))
    w1p = jnp.pad(w1.astype(F32), ((0, pad - emb), (0, pad - hid)))
    b1p = jnp.pad(b1.astype(F32), (0, pad - hid)).reshape(1, pad)
    frp = jnp.pad(freq.astype(F32), ((0, 0), (0, pad - hid)), constant_values=1.0)
    w2p = jnp.pad(w2.astype(F32), ((0, pad - hid), (0, pad - hid)))
    b2p = jnp.pad(b2.astype(F32), (0, pad - hid)).reshape(1, pad)
    w3p = jnp.pad(w3.astype(F32), ((0, pad - hid), (0, 0))).reshape(pad, order, 2, ch).transpose(2, 1, 0, 3)
    b3p = b3.astype(F32).reshape(order, 2, 1, ch).transpose(1, 0, 2, 3)
    deltas = jnp.abs(jnp.linspace(HY_MIN_DECAY, HY_MAX_DECAY, ch, dtype=F32)).reshape(1, ch)
    nb = n // rb
    half = nb // 2
    full = lambda i: (0, 0)
    return pl.pallas_call(
        functools.partial(_filter_kernel, seq_len=seq_len, rb=rb),
        grid=(nb,),
        in_specs=[pl.BlockSpec((rb, pad), lambda i: (i, 0)),
                  pl.BlockSpec((pad, pad), full), pl.BlockSpec((1, pad), full),
                  pl.BlockSpec((2, pad), full),
                  pl.BlockSpec((pad, pad), full), pl.BlockSpec((1, pad), full),
                  pl.BlockSpec((None, order, pad, ch), lambda i: (i // half, 0, 0, 0)),
                  pl.BlockSpec((None, order, 1, ch), lambda i: (i // half, 0, 0, 0)),
                  pl.BlockSpec((1, ch), full)],
        out_specs=pl.BlockSpec((rb, order * ch), lambda i: (i, 0)),
        out_shape=jax.ShapeDtypeStruct((n, order * ch), BF16),
        compiler_params=_cparams(("parallel",)),
        name="hyena_filter_mlp",
    )(z2, w1p, b1p, frp, w2p, b2p, w3p, b3p, deltas)


def _dft_tables(seq_len):
    n = 2 * seq_len
    n2 = DFT_N2
    n1 = n // n2
    hk = n2 // 2
    kg = min(n1, DFT_GROUP)
    ng = n1 // kg
    pi = math.pi
    lows = np.arange(n1 // 2).reshape(ng, kg // 2)
    k1 = jnp.asarray(np.concatenate([lows, n1 - 1 - lows], axis=1).reshape(-1), jnp.int32)
    mirror = jnp.asarray(np.tile(np.arange(kg) >= kg // 2, ng))
    slab = jnp.arange(n1, dtype=jnp.int32)
    ph1 = ((2 * k1[:, None] + 1) * slab[None, :]) % (2 * n1)
    th1 = ph1.astype(F32) * (pi / n1)
    c1 = jnp.cos(th1).reshape(ng, kg, n1)
    s1 = jnp.sin(th1).reshape(ng, kg, n1)
    f1 = jnp.concatenate([c1[:, : kg // 2], -s1[:, : kg // 2]], axis=1)
    fb = jnp.concatenate([c1, -s1], axis=1).transpose(0, 2, 1)[:, : n1 // 2] * (2.0 / n)
    k = k1[:, None, None] + n1 * jnp.arange(hk, dtype=jnp.int32)[None, :, None]
    ph2 = ((2 * k + 1) * jnp.arange(n2, dtype=jnp.int32)[None, None, :]) % (2 * n)
    th2 = ph2.astype(F32) * (pi / n)
    c2, s2 = jnp.cos(th2), jnp.sin(th2)
    sgn = jnp.where(mirror, -1.0, 1.0)[:, None, None]
    g = jnp.concatenate([jnp.concatenate([c2, sgn * s2], axis=2),
                         jnp.concatenate([-s2, sgn * c2], axis=2)], axis=1)
    c2t, s2t = c2.transpose(0, 2, 1), s2.transpose(0, 2, 1)
    h = jnp.concatenate([jnp.concatenate([c2t, -s2t], axis=2),
                         jnp.concatenate([s2t, c2t], axis=2)], axis=1)
    nl = 2 if n1 <= DFT_GROUP else 1
    return dict(n1=n1, kg=kg, ng=ng, nl=nl, f1_full=f1.astype(BF16), f1_half=f1[:, :, : n1 // 2].astype(BF16),
                fb=fb.astype(BF16), g=g.astype(BF16), h=h.astype(BF16))


def _load_lanes(ref, rows):
    parts = [ref[s, rows, :] for s in range(ref.shape[0])]
    return parts[0] if len(parts) == 1 else jnp.concatenate(parts, axis=1)


def _store_lanes(ref, rows, val):
    for s in range(ref.shape[0]):
        ref[s, rows, :] = val[:, s * LANES:(s + 1) * LANES]


def _fill_slabs(x_ref, xs, slabs, pitch):
    def body(n1, carry):
        src = pl.ds(pl.multiple_of(n1 * DFT_N2, DFT_N2), DFT_N2)
        _store_lanes(xs, pl.ds(pl.multiple_of(n1 * pitch, SUBLANES), DFT_N2), x_ref[src, :].astype(F32))
        return carry
    lax.fori_loop(0, slabs, body, 0, unroll=4)


def _slab_stage(xs, a_s, f1, slabs, p_x, p_a):
    rows = f1.shape[0]

    def body(i, carry):
        xn = _load_lanes(xs, pl.ds(i, slabs, stride=p_x)).astype(BF16)
        _store_lanes(a_s, pl.ds(pl.multiple_of(i * p_a, SUBLANES), rows),
                     jnp.dot(f1, xn, preferred_element_type=F32))
        return carry
    lax.fori_loop(0, DFT_N2, body, 0, unroll=32)


def _load_slab_freq(a_s, low, kg, p_a):
    are = _load_lanes(a_s, pl.ds(low, DFT_N2, stride=p_a))
    aim = _load_lanes(a_s, pl.ds(kg // 2 + low, DFT_N2, stride=p_a))
    return jnp.concatenate([are, aim], axis=0).astype(BF16)


def _for_each_slot(kg, body):
    for first in (0, kg // 2):
        def step(low, carry, first=first):
            body(first + low, low)
            return carry
        lax.fori_loop(0, kg // 2, step, 0, unroll=8)


def _filter_spec_kernel(x_ref, f1_ref, g_ref, o_ref, xs, a_s, *, slabs, kg):
    grp = pl.program_id(1)
    p_x, p_a = _pitch(DFT_N2), _pitch(kg)

    @pl.when(grp == 0)
    def _():
        _fill_slabs(x_ref, xs, slabs, p_x)

    _slab_stage(xs, a_s, f1_ref[0], slabs, p_x, p_a)

    def body(j, low):
        o_ref[j] = jnp.dot(g_ref[j], _load_slab_freq(a_s, low, kg, p_a), preferred_element_type=F32)
    _for_each_slot(kg, body)


def _filter_spectrum(f, tables):
    n, cols = f.shape
    n1, kg, ng, nl = tables["n1"], tables["kg"], tables["ng"], tables["nl"]
    p_x, p_a = _pitch(DFT_N2), _pitch(kg)
    wl = nl * LANES
    return pl.pallas_call(
        functools.partial(_filter_spec_kernel, slabs=n1, kg=kg),
        grid=(cols // wl, ng),
        in_specs=[pl.BlockSpec((n, wl), lambda c, g: (0, c)),
                  pl.BlockSpec((1, kg, n1), lambda c, g: (g, 0, 0)),
                  pl.BlockSpec((kg, DFT_N2, 2 * DFT_N2), lambda c, g: (g, 0, 0))],
        out_specs=pl.BlockSpec((kg, DFT_N2, wl), lambda c, g: (g, 0, c)),
        out_shape=jax.ShapeDtypeStruct((n1, DFT_N2, cols), F32),
        scratch_shapes=[pltpu.VMEM((nl, n1 * p_x, LANES), F32),
                        pltpu.VMEM((nl, DFT_N2 * p_a, LANES), F32)],
        compiler_params=_cparams(("parallel", "arbitrary")),
        name="hyena_filter_spectrum",
    )(f, tables["f1_full"], tables["g"])


def _long_conv_kernel(x_ref, gate_ref, f1_ref, g_ref, h_ref, kf_ref, fb_ref, skip_ref, o_ref,
                      xs, a_s, z_s, y_s, *, slabs, kg, ng):
    grp = pl.program_id(2)
    n2 = DFT_N2
    hk = n2 // 2
    p_x, p_a, p_z, p_y = _pitch(n2), _pitch(kg), _pitch(2 * n2), _pitch(slabs)

    @pl.when(grp == 0)
    def _():
        _fill_slabs(x_ref.at[0], xs, slabs, p_x)
        y_s[...] = jnp.zeros_like(y_s)

    _slab_stage(xs, a_s, f1_ref[0], slabs, p_x, p_a)

    def freq_body(j, low):
        x = jnp.dot(g_ref[j], _load_slab_freq(a_s, low, kg, p_a), preferred_element_type=F32)
        kf = kf_ref[j]
        xr, xi = x[:hk], x[hk:]
        kr, ki = kf[:hk], kf[hk:]
        y = jnp.concatenate([xr * kr - xi * ki, xr * ki + xi * kr], axis=0)
        _store_lanes(z_s, pl.ds(pl.multiple_of(j * p_z, SUBLANES), n2), y)
    _for_each_slot(kg, freq_body)

    def inv_freq_body(j, carry):
        y = _load_lanes(z_s, pl.ds(pl.multiple_of(j * p_z, SUBLANES), n2)).astype(BF16)
        _store_lanes(z_s, pl.ds(pl.multiple_of(j * p_z, SUBLANES), 2 * n2),
                     jnp.dot(h_ref[j], y, preferred_element_type=F32))
        return carry
    lax.fori_loop(0, kg, inv_freq_body, 0, unroll=8)

    fb = fb_ref[0]

    def inv_body(t, carry):
        zre = _load_lanes(z_s, pl.ds(t, kg, stride=p_z))
        zim = _load_lanes(z_s, pl.ds(n2 + t, kg, stride=p_z))
        zz = jnp.concatenate([zre, zim], axis=0).astype(BF16)
        dst = pl.ds(pl.multiple_of(t * p_y, SUBLANES), slabs)
        _store_lanes(y_s, dst, _load_lanes(y_s, dst) + jnp.dot(fb, zz, preferred_element_type=F32))
        return carry
    lax.fori_loop(0, n2, inv_body, 0, unroll=32)

    @pl.when(grp == ng - 1)
    def _():
        skip = skip_ref[...]

        def out_body(t1, carry):
            y = _load_lanes(y_s, pl.ds(t1, n2, stride=p_y))
            u = _load_lanes(xs, pl.ds(pl.multiple_of(t1 * p_x, SUBLANES), n2))
            rows = pl.ds(pl.multiple_of(t1 * n2, n2), n2)
            gate = gate_ref[0, rows, :].astype(F32)
            o_ref[0, rows, :] = ((y + skip * u) * gate).astype(o_ref.dtype)
            return carry
        lax.fori_loop(0, slabs, out_body, 0, unroll=4)


def _long_conv_gate(u, u_blk0, gate, gate_blk0, tables, kf, kf_blk0, skip, ch):
    bsz, seq_len, _ = u.shape
    n1, kg, ng, nl = tables["n1"], tables["kg"], tables["ng"], tables["nl"]
    slabs = n1 // 2
    p_x, p_a, p_z, p_y = _pitch(DFT_N2), _pitch(kg), _pitch(2 * DFT_N2), _pitch(slabs)
    wl = nl * LANES
    return pl.pallas_call(
        functools.partial(_long_conv_kernel, slabs=slabs, kg=kg, ng=ng),
        grid=(bsz, ch // wl, ng),
        in_specs=[pl.BlockSpec((1, seq_len, wl), lambda b, c, g: (b, 0, u_blk0 + c)),
                  pl.BlockSpec((1, seq_len, wl), lambda b, c, g: (b, 0, gate_blk0 + c)),
                  pl.BlockSpec((1, kg, slabs), lambda b, c, g: (g, 0, 0)),
                  pl.BlockSpec((kg, DFT_N2, 2 * DFT_N2), lambda b, c, g: (g, 0, 0)),
                  pl.BlockSpec((kg, 2 * DFT_N2, DFT_N2), lambda b, c, g: (g, 0, 0)),
                  pl.BlockSpec((kg, DFT_N2, wl), lambda b, c, g: (g, 0, kf_blk0 + c)),
                  pl.BlockSpec((1, slabs, 2 * kg), lambda b, c, g: (g, 0, 0)),
                  pl.BlockSpec((1, wl), lambda b, c, g: (0, c))],
        out_specs=pl.BlockSpec((1, seq_len, wl), lambda b, c, g: (b, 0, c)),
        out_shape=jax.ShapeDtypeStruct((bsz, seq_len, ch), BF16),
        scratch_shapes=[pltpu.VMEM((nl, slabs * p_x, LANES), F32),
                        pltpu.VMEM((nl, DFT_N2 * p_a, LANES), F32),
                        pltpu.VMEM((nl, kg * p_z, LANES), F32),
                        pltpu.VMEM((nl, DFT_N2 * p_y, LANES), F32)],
        compiler_params=_cparams(("parallel", "parallel", "arbitrary")),
        name="hyena_long_conv",
    )(u, gate, tables["f1_half"], tables["g"], tables["h"], kf, tables["fb"],
      skip.astype(F32).reshape(1, ch))


def _hyena(hy, tables, kf, skip, order, ch):
    nblk = ch // (tables["nl"] * LANES)
    z, z_blk0 = hy, 0
    for o in range(order):
        z = _long_conv_gate(z, z_blk0, hy, (o + 1) * nblk, tables, kf, o * nblk, skip[o], ch)
    return z


def _rope(x, cos, sin):
    half = x.shape[-1] // 2
    x1, x2 = x[:, :half], x[:, half:]
    return jnp.concatenate([x1 * cos - x2 * sin, x1 * sin + x2 * cos], axis=-1)


def _ret_kernel(lg_ref, q_ref, k_ref, v_ref, g_ref, cos_ref, sin_ref, o_ref,
                sb_all, kr_all, kz_all, sf_ref, sb_ref, *, n_super, chunks_per_super, k_scale):
    h = pl.program_id(1)
    s = pl.program_id(2)
    cc = RET_CHUNK
    dh = q_ref.shape[-1]
    lgf = lg_ref[0, h]
    lgb = lg_ref[1, h]
    row = lax.broadcasted_iota(jnp.int32, (cc, dh), 0).astype(F32)
    zeta_f = jnp.exp(lgf * (cc - 1.0 - row))

    @pl.when(s < n_super)
    def _backward_sweep():
        @pl.when(s == 0)
        def _():
            sb_ref[...] = jnp.zeros_like(sb_ref)

        sup = n_super - 1 - s
        zeta_b = jnp.exp(lgb * row)
        chunk_decay = jnp.exp(lgb * jnp.full((1, dh), float(cc), F32))

        def body(t, carry):
            c = chunks_per_super - 1 - t
            c0 = pl.multiple_of(c * cc, cc)
            n = sup * chunks_per_super + c
            sb_all[n] = sb_ref[...].astype(BF16)
            k = _rope(k_ref[0, pl.ds(c0, cc), :].astype(F32), cos_ref[pl.ds(c0, cc), :],
                      sin_ref[pl.ds(c0, cc), :]) * k_scale
            kr_all[n] = k.astype(BF16)
            kz_all[n] = (k * zeta_f).astype(BF16)
            kz = (k * zeta_b).astype(BF16)
            v = v_ref[0, pl.ds(c0, cc), :]
            upd = lax.dot_general(kz, v, (((0,), (0,)), ((), ())), preferred_element_type=F32)
            sb_ref[...] = sb_ref[...] * chunk_decay + upd
            return carry

        lax.fori_loop(0, chunks_per_super, body, 0, unroll=4)

    @pl.when(s >= n_super)
    def _forward_sweep():
        @pl.when(s == n_super)
        def _():
            sf_ref[...] = jnp.zeros_like(sf_ref)

        sup = s - n_super
        col = lax.broadcasted_iota(jnp.int32, (cc, cc), 1).astype(F32)
        rowc = lax.broadcasted_iota(jnp.int32, (cc, cc), 0).astype(F32)
        diff = rowc - col
        inner_decay = jnp.where(diff >= 0, jnp.exp(lgf * jnp.maximum(diff, 0.0)),
                                jnp.exp(lgb * jnp.maximum(-diff, 0.0)))
        xi_f = jnp.exp(lgf * (row + 1.0))
        xi_b = jnp.exp(lgb * (cc - row))
        chunk_decay = jnp.exp(lgf * jnp.full((1, dh), float(cc), F32))

        def body(c, carry):
            c0 = pl.multiple_of(c * cc, cc)
            n = sup * chunks_per_super + c
            q = _rope(q_ref[0, pl.ds(c0, cc), :].astype(F32), cos_ref[pl.ds(c0, cc), :],
                      sin_ref[pl.ds(c0, cc), :]).astype(BF16)
            v = v_ref[0, pl.ds(c0, cc), :]
            sc = lax.dot_general(q, kr_all[n], (((1,), (1,)), ((), ())),
                                 preferred_element_type=F32) * inner_decay
            ret = jnp.dot(sc.astype(BF16), v, preferred_element_type=F32)
            ret = ret + xi_f * jnp.dot(q, sf_ref[...].astype(BF16), preferred_element_type=F32)
            ret = ret + xi_b * jnp.dot(q, sb_all[n], preferred_element_type=F32)
            upd = lax.dot_general(kz_all[n], v, (((0,), (0,)), ((), ())), preferred_element_type=F32)
            sf_ref[...] = sf_ref[...] * chunk_decay + upd
            ret = ret * lax.rsqrt(jnp.mean(ret * ret, -1, keepdims=True) + RMS_EPS)
            gate = g_ref[0, pl.ds(c0, cc), :].astype(F32)
            o_ref[0, pl.ds(c0, cc), :] = (ret * (gate / (1.0 + jnp.exp(-gate)))).astype(o_ref.dtype)
            return carry

        lax.fori_loop(0, chunks_per_super, body, 0, unroll=4)


def _retention(proj3, b0, bsz, seq_len, col0, nh, dh, log_gamma, cos, sin):
    sup_len = min(RET_SUPER, seq_len)
    n_super = seq_len // sup_len
    cps = sup_len // RET_CHUNK
    cb0 = col0 // dh

    def fwd_idx(s):
        return jnp.maximum(s - n_super, 0)

    def kv_idx(s):
        return jnp.where(s < n_super, n_super - 1 - s, s - n_super)

    def k_idx(s):
        return jnp.maximum(n_super - 1 - s, 0)

    n_chunks = seq_len // RET_CHUNK
    return pl.pallas_call(
        functools.partial(_ret_kernel, n_super=n_super, chunks_per_super=cps, k_scale=dh ** -0.5),
        grid=(bsz, nh, 2 * n_super),
        in_specs=[pl.BlockSpec(memory_space=pltpu.SMEM),
                  pl.BlockSpec((1, sup_len, dh), lambda b, h, s: (b0 + b, fwd_idx(s), cb0 + h)),
                  pl.BlockSpec((1, sup_len, dh), lambda b, h, s: (b0 + b, k_idx(s), cb0 + nh + h)),
                  pl.BlockSpec((1, sup_len, dh), lambda b, h, s: (b0 + b, kv_idx(s), cb0 + 2 * nh + h)),
                  pl.BlockSpec((1, sup_len, dh), lambda b, h, s: (b0 + b, fwd_idx(s), cb0 + 3 * nh + h)),
                  pl.BlockSpec((sup_len, dh // 2), lambda b, h, s: (kv_idx(s), 0)),
                  pl.BlockSpec((sup_len, dh // 2), lambda b, h, s: (kv_idx(s), 0))],
        out_specs=pl.BlockSpec((1, sup_len, dh), lambda b, h, s: (b, fwd_idx(s), h)),
        out_shape=jax.ShapeDtypeStruct((bsz, seq_len, nh * dh), BF16),
        scratch_shapes=[pltpu.VMEM((n_chunks, dh, dh), BF16),
                        pltpu.VMEM((n_chunks, RET_CHUNK, dh), BF16),
                        pltpu.VMEM((n_chunks, RET_CHUNK, dh), BF16),
                        pltpu.VMEM((dh, dh), F32),
                        pltpu.VMEM((dh, dh), F32)],
        compiler_params=_cparams(("parallel", "parallel", "arbitrary")),
        name="retention",
    )(log_gamma, proj3, proj3, proj3, proj3, cos, sin)


def _assemble_kernel(a1_ref, a2_ref, z1_ref, z2_ref, c1_ref, c2_ref, ga_ref, gb_ref, o_ref, *, nb1):
    first = pl.program_id(0) < nb1
    wa = a1_ref.shape[-1]
    wb = z1_ref.shape[-1]
    a = jnp.where(first, a1_ref[...], a2_ref[...]).astype(F32)
    a = a * lax.rsqrt(jnp.mean(a * a, -1, keepdims=True) + RMS_EPS) * ga_ref[...]
    z = jnp.where(first, z1_ref[...], z2_ref[...]).astype(F32)
    z = z * lax.rsqrt(jnp.mean(z * z, -1, keepdims=True) + RMS_EPS) * gb_ref[...]
    o_ref[:, :wa] = a.astype(o_ref.dtype)
    o_ref[:, wa:wa + wb] = z.astype(o_ref.dtype)
    o_ref[:, wa + wb:] = jnp.where(first, c1_ref[...], c2_ref[...])


def _assemble(a_parts, z_parts, c_parts, gain_a, gain_b, bm=1024):
    m1, wa = a_parts[0].shape
    m = m1 + a_parts[1].shape[0]
    assert m1 % bm == 0 and m % bm == 0
    wb, wc = z_parts[0].shape[1], c_parts[0].shape[1]
    nb1 = m1 // bm
    part_specs = []
    for w in (wa, wb, wc):
        part_specs += [pl.BlockSpec((bm, w), _first_rows(nb1)), pl.BlockSpec((bm, w), _second_rows(nb1))]
    return pl.pallas_call(
        functools.partial(_assemble_kernel, nb1=nb1),
        grid=(m // bm,),
        in_specs=part_specs + [pl.BlockSpec((1, wa), lambda i: (0, 0)),
                               pl.BlockSpec((1, wb), lambda i: (0, 0))],
        out_specs=pl.BlockSpec((bm, wa + wb + wc), lambda i: (i, 0)),
        out_shape=jax.ShapeDtypeStruct((m, wa + wb + wc), BF16),
        compiler_params=_cparams(("arbitrary",)),
        name="assemble_mixers",
    )(*a_parts, *z_parts, *c_parts, gain_a.astype(F32).reshape(1, wa), gain_b.astype(F32).reshape(1, wb))


def _token_mixers(proj, requests, p, l, dims):
    nh_a, hd_a, ch, order, nh_r, dh_r = dims
    in_cols = proj.shape[1]
    total = proj.shape[0]
    wa = nh_a * hd_a
    hy0 = 3 * wa
    ret0 = hy0 + (order + 1) * ch
    log_gamma = jnp.log1p(-jnp.exp2(-p["ret_decay_exp"][l].astype(F32)))
    a_parts, z_parts, c_parts = [], [], []
    bias = _na_bias_tables(p["na_rpb"][l])
    for (bsz, seq_len, tok0) in requests:
        assert tok0 % seq_len == 0 and total % seq_len == 0
        proj3 = proj.reshape(total // seq_len, seq_len, in_cols)
        b0 = tok0 // seq_len
        a = _neighbourhood_attention(proj3, b0, bsz, seq_len, bias, nh_a, hd_a)
        a_parts.append(a.reshape(bsz * seq_len, wa))
        tables = _dft_tables(seq_len)
        f = _hyena_filter_signal(seq_len, p["hy_f_w1"][l], p["hy_f_b1"][l], p["hy_f_freq"][l],
                                 p["hy_f_w2"][l], p["hy_f_b2"][l], p["hy_f_w3"][l], p["hy_f_b3"][l],
                                 order, ch)
        kf = _filter_spectrum(f, tables)
        hy = _short_conv3(proj3, b0, bsz, seq_len, hy0, p["hy_conv_w"][l].astype(F32),
                          p["hy_conv_b"][l].astype(F32))
        z = _hyena(hy, tables, kf, p["hy_skip"][l], order, ch)
        z_parts.append(z.reshape(bsz * seq_len, ch))
        half = dh_r // 2
        inv_freq = 1.0 / (ROPE_BASE ** jnp.linspace(0.0, 1.0, half, dtype=F32))
        ang = jnp.arange(seq_len, dtype=F32)[:, None] * inv_freq[None, :]
        c = _retention(proj3, b0, bsz, seq_len, ret0, nh_r, dh_r, log_gamma, jnp.cos(ang), jnp.sin(ang))
        c_parts.append(c.reshape(bsz * seq_len, nh_r * dh_r))
    return _assemble(a_parts, z_parts, c_parts, p["grp_gain_a"][l], p["grp_gain_b"][l])


def _pick(n, candidates):
    for c in candidates:
        if n % c == 0:
            return c
    raise ValueError(f"no block size for {n}")


def kernel(x_prompt, x_sample, ln_in_g, ln_in_b, w_in, na_rpb, hy_conv_w, hy_conv_b, hy_f_w1, hy_f_b1,
           hy_f_freq, hy_f_w2, hy_f_b2, hy_f_w3, hy_f_b3, hy_skip, ret_decay_exp, grp_gain_a, grp_gain_b,
           w_out, ln1_g, ln1_b, w_ffn_in, w_ffn_out, ln2_g, ln2_b):
    p = dict(na_rpb=na_rpb, hy_conv_w=hy_conv_w, hy_conv_b=hy_conv_b, hy_f_w1=hy_f_w1, hy_f_b1=hy_f_b1,
             hy_f_freq=hy_f_freq, hy_f_w2=hy_f_w2, hy_f_b2=hy_f_b2, hy_f_w3=hy_f_w3, hy_f_b3=hy_f_b3,
             hy_skip=hy_skip, ret_decay_exp=ret_decay_exp, grp_gain_a=grp_gain_a, grp_gain_b=grp_gain_b)
    depth, d_model, in_cols = w_in.shape
    nh_a = na_rpb.shape[1]
    wa = grp_gain_a.shape[1]
    ch = grp_gain_b.shape[1]
    order = hy_skip.shape[1]
    nh_r = ret_decay_exp.shape[2]
    wr = (in_cols - 3 * wa - (order + 1) * ch) // 4
    dims = (nh_a, wa // nh_a, ch, order, nh_r, wr // nh_r)
    d_ff = w_ffn_out.shape[1]
    alpha = (2.0 * depth) ** 0.25

    bp, lp, _ = x_prompt.shape
    bs, ls, _ = x_sample.shape
    requests = [(bp, lp, 0), (bs, ls, bp * lp)]
    m1 = bp * lp
    m = m1 + bs * ls
    bm = _pick(m, (1024, 512, 256))
    bm_ln = 512
    assert m1 % bm_ln == 0 and m % bm_ln == 0

    xf, xb = _layernorm_join(x_prompt.reshape(m1, d_model), x_sample.reshape(bs * ls, d_model),
                             ln_in_g.astype(F32), ln_in_b.astype(F32), bm_ln // 2)
    w_in_b = w_in[0:1].astype(BF16)
    resid = (xf,)
    for l in range(depth):
        bn_out = _pick(d_model, (1024, 512, 256) if len(resid) == 1 else (512, 256))
        cast = [(w_out, l, bn_out), (w_ffn_in, l, _pick(d_ff, (512, 256, 128))),
                (w_ffn_out, l, _pick(d_model, (512, 256)))]
        if l + 1 < depth:
            cast.append((w_in, l + 1, in_cols))
        proj, cast_w = _matmul(xb, w_in_b, 0, bm, _pick(in_cols, (1024, 512, 256)), cast)
        w_out_b, w_ffn_in_b, w_ffn_out_b = cast_w[:3]
        if l + 1 < depth:
            w_in_b = cast_w[3]
        mixed = _token_mixers(proj, requests, p, l, dims)
        y = _matmul_residual(mixed, w_out_b, resid, alpha, bm)
        g1, b1 = ln1_g[l].astype(F32), ln1_b[l].astype(F32)
        xb, stats = _layernorm_stats(y, g1, b1, bm_ln)
        hmid = _matmul_swiglu(xb, w_ffn_in_b, _pick(m, (2048, 1024, 512, 256)))
        y = _matmul_residual(hmid, w_ffn_out_b, (y, stats, g1, b1), alpha, _pick(m, (512, 256)))
        if l + 1 < depth:
            g2, b2 = ln2_g[l].astype(F32), ln2_b[l].astype(F32)
            xb, stats = _layernorm_stats(y, g2, b2, bm_ln)
            resid = (y, stats, g2, b2)
    o1, o2 = _layernorm_split(y, ln2_g[depth - 1].astype(F32), ln2_b[depth - 1].astype(F32), m1, bm_ln // 2)
    return (o1.reshape(bp, lp, d_model), o2.reshape(bs, ls, d_model))
```

```python
import functools
import math

import numpy as np
import jax
import jax.numpy as jnp
from jax import lax
from jax.experimental import pallas as pl
from jax.experimental.pallas import tpu as pltpu

F32 = jnp.float32
BF16 = jnp.bfloat16

GRID_W = 64
WIN_R = 8
WIN_C = 16
ROPE_BASE = 10000.0
LN_EPS = 1e-5
RMS_EPS = 1e-6
NEG_INF = -1e30
HY_TARGET = 1e-2
HY_FAST_DECAY = 0.3
HY_SLOW_DECAY = 1.5
HY_MIN_DECAY = math.log(HY_TARGET) / HY_SLOW_DECAY
HY_MAX_DECAY = math.log(HY_TARGET) / HY_FAST_DECAY

V7X_VMEM_LIMIT = 56 * 1024 * 1024
LANES = 128
SUBLANES = 8
DFT_N2 = 128
DFT_GROUP = 32
NA_ROWS_PER_STEP = 8
RET_CHUNK = 256
RET_SUPER = 2048


def _cparams(sem, vmem=V7X_VMEM_LIMIT):
    return pltpu.CompilerParams(dimension_semantics=sem, vmem_limit_bytes=vmem)


def _pitch(rows):
    p = -(-rows // SUBLANES)
    return SUBLANES * (p + 1 - p % 2)


def _ln(x, g, b):
    mu = jnp.mean(x, -1, keepdims=True)
    xc = x - mu
    var = jnp.mean(xc * xc, -1, keepdims=True)
    return xc * lax.rsqrt(var + LN_EPS) * g + b


def _ln_stats_kernel(x_ref, g_ref, b_ref, ob_ref, st_ref):
    x = x_ref[...]
    mu = jnp.mean(x, -1, keepdims=True)
    xc = x - mu
    rstd = lax.rsqrt(jnp.mean(xc * xc, -1, keepdims=True) + LN_EPS)
    ob_ref[...] = (xc * rstd * g_ref[...] + b_ref[...]).astype(BF16)
    st_ref[:, :LANES] = jnp.broadcast_to(mu, (x.shape[0], LANES))
    st_ref[:, LANES:] = jnp.broadcast_to(rstd, (x.shape[0], LANES))


def _ln_join_kernel(x1_ref, x2_ref, g_ref, b_ref, of_ref, ob_ref, *, nb1):
    x = jnp.where(pl.program_id(0) < nb1, x1_ref[...], x2_ref[...])
    y = _ln(x, g_ref[...], b_ref[...])
    of_ref[...] = y
    ob_ref[...] = y.astype(BF16)


def _ln_split_kernel(x_ref, g_ref, b_ref, o1_ref, o2_ref, *, nb1):
    y = _ln(x_ref[...], g_ref[...], b_ref[...])

    @pl.when(pl.program_id(0) < nb1)
    def _():
        o1_ref[...] = y

    @pl.when(pl.program_id(0) >= nb1)
    def _():
        o2_ref[...] = y


def _first_rows(nb1):
    return lambda i: (jnp.minimum(i, nb1 - 1), 0)


def _second_rows(nb1):
    return lambda i: (jnp.maximum(i - nb1, 0), 0)


def _layernorm_join(x1, x2, g, b, bm=256):
    m1, d = x1.shape
    m = m1 + x2.shape[0]
    nb1 = m1 // bm
    return pl.pallas_call(
        functools.partial(_ln_join_kernel, nb1=nb1),
        grid=(m // bm,),
        in_specs=[pl.BlockSpec((bm, d), _first_rows(nb1)),
                  pl.BlockSpec((bm, d), _second_rows(nb1)),
                  pl.BlockSpec((1, d), lambda i: (0, 0)),
                  pl.BlockSpec((1, d), lambda i: (0, 0))],
        out_specs=[pl.BlockSpec((bm, d), lambda i: (i, 0)),
                   pl.BlockSpec((bm, d), lambda i: (i, 0))],
        out_shape=[jax.ShapeDtypeStruct((m, d), F32), jax.ShapeDtypeStruct((m, d), BF16)],
        compiler_params=_cparams(("arbitrary",)),
        name="layernorm_join",
    )(x1, x2, g.reshape(1, d), b.reshape(1, d))


def _layernorm_split(x, g, b, m1, bm=256):
    m, d = x.shape
    nb1 = m1 // bm
    return pl.pallas_call(
        functools.partial(_ln_split_kernel, nb1=nb1),
        grid=(m // bm,),
        in_specs=[pl.BlockSpec((bm, d), lambda i: (i, 0)),
                  pl.BlockSpec((1, d), lambda i: (0, 0)),
                  pl.BlockSpec((1, d), lambda i: (0, 0))],
        out_specs=[pl.BlockSpec((bm, d), _first_rows(nb1)),
                   pl.BlockSpec((bm, d), _second_rows(nb1))],
        out_shape=[jax.ShapeDtypeStruct((m1, d), F32), jax.ShapeDtypeStruct((m - m1, d), F32)],
        compiler_params=_cparams(("arbitrary",)),
        name="layernorm_split",
    )(x, g.reshape(1, d), b.reshape(1, d))


def _layernorm_stats(x, g, b, bm=256):
    m, d = x.shape
    return pl.pallas_call(
        _ln_stats_kernel,
        grid=(m // bm,),
        in_specs=[pl.BlockSpec((bm, d), lambda i: (i, 0)),
                  pl.BlockSpec((1, d), lambda i: (0, 0)),
                  pl.BlockSpec((1, d), lambda i: (0, 0))],
        out_specs=[pl.BlockSpec((bm, d), lambda i: (i, 0)),
                   pl.BlockSpec((bm, 2 * LANES), lambda i: (i, 0))],
        out_shape=[jax.ShapeDtypeStruct((m, d), BF16), jax.ShapeDtypeStruct((m, 2 * LANES), F32)],
        compiler_params=_cparams(("parallel",)),
        name="layernorm_stats",
    )(x, g.reshape(1, d), b.reshape(1, d))


def _mm_res_kernel(x_ref, w_ref, r_ref, o_ref, *, alpha):
    o_ref[...] = alpha * r_ref[...] + jnp.dot(x_ref[...], w_ref[...], preferred_element_type=F32)


def _mm_res_ln_kernel(x_ref, w_ref, y_ref, st_ref, g_ref, b_ref, o_ref, *, alpha):
    acc = jnp.dot(x_ref[...], w_ref[...], preferred_element_type=F32)
    mu = st_ref[:, :LANES]
    rstd = st_ref[:, LANES:]
    for c in range(y_ref.shape[1] // LANES):
        cols = slice(c * LANES, (c + 1) * LANES)
        r = (y_ref[:, cols] - mu) * rstd * g_ref[:, cols] + b_ref[:, cols]
        o_ref[:, cols] = alpha * r + acc[:, cols]


def _mm_swiglu_kernel(x_ref, wg_ref, wu_ref, o_ref):
    x = x_ref[...]
    g = jnp.dot(x, wg_ref[...], preferred_element_type=F32)
    u = jnp.dot(x, wu_ref[...], preferred_element_type=F32)
    o_ref[...] = (g / (1.0 + jnp.exp(-g)) * u).astype(o_ref.dtype)


def _mm_cast_kernel(x_ref, w_ref, *refs):
    n_cast = (len(refs) - 1) // 2
    o_ref = refs[n_cast]
    o_ref[...] = jnp.dot(x_ref[...], w_ref[...], preferred_element_type=F32).astype(o_ref.dtype)
    for src, dst in zip(refs[:n_cast], refs[n_cast + 1:]):
        tile = dst.shape[-1]
        for t in range(dst.shape[0]):
            dst[t] = src[:, t * tile:(t + 1) * tile].astype(dst.dtype)


def _cast_rows(rows, n_steps):
    rb = 16
    while rows % rb or rows // rb > n_steps:
        rb += 16
        assert rb <= rows
    return rb


def _matmul(x, w, layer, bm, bn, cast=()):
    m, k = x.shape
    n = w.shape[2]
    nj = n // bn
    n_steps = (m // bm) * nj
    cast_in, cast_out, cast_shapes, cast_args = [], [], [], []
    for stack, lyr, tile in cast:
        _, rows, cols = stack.shape
        rb = _cast_rows(rows, n_steps)
        last = rows // rb - 1
        cast_in.append(pl.BlockSpec((None, rb, cols),
                                    lambda i, j, lyr=lyr, last=last: (lyr, jnp.minimum(i * nj + j, last), 0)))
        cast_out.append(pl.BlockSpec((cols // tile, rb, tile),
                                     lambda i, j, last=last: (0, jnp.minimum(i * nj + j, last), 0)))
        cast_shapes.append(jax.ShapeDtypeStruct((cols // tile, rows, tile), BF16))
        cast_args.append(stack)
    outs = pl.pallas_call(
        _mm_cast_kernel,
        grid=(m // bm, nj),
        in_specs=[pl.BlockSpec((bm, k), lambda i, j: (i, 0)),
                  pl.BlockSpec((None, k, bn), lambda i, j: (layer, 0, j))] + cast_in,
        out_specs=[pl.BlockSpec((bm, bn), lambda i, j: (i, j))] + cast_out,
        out_shape=[jax.ShapeDtypeStruct((m, n), BF16)] + cast_shapes,
        compiler_params=_cparams(("arbitrary", "arbitrary")),
        name="matmul",
    )(x, w, *cast_args)
    return outs[0], outs[1:]


def _matmul_residual(x, w, resid, alpha, bm):
    m, k = x.shape
    bn = w.shape[2]
    n = w.shape[0] * bn
    tile = pl.BlockSpec((bm, bn), lambda i, j: (i, j))
    if len(resid) == 1:
        body, resid_specs = _mm_res_kernel, [tile]
    else:
        body = _mm_res_ln_kernel
        resid = (resid[0], resid[1], resid[2].reshape(1, n), resid[3].reshape(1, n))
        resid_specs = [tile, pl.BlockSpec((bm, 2 * LANES), lambda i, j: (i, 0)),
                       pl.BlockSpec((1, bn), lambda i, j: (0, j)), pl.BlockSpec((1, bn), lambda i, j: (0, j))]
    return pl.pallas_call(
        functools.partial(body, alpha=alpha),
        grid=(m // bm, n // bn),
        in_specs=[pl.BlockSpec((bm, k), lambda i, j: (i, 0)),
                  pl.BlockSpec((None, k, bn), lambda i, j: (j, 0, 0))] + resid_specs,
        out_specs=tile,
        out_shape=jax.ShapeDtypeStruct((m, n), F32),
        compiler_params=_cparams(("parallel", "arbitrary")),
        name="matmul_residual",
    )(x, w, *resid)


def _matmul_swiglu(x, w, bm):
    m, k = x.shape
    bn = w.shape[2]
    nb = w.shape[0] // 2
    f = nb * bn
    return pl.pallas_call(
        _mm_swiglu_kernel,
        grid=(m // bm, nb),
        in_specs=[pl.BlockSpec((bm, k), lambda i, j: (i, 0)),
                  pl.BlockSpec((None, k, bn), lambda i, j: (j, 0, 0)),
                  pl.BlockSpec((None, k, bn), lambda i, j: (j + nb, 0, 0))],
        out_specs=pl.BlockSpec((bm, bn), lambda i, j: (i, j)),
        out_shape=jax.ShapeDtypeStruct((m, f), BF16),
        compiler_params=_cparams(("parallel", "arbitrary")),
        name="matmul_swiglu",
    )(x, w, w)


def _na_bias_tables(rpb):
    nh = rpb.shape[0]
    qc = np.arange(GRID_W)[:, None]
    kc = np.arange(GRID_W)[None, :]
    win_start = np.clip(qc - WIN_C // 2, 0, GRID_W - WIN_C)
    col_ok = (kc >= win_start) & (kc < win_start + WIN_C)
    dc_idx = np.clip(kc - qc + (WIN_C - 1), 0, 2 * WIN_C - 2)
    onehot = (dc_idx.reshape(-1)[None, :] == np.arange(2 * WIN_C - 1)[:, None]).astype(np.float32)
    tiles = jnp.einsum("hdc,cq->hdq", rpb.astype(F32), jnp.asarray(onehot),
                       precision=lax.Precision.HIGHEST).reshape(nh, 2 * WIN_R - 1, GRID_W, GRID_W)
    tiles = jnp.where(jnp.asarray(col_ok)[None, None], tiles, NEG_INF)
    dr = np.arange(WIN_R)[:, None] + np.arange(WIN_R)[None, :]
    t = tiles[:, dr]
    return t.transpose(0, 1, 3, 2, 4).reshape(nh, WIN_R, GRID_W, WIN_R * GRID_W)


def _na_kernel(q_ref, k_ref, v_ref, bias_ref, o_ref, *, rows, scale):
    nk = WIN_R * GRID_W
    nq = NA_ROWS_PER_STEP * GRID_W
    hd = q_ref.shape[-1]

    def group(g, carry):
        ks, vs, bias = [], [], []
        for i in range(NA_ROWS_PER_STEP):
            r = g * NA_ROWS_PER_STEP + i
            row_start = jnp.clip(r - WIN_R // 2, 0, rows - WIN_R)
            k0 = pl.multiple_of(row_start * GRID_W, GRID_W)
            ks.append(k_ref[0, pl.ds(k0, nk), :])
            vs.append(v_ref[0, pl.ds(k0, nk), :])
            bias.append(bias_ref[row_start - r + (WIN_R - 1)])
        q_rows = pl.ds(pl.multiple_of(g * nq, nq), nq)
        q = q_ref[0, q_rows, :].reshape(NA_ROWS_PER_STEP, GRID_W, hd)
        s = jnp.einsum("rqd,rkd->rqk", q, jnp.stack(ks), preferred_element_type=F32)
        s = s * scale + jnp.stack(bias)
        m = jnp.max(s, -1, keepdims=True)
        p = jnp.exp(s - m)
        l = jnp.sum(p, -1, keepdims=True)
        o = jnp.einsum("rqk,rkd->rqd", p.astype(BF16), jnp.stack(vs), preferred_element_type=F32) / l
        o_ref[0, q_rows, :] = o.reshape(nq, hd).astype(o_ref.dtype)
        return carry

    lax.fori_loop(0, rows // NA_ROWS_PER_STEP, group, 0, unroll=2)


def _neighbourhood_attention(proj3, b0, bsz, seq_len, bias, nh, hd):
    rows = seq_len // GRID_W
    assert rows % NA_ROWS_PER_STEP == 0 and rows >= WIN_R
    return pl.pallas_call(
        functools.partial(_na_kernel, rows=rows, scale=hd ** -0.5),
        grid=(nh, bsz),
        in_specs=[pl.BlockSpec((1, seq_len, hd), lambda h, b: (b0 + b, 0, h)),
                  pl.BlockSpec((1, seq_len, hd), lambda h, b: (b0 + b, 0, nh + h)),
                  pl.BlockSpec((1, seq_len, hd), lambda h, b: (b0 + b, 0, 2 * nh + h)),
                  pl.BlockSpec((None, WIN_R, GRID_W, WIN_R * GRID_W), lambda h, b: (h, 0, 0, 0))],
        out_specs=pl.BlockSpec((1, seq_len, hd), lambda h, b: (b, 0, h)),
        out_shape=jax.ShapeDtypeStruct((bsz, seq_len, nh * hd), BF16),
        compiler_params=_cparams(("parallel", "parallel")),
        name="neighbourhood_attention",
    )(proj3, proj3, proj3, bias)


def _conv3_kernel(x_ref, w_ref, b_ref, o_ref, *, seq_len, rb):
    w = w_ref[...]
    bias = b_ref[...]
    n_chunks = seq_len // rb

    def body(c, carry):
        r0 = pl.multiple_of(c * rb, rb)
        x = x_ref[0, pl.ds(r0, rb), :].astype(F32)
        row = lax.broadcasted_iota(jnp.int32, x.shape, 0)
        p0 = pl.multiple_of(jnp.maximum(r0 - 16, 0), 16)
        n0 = pl.multiple_of(jnp.minimum(r0 + rb, seq_len - 16), 16)
        prev_row = x_ref[0, pl.ds(p0, 16), :].astype(F32)[15:16]
        next_row = x_ref[0, pl.ds(n0, 16), :].astype(F32)[0:1]
        prev_row = jnp.where(c == 0, 0.0, prev_row)
        next_row = jnp.where(c == n_chunks - 1, 0.0, next_row)
        up = jnp.where(row == 0, prev_row, pltpu.roll(x, 1, 0))
        down = jnp.where(row == rb - 1, next_row, pltpu.roll(x, rb - 1, 0))
        y = up * w[0:1] + x * w[1:2] + down * w[2:3] + bias
        o_ref[0, pl.ds(r0, rb), :] = y.astype(o_ref.dtype)
        return carry

    lax.fori_loop(0, n_chunks, body, 0)


def _short_conv3(proj3, b0, bsz, seq_len, col0, w, b, cb=256, rb=512):
    width = w.shape[1]
    rb = min(rb, seq_len)
    c0 = col0 // cb
    return pl.pallas_call(
        functools.partial(_conv3_kernel, seq_len=seq_len, rb=rb),
        grid=(bsz, width // cb),
        in_specs=[pl.BlockSpec((1, seq_len, cb), lambda bi, c: (b0 + bi, 0, c0 + c)),
                  pl.BlockSpec((3, cb), lambda bi, c: (0, c)),
                  pl.BlockSpec((1, cb), lambda bi, c: (0, c))],
        out_specs=pl.BlockSpec((1, seq_len, cb), lambda bi, c: (bi, 0, c)),
        out_shape=jax.ShapeDtypeStruct((bsz, seq_len, width), BF16),
        compiler_params=_cparams(("parallel", "parallel")),
        name="hyena_short_conv",
    )(proj3, w, b.reshape(1, width))


def _filter_kernel(z_ref, w1_ref, b1_ref, fr_ref, w2_ref, b2_ref, w3_ref, b3_ref, dl_ref, o_ref, *,
                   seq_len, rb):
    i = pl.program_id(0)
    hi = lax.Precision.HIGHEST
    z = z_ref[...]
    h = jnp.sin(fr_ref[0:1] * (jnp.dot(z, w1_ref[...], precision=hi, preferred_element_type=F32)
                               + b1_ref[...]))
    h = jnp.sin(fr_ref[1:2] * (jnp.dot(h, w2_ref[...], precision=hi, preferred_element_type=F32)
                               + b2_ref[...]))
    ch = dl_ref.shape[-1]
    n = i * rb + lax.broadcasted_iota(jnp.int32, (rb, ch), 0)
    sign = jnp.where(n < seq_len, 1.0, jnp.where(n == seq_len, 0.0, -1.0))
    window = jnp.exp(-z[:, 0:1] * dl_ref[...]) * sign
    hb = h.astype(BF16)
    for o in range(w3_ref.shape[0]):
        f = jnp.dot(hb, w3_ref[o].astype(BF16), preferred_element_type=F32) + b3_ref[o]
        o_ref[:, o * ch:(o + 1) * ch] = (f * window).astype(o_ref.dtype)


def _hyena_filter_signal(seq_len, w1, b1, freq, w2, b2, w3, b3, order, ch, rb=512):
    n = 2 * seq_len
    emb, hid = w1.shape
    pad = LANES
    t = jnp.linspace(0.0, 1.0, seq_len, dtype=F32)[:, None]
    bands = (emb - 1) // 2
    fr = jnp.linspace(1e-4, bands - 1, bands, dtype=F32)[None, :]
    wpos = 2.0 * math.pi * jnp.arange(seq_len, dtype=F32)[:, None] / seq_len
    z = jnp.concatenate([t, jnp.cos(fr * wpos), -jnp.sin(fr * wpos)], axis=-1)
    pos = np.arange(n)
    src = np.clip(np.where(pos < seq_len, pos, n - pos), 0, seq_len - 1)
    z2 = jnp.pad(z[src], ((0, 0), (0, pad - emb)))
    w1p = jnp.pad(w1.astype(F32), ((0, pad - emb), (0, pad - hid)))
    b1p = jnp.pad(b1.astype(F32), (0, pad - hid)).reshape(1, pad)
    frp = jnp.pad(freq.astype(F32), ((0, 0), (0, pad - hid)), constant_values=1.0)
    w2p = jnp.pad(w2.astype(F32), ((0, pad - hid), (0, pad - hid)))
    b2p = jnp.pad(b2.astype(F32), (0, pad - hid)).reshape(1, pad)
    w3p = jnp.pad(w3.astype(F32), ((0, pad - hid), (0, 0))).reshape(pad, order, 2, ch).transpose(2, 1, 0, 3)
    b3p = b3.astype(F32).reshape(order, 2, 1, ch).transpose(1, 0, 2, 3)
    deltas = jnp.abs(jnp.linspace(HY_MIN_DECAY, HY_MAX_DECAY, ch, dtype=F32)).reshape(1, ch)
    nb = n // rb
    half = nb // 2
    full = lambda i: (0, 0)
    return pl.pallas_call(
        functools.partial(_filter_kernel, seq_len=seq_len, rb=rb),
        grid=(nb,),
        in_specs=[pl.BlockSpec((rb, pad), lambda i: (i, 0)),
                  pl.BlockSpec((pad, pad), full), pl.BlockSpec((1, pad), full),
                  pl.BlockSpec((2, pad), full),
                  pl.BlockSpec((pad, pad), full), pl.BlockSpec((1, pad), full),
                  pl.BlockSpec((None, order, pad, ch), lambda i: (i // half, 0, 0, 0)),
                  pl.BlockSpec((None, order, 1, ch), lambda i: (i // half, 0, 0, 0)),
                  pl.BlockSpec((1, ch), full)],
        out_specs=pl.BlockSpec((rb, order * ch), lambda i: (i, 0)),
        out_shape=jax.ShapeDtypeStruct((n, order * ch), BF16),
        compiler_params=_cparams(("parallel",)),
        name="hyena_filter_mlp",
    )(z2, w1p, b1p, frp, w2p, b2p, w3p, b3p, deltas)


def _dft_tables(seq_len):
    n = 2 * seq_len
    n2 = DFT_N2
    n1 = n // n2
    hk = n2 // 2
    kg = min(n1, DFT_GROUP)
    ng = n1 // kg
    pi = math.pi
    lows = np.arange(n1 // 2).reshape(ng, kg // 2)
    k1 = jnp.asarray(np.concatenate([lows, n1 - 1 - lows], axis=1).reshape(-1), jnp.int32)
    mirror = jnp.asarray(np.tile(np.arange(kg) >= kg // 2, ng))
    slab = jnp.arange(n1, dtype=jnp.int32)
    ph1 = ((2 * k1[:, None] + 1) * slab[None, :]) % (2 * n1)
    th1 = ph1.astype(F32) * (pi / n1)
    c1 = jnp.cos(th1).reshape(ng, kg, n1)
    s1 = jnp.sin(th1).reshape(ng, kg, n1)
    f1 = jnp.concatenate([c1[:, : kg // 2], -s1[:, : kg // 2]], axis=1)
    fb = jnp.concatenate([c1, -s1], axis=1).transpose(0, 2, 1)[:, : n1 // 2] * (2.0 / n)
    k = k1[:, None, None] + n1 * jnp.arange(hk, dtype=jnp.int32)[None, :, None]
    ph2 = ((2 * k + 1) * jnp.arange(n2, dtype=jnp.int32)[None, None, :]) % (2 * n)
    th2 = ph2.astype(F32) * (pi / n)
    c2, s2 = jnp.cos(th2), jnp.sin(th2)
    sgn = jnp.where(mirror, -1.0, 1.0)[:, None, None]
    g = jnp.concatenate([jnp.concatenate([c2, sgn * s2], axis=2),
                         jnp.concatenate([-s2, sgn * c2], axis=2)], axis=1)
    c2t, s2t = c2.transpose(0, 2, 1), s2.transpose(0, 2, 1)
    h = jnp.concatenate([jnp.concatenate([c2t, -s2t], axis=2),
                         jnp.concatenate([s2t, c2t], axis=2)], axis=1)
    nl = 2 if n1 <= DFT_GROUP else 1
    return dict(n1=n1, kg=kg, ng=ng, nl=nl, f1_full=f1.astype(BF16), f1_half=f1[:, :, : n1 // 2].astype(BF16),
                fb=fb.astype(BF16), g=g.astype(BF16), h=h.astype(BF16))


def _load_lanes(ref, rows):
    parts = [ref[s, rows, :] for s in range(ref.shape[0])]
    return parts[0] if len(parts) == 1 else jnp.concatenate(parts, axis=1)


def _store_lanes(ref, rows, val):
    for s in range(ref.shape[0]):
        ref[s, rows, :] = val[:, s * LANES:(s + 1) * LANES]


def _fill_slabs(x_ref, xs, slabs, pitch):
    def body(n1, carry):
        src = pl.ds(pl.multiple_of(n1 * DFT_N2, DFT_N2), DFT_N2)
        _store_lanes(xs, pl.ds(pl.multiple_of(n1 * pitch, SUBLANES), DFT_N2), x_ref[src, :].astype(F32))
        return carry
    lax.fori_loop(0, slabs, body, 0, unroll=4)


def _slab_stage(xs, a_s, f1, slabs, p_x, p_a):
    rows = f1.shape[0]

    def body(i, carry):
        xn = _load_lanes(xs, pl.ds(i, slabs, stride=p_x)).astype(BF16)
        _store_lanes(a_s, pl.ds(pl.multiple_of(i * p_a, SUBLANES), rows),
                     jnp.dot(f1, xn, preferred_element_type=F32))
        return carry
    lax.fori_loop(0, DFT_N2, body, 0, unroll=64)


def _load_slab_freq(a_s, low, kg, p_a):
    are = _load_lanes(a_s, pl.ds(low, DFT_N2, stride=p_a))
    aim = _load_lanes(a_s, pl.ds(kg // 2 + low, DFT_N2, stride=p_a))
    return jnp.concatenate([are, aim], axis=0).astype(BF16)


def _for_each_slot(kg, body):
    for first in (0, kg // 2):
        def step(low, carry, first=first):
            body(first + low, low)
            return carry
        lax.fori_loop(0, kg // 2, step, 0, unroll=8)


def _filter_spec_kernel(x_ref, f1_ref, g_ref, o_ref, xs, a_s, *, slabs, kg):
    grp = pl.program_id(1)
    p_x, p_a = _pitch(DFT_N2), _pitch(kg)

    @pl.when(grp == 0)
    def _():
        _fill_slabs(x_ref, xs, slabs, p_x)

    _slab_stage(xs, a_s, f1_ref[0], slabs, p_x, p_a)

    def body(j, low):
        o_ref[j] = jnp.dot(g_ref[j], _load_slab_freq(a_s, low, kg, p_a), preferred_element_type=F32)
    _for_each_slot(kg, body)


def _filter_spectrum(f, tables):
    n, cols = f.shape
    n1, kg, ng, nl = tables["n1"], tables["kg"], tables["ng"], tables["nl"]
    p_x, p_a = _pitch(DFT_N2), _pitch(kg)
    wl = nl * LANES
    return pl.pallas_call(
        functools.partial(_filter_spec_kernel, slabs=n1, kg=kg),
        grid=(cols // wl, ng),
        in_specs=[pl.BlockSpec((n, wl), lambda c, g: (0, c)),
                  pl.BlockSpec((1, kg, n1), lambda c, g: (g, 0, 0)),
                  pl.BlockSpec((kg, DFT_N2, 2 * DFT_N2), lambda c, g: (g, 0, 0))],
        out_specs=pl.BlockSpec((kg, DFT_N2, wl), lambda c, g: (g, 0, c)),
        out_shape=jax.ShapeDtypeStruct((n1, DFT_N2, cols), F32),
        scratch_shapes=[pltpu.VMEM((nl, n1 * p_x, LANES), F32),
                        pltpu.VMEM((nl, DFT_N2 * p_a, LANES), F32)],
        compiler_params=_cparams(("parallel", "arbitrary")),
        name="hyena_filter_spectrum",
    )(f, tables["f1_full"], tables["g"])


def _long_conv_kernel(x_ref, gate_ref, f1_ref, g_ref, h_ref, kf_ref, fb_ref, skip_ref, o_ref,
                      xs, a_s, z_s, y_s, *, slabs, kg, ng):
    grp = pl.program_id(2)
    n2 = DFT_N2
    hk = n2 // 2
    p_x, p_a, p_z, p_y = _pitch(n2), _pitch(kg), _pitch(2 * n2), _pitch(slabs)

    @pl.when(grp == 0)
    def _():
        _fill_slabs(x_ref.at[0], xs, slabs, p_x)
        y_s[...] = jnp.zeros_like(y_s)

    _slab_stage(xs, a_s, f1_ref[0], slabs, p_x, p_a)

    def freq_body(j, low):
        x = jnp.dot(g_ref[j], _load_slab_freq(a_s, low, kg, p_a), preferred_element_type=F32)
        kf = kf_ref[j]
        xr, xi = x[:hk], x[hk:]
        kr, ki = kf[:hk], kf[hk:]
        y = jnp.concatenate([xr * kr - xi * ki, xr * ki + xi * kr], axis=0)
        _store_lanes(z_s, pl.ds(pl.multiple_of(j * p_z, SUBLANES), n2), y)
    _for_each_slot(kg, freq_body)

    def inv_freq_body(j, carry):
        y = _load_lanes(z_s, pl.ds(pl.multiple_of(j * p_z, SUBLANES), n2)).astype(BF16)
        _store_lanes(z_s, pl.ds(pl.multiple_of(j * p_z, SUBLANES), 2 * n2),
                     jnp.dot(h_ref[j], y, preferred_element_type=F32))
        return carry
    lax.fori_loop(0, kg, inv_freq_body, 0, unroll=8)

    fb = fb_ref[0]

    def inv_body(t, carry):
        zre = _load_lanes(z_s, pl.ds(t, kg, stride=p_z))
        zim = _load_lanes(z_s, pl.ds(n2 + t, kg, stride=p_z))
        zz = jnp.concatenate([zre, zim], axis=0).astype(BF16)
        dst = pl.ds(pl.multiple_of(t * p_y, SUBLANES), slabs)
        _store_lanes(y_s, dst, _load_lanes(y_s, dst) + jnp.dot(fb, zz, preferred_element_type=F32))
        return carry
    lax.fori_loop(0, n2, inv_body, 0, unroll=64)

    @pl.when(grp == ng - 1)
    def _():
        skip = skip_ref[...]

        def out_body(t1, carry):
            y = _load_lanes(y_s, pl.ds(t1, n2, stride=p_y))
            u = _load_lanes(xs, pl.ds(pl.multiple_of(t1 * p_x, SUBLANES), n2))
            rows = pl.ds(pl.multiple_of(t1 * n2, n2), n2)
            gate = gate_ref[0, rows, :].astype(F32)
            o_ref[0, rows, :] = ((y + skip * u) * gate).astype(o_ref.dtype)
            return carry
        lax.fori_loop(0, slabs, out_body, 0, unroll=4)


def _long_conv_gate(u, u_blk0, gate, gate_blk0, tables, kf, kf_blk0, skip, ch):
    bsz, seq_len, _ = u.shape
    n1, kg, ng, nl = tables["n1"], tables["kg"], tables["ng"], tables["nl"]
    slabs = n1 // 2
    p_x, p_a, p_z, p_y = _pitch(DFT_N2), _pitch(kg), _pitch(2 * DFT_N2), _pitch(slabs)
    wl = nl * LANES
    return pl.pallas_call(
        functools.partial(_long_conv_kernel, slabs=slabs, kg=kg, ng=ng),
        grid=(bsz, ch // wl, ng),
        in_specs=[pl.BlockSpec((1, seq_len, wl), lambda b, c, g: (b, 0, u_blk0 + c)),
                  pl.BlockSpec((1, seq_len, wl), lambda b, c, g: (b, 0, gate_blk0 + c)),
                  pl.BlockSpec((1, kg, slabs), lambda b, c, g: (g, 0, 0)),
                  pl.BlockSpec((kg, DFT_N2, 2 * DFT_N2), lambda b, c, g: (g, 0, 0)),
                  pl.BlockSpec((kg, 2 * DFT_N2, DFT_N2), lambda b, c, g: (g, 0, 0)),
                  pl.BlockSpec((kg, DFT_N2, wl), lambda b, c, g: (g, 0, kf_blk0 + c)),
                  pl.BlockSpec((1, slabs, 2 * kg), lambda b, c, g: (g, 0, 0)),
                  pl.BlockSpec((1, wl), lambda b, c, g: (0, c))],
        out_specs=pl.BlockSpec((1, seq_len, wl), lambda b, c, g: (b, 0, c)),
        out_shape=jax.ShapeDtypeStruct((bsz, seq_len, ch), BF16),
        scratch_shapes=[pltpu.VMEM((nl, slabs * p_x, LANES), F32),
                        pltpu.VMEM((nl, DFT_N2 * p_a, LANES), F32),
                        pltpu.VMEM((nl, kg * p_z, LANES), F32),
                        pltpu.VMEM((nl, DFT_N2 * p_y, LANES), F32)],
        compiler_params=_cparams(("parallel", "parallel", "arbitrary")),
        name="hyena_long_conv",
    )(u, gate, tables["f1_half"], tables["g"], tables["h"], kf, tables["fb"],
      skip.astype(F32).reshape(1, ch))


def _hyena(hy, tables, kf, skip, order, ch):
    nblk = ch // (tables["nl"] * LANES)
    z, z_blk0 = hy, 0
    for o in range(order):
        z = _long_conv_gate(z, z_blk0, hy, (o + 1) * nblk, tables, kf, o * nblk, skip[o], ch)
    return z


def _rope(x, cos, sin):
    half = x.shape[-1] // 2
    x1, x2 = x[:, :half], x[:, half:]
    return jnp.concatenate([x1 * cos - x2 * sin, x1 * sin + x2 * cos], axis=-1)


def _ret_kernel(lg_ref, q_ref, k_ref, v_ref, g_ref, cos_ref, sin_ref, o_ref,
                sb_all, kr_all, kz_all, sf_ref, sb_ref, *, n_super, chunks_per_super, k_scale):
    h = pl.program_id(1)
    s = pl.program_id(2)
    cc = RET_CHUNK
    dh = q_ref.shape[-1]
    lgf = lg_ref[0, h]
    lgb = lg_ref[1, h]
    row = lax.broadcasted_iota(jnp.int32, (cc, dh), 0).astype(F32)
    zeta_f = jnp.exp(lgf * (cc - 1.0 - row))

    @pl.when(s < n_super)
    def _backward_sweep():
        @pl.when(s == 0)
        def _():
            sb_ref[...] = jnp.zeros_like(sb_ref)

        sup = n_super - 1 - s
        zeta_b = jnp.exp(lgb * row)
        chunk_decay = jnp.exp(lgb * jnp.full((1, dh), float(cc), F32))

        def body(t, carry):
            c = chunks_per_super - 1 - t
            c0 = pl.multiple_of(c * cc, cc)
            n = sup * chunks_per_super + c
            sb_all[n] = sb_ref[...].astype(BF16)
            k = _rope(k_ref[0, pl.ds(c0, cc), :].astype(F32), cos_ref[pl.ds(c0, cc), :],
                      sin_ref[pl.ds(c0, cc), :]) * k_scale
            kr_all[n] = k.astype(BF16)
            kz_all[n] = (k * zeta_f).astype(BF16)
            kz = (k * zeta_b).astype(BF16)
            v = v_ref[0, pl.ds(c0, cc), :]
            upd = lax.dot_general(kz, v, (((0,), (0,)), ((), ())), preferred_element_type=F32)
            sb_ref[...] = sb_ref[...] * chunk_decay + upd
            return carry

        lax.fori_loop(0, chunks_per_super, body, 0, unroll=4)

    @pl.when(s >= n_super)
    def _forward_sweep():
        @pl.when(s == n_super)
        def _():
            sf_ref[...] = jnp.zeros_like(sf_ref)

        sup = s - n_super
        col = lax.broadcasted_iota(jnp.int32, (cc, cc), 1).astype(F32)
        rowc = lax.broadcasted_iota(jnp.int32, (cc, cc), 0).astype(F32)
        diff = rowc - col
        inner_decay = jnp.where(diff >= 0, jnp.exp(lgf * jnp.maximum(diff, 0.0)),
                                jnp.exp(lgb * jnp.maximum(-diff, 0.0)))
        xi_f = jnp.exp(lgf * (row + 1.0))
        xi_b = jnp.exp(lgb * (cc - row))
        chunk_decay = jnp.exp(lgf * jnp.full((1, dh), float(cc), F32))

        def body(c, carry):
            c0 = pl.multiple_of(c * cc, cc)
            n = sup * chunks_per_super + c
            q = _rope(q_ref[0, pl.ds(c0, cc), :].astype(F32), cos_ref[pl.ds(c0, cc), :],
                      sin_ref[pl.ds(c0, cc), :]).astype(BF16)
            v = v_ref[0, pl.ds(c0, cc), :]
            sc = lax.dot_general(q, kr_all[n], (((1,), (1,)), ((), ())),
                                 preferred_element_type=F32) * inner_decay
            ret = jnp.dot(sc.astype(BF16), v, preferred_element_type=F32)
            ret = ret + xi_f * jnp.dot(q, sf_ref[...].astype(BF16), preferred_element_type=F32)
            ret = ret + xi_b * jnp.dot(q, sb_all[n], preferred_element_type=F32)
            upd = lax.dot_general(kz_all[n], v, (((0,), (0,)), ((), ())), preferred_element_type=F32)
            sf_ref[...] = sf_ref[...] * chunk_decay + upd
            ret = ret * lax.rsqrt(jnp.mean(ret * ret, -1, keepdims=True) + RMS_EPS)
            gate = g_ref[0, pl.ds(c0, cc), :].astype(F32)
            o_ref[0, pl.ds(c0, cc), :] = (ret * (gate / (1.0 + jnp.exp(-gate)))).astype(o_ref.dtype)
            return carry

        lax.fori_loop(0, chunks_per_super, body, 0, unroll=4)


def _retention(proj3, b0, bsz, seq_len, col0, nh, dh, log_gamma, cos, sin):
    sup_len = min(RET_SUPER, seq_len)
    n_super = seq_len // sup_len
    cps = sup_len // RET_CHUNK
    cb0 = col0 // dh

    def fwd_idx(s):
        return jnp.maximum(s - n_super, 0)

    def kv_idx(s):
        return jnp.where(s < n_super, n_super - 1 - s, s - n_super)

    def k_idx(s):
        return jnp.maximum(n_super - 1 - s, 0)

    n_chunks = seq_len // RET_CHUNK
    return pl.pallas_call(
        functools.partial(_ret_kernel, n_super=n_super, chunks_per_super=cps, k_scale=dh ** -0.5),
        grid=(bsz, nh, 2 * n_super),
        in_specs=[pl.BlockSpec(memory_space=pltpu.SMEM),
                  pl.BlockSpec((1, sup_len, dh), lambda b, h, s: (b0 + b, fwd_idx(s), cb0 + h)),
                  pl.BlockSpec((1, sup_len, dh), lambda b, h, s: (b0 + b, k_idx(s), cb0 + nh + h)),
                  pl.BlockSpec((1, sup_len, dh), lambda b, h, s: (b0 + b, kv_idx(s), cb0 + 2 * nh + h)),
                  pl.BlockSpec((1, sup_len, dh), lambda b, h, s: (b0 + b, fwd_idx(s), cb0 + 3 * nh + h)),
                  pl.BlockSpec((sup_len, dh // 2), lambda b, h, s: (kv_idx(s), 0)),
                  pl.BlockSpec((sup_len, dh // 2), lambda b, h, s: (kv_idx(s), 0))],
        out_specs=pl.BlockSpec((1, sup_len, dh), lambda b, h, s: (b, fwd_idx(s), h)),
        out_shape=jax.ShapeDtypeStruct((bsz, seq_len, nh * dh), BF16),
        scratch_shapes=[pltpu.VMEM((n_chunks, dh, dh), BF16),
                        pltpu.VMEM((n_chunks, RET_CHUNK, dh), BF16),
                        pltpu.VMEM((n_chunks, RET_CHUNK, dh), BF16),
                        pltpu.VMEM((dh, dh), F32),
                        pltpu.VMEM((dh, dh), F32)],
        compiler_params=_cparams(("parallel", "parallel", "arbitrary")),
        name="retention",
    )(log_gamma, proj3, proj3, proj3, proj3, cos, sin)


def _assemble_kernel(a1_ref, a2_ref, z1_ref, z2_ref, c1_ref, c2_ref, ga_ref, gb_ref, o_ref, *, nb1):
    first = pl.program_id(0) < nb1
    wa = a1_ref.shape[-1]
    wb = z1_ref.shape[-1]
    a = jnp.where(first, a1_ref[...], a2_ref[...]).astype(F32)
    a = a * lax.rsqrt(jnp.mean(a * a, -1, keepdims=True) + RMS_EPS) * ga_ref[...]
    z = jnp.where(first, z1_ref[...], z2_ref[...]).astype(F32)
    z = z * lax.rsqrt(jnp.mean(z * z, -1, keepdims=True) + RMS_EPS) * gb_ref[...]
    o_ref[:, :wa] = a.astype(o_ref.dtype)
    o_ref[:, wa:wa + wb] = z.astype(o_ref.dtype)
    o_ref[:, wa + wb:] = jnp.where(first, c1_ref[...], c2_ref[...])


def _assemble(a_parts, z_parts, c_parts, gain_a, gain_b, bm=1024):
    m1, wa = a_parts[0].shape
    m = m1 + a_parts[1].shape[0]
    assert m1 % bm == 0 and m % bm == 0
    wb, wc = z_parts[0].shape[1], c_parts[0].shape[1]
    nb1 = m1 // bm
    part_specs = []
    for w in (wa, wb, wc):
        part_specs += [pl.BlockSpec((bm, w), _first_rows(nb1)), pl.BlockSpec((bm, w), _second_rows(nb1))]
    return pl.pallas_call(
        functools.partial(_assemble_kernel, nb1=nb1),
        grid=(m // bm,),
        in_specs=part_specs + [pl.BlockSpec((1, wa), lambda i: (0, 0)),
                               pl.BlockSpec((1, wb), lambda i: (0, 0))],
        out_specs=pl.BlockSpec((bm, wa + wb + wc), lambda i: (i, 0)),
        out_shape=jax.ShapeDtypeStruct((m, wa + wb + wc), BF16),
        compiler_params=_cparams(("arbitrary",)),
        name="assemble_mixers",
    )(*a_parts, *z_parts, *c_parts, gain_a.astype(F32).reshape(1, wa), gain_b.astype(F32).reshape(1, wb))


def _token_mixers(proj, requests, p, l, dims):
    nh_a, hd_a, ch, order, nh_r, dh_r = dims
    in_cols = proj.shape[1]
    total = proj.shape[0]
    wa = nh_a * hd_a
    hy0 = 3 * wa
    ret0 = hy0 + (order + 1) * ch
    log_gamma = jnp.log1p(-jnp.exp2(-p["ret_decay_exp"][l].astype(F32)))
    a_parts, z_parts, c_parts = [], [], []
    bias = _na_bias_tables(p["na_rpb"][l])
    for (bsz, seq_len, tok0) in requests:
        assert tok0 % seq_len == 0 and total % seq_len == 0
        proj3 = proj.reshape(total // seq_len, seq_len, in_cols)
        b0 = tok0 // seq_len
        a = _neighbourhood_attention(proj3, b0, bsz, seq_len, bias, nh_a, hd_a)
        a_parts.append(a.reshape(bsz * seq_len, wa))
        tables = _dft_tables(seq_len)
        f = _hyena_filter_signal(seq_len, p["hy_f_w1"][l], p["hy_f_b1"][l], p["hy_f_freq"][l],
                                 p["hy_f_w2"][l], p["hy_f_b2"][l], p["hy_f_w3"][l], p["hy_f_b3"][l],
                                 order, ch)
        kf = _filter_spectrum(f, tables)
        hy = _short_conv3(proj3, b0, bsz, seq_len, hy0, p["hy_conv_w"][l].astype(F32),
                          p["hy_conv_b"][l].astype(F32))
        z = _hyena(hy, tables, kf, p["hy_skip"][l], order, ch)
        z_parts.append(z.reshape(bsz * seq_len, ch))
        half = dh_r // 2
        inv_freq = 1.0 / (ROPE_BASE ** jnp.linspace(0.0, 1.0, half, dtype=F32))
        ang = jnp.arange(seq_len, dtype=F32)[:, None] * inv_freq[None, :]
        c = _retention(proj3, b0, bsz, seq_len, ret0, nh_r, dh_r, log_gamma, jnp.cos(ang), jnp.sin(ang))
        c_parts.append(c.reshape(bsz * seq_len, nh_r * dh_r))
    return _assemble(a_parts, z_parts, c_parts, p["grp_gain_a"][l], p["grp_gain_b"][l])


def _pick(n, candidates):
    for c in candidates:
        if n % c == 0:
            return c
    raise ValueError(f"no block size for {n}")


def kernel(x_prompt, x_sample, ln_in_g, ln_in_b, w_in, na_rpb, hy_conv_w, hy_conv_b, hy_f_w1, hy_f_b1,
           hy_f_freq, hy_f_w2, hy_f_b2, hy_f_w3, hy_f_b3, hy_skip, ret_decay_exp, grp_gain_a, grp_gain_b,
           w_out, ln1_g, ln1_b, w_ffn_in, w_ffn_out, ln2_g, ln2_b):
    p = dict(na_rpb=na_rpb, hy_conv_w=hy_conv_w, hy_conv_b=hy_conv_b, hy_f_w1=hy_f_w1, hy_f_b1=hy_f_b1,
             hy_f_freq=hy_f_freq, hy_f_w2=hy_f_w2, hy_f_b2=hy_f_b2, hy_f_w3=hy_f_w3, hy_f_b3=hy_f_b3,
             hy_skip=hy_skip, ret_decay_exp=ret_decay_exp, grp_gain_a=grp_gain_a, grp_gain_b=grp_gain_b)
    depth, d_model, in_cols = w_in.shape
    nh_a = na_rpb.shape[1]
    wa = grp_gain_a.shape[1]
    ch = grp_gain_b.shape[1]
    order = hy_skip.shape[1]
    nh_r = ret_decay_exp.shape[2]
    wr = (in_cols - 3 * wa - (order + 1) * ch) // 4
    dims = (nh_a, wa // nh_a, ch, order, nh_r, wr // nh_r)
    d_ff = w_ffn_out.shape[1]
    alpha = (2.0 * depth) ** 0.25

    bp, lp, _ = x_prompt.shape
    bs, ls, _ = x_sample.shape
    requests = [(bp, lp, 0), (bs, ls, bp * lp)]
    m1 = bp * lp
    m = m1 + bs * ls
    bm = _pick(m, (1024, 512, 256))
    bm_ln = 512
    assert m1 % bm_ln == 0 and m % bm_ln == 0

    xf, xb = _layernorm_join(x_prompt.reshape(m1, d_model), x_sample.reshape(bs * ls, d_model),
                             ln_in_g.astype(F32), ln_in_b.astype(F32), bm_ln // 2)
    w_in_b = w_in[0:1].astype(BF16)
    resid = (xf,)
    for l in range(depth):
        bn_out = _pick(d_model, (1024, 512, 256) if len(resid) == 1 else (512, 256))
        cast = [(w_out, l, bn_out), (w_ffn_in, l, _pick(d_ff, (512, 256, 128))),
                (w_ffn_out, l, _pick(d_model, (512, 256)))]
        if l + 1 < depth:
            cast.append((w_in, l + 1, in_cols))
        proj, cast_w = _matmul(xb, w_in_b, 0, bm, _pick(in_cols, (1024, 512, 256)), cast)
        w_out_b, w_ffn_in_b, w_ffn_out_b = cast_w[:3]
        if l + 1 < depth:
            w_in_b = cast_w[3]
        mixed = _token_mixers(proj, requests, p, l, dims)
        y = _matmul_residual(mixed, w_out_b, resid, alpha, bm)
        g1, b1 = ln1_g[l].astype(F32), ln1_b[l].astype(F32)
        xb, stats = _layernorm_stats(y, g1, b1, bm_ln)
        hmid = _matmul_swiglu(xb, w_ffn_in_b, _pick(m, (2048, 1024, 512, 256)))
        y = _matmul_residual(hmid, w_ffn_out_b, (y, stats, g1, b1), alpha, _pick(m, (512, 256)))
        if l + 1 < depth:
            g2, b2 = ln2_g[l].astype(F32), ln2_b[l].astype(F32)
            xb, stats = _layernorm_stats(y, g2, b2, bm_ln)
            resid = (y, stats, g2, b2)
    o1, o2 = _layernorm_split(y, ln2_g[depth - 1].astype(F32), ln2_b[depth - 1].astype(F32), m1, bm_ln // 2)
    return (o1.reshape(bp, lp, d_model), o2.reshape(bs, ls, d_model))
```

```python
import functools
import math

import numpy as np
import jax
import jax.numpy as jnp
from jax import lax
from jax.experimental import pallas as pl
from jax.experimental.pallas import tpu as pltpu

F32 = jnp.float32
BF16 = jnp.bfloat16

GRID_W = 64
WIN_R = 8
WIN_C = 16
ROPE_BASE = 10000.0
LN_EPS = 1e-5
RMS_EPS = 1e-6
NEG_INF = -1e30
HY_TARGET = 1e-2
HY_FAST_DECAY = 0.3
HY_SLOW_DECAY = 1.5
HY_MIN_DECAY = math.log(HY_TARGET) / HY_SLOW_DECAY
HY_MAX_DECAY = math.log(HY_TARGET) / HY_FAST_DECAY

V7X_VMEM_LIMIT = 56 * 1024 * 1024
LANES = 128
SUBLANES = 8
DFT_N2 = 128
DFT_GROUP = 32
NA_ROWS_PER_STEP = 8
RET_CHUNK = 256
RET_SUPER = 2048


def _cparams(sem, vmem=V7X_VMEM_LIMIT):
    return pltpu.CompilerParams(dimension_semantics=sem, vmem_limit_bytes=vmem)


def _pitch(rows):
    p = -(-rows // SUBLANES)
    return SUBLANES * (p + 1 - p % 2)


def _ln(x, g, b):
    mu = jnp.mean(x, -1, keepdims=True)
    xc = x - mu
    var = jnp.mean(xc * xc, -1, keepdims=True)
    return xc * lax.rsqrt(var + LN_EPS) * g + b


def _ln_stats_kernel(x_ref, g_ref, b_ref, ob_ref, st_ref):
    x = x_ref[...]
    mu = jnp.mean(x, -1, keepdims=True)
    xc = x - mu
    rstd = lax.rsqrt(jnp.mean(xc * xc, -1, keepdims=True) + LN_EPS)
    ob_ref[...] = (xc * rstd * g_ref[...] + b_ref[...]).astype(BF16)
    st_ref[:, :LANES] = jnp.broadcast_to(mu, (x.shape[0], LANES))
    st_ref[:, LANES:] = jnp.broadcast_to(rstd, (x.shape[0], LANES))


def _ln_join_kernel(x1_ref, x2_ref, g_ref, b_ref, of_ref, ob_ref, *, nb1):
    x = jnp.where(pl.program_id(0) < nb1, x1_ref[...], x2_ref[...])
    y = _ln(x, g_ref[...], b_ref[...])
    of_ref[...] = y
    ob_ref[...] = y.astype(BF16)


def _ln_split_kernel(x_ref, g_ref, b_ref, o1_ref, o2_ref, *, nb1):
    y = _ln(x_ref[...], g_ref[...], b_ref[...])

    @pl.when(pl.program_id(0) < nb1)
    def _():
        o1_ref[...] = y

    @pl.when(pl.program_id(0) >= nb1)
    def _():
        o2_ref[...] = y


def _first_rows(nb1):
    return lambda i: (jnp.minimum(i, nb1 - 1), 0)


def _second_rows(nb1):
    return lambda i: (jnp.maximum(i - nb1, 0), 0)


def _layernorm_join(x1, x2, g, b, bm=256):
    m1, d = x1.shape
    m = m1 + x2.shape[0]
    nb1 = m1 // bm
    return pl.pallas_call(
        functools.partial(_ln_join_kernel, nb1=nb1),
        grid=(m // bm,),
        in_specs=[pl.BlockSpec((bm, d), _first_rows(nb1)),
                  pl.BlockSpec((bm, d), _second_rows(nb1)),
                  pl.BlockSpec((1, d), lambda i: (0, 0)),
                  pl.BlockSpec((1, d), lambda i: (0, 0))],
        out_specs=[pl.BlockSpec((bm, d), lambda i: (i, 0)),
                   pl.BlockSpec((bm, d), lambda i: (i, 0))],
        out_shape=[jax.ShapeDtypeStruct((m, d), F32), jax.ShapeDtypeStruct((m, d), BF16)],
        compiler_params=_cparams(("arbitrary",)),
        name="layernorm_join",
    )(x1, x2, g.reshape(1, d), b.reshape(1, d))


def _layernorm_split(x, g, b, m1, bm=256):
    m, d = x.shape
    nb1 = m1 // bm
    return pl.pallas_call(
        functools.partial(_ln_split_kernel, nb1=nb1),
        grid=(m // bm,),
        in_specs=[pl.BlockSpec((bm, d), lambda i: (i, 0)),
                  pl.BlockSpec((1, d), lambda i: (0, 0)),
                  pl.BlockSpec((1, d), lambda i: (0, 0))],
        out_specs=[pl.BlockSpec((bm, d), _first_rows(nb1)),
                   pl.BlockSpec((bm, d), _second_rows(nb1))],
        out_shape=[jax.ShapeDtypeStruct((m1, d), F32), jax.ShapeDtypeStruct((m - m1, d), F32)],
        compiler_params=_cparams(("arbitrary",)),
        name="layernorm_split",
    )(x, g.reshape(1, d), b.reshape(1, d))


def _layernorm_stats(x, g, b, bm=256):
    m, d = x.shape
    return pl.pallas_call(
        _ln_stats_kernel,
        grid=(m // bm,),
        in_specs=[pl.BlockSpec((bm, d), lambda i: (i, 0)),
                  pl.BlockSpec((1, d), lambda i: (0, 0)),
                  pl.BlockSpec((1, d), lambda i: (0, 0))],
        out_specs=[pl.BlockSpec((bm, d), lambda i: (i, 0)),
                   pl.BlockSpec((bm, 2 * LANES), lambda i: (i, 0))],
        out_shape=[jax.ShapeDtypeStruct((m, d), BF16), jax.ShapeDtypeStruct((m, 2 * LANES), F32)],
        compiler_params=_cparams(("parallel",)),
        name="layernorm_stats",
    )(x, g.reshape(1, d), b.reshape(1, d))


def _mm_res_kernel(x_ref, w_ref, r_ref, o_ref, *, alpha):
    o_ref[...] = alpha * r_ref[...] + jnp.dot(x_ref[...], w_ref[...], preferred_element_type=F32)


def _mm_res_ln_kernel(x_ref, w_ref, y_ref, st_ref, g_ref, b_ref, o_ref, *, alpha):
    acc = jnp.dot(x_ref[...], w_ref[...], preferred_element_type=F32)
    mu = st_ref[:, :LANES]
    rstd = st_ref[:, LANES:]
    for c in range(y_ref.shape[1] // LANES):
        cols = slice(c * LANES, (c + 1) * LANES)
        r = (y_ref[:, cols] - mu) * rstd * g_ref[:, cols] + b_ref[:, cols]
        o_ref[:, cols] = alpha * r + acc[:, cols]


def _mm_swiglu_kernel(x_ref, wg_ref, wu_ref, o_ref):
    x = x_ref[...]
    g = jnp.dot(x, wg_ref[...], preferred_element_type=F32)
    u = jnp.dot(x, wu_ref[...], preferred_element_type=F32)
    o_ref[...] = (g / (1.0 + jnp.exp(-g)) * u).astype(o_ref.dtype)


def _mm_cast_kernel(x_ref, w_ref, *refs):
    n_cast = (len(refs) - 1) // 2
    o_ref = refs[n_cast]
    o_ref[...] = jnp.dot(x_ref[...], w_ref[...], preferred_element_type=F32).astype(o_ref.dtype)
    for src, dst in zip(refs[:n_cast], refs[n_cast + 1:]):
        tile = dst.shape[-1]
        for t in range(dst.shape[0]):
            dst[t] = src[:, t * tile:(t + 1) * tile].astype(dst.dtype)


def _cast_rows(rows, n_steps):
    rb = 16
    while rows % rb or rows // rb > n_steps:
        rb += 16
        assert rb <= rows
    return rb


def _matmul(x, w, layer, bm, bn, cast=()):
    m, k = x.shape
    n = w.shape[2]
    nj = n // bn
    n_steps = (m // bm) * nj
    cast_in, cast_out, cast_shapes, cast_args = [], [], [], []
    for stack, lyr, tile in cast:
        _, rows, cols = stack.shape
        rb = _cast_rows(rows, n_steps)
        last = rows // rb - 1
        cast_in.append(pl.BlockSpec((None, rb, cols),
                                    lambda i, j, lyr=lyr, last=last: (lyr, jnp.minimum(i * nj + j, last), 0)))
        cast_out.append(pl.BlockSpec((cols // tile, rb, tile),
                                     lambda i, j, last=last: (0, jnp.minimum(i * nj + j, last), 0)))
        cast_shapes.append(jax.ShapeDtypeStruct((cols // tile, rows, tile), BF16))
        cast_args.append(stack)
    outs = pl.pallas_call(
        _mm_cast_kernel,
        grid=(m // bm, nj),
        in_specs=[pl.BlockSpec((bm, k), lambda i, j: (i, 0)),
                  pl.BlockSpec((None, k, bn), lambda i, j: (layer, 0, j))] + cast_in,
        out_specs=[pl.BlockSpec((bm, bn), lambda i, j: (i, j))] + cast_out,
        out_shape=[jax.ShapeDtypeStruct((m, n), BF16)] + cast_shapes,
        compiler_params=_cparams(("arbitrary", "arbitrary")),
        name="matmul",
    )(x, w, *cast_args)
    return outs[0], outs[1:]


def _matmul_residual(x, w, resid, alpha, bm):
    m, k = x.shape
    bn = w.shape[2]
    n = w.shape[0] * bn
    tile = pl.BlockSpec((bm, bn), lambda i, j: (i, j))
    if len(resid) == 1:
        body, resid_specs = _mm_res_kernel, [tile]
    else:
        body = _mm_res_ln_kernel
        resid = (resid[0], resid[1], resid[2].reshape(1, n), resid[3].reshape(1, n))
        resid_specs = [tile, pl.BlockSpec((bm, 2 * LANES), lambda i, j: (i, 0)),
                       pl.BlockSpec((1, bn), lambda i, j: (0, j)), pl.BlockSpec((1, bn), lambda i, j: (0, j))]
    return pl.pallas_call(
        functools.partial(body, alpha=alpha),
        grid=(m // bm, n // bn),
        in_specs=[pl.BlockSpec((bm, k), lambda i, j: (i, 0)),
                  pl.BlockSpec((None, k, bn), lambda i, j: (j, 0, 0))] + resid_specs,
        out_specs=tile,
        out_shape=jax.ShapeDtypeStruct((m, n), F32),
        compiler_params=_cparams(("parallel", "arbitrary")),
        name="matmul_residual",
    )(x, w, *resid)


def _matmul_swiglu(x, w, bm):
    m, k = x.shape
    bn = w.shape[2]
    nb = w.shape[0] // 2
    f = nb * bn
    return pl.pallas_call(
        _mm_swiglu_kernel,
        grid=(m // bm, nb),
        in_specs=[pl.BlockSpec((bm, k), lambda i, j: (i, 0)),
                  pl.BlockSpec((None, k, bn), lambda i, j: (j, 0, 0)),
                  pl.BlockSpec((None, k, bn), lambda i, j: (j + nb, 0, 0))],
        out_specs=pl.BlockSpec((bm, bn), lambda i, j: (i, j)),
        out_shape=jax.ShapeDtypeStruct((m, f), BF16),
        compiler_params=_cparams(("parallel", "arbitrary")),
        name="matmul_swiglu",
    )(x, w, w)


def _na_bias_tables(rpb):
    nh = rpb.shape[0]
    qc = np.arange(GRID_W)[:, None]
    kc = np.arange(GRID_W)[None, :]
    win_start = np.clip(qc - WIN_C // 2, 0, GRID_W - WIN_C)
    col_ok = (kc >= win_start) & (kc < win_start + WIN_C)
    dc_idx = np.clip(kc - qc + (WIN_C - 1), 0, 2 * WIN_C - 2)
    onehot = (dc_idx.reshape(-1)[None, :] == np.arange(2 * WIN_C - 1)[:, None]).astype(np.float32)
    tiles = jnp.einsum("hdc,cq->hdq", rpb.astype(F32), jnp.asarray(onehot),
                       precision=lax.Precision.HIGHEST).reshape(nh, 2 * WIN_R - 1, GRID_W, GRID_W)
    tiles = jnp.where(jnp.asarray(col_ok)[None, None], tiles, NEG_INF)
    dr = np.arange(WIN_R)[:, None] + np.arange(WIN_R)[None, :]
    t = tiles[:, dr]
    return t.transpose(0, 1, 3, 2, 4).reshape(nh, WIN_R, GRID_W, WIN_R * GRID_W)


def _na_kernel(q_ref, k_ref, v_ref, bias_ref, o_ref, *, rows, scale):
    nk = WIN_R * GRID_W
    nq = NA_ROWS_PER_STEP * GRID_W
    hd = q_ref.shape[-1]

    def group(g, carry):
        ks, vs, bias = [], [], []
        for i in range(NA_ROWS_PER_STEP):
            r = g * NA_ROWS_PER_STEP + i
            row_start = jnp.clip(r - WIN_R // 2, 0, rows - WIN_R)
            k0 = pl.multiple_of(row_start * GRID_W, GRID_W)
            ks.append(k_ref[0, pl.ds(k0, nk), :])
            vs.append(v_ref[0, pl.ds(k0, nk), :])
            bias.append(bias_ref[row_start - r + (WIN_R - 1)])
        q_rows = pl.ds(pl.multiple_of(g * nq, nq), nq)
        q = q_ref[0, q_rows, :].reshape(NA_ROWS_PER_STEP, GRID_W, hd)
        s = jnp.einsum("rqd,rkd->rqk", q, jnp.stack(ks), preferred_element_type=F32)
        s = s * scale + jnp.stack(bias)
        m = jnp.max(s, -1, keepdims=True)
        p = jnp.exp(s - m)
        l = jnp.sum(p, -1, keepdims=True)
        o = jnp.einsum("rqk,rkd->rqd", p.astype(BF16), jnp.stack(vs), preferred_element_type=F32) / l
        o_ref[0, q_rows, :] = o.reshape(nq, hd).astype(o_ref.dtype)
        return carry

    lax.fori_loop(0, rows // NA_ROWS_PER_STEP, group, 0, unroll=2)


def _neighbourhood_attention(proj3, b0, bsz, seq_len, bias, nh, hd):
    rows = seq_len // GRID_W
    assert rows % NA_ROWS_PER_STEP == 0 and rows >= WIN_R
    return pl.pallas_call(
        functools.partial(_na_kernel, rows=rows, scale=hd ** -0.5),
        grid=(nh, bsz),
        in_specs=[pl.BlockSpec((1, seq_len, hd), lambda h, b: (b0 + b, 0, h)),
                  pl.BlockSpec((1, seq_len, hd), lambda h, b: (b0 + b, 0, nh + h)),
                  pl.BlockSpec((1, seq_len, hd), lambda h, b: (b0 + b, 0, 2 * nh + h)),
                  pl.BlockSpec((None, WIN_R, GRID_W, WIN_R * GRID_W), lambda h, b: (h, 0, 0, 0))],
        out_specs=pl.BlockSpec((1, seq_len, hd), lambda h, b: (b, 0, h)),
        out_shape=jax.ShapeDtypeStruct((bsz, seq_len, nh * hd), BF16),
        compiler_params=_cparams(("parallel", "parallel")),
        name="neighbourhood_attention",
    )(proj3, proj3, proj3, bias)


def _conv3_kernel(x_ref, w_ref, b_ref, o_ref, *, seq_len, rb):
    w = w_ref[...]
    bias = b_ref[...]
    n_chunks = seq_len // rb

    def body(c, carry):
        r0 = pl.multiple_of(c * rb, rb)
        x = x_ref[0, pl.ds(r0, rb), :].astype(F32)
        row = lax.broadcasted_iota(jnp.int32, x.shape, 0)
        p0 = pl.multiple_of(jnp.maximum(r0 - 16, 0), 16)
        n0 = pl.multiple_of(jnp.minimum(r0 + rb, seq_len - 16), 16)
        prev_row = x_ref[0, pl.ds(p0, 16), :].astype(F32)[15:16]
        next_row = x_ref[0, pl.ds(n0, 16), :].astype(F32)[0:1]
        prev_row = jnp.where(c == 0, 0.0, prev_row)
        next_row = jnp.where(c == n_chunks - 1, 0.0, next_row)
        up = jnp.where(row == 0, prev_row, pltpu.roll(x, 1, 0))
        down = jnp.where(row == rb - 1, next_row, pltpu.roll(x, rb - 1, 0))
        y = up * w[0:1] + x * w[1:2] + down * w[2:3] + bias
        o_ref[0, pl.ds(r0, rb), :] = y.astype(o_ref.dtype)
        return carry

    lax.fori_loop(0, n_chunks, body, 0)


def _short_conv3(proj3, b0, bsz, seq_len, col0, w, b, cb=256, rb=512):
    width = w.shape[1]
    rb = min(rb, seq_len)
    c0 = col0 // cb
    return pl.pallas_call(
        functools.partial(_conv3_kernel, seq_len=seq_len, rb=rb),
        grid=(bsz, width // cb),
        in_specs=[pl.BlockSpec((1, seq_len, cb), lambda bi, c: (b0 + bi, 0, c0 + c)),
                  pl.BlockSpec((3, cb), lambda bi, c: (0, c)),
                  pl.BlockSpec((1, cb), lambda bi, c: (0, c))],
        out_specs=pl.BlockSpec((1, seq_len, cb), lambda bi, c: (bi, 0, c)),
        out_shape=jax.ShapeDtypeStruct((bsz, seq_len, width), BF16),
        compiler_params=_cparams(("parallel", "parallel")),
        name="hyena_short_conv",
    )(proj3, w, b.reshape(1, width))


def _filter_kernel(z_ref, w1_ref, b1_ref, fr_ref, w2_ref, b2_ref, w3_ref, b3_ref, dl_ref, o_ref, *,
                   seq_len, rb):
    i = pl.program_id(0)
    hi = lax.Precision.HIGHEST
    z = z_ref[...]
    h = jnp.sin(fr_ref[0:1] * (jnp.dot(z, w1_ref[...], precision=hi, preferred_element_type=F32)
                               + b1_ref[...]))
    h = jnp.sin(fr_ref[1:2] * (jnp.dot(h, w2_ref[...], precision=hi, preferred_element_type=F32)
                               + b2_ref[...]))
    ch = dl_ref.shape[-1]
    n = i * rb + lax.broadcasted_iota(jnp.int32, (rb, ch), 0)
    sign = jnp.where(n < seq_len, 1.0, jnp.where(n == seq_len, 0.0, -1.0))
    window = jnp.exp(-z[:, 0:1] * dl_ref[...]) * sign
    hb = h.astype(BF16)
    for o in range(w3_ref.shape[0]):
        f = jnp.dot(hb, w3_ref[o].astype(BF16), preferred_element_type=F32) + b3_ref[o]
        o_ref[:, o * ch:(o + 1) * ch] = (f * window).astype(o_ref.dtype)


def _hyena_filter_signal(seq_len, w1, b1, freq, w2, b2, w3, b3, order, ch, rb=512):
    n = 2 * seq_len
    emb, hid = w1.shape
    pad = LANES
    t = jnp.linspace(0.0, 1.0, seq_len, dtype=F32)[:, None]
    bands = (emb - 1) // 2
    fr = jnp.linspace(1e-4, bands - 1, bands, dtype=F32)[None, :]
    wpos = 2.0 * math.pi * jnp.arange(seq_len, dtype=F32)[:, None] / seq_len
    z = jnp.concatenate([t, jnp.cos(fr * wpos), -jnp.sin(fr * wpos)], axis=-1)
    pos = np.arange(n)
    src = np.clip(np.where(pos < seq_len, pos, n - pos), 0, seq_len - 1)
    z2 = jnp.pad(z[src], ((0, 0), (0, pad - emb)))
    w1p = jnp.pad(w1.astype(F32), ((0, pad - emb), (0, pad - hid)))
    b1p = jnp.pad(b1.astype(F32), (0, pad - hid)).reshape(1, pad)
    frp = jnp.pad(freq.astype(F32), ((0, 0), (0, pad - hid)), constant_values=1.0)
    w2p = jnp.pad(w2.astype(F32), ((0, pad - hid), (0, pad - hid)))
    b2p = jnp.pad(b2.astype(F32), (0, pad - hid)).reshape(1, pad)
    w3p = jnp.pad(w3.astype(F32), ((0, pad - hid), (0, 0))).reshape(pad, order, 2, ch).transpose(2, 1, 0, 3)
    b3p = b3.astype(F32).reshape(order, 2, 1, ch).transpose(1, 0, 2, 3)
    deltas = jnp.abs(jnp.linspace(HY_MIN_DECAY, HY_MAX_DECAY, ch, dtype=F32)).reshape(1, ch)
    nb = n // rb
    half = nb // 2
    full = lambda i: (0, 0)
    return pl.pallas_call(
        functools.partial(_filter_kernel, seq_len=seq_len, rb=rb),
        grid=(nb,),
        in_specs=[pl.BlockSpec((rb, pad), lambda i: (i, 0)),
                  pl.BlockSpec((pad, pad), full), pl.BlockSpec((1, pad), full),
                  pl.BlockSpec((2, pad), full),
                  pl.BlockSpec((pad, pad), full), pl.BlockSpec((1, pad), full),
                  pl.BlockSpec((None, order, pad, ch), lambda i: (i // half, 0, 0, 0)),
                  pl.BlockSpec((None, order, 1, ch), lambda i: (i // half, 0, 0, 0)),
                  pl.BlockSpec((1, ch), full)],
        out_specs=pl.BlockSpec((rb, order * ch), lambda i: (i, 0)),
        out_shape=jax.ShapeDtypeStruct((n, order * ch), BF16),
        compiler_params=_cparams(("parallel",)),
        name="hyena_filter_mlp",
    )(z2, w1p, b1p, frp, w2p, b2p, w3p, b3p, deltas)


def _dft_tables(seq_len):
    n = 2 * seq_len
    n2 = DFT_N2
    n1 = n // n2
    hk = n2 // 2
    kg = min(n1, DFT_GROUP)
    ng = n1 // kg
    pi = math.pi
    lows = np.arange(n1 // 2).reshape(ng, kg // 2)
    k1 = jnp.asarray(np.concatenate([lows, n1 - 1 - lows], axis=1).reshape(-1), jnp.int32)
    mirror = jnp.asarray(np.tile(np.arange(kg) >= kg // 2, ng))
    slab = jnp.arange(n1, dtype=jnp.int32)
    ph1 = ((2 * k1[:, None] + 1) * slab[None, :]) % (2 * n1)
    th1 = ph1.astype(F32) * (pi / n1)
    c1 = jnp.cos(th1).reshape(ng, kg, n1)
    s1 = jnp.sin(th1).reshape(ng, kg, n1)
    f1 = jnp.concatenate([c1[:, : kg // 2], -s1[:, : kg // 2]], axis=1)
    fb = jnp.concatenate([c1, -s1], axis=1).transpose(0, 2, 1)[:, : n1 // 2] * (2.0 / n)
    k = k1[:, None, None] + n1 * jnp.arange(hk, dtype=jnp.int32)[None, :, None]
    ph2 = ((2 * k + 1) * jnp.arange(n2, dtype=jnp.int32)[None, None, :]) % (2 * n)
    th2 = ph2.astype(F32) * (pi / n)
    c2, s2 = jnp.cos(th2), jnp.sin(th2)
    sgn = jnp.where(mirror, -1.0, 1.0)[:, None, None]
    g = jnp.concatenate([jnp.concatenate([c2, sgn * s2], axis=2),
                         jnp.concatenate([-s2, sgn * c2], axis=2)], axis=1)
    c2t, s2t = c2.transpose(0, 2, 1), s2.transpose(0, 2, 1)
    h = jnp.concatenate([jnp.concatenate([c2t, -s2t], axis=2),
                         jnp.concatenate([s2t, c2t], axis=2)], axis=1)
    nl = 2 if n1 <= DFT_GROUP else 1
    return dict(n1=n1, kg=kg, ng=ng, nl=nl, f1_full=f1.astype(BF16), f1_half=f1[:, :, : n1 // 2].astype(BF16),
                fb=fb.astype(BF16), g=g.astype(BF16), h=h.astype(BF16))


def _load_lanes(ref, rows):
    parts = [ref[s, rows, :] for s in range(ref.shape[0])]
    return parts[0] if len(parts) == 1 else jnp.concatenate(parts, axis=1)


def _store_lanes(ref, rows, val):
    for s in range(ref.shape[0]):
        ref[s, rows, :] = val[:, s * LANES:(s + 1) * LANES]


def _fill_slabs(x_ref, xs, slabs, pitch):
    def body(n1, carry):
        src = pl.ds(pl.multiple_of(n1 * DFT_N2, DFT_N2), DFT_N2)
        _store_lanes(xs, pl.ds(pl.multiple_of(n1 * pitch, SUBLANES), DFT_N2), x_ref[src, :].astype(F32))
        return carry
    lax.fori_loop(0, slabs, body, 0, unroll=4)


def _slab_stage(xs, a_s, f1, slabs, p_x, p_a):
    rows = f1.shape[0]

    def body(i, carry):
        xn = _load_lanes(xs, pl.ds(i, slabs, stride=p_x)).astype(BF16)
        _store_lanes(a_s, pl.ds(pl.multiple_of(i * p_a, SUBLANES), rows),
                     jnp.dot(f1, xn, preferred_element_type=F32))
        return carry
    lax.fori_loop(0, DFT_N2, body, 0, unroll=64)


def _load_slab_freq(a_s, low, kg, p_a):
    are = _load_lanes(a_s, pl.ds(low, DFT_N2, stride=p_a))
    aim = _load_lanes(a_s, pl.ds(kg // 2 + low, DFT_N2, stride=p_a))
    return jnp.concatenate([are, aim], axis=0).astype(BF16)


def _for_each_slot(kg, body):
    for first in (0, kg // 2):
        def step(low, carry, first=first):
            body(first + low, low)
            return carry
        lax.fori_loop(0, kg // 2, step, 0, unroll=16)


def _filter_spec_kernel(x_ref, f1_ref, g_ref, o_ref, xs, a_s, *, slabs, kg):
    grp = pl.program_id(1)
    p_x, p_a = _pitch(DFT_N2), _pitch(kg)

    @pl.when(grp == 0)
    def _():
        _fill_slabs(x_ref, xs, slabs, p_x)

    _slab_stage(xs, a_s, f1_ref[0], slabs, p_x, p_a)

    def body(j, low):
        o_ref[j] = jnp.dot(g_ref[j], _load_slab_freq(a_s, low, kg, p_a), preferred_element_type=F32)
    _for_each_slot(kg, body)


def _filter_spectrum(f, tables):
    n, cols = f.shape
    n1, kg, ng, nl = tables["n1"], tables["kg"], tables["ng"], tables["nl"]
    p_x, p_a = _pitch(DFT_N2), _pitch(kg)
    wl = nl * LANES
    return pl.pallas_call(
        functools.partial(_filter_spec_kernel, slabs=n1, kg=kg),
        grid=(cols // wl, ng),
        in_specs=[pl.BlockSpec((n, wl), lambda c, g: (0, c)),
                  pl.BlockSpec((1, kg, n1), lambda c, g: (g, 0, 0)),
                  pl.BlockSpec((kg, DFT_N2, 2 * DFT_N2), lambda c, g: (g, 0, 0))],
        out_specs=pl.BlockSpec((kg, DFT_N2, wl), lambda c, g: (g, 0, c)),
        out_shape=jax.ShapeDtypeStruct((n1, DFT_N2, cols), F32),
        scratch_shapes=[pltpu.VMEM((nl, n1 * p_x, LANES), F32),
                        pltpu.VMEM((nl, DFT_N2 * p_a, LANES), F32)],
        compiler_params=_cparams(("parallel", "arbitrary")),
        name="hyena_filter_spectrum",
    )(f, tables["f1_full"], tables["g"])


def _long_conv_kernel(x_ref, gate_ref, f1_ref, g_ref, h_ref, kf_ref, fb_ref, skip_ref, o_ref,
                      xs, a_s, z_s, y_s, *, slabs, kg, ng):
    grp = pl.program_id(2)
    n2 = DFT_N2
    hk = n2 // 2
    p_x, p_a, p_z, p_y = _pitch(n2), _pitch(kg), _pitch(2 * n2), _pitch(slabs)

    @pl.when(grp == 0)
    def _():
        _fill_slabs(x_ref.at[0], xs, slabs, p_x)
        y_s[...] = jnp.zeros_like(y_s)

    _slab_stage(xs, a_s, f1_ref[0], slabs, p_x, p_a)

    def freq_body(j, low):
        x = jnp.dot(g_ref[j], _load_slab_freq(a_s, low, kg, p_a), preferred_element_type=F32)
        kf = kf_ref[j]
        xr, xi = x[:hk], x[hk:]
        kr, ki = kf[:hk], kf[hk:]
        y = jnp.concatenate([xr * kr - xi * ki, xr * ki + xi * kr], axis=0)
        _store_lanes(z_s, pl.ds(pl.multiple_of(j * p_z, SUBLANES), n2), y)
    _for_each_slot(kg, freq_body)

    def inv_freq_body(j, carry):
        y = _load_lanes(z_s, pl.ds(pl.multiple_of(j * p_z, SUBLANES), n2)).astype(BF16)
        _store_lanes(z_s, pl.ds(pl.multiple_of(j * p_z, SUBLANES), 2 * n2),
                     jnp.dot(h_ref[j], y, preferred_element_type=F32))
        return carry
    lax.fori_loop(0, kg, inv_freq_body, 0, unroll=16)

    fb = fb_ref[0]

    def inv_body(t, carry):
        zre = _load_lanes(z_s, pl.ds(t, kg, stride=p_z))
        zim = _load_lanes(z_s, pl.ds(n2 + t, kg, stride=p_z))
        zz = jnp.concatenate([zre, zim], axis=0).astype(BF16)
        dst = pl.ds(pl.multiple_of(t * p_y, SUBLANES), slabs)
        _store_lanes(y_s, dst, _load_lanes(y_s, dst) + jnp.dot(fb, zz, preferred_element_type=F32))
        return carry
    lax.fori_loop(0, n2, inv_body, 0, unroll=64)

    @pl.when(grp == ng - 1)
    def _():
        skip = skip_ref[...]

        def out_body(t1, carry):
            y = _load_lanes(y_s, pl.ds(t1, n2, stride=p_y))
            u = _load_lanes(xs, pl.ds(pl.multiple_of(t1 * p_x, SUBLANES), n2))
            rows = pl.ds(pl.multiple_of(t1 * n2, n2), n2)
            gate = gate_ref[0, rows, :].astype(F32)
            o_ref[0, rows, :] = ((y + skip * u) * gate).astype(o_ref.dtype)
            return carry
        lax.fori_loop(0, slabs, out_body, 0, unroll=4)


def _long_conv_gate(u, u_blk0, gate, gate_blk0, tables, kf, kf_blk0, skip, ch):
    bsz, seq_len, _ = u.shape
    n1, kg, ng, nl = tables["n1"], tables["kg"], tables["ng"], tables["nl"]
    slabs = n1 // 2
    p_x, p_a, p_z, p_y = _pitch(DFT_N2), _pitch(kg), _pitch(2 * DFT_N2), _pitch(slabs)
    wl = nl * LANES
    return pl.pallas_call(
        functools.partial(_long_conv_kernel, slabs=slabs, kg=kg, ng=ng),
        grid=(bsz, ch // wl, ng),
        in_specs=[pl.BlockSpec((1, seq_len, wl), lambda b, c, g: (b, 0, u_blk0 + c)),
                  pl.BlockSpec((1, seq_len, wl), lambda b, c, g: (b, 0, gate_blk0 + c)),
                  pl.BlockSpec((1, kg, slabs), lambda b, c, g: (g, 0, 0)),
                  pl.BlockSpec((kg, DFT_N2, 2 * DFT_N2), lambda b, c, g: (g, 0, 0)),
                  pl.BlockSpec((kg, 2 * DFT_N2, DFT_N2), lambda b, c, g: (g, 0, 0)),
                  pl.BlockSpec((kg, DFT_N2, wl), lambda b, c, g: (g, 0, kf_blk0 + c)),
                  pl.BlockSpec((1, slabs, 2 * kg), lambda b, c, g: (g, 0, 0)),
                  pl.BlockSpec((1, wl), lambda b, c, g: (0, c))],
        out_specs=pl.BlockSpec((1, seq_len, wl), lambda b, c, g: (b, 0, c)),
        out_shape=jax.ShapeDtypeStruct((bsz, seq_len, ch), BF16),
        scratch_shapes=[pltpu.VMEM((nl, slabs * p_x, LANES), F32),
                        pltpu.VMEM((nl, DFT_N2 * p_a, LANES), F32),
                        pltpu.VMEM((nl, kg * p_z, LANES), F32),
                        pltpu.VMEM((nl, DFT_N2 * p_y, LANES), F32)],
        compiler_params=_cparams(("parallel", "parallel", "arbitrary")),
        name="hyena_long_conv",
    )(u, gate, tables["f1_half"], tables["g"], tables["h"], kf, tables["fb"],
      skip.astype(F32).reshape(1, ch))


def _hyena(hy, tables, kf, skip, order, ch):
    nblk = ch // (tables["nl"] * LANES)
    z, z_blk0 = hy, 0
    for o in range(order):
        z = _long_conv_gate(z, z_blk0, hy, (o + 1) * nblk, tables, kf, o * nblk, skip[o], ch)
    return z


def _rope(x, cos, sin):
    half = x.shape[-1] // 2
    x1, x2 = x[:, :half], x[:, half:]
    return jnp.concatenate([x1 * cos - x2 * sin, x1 * sin + x2 * cos], axis=-1)


def _ret_kernel(lg_ref, q_ref, k_ref, v_ref, g_ref, cos_ref, sin_ref, o_ref,
                sb_all, kr_all, kz_all, sf_ref, sb_ref, *, n_super, chunks_per_super, k_scale):
    h = pl.program_id(1)
    s = pl.program_id(2)
    cc = RET_CHUNK
    dh = q_ref.shape[-1]
    lgf = lg_ref[0, h]
    lgb = lg_ref[1, h]
    row = lax.broadcasted_iota(jnp.int32, (cc, dh), 0).astype(F32)
    zeta_f = jnp.exp(lgf * (cc - 1.0 - row))

    @pl.when(s < n_super)
    def _backward_sweep():
        @pl.when(s == 0)
        def _():
            sb_ref[...] = jnp.zeros_like(sb_ref)

        sup = n_super - 1 - s
        zeta_b = jnp.exp(lgb * row)
        chunk_decay = jnp.exp(lgb * jnp.full((1, dh), float(cc), F32))

        def body(t, carry):
            c = chunks_per_super - 1 - t
            c0 = pl.multiple_of(c * cc, cc)
            n = sup * chunks_per_super + c
            sb_all[n] = sb_ref[...].astype(BF16)
            k = _rope(k_ref[0, pl.ds(c0, cc), :].astype(F32), cos_ref[pl.ds(c0, cc), :],
                      sin_ref[pl.ds(c0, cc), :]) * k_scale
            kr_all[n] = k.astype(BF16)
            kz_all[n] = (k * zeta_f).astype(BF16)
            kz = (k * zeta_b).astype(BF16)
            v = v_ref[0, pl.ds(c0, cc), :]
            upd = lax.dot_general(kz, v, (((0,), (0,)), ((), ())), preferred_element_type=F32)
            sb_ref[...] = sb_ref[...] * chunk_decay + upd
            return carry

        lax.fori_loop(0, chunks_per_super, body, 0, unroll=4)

    @pl.when(s >= n_super)
    def _forward_sweep():
        @pl.when(s == n_super)
        def _():
            sf_ref[...] = jnp.zeros_like(sf_ref)

        sup = s - n_super
        col = lax.broadcasted_iota(jnp.int32, (cc, cc), 1).astype(F32)
        rowc = lax.broadcasted_iota(jnp.int32, (cc, cc), 0).astype(F32)
        diff = rowc - col
        inner_decay = jnp.where(diff >= 0, jnp.exp(lgf * jnp.maximum(diff, 0.0)),
                                jnp.exp(lgb * jnp.maximum(-diff, 0.0)))
        xi_f = jnp.exp(lgf * (row + 1.0))
        xi_b = jnp.exp(lgb * (cc - row))
        chunk_decay = jnp.exp(lgf * jnp.full((1, dh), float(cc), F32))

        def body(c, carry):
            c0 = pl.multiple_of(c * cc, cc)
            n = sup * chunks_per_super + c
            q = _rope(q_ref[0, pl.ds(c0, cc), :].astype(F32), cos_ref[pl.ds(c0, cc), :],
                      sin_ref[pl.ds(c0, cc), :]).astype(BF16)
            v = v_ref[0, pl.ds(c0, cc), :]
            sc = lax.dot_general(q, kr_all[n], (((1,), (1,)), ((), ())),
                                 preferred_element_type=F32) * inner_decay
            ret = jnp.dot(sc.astype(BF16), v, preferred_element_type=F32)
            ret = ret + xi_f * jnp.dot(q, sf_ref[...].astype(BF16), preferred_element_type=F32)
            ret = ret + xi_b * jnp.dot(q, sb_all[n], preferred_element_type=F32)
            upd = lax.dot_general(kz_all[n], v, (((0,), (0,)), ((), ())), preferred_element_type=F32)
            sf_ref[...] = sf_ref[...] * chunk_decay + upd
            ret = ret * lax.rsqrt(jnp.mean(ret * ret, -1, keepdims=True) + RMS_EPS)
            gate = g_ref[0, pl.ds(c0, cc), :].astype(F32)
            o_ref[0, pl.ds(c0, cc), :] = (ret * (gate / (1.0 + jnp.exp(-gate)))).astype(o_ref.dtype)
            return carry

        lax.fori_loop(0, chunks_per_super, body, 0, unroll=4)


def _retention(proj3, b0, bsz, seq_len, col0, nh, dh, log_gamma, cos, sin):
    sup_len = min(RET_SUPER, seq_len)
    n_super = seq_len // sup_len
    cps = sup_len // RET_CHUNK
    cb0 = col0 // dh

    def fwd_idx(s):
        return jnp.maximum(s - n_super, 0)

    def kv_idx(s):
        return jnp.where(s < n_super, n_super - 1 - s, s - n_super)

    def k_idx(s):
        return jnp.maximum(n_super - 1 - s, 0)

    n_chunks = seq_len // RET_CHUNK
    return pl.pallas_call(
        functools.partial(_ret_kernel, n_super=n_super, chunks_per_super=cps, k_scale=dh ** -0.5),
        grid=(bsz, nh, 2 * n_super),
        in_specs=[pl.BlockSpec(memory_space=pltpu.SMEM),
                  pl.BlockSpec((1, sup_len, dh), lambda b, h, s: (b0 + b, fwd_idx(s), cb0 + h)),
                  pl.BlockSpec((1, sup_len, dh), lambda b, h, s: (b0 + b, k_idx(s), cb0 + nh + h)),
                  pl.BlockSpec((1, sup_len, dh), lambda b, h, s: (b0 + b, kv_idx(s), cb0 + 2 * nh + h)),
                  pl.BlockSpec((1, sup_len, dh), lambda b, h, s: (b0 + b, fwd_idx(s), cb0 + 3 * nh + h)),
                  pl.BlockSpec((sup_len, dh // 2), lambda b, h, s: (kv_idx(s), 0)),
                  pl.BlockSpec((sup_len, dh // 2), lambda b, h, s: (kv_idx(s), 0))],
        out_specs=pl.BlockSpec((1, sup_len, dh), lambda b, h, s: (b, fwd_idx(s), h)),
        out_shape=jax.ShapeDtypeStruct((bsz, seq_len, nh * dh), BF16),
        scratch_shapes=[pltpu.VMEM((n_chunks, dh, dh), BF16),
                        pltpu.VMEM((n_chunks, RET_CHUNK, dh), BF16),
                        pltpu.VMEM((n_chunks, RET_CHUNK, dh), BF16),
                        pltpu.VMEM((dh, dh), F32),
                        pltpu.VMEM((dh, dh), F32)],
        compiler_params=_cparams(("parallel", "parallel", "arbitrary")),
        name="retention",
    )(log_gamma, proj3, proj3, proj3, proj3, cos, sin)


def _assemble_kernel(a1_ref, a2_ref, z1_ref, z2_ref, c1_ref, c2_ref, ga_ref, gb_ref, o_ref, *, nb1):
    first = pl.program_id(0) < nb1
    wa = a1_ref.shape[-1]
    wb = z1_ref.shape[-1]
    a = jnp.where(first, a1_ref[...], a2_ref[...]).astype(F32)
    a = a * lax.rsqrt(jnp.mean(a * a, -1, keepdims=True) + RMS_EPS) * ga_ref[...]
    z = jnp.where(first, z1_ref[...], z2_ref[...]).astype(F32)
    z = z * lax.rsqrt(jnp.mean(z * z, -1, keepdims=True) + RMS_EPS) * gb_ref[...]
    o_ref[:, :wa] = a.astype(o_ref.dtype)
    o_ref[:, wa:wa + wb] = z.astype(o_ref.dtype)
    o_ref[:, wa + wb:] = jnp.where(first, c1_ref[...], c2_ref[...])


def _assemble(a_parts, z_parts, c_parts, gain_a, gain_b, bm=1024):
    m1, wa = a_parts[0].shape
    m = m1 + a_parts[1].shape[0]
    assert m1 % bm == 0 and m % bm == 0
    wb, wc = z_parts[0].shape[1], c_parts[0].shape[1]
    nb1 = m1 // bm
    part_specs = []
    for w in (wa, wb, wc):
        part_specs += [pl.BlockSpec((bm, w), _first_rows(nb1)), pl.BlockSpec((bm, w), _second_rows(nb1))]
    return pl.pallas_call(
        functools.partial(_assemble_kernel, nb1=nb1),
        grid=(m // bm,),
        in_specs=part_specs + [pl.BlockSpec((1, wa), lambda i: (0, 0)),
                               pl.BlockSpec((1, wb), lambda i: (0, 0))],
        out_specs=pl.BlockSpec((bm, wa + wb + wc), lambda i: (i, 0)),
        out_shape=jax.ShapeDtypeStruct((m, wa + wb + wc), BF16),
        compiler_params=_cparams(("arbitrary",)),
        name="assemble_mixers",
    )(*a_parts, *z_parts, *c_parts, gain_a.astype(F32).reshape(1, wa), gain_b.astype(F32).reshape(1, wb))


def _token_mixers(proj, requests, p, l, dims):
    nh_a, hd_a, ch, order, nh_r, dh_r = dims
    in_cols = proj.shape[1]
    total = proj.shape[0]
    wa = nh_a * hd_a
    hy0 = 3 * wa
    ret0 = hy0 + (order + 1) * ch
    log_gamma = jnp.log1p(-jnp.exp2(-p["ret_decay_exp"][l].astype(F32)))
    a_parts, z_parts, c_parts = [], [], []
    bias = _na_bias_tables(p["na_rpb"][l])
    for (bsz, seq_len, tok0) in requests:
        assert tok0 % seq_len == 0 and total % seq_len == 0
        proj3 = proj.reshape(total // seq_len, seq_len, in_cols)
        b0 = tok0 // seq_len
        a = _neighbourhood_attention(proj3, b0, bsz, seq_len, bias, nh_a, hd_a)
        a_parts.append(a.reshape(bsz * seq_len, wa))
        tables = _dft_tables(seq_len)
        f = _hyena_filter_signal(seq_len, p["hy_f_w1"][l], p["hy_f_b1"][l], p["hy_f_freq"][l],
                                 p["hy_f_w2"][l], p["hy_f_b2"][l], p["hy_f_w3"][l], p["hy_f_b3"][l],
                                 order, ch)
        kf = _filter_spectrum(f, tables)
        hy = _short_conv3(proj3, b0, bsz, seq_len, hy0, p["hy_conv_w"][l].astype(F32),
                          p["hy_conv_b"][l].astype(F32))
        z = _hyena(hy, tables, kf, p["hy_skip"][l], order, ch)
        z_parts.append(z.reshape(bsz * seq_len, ch))
        half = dh_r // 2
        inv_freq = 1.0 / (ROPE_BASE ** jnp.linspace(0.0, 1.0, half, dtype=F32))
        ang = jnp.arange(seq_len, dtype=F32)[:, None] * inv_freq[None, :]
        c = _retention(proj3, b0, bsz, seq_len, ret0, nh_r, dh_r, log_gamma, jnp.cos(ang), jnp.sin(ang))
        c_parts.append(c.reshape(bsz * seq_len, nh_r * dh_r))
    return _assemble(a_parts, z_parts, c_parts, p["grp_gain_a"][l], p["grp_gain_b"][l])


def _pick(n, candidates):
    for c in candidates:
        if n % c == 0:
            return c
    raise ValueError(f"no block size for {n}")


def kernel(x_prompt, x_sample, ln_in_g, ln_in_b, w_in, na_rpb, hy_conv_w, hy_conv_b, hy_f_w1, hy_f_b1,
           hy_f_freq, hy_f_w2, hy_f_b2, hy_f_w3, hy_f_b3, hy_skip, ret_decay_exp, grp_gain_a, grp_gain_b,
           w_out, ln1_g, ln1_b, w_ffn_in, w_ffn_out, ln2_g, ln2_b):
    p = dict(na_rpb=na_rpb, hy_conv_w=hy_conv_w, hy_conv_b=hy_conv_b, hy_f_w1=hy_f_w1, hy_f_b1=hy_f_b1,
             hy_f_freq=hy_f_freq, hy_f_w2=hy_f_w2, hy_f_b2=hy_f_b2, hy_f_w3=hy_f_w3, hy_f_b3=hy_f_b3,
             hy_skip=hy_skip, ret_decay_exp=ret_decay_exp, grp_gain_a=grp_gain_a, grp_gain_b=grp_gain_b)
    depth, d_model, in_cols = w_in.shape
    nh_a = na_rpb.shape[1]
    wa = grp_gain_a.shape[1]
    ch = grp_gain_b.shape[1]
    order = hy_skip.shape[1]
    nh_r = ret_decay_exp.shape[2]
    wr = (in_cols - 3 * wa - (order + 1) * ch) // 4
    dims = (nh_a, wa // nh_a, ch, order, nh_r, wr // nh_r)
    d_ff = w_ffn_out.shape[1]
    alpha = (2.0 * depth) ** 0.25

    bp, lp, _ = x_prompt.shape
    bs, ls, _ = x_sample.shape
    requests = [(bp, lp, 0), (bs, ls, bp * lp)]
    m1 = bp * lp
    m = m1 + bs * ls
    bm = _pick(m, (1024, 512, 256))
    bm_ln = 512
    assert m1 % bm_ln == 0 and m % bm_ln == 0

    xf, xb = _layernorm_join(x_prompt.reshape(m1, d_model), x_sample.reshape(bs * ls, d_model),
                             ln_in_g.astype(F32), ln_in_b.astype(F32), bm_ln // 2)
    w_in_b = w_in[0:1].astype(BF16)
    resid = (xf,)
    for l in range(depth):
        bn_out = _pick(d_model, (1024, 512, 256) if len(resid) == 1 else (512, 256))
        cast = [(w_out, l, bn_out), (w_ffn_in, l, _pick(d_ff, (512, 256, 128))),
                (w_ffn_out, l, _pick(d_model, (512, 256)))]
        if l + 1 < depth:
            cast.append((w_in, l + 1, in_cols))
        proj, cast_w = _matmul(xb, w_in_b, 0, bm, _pick(in_cols, (1024, 512, 256)), cast)
        w_out_b, w_ffn_in_b, w_ffn_out_b = cast_w[:3]
        if l + 1 < depth:
            w_in_b = cast_w[3]
        mixed = _token_mixers(proj, requests, p, l, dims)
        y = _matmul_residual(mixed, w_out_b, resid, alpha, bm)
        g1, b1 = ln1_g[l].astype(F32), ln1_b[l].astype(F32)
        xb, stats = _layernorm_stats(y, g1, b1, bm_ln)
        hmid = _matmul_swiglu(xb, w_ffn_in_b, _pick(m, (2048, 1024, 512, 256)))
        y = _matmul_residual(hmid, w_ffn_out_b, (y, stats, g1, b1), alpha, _pick(m, (512, 256)))
        if l + 1 < depth:
            g2, b2 = ln2_g[l].astype(F32), ln2_b[l].astype(F32)
            xb, stats = _layernorm_stats(y, g2, b2, bm_ln)
            resid = (y, stats, g2, b2)
    o1, o2 = _layernorm_split(y, ln2_g[depth - 1].astype(F32), ln2_b[depth - 1].astype(F32), m1, bm_ln // 2)
    return (o1.reshape(bp, lp, d_model), o2.reshape(bs, ls, d_model))
```

```python
import functools
import math

import numpy as np
import jax
import jax.numpy as jnp
from jax import lax
from jax.experimental import pallas as pl
from jax.experimental.pallas import tpu as pltpu

F32 = jnp.float32
BF16 = jnp.bfloat16

GRID_W = 64
WIN_R = 8
WIN_C = 16
ROPE_BASE = 10000.0
LN_EPS = 1e-5
RMS_EPS = 1e-6
NEG_INF = -1e30
HY_TARGET = 1e-2
HY_FAST_DECAY = 0.3
HY_SLOW_DECAY = 1.5
HY_MIN_DECAY = math.log(HY_TARGET) / HY_SLOW_DECAY
HY_MAX_DECAY = math.log(HY_TARGET) / HY_FAST_DECAY

V7X_VMEM_LIMIT = 56 * 1024 * 1024
LANES = 128
SUBLANES = 8
DFT_N2 = 128
DFT_GROUP = 32
NA_ROWS_PER_STEP = 8
RET_CHUNK = 256
RET_SUPER = 2048


def _cparams(sem, vmem=V7X_VMEM_LIMIT):
    return pltpu.CompilerParams(dimension_semantics=sem, vmem_limit_bytes=vmem)


def _pitch(rows):
    p = -(-rows // SUBLANES)
    return SUBLANES * (p + 1 - p % 2)


def _ln(x, g, b):
    mu = jnp.mean(x, -1, keepdims=True)
    xc = x - mu
    var = jnp.mean(xc * xc, -1, keepdims=True)
    return xc * lax.rsqrt(var + LN_EPS) * g + b


def _ln_stats_kernel(x_ref, g_ref, b_ref, ob_ref, st_ref):
    x = x_ref[...]
    mu = jnp.mean(x, -1, keepdims=True)
    xc = x - mu
    rstd = lax.rsqrt(jnp.mean(xc * xc, -1, keepdims=True) + LN_EPS)
    ob_ref[...] = (xc * rstd * g_ref[...] + b_ref[...]).astype(BF16)
    st_ref[:, :LANES] = jnp.broadcast_to(mu, (x.shape[0], LANES))
    st_ref[:, LANES:] = jnp.broadcast_to(rstd, (x.shape[0], LANES))


def _ln_join_kernel(x1_ref, x2_ref, g_ref, b_ref, of_ref, ob_ref, *, nb1):
    x = jnp.where(pl.program_id(0) < nb1, x1_ref[...], x2_ref[...])
    y = _ln(x, g_ref[...], b_ref[...])
    of_ref[...] = y
    ob_ref[...] = y.astype(BF16)


def _ln_split_kernel(x_ref, g_ref, b_ref, o1_ref, o2_ref, *, nb1):
    y = _ln(x_ref[...], g_ref[...], b_ref[...])

    @pl.when(pl.program_id(0) < nb1)
    def _():
        o1_ref[...] = y

    @pl.when(pl.program_id(0) >= nb1)
    def _():
        o2_ref[...] = y


def _first_rows(nb1):
    return lambda i: (jnp.minimum(i, nb1 - 1), 0)


def _second_rows(nb1):
    return lambda i: (jnp.maximum(i - nb1, 0), 0)


def _layernorm_join(x1, x2, g, b, bm=256):
    m1, d = x1.shape
    m = m1 + x2.shape[0]
    nb1 = m1 // bm
    return pl.pallas_call(
        functools.partial(_ln_join_kernel, nb1=nb1),
        grid=(m // bm,),
        in_specs=[pl.BlockSpec((bm, d), _first_rows(nb1)),
                  pl.BlockSpec((bm, d), _second_rows(nb1)),
                  pl.BlockSpec((1, d), lambda i: (0, 0)),
                  pl.BlockSpec((1, d), lambda i: (0, 0))],
        out_specs=[pl.BlockSpec((bm, d), lambda i: (i, 0)),
                   pl.BlockSpec((bm, d), lambda i: (i, 0))],
        out_shape=[jax.ShapeDtypeStruct((m, d), F32), jax.ShapeDtypeStruct((m, d), BF16)],
        compiler_params=_cparams(("arbitrary",)),
        name="layernorm_join",
    )(x1, x2, g.reshape(1, d), b.reshape(1, d))


def _layernorm_split(x, g, b, m1, bm=256):
    m, d = x.shape
    nb1 = m1 // bm
    return pl.pallas_call(
        functools.partial(_ln_split_kernel, nb1=nb1),
        grid=(m // bm,),
        in_specs=[pl.BlockSpec((bm, d), lambda i: (i, 0)),
                  pl.BlockSpec((1, d), lambda i: (0, 0)),
                  pl.BlockSpec((1, d), lambda i: (0, 0))],
        out_specs=[pl.BlockSpec((bm, d), _first_rows(nb1)),
                   pl.BlockSpec((bm, d), _second_rows(nb1))],
        out_shape=[jax.ShapeDtypeStruct((m1, d), F32), jax.ShapeDtypeStruct((m - m1, d), F32)],
        compiler_params=_cparams(("arbitrary",)),
        name="layernorm_split",
    )(x, g.reshape(1, d), b.reshape(1, d))


def _layernorm_stats(x, g, b, bm=256):
    m, d = x.shape
    return pl.pallas_call(
        _ln_stats_kernel,
        grid=(m // bm,),
        in_specs=[pl.BlockSpec((bm, d), lambda i: (i, 0)),
                  pl.BlockSpec((1, d), lambda i: (0, 0)),
                  pl.BlockSpec((1, d), lambda i: (0, 0))],
        out_specs=[pl.BlockSpec((bm, d), lambda i: (i, 0)),
                   pl.BlockSpec((bm, 2 * LANES), lambda i: (i, 0))],
        out_shape=[jax.ShapeDtypeStruct((m, d), BF16), jax.ShapeDtypeStruct((m, 2 * LANES), F32)],
        compiler_params=_cparams(("parallel",)),
        name="layernorm_stats",
    )(x, g.reshape(1, d), b.reshape(1, d))


def _mm_res_kernel(x_ref, w_ref, r_ref, o_ref, *, alpha):
    o_ref[...] = alpha * r_ref[...] + jnp.dot(x_ref[...], w_ref[...], preferred_element_type=F32)


def _mm_res_ln_kernel(x_ref, w_ref, y_ref, st_ref, g_ref, b_ref, o_ref, *, alpha):
    acc = jnp.dot(x_ref[...], w_ref[...], preferred_element_type=F32)
    mu = st_ref[:, :LANES]
    rstd = st_ref[:, LANES:]
    for c in range(y_ref.shape[1] // LANES):
        cols = slice(c * LANES, (c + 1) * LANES)
        r = (y_ref[:, cols] - mu) * rstd * g_ref[:, cols] + b_ref[:, cols]
        o_ref[:, cols] = alpha * r + acc[:, cols]


def _mm_swiglu_kernel(x_ref, wg_ref, wu_ref, o_ref):
    x = x_ref[...]
    g = jnp.dot(x, wg_ref[...], preferred_element_type=F32)
    u = jnp.dot(x, wu_ref[...], preferred_element_type=F32)
    o_ref[...] = (g / (1.0 + jnp.exp(-g)) * u).astype(o_ref.dtype)


def _mm_cast_kernel(x_ref, w_ref, *refs):
    n_cast = (len(refs) - 1) // 2
    o_ref = refs[n_cast]
    o_ref[...] = jnp.dot(x_ref[...], w_ref[...], preferred_element_type=F32).astype(o_ref.dtype)
    for src, dst in zip(refs[:n_cast], refs[n_cast + 1:]):
        tile = dst.shape[-1]
        for t in range(dst.shape[0]):
            dst[t] = src[:, t * tile:(t + 1) * tile].astype(dst.dtype)


def _cast_rows(rows, n_steps):
    rb = 16
    while rows % rb or rows // rb > n_steps:
        rb += 16
        assert rb <= rows
    return rb


def _matmul(x, w, layer, bm, bn, cast=()):
    m, k = x.shape
    n = w.shape[2]
    nj = n // bn
    n_steps = (m // bm) * nj
    cast_in, cast_out, cast_shapes, cast_args = [], [], [], []
    for stack, lyr, tile in cast:
        _, rows, cols = stack.shape
        rb = _cast_rows(rows, n_steps)
        last = rows // rb - 1
        cast_in.append(pl.BlockSpec((None, rb, cols),
                                    lambda i, j, lyr=lyr, last=last: (lyr, jnp.minimum(i * nj + j, last), 0)))
        cast_out.append(pl.BlockSpec((cols // tile, rb, tile),
                                     lambda i, j, last=last: (0, jnp.minimum(i * nj + j, last), 0)))
        cast_shapes.append(jax.ShapeDtypeStruct((cols // tile, rows, tile), BF16))
        cast_args.append(stack)
    outs = pl.pallas_call(
        _mm_cast_kernel,
        grid=(m // bm, nj),
        in_specs=[pl.BlockSpec((bm, k), lambda i, j: (i, 0)),
                  pl.BlockSpec((None, k, bn), lambda i, j: (layer, 0, j))] + cast_in,
        out_specs=[pl.BlockSpec((bm, bn), lambda i, j: (i, j))] + cast_out,
        out_shape=[jax.ShapeDtypeStruct((m, n), BF16)] + cast_shapes,
        compiler_params=_cparams(("arbitrary", "arbitrary")),
        name="matmul",
    )(x, w, *cast_args)
    return outs[0], outs[1:]


def _matmul_residual(x, w, resid, alpha, bm):
    m, k = x.shape
    bn = w.shape[2]
    n = w.shape[0] * bn
    tile = pl.BlockSpec((bm, bn), lambda i, j: (i, j))
    if len(resid) == 1:
        body, resid_specs = _mm_res_kernel, [tile]
    else:
        body = _mm_res_ln_kernel
        resid = (resid[0], resid[1], resid[2].reshape(1, n), resid[3].reshape(1, n))
        resid_specs = [tile, pl.BlockSpec((bm, 2 * LANES), lambda i, j: (i, 0)),
                       pl.BlockSpec((1, bn), lambda i, j: (0, j)), pl.BlockSpec((1, bn), lambda i, j: (0, j))]
    return pl.pallas_call(
        functools.partial(body, alpha=alpha),
        grid=(m // bm, n // bn),
        in_specs=[pl.BlockSpec((bm, k), lambda i, j: (i, 0)),
                  pl.BlockSpec((None, k, bn), lambda i, j: (j, 0, 0))] + resid_specs,
        out_specs=tile,
        out_shape=jax.ShapeDtypeStruct((m, n), F32),
        compiler_params=_cparams(("parallel", "arbitrary")),
        name="matmul_residual",
    )(x, w, *resid)


def _matmul_swiglu(x, w, bm):
    m, k = x.shape
    bn = w.shape[2]
    nb = w.shape[0] // 2
    f = nb * bn
    return pl.pallas_call(
        _mm_swiglu_kernel,
        grid=(m // bm, nb),
        in_specs=[pl.BlockSpec((bm, k), lambda i, j: (i, 0)),
                  pl.BlockSpec((None, k, bn), lambda i, j: (j, 0, 0)),
                  pl.BlockSpec((None, k, bn), lambda i, j: (j + nb, 0, 0))],
        out_specs=pl.BlockSpec((bm, bn), lambda i, j: (i, j)),
        out_shape=jax.ShapeDtypeStruct((m, f), BF16),
        compiler_params=_cparams(("parallel", "arbitrary")),
        name="matmul_swiglu",
    )(x, w, w)


def _na_bias_tables(rpb):
    nh = rpb.shape[0]
    qc = np.arange(GRID_W)[:, None]
    kc = np.arange(GRID_W)[None, :]
    win_start = np.clip(qc - WIN_C // 2, 0, GRID_W - WIN_C)
    col_ok = (kc >= win_start) & (kc < win_start + WIN_C)
    dc_idx = np.clip(kc - qc + (WIN_C - 1), 0, 2 * WIN_C - 2)
    onehot = (dc_idx.reshape(-1)[None, :] == np.arange(2 * WIN_C - 1)[:, None]).astype(np.float32)
    tiles = jnp.einsum("hdc,cq->hdq", rpb.astype(F32), jnp.asarray(onehot),
                       precision=lax.Precision.HIGHEST).reshape(nh, 2 * WIN_R - 1, GRID_W, GRID_W)
    tiles = jnp.where(jnp.asarray(col_ok)[None, None], tiles, NEG_INF)
    dr = np.arange(WIN_R)[:, None] + np.arange(WIN_R)[None, :]
    t = tiles[:, dr]
    return t.transpose(0, 1, 3, 2, 4).reshape(nh, WIN_R, GRID_W, WIN_R * GRID_W)


def _na_kernel(q_ref, k_ref, v_ref, bias_ref, o_ref, *, rows, scale):
    nk = WIN_R * GRID_W
    nq = NA_ROWS_PER_STEP * GRID_W
    hd = q_ref.shape[-1]

    def group(g, carry):
        ks, vs, bias = [], [], []
        for i in range(NA_ROWS_PER_STEP):
            r = g * NA_ROWS_PER_STEP + i
            row_start = jnp.clip(r - WIN_R // 2, 0, rows - WIN_R)
            k0 = pl.multiple_of(row_start * GRID_W, GRID_W)
            ks.append(k_ref[0, pl.ds(k0, nk), :])
            vs.append(v_ref[0, pl.ds(k0, nk), :])
            bias.append(bias_ref[row_start - r + (WIN_R - 1)])
        q_rows = pl.ds(pl.multiple_of(g * nq, nq), nq)
        q = q_ref[0, q_rows, :].reshape(NA_ROWS_PER_STEP, GRID_W, hd)
        s = jnp.einsum("rqd,rkd->rqk", q, jnp.stack(ks), preferred_element_type=F32)
        s = s * scale + jnp.stack(bias)
        m = jnp.max(s, -1, keepdims=True)
        p = jnp.exp(s - m)
        l = jnp.sum(p, -1, keepdims=True)
        o = jnp.einsum("rqk,rkd->rqd", p.astype(BF16), jnp.stack(vs), preferred_element_type=F32) / l
        o_ref[0, q_rows, :] = o.reshape(nq, hd).astype(o_ref.dtype)
        return carry

    lax.fori_loop(0, rows // NA_ROWS_PER_STEP, group, 0, unroll=2)


def _neighbourhood_attention(proj3, b0, bsz, seq_len, bias, nh, hd):
    rows = seq_len // GRID_W
    assert rows % NA_ROWS_PER_STEP == 0 and rows >= WIN_R
    return pl.pallas_call(
        functools.partial(_na_kernel, rows=rows, scale=hd ** -0.5),
        grid=(nh, bsz),
        in_specs=[pl.BlockSpec((1, seq_len, hd), lambda h, b: (b0 + b, 0, h)),
                  pl.BlockSpec((1, seq_len, hd), lambda h, b: (b0 + b, 0, nh + h)),
                  pl.BlockSpec((1, seq_len, hd), lambda h, b: (b0 + b, 0, 2 * nh + h)),
                  pl.BlockSpec((None, WIN_R, GRID_W, WIN_R * GRID_W), lambda h, b: (h, 0, 0, 0))],
        out_specs=pl.BlockSpec((1, seq_len, hd), lambda h, b: (b, 0, h)),
        out_shape=jax.ShapeDtypeStruct((bsz, seq_len, nh * hd), BF16),
        compiler_params=_cparams(("parallel", "parallel")),
        name="neighbourhood_attention",
    )(proj3, proj3, proj3, bias)


def _conv3_kernel(x_ref, w_ref, b_ref, o_ref, *, seq_len, rb):
    w = w_ref[...]
    bias = b_ref[...]
    n_chunks = seq_len // rb

    def body(c, carry):
        r0 = pl.multiple_of(c * rb, rb)
        x = x_ref[0, pl.ds(r0, rb), :].astype(F32)
        row = lax.broadcasted_iota(jnp.int32, x.shape, 0)
        p0 = pl.multiple_of(jnp.maximum(r0 - 16, 0), 16)
        n0 = pl.multiple_of(jnp.minimum(r0 + rb, seq_len - 16), 16)
        prev_row = x_ref[0, pl.ds(p0, 16), :].astype(F32)[15:16]
        next_row = x_ref[0, pl.ds(n0, 16), :].astype(F32)[0:1]
        prev_row = jnp.where(c == 0, 0.0, prev_row)
        next_row = jnp.where(c == n_chunks - 1, 0.0, next_row)
        up = jnp.where(row == 0, prev_row, pltpu.roll(x, 1, 0))
        down = jnp.where(row == rb - 1, next_row, pltpu.roll(x, rb - 1, 0))
        y = up * w[0:1] + x * w[1:2] + down * w[2:3] + bias
        o_ref[0, pl.ds(r0, rb), :] = y.astype(o_ref.dtype)
        return carry

    lax.fori_loop(0, n_chunks, body, 0)


def _short_conv3(proj3, b0, bsz, seq_len, col0, w, b, cb=256, rb=512):
    width = w.shape[1]
    rb = min(rb, seq_len)
    c0 = col0 // cb
    return pl.pallas_call(
        functools.partial(_conv3_kernel, seq_len=seq_len, rb=rb),
        grid=(bsz, width // cb),
        in_specs=[pl.BlockSpec((1, seq_len, cb), lambda bi, c: (b0 + bi, 0, c0 + c)),
                  pl.BlockSpec((3, cb), lambda bi, c: (0, c)),
                  pl.BlockSpec((1, cb), lambda bi, c: (0, c))],
        out_specs=pl.BlockSpec((1, seq_len, cb), lambda bi, c: (bi, 0, c)),
        out_shape=jax.ShapeDtypeStruct((bsz, seq_len, width), BF16),
        compiler_params=_cparams(("parallel", "parallel")),
        name="hyena_short_conv",
    )(proj3, w, b.reshape(1, width))


def _filter_kernel(z_ref, w1_ref, b1_ref, fr_ref, w2_ref, b2_ref, w3_ref, b3_ref, dl_ref, o_ref, *,
                   seq_len, rb):
    i = pl.program_id(0)
    hi = lax.Precision.HIGHEST
    z = z_ref[...]
    h = jnp.sin(fr_ref[0:1] * (jnp.dot(z, w1_ref[...], precision=hi, preferred_element_type=F32)
                               + b1_ref[...]))
    h = jnp.sin(fr_ref[1:2] * (jnp.dot(h, w2_ref[...], precision=hi, preferred_element_type=F32)
                               + b2_ref[...]))
    ch = dl_ref.shape[-1]
    n = i * rb + lax.broadcasted_iota(jnp.int32, (rb, ch), 0)
    sign = jnp.where(n < seq_len, 1.0, jnp.where(n == seq_len, 0.0, -1.0))
    window = jnp.exp(-z[:, 0:1] * dl_ref[...]) * sign
    hb = h.astype(BF16)
    for o in range(w3_ref.shape[0]):
        f = jnp.dot(hb, w3_ref[o].astype(BF16), preferred_element_type=F32) + b3_ref[o]
        o_ref[:, o * ch:(o + 1) * ch] = (f * window).astype(o_ref.dtype)


def _hyena_filter_signal(seq_len, w1, b1, freq, w2, b2, w3, b3, order, ch, rb=512):
    n = 2 * seq_len
    emb, hid = w1.shape
    pad = LANES
    t = jnp.linspace(0.0, 1.0, seq_len, dtype=F32)[:, None]
    bands = (emb - 1) // 2
    fr = jnp.linspace(1e-4, bands - 1, bands, dtype=F32)[None, :]
    wpos = 2.0 * math.pi * jnp.arange(seq_len, dtype=F32)[:, None] / seq_len
    z = jnp.concatenate([t, jnp.cos(fr * wpos), -jnp.sin(fr * wpos)], axis=-1)
    pos = np.arange(n)
    src = np.clip(np.where(pos < seq_len, pos, n - pos), 0, seq_len - 1)
    z2 = jnp.pad(z[src], ((0, 0), (0, pad - emb)))
    w1p = jnp.pad(w1.astype(F32), ((0, pad - emb), (0, pad - hid)))
    b1p = jnp.pad(b1.astype(F32), (0, pad - hid)).reshape(1, pad)
    frp = jnp.pad(freq.astype(F32), ((0, 0), (0, pad - hid)), constant_values=1.0)
    w2p = jnp.pad(w2.astype(F32), ((0, pad - hid), (0, pad - hid)))
    b2p = jnp.pad(b2.astype(F32), (0, pad - hid)).reshape(1, pad)
    w3p = jnp.pad(w3.astype(F32), ((0, pad - hid), (0, 0))).reshape(pad, order, 2, ch).transpose(2, 1, 0, 3)
    b3p = b3.astype(F32).reshape(order, 2, 1, ch).transpose(1, 0, 2, 3)
    deltas = jnp.abs(jnp.linspace(HY_MIN_DECAY, HY_MAX_DECAY, ch, dtype=F32)).reshape(1, ch)
    nb = n // rb
    half = nb // 2
    full = lambda i: (0, 0)
    return pl.pallas_call(
        functools.partial(_filter_kernel, seq_len=seq_len, rb=rb),
        grid=(nb,),
        in_specs=[pl.BlockSpec((rb, pad), lambda i: (i, 0)),
                  pl.BlockSpec((pad, pad), full), pl.BlockSpec((1, pad), full),
                  pl.BlockSpec((2, pad), full),
                  pl.BlockSpec((pad, pad), full), pl.BlockSpec((1, pad), full),
                  pl.BlockSpec((None, order, pad, ch), lambda i: (i // half, 0, 0, 0)),
                  pl.BlockSpec((None, order, 1, ch), lambda i: (i // half, 0, 0, 0)),
                  pl.BlockSpec((1, ch), full)],
        out_specs=pl.BlockSpec((rb, order * ch), lambda i: (i, 0)),
        out_shape=jax.ShapeDtypeStruct((n, order * ch), BF16),
        compiler_params=_cparams(("parallel",)),
        name="hyena_filter_mlp",
    )(z2, w1p, b1p, frp, w2p, b2p, w3p, b3p, deltas)


def _dft_tables(seq_len):
    n = 2 * seq_len
    n2 = DFT_N2
    n1 = n // n2
    hk = n2 // 2
    kg = min(n1, DFT_GROUP)
    ng = n1 // kg
    pi = math.pi
    lows = np.arange(n1 // 2).reshape(ng, kg // 2)
    k1 = jnp.asarray(np.concatenate([lows, n1 - 1 - lows], axis=1).reshape(-1), jnp.int32)
    mirror = jnp.asarray(np.tile(np.arange(kg) >= kg // 2, ng))
    slab = jnp.arange(n1, dtype=jnp.int32)
    ph1 = ((2 * k1[:, None] + 1) * slab[None, :]) % (2 * n1)
    th1 = ph1.astype(F32) * (pi / n1)
    c1 = jnp.cos(th1).reshape(ng, kg, n1)
    s1 = jnp.sin(th1).reshape(ng, kg, n1)
    f1 = jnp.concatenate([c1[:, : kg // 2], -s1[:, : kg // 2]], axis=1)
    fb = jnp.concatenate([c1, -s1], axis=1).transpose(0, 2, 1)[:, : n1 // 2] * (2.0 / n)
    k = k1[:, None, None] + n1 * jnp.arange(hk, dtype=jnp.int32)[None, :, None]
    ph2 = ((2 * k + 1) * jnp.arange(n2, dtype=jnp.int32)[None, None, :]) % (2 * n)
    th2 = ph2.astype(F32) * (pi / n)
    c2, s2 = jnp.cos(th2), jnp.sin(th2)
    sgn = jnp.where(mirror, -1.0, 1.0)[:, None, None]
    g = jnp.concatenate([jnp.concatenate([c2, sgn * s2], axis=2),
                         jnp.concatenate([-s2, sgn * c2], axis=2)], axis=1)
    c2t, s2t = c2.transpose(0, 2, 1), s2.transpose(0, 2, 1)
    h = jnp.concatenate([jnp.concatenate([c2t, -s2t], axis=2),
                         jnp.concatenate([s2t, c2t], axis=2)], axis=1)
    nl = 2 if n1 <= DFT_GROUP else 1
    return dict(n1=n1, kg=kg, ng=ng, nl=nl, f1_full=f1.astype(BF16), f1_half=f1[:, :, : n1 // 2].astype(BF16),
                fb=fb.astype(BF16), g=g.astype(BF16), h=h.astype(BF16))


def _load_lanes(ref, rows):
    parts = [ref[s, rows, :] for s in range(ref.shape[0])]
    return parts[0] if len(parts) == 1 else jnp.concatenate(parts, axis=1)


def _store_lanes(ref, rows, val):
    for s in range(ref.shape[0]):
        ref[s, rows, :] = val[:, s * LANES:(s + 1) * LANES]


def _fill_slabs(x_ref, xs, slabs, pitch):
    def body(n1, carry):
        src = pl.ds(pl.multiple_of(n1 * DFT_N2, DFT_N2), DFT_N2)
        _store_lanes(xs, pl.ds(pl.multiple_of(n1 * pitch, SUBLANES), DFT_N2), x_ref[src, :].astype(F32))
        return carry
    lax.fori_loop(0, slabs, body, 0, unroll=4)


def _slab_stage(xs, a_s, f1, slabs, p_x, p_a):
    rows = f1.shape[0]

    def body(i, carry):
        xn = _load_lanes(xs, pl.ds(i, slabs, stride=p_x)).astype(BF16)
        _store_lanes(a_s, pl.ds(pl.multiple_of(i * p_a, SUBLANES), rows),
                     jnp.dot(f1, xn, preferred_element_type=F32))
        return carry
    lax.fori_loop(0, DFT_N2, body, 0, unroll=128)


def _load_slab_freq(a_s, low, kg, p_a):
    are = _load_lanes(a_s, pl.ds(low, DFT_N2, stride=p_a))
    aim = _load_lanes(a_s, pl.ds(kg // 2 + low, DFT_N2, stride=p_a))
    return jnp.concatenate([are, aim], axis=0).astype(BF16)


def _for_each_slot(kg, body):
    for first in (0, kg // 2):
        def step(low, carry, first=first):
            body(first + low, low)
            return carry
        lax.fori_loop(0, kg // 2, step, 0, unroll=16)


def _filter_spec_kernel(x_ref, f1_ref, g_ref, o_ref, xs, a_s, *, slabs, kg):
    grp = pl.program_id(1)
    p_x, p_a = _pitch(DFT_N2), _pitch(kg)

    @pl.when(grp == 0)
    def _():
        _fill_slabs(x_ref, xs, slabs, p_x)

    _slab_stage(xs, a_s, f1_ref[0], slabs, p_x, p_a)

    def body(j, low):
        o_ref[j] = jnp.dot(g_ref[j], _load_slab_freq(a_s, low, kg, p_a), preferred_element_type=F32)
    _for_each_slot(kg, body)


def _filter_spectrum(f, tables):
    n, cols = f.shape
    n1, kg, ng, nl = tables["n1"], tables["kg"], tables["ng"], tables["nl"]
    p_x, p_a = _pitch(DFT_N2), _pitch(kg)
    wl = nl * LANES
    return pl.pallas_call(
        functools.partial(_filter_spec_kernel, slabs=n1, kg=kg),
        grid=(cols // wl, ng),
        in_specs=[pl.BlockSpec((n, wl), lambda c, g: (0, c)),
                  pl.BlockSpec((1, kg, n1), lambda c, g: (g, 0, 0)),
                  pl.BlockSpec((kg, DFT_N2, 2 * DFT_N2), lambda c, g: (g, 0, 0))],
        out_specs=pl.BlockSpec((kg, DFT_N2, wl), lambda c, g: (g, 0, c)),
        out_shape=jax.ShapeDtypeStruct((n1, DFT_N2, cols), F32),
        scratch_shapes=[pltpu.VMEM((nl, n1 * p_x, LANES), F32),
                        pltpu.VMEM((nl, DFT_N2 * p_a, LANES), F32)],
        compiler_params=_cparams(("parallel", "arbitrary")),
        name="hyena_filter_spectrum",
    )(f, tables["f1_full"], tables["g"])


def _long_conv_kernel(x_ref, gate_ref, f1_ref, g_ref, h_ref, kf_ref, fb_ref, skip_ref, o_ref,
                      xs, a_s, z_s, y_s, *, slabs, kg, ng):
    grp = pl.program_id(2)
    n2 = DFT_N2
    hk = n2 // 2
    p_x, p_a, p_z, p_y = _pitch(n2), _pitch(kg), _pitch(2 * n2), _pitch(slabs)

    @pl.when(grp == 0)
    def _():
        _fill_slabs(x_ref.at[0], xs, slabs, p_x)
        y_s[...] = jnp.zeros_like(y_s)

    _slab_stage(xs, a_s, f1_ref[0], slabs, p_x, p_a)

    def freq_body(j, low):
        x = jnp.dot(g_ref[j], _load_slab_freq(a_s, low, kg, p_a), preferred_element_type=F32)
        kf = kf_ref[j]
        xr, xi = x[:hk], x[hk:]
        kr, ki = kf[:hk], kf[hk:]
        y = jnp.concatenate([xr * kr - xi * ki, xr * ki + xi * kr], axis=0)
        _store_lanes(z_s, pl.ds(pl.multiple_of(j * p_z, SUBLANES), n2), y)
    _for_each_slot(kg, freq_body)

    def inv_freq_body(j, carry):
        y = _load_lanes(z_s, pl.ds(pl.multiple_of(j * p_z, SUBLANES), n2)).astype(BF16)
        _store_lanes(z_s, pl.ds(pl.multiple_of(j * p_z, SUBLANES), 2 * n2),
                     jnp.dot(h_ref[j], y, preferred_element_type=F32))
        return carry
    lax.fori_loop(0, kg, inv_freq_body, 0, unroll=16)

    fb = fb_ref[0]

    def inv_body(t, carry):
        zre = _load_lanes(z_s, pl.ds(t, kg, stride=p_z))
        zim = _load_lanes(z_s, pl.ds(n2 + t, kg, stride=p_z))
        zz = jnp.concatenate([zre, zim], axis=0).astype(BF16)
        dst = pl.ds(pl.multiple_of(t * p_y, SUBLANES), slabs)
        _store_lanes(y_s, dst, _load_lanes(y_s, dst) + jnp.dot(fb, zz, preferred_element_type=F32))
        return carry
    lax.fori_loop(0, n2, inv_body, 0, unroll=128)

    @pl.when(grp == ng - 1)
    def _():
        skip = skip_ref[...]

        def out_body(t1, carry):
            y = _load_lanes(y_s, pl.ds(t1, n2, stride=p_y))
            u = _load_lanes(xs, pl.ds(pl.multiple_of(t1 * p_x, SUBLANES), n2))
            rows = pl.ds(pl.multiple_of(t1 * n2, n2), n2)
            gate = gate_ref[0, rows, :].astype(F32)
            o_ref[0, rows, :] = ((y + skip * u) * gate).astype(o_ref.dtype)
            return carry
        lax.fori_loop(0, slabs, out_body, 0, unroll=4)


def _long_conv_gate(u, u_blk0, gate, gate_blk0, tables, kf, kf_blk0, skip, ch):
    bsz, seq_len, _ = u.shape
    n1, kg, ng, nl = tables["n1"], tables["kg"], tables["ng"], tables["nl"]
    slabs = n1 // 2
    p_x, p_a, p_z, p_y = _pitch(DFT_N2), _pitch(kg), _pitch(2 * DFT_N2), _pitch(slabs)
    wl = nl * LANES
    return pl.pallas_call(
        functools.partial(_long_conv_kernel, slabs=slabs, kg=kg, ng=ng),
        grid=(bsz, ch // wl, ng),
        in_specs=[pl.BlockSpec((1, seq_len, wl), lambda b, c, g: (b, 0, u_blk0 + c)),
                  pl.BlockSpec((1, seq_len, wl), lambda b, c, g: (b, 0, gate_blk0 + c)),
                  pl.BlockSpec((1, kg, slabs), lambda b, c, g: (g, 0, 0)),
                  pl.BlockSpec((kg, DFT_N2, 2 * DFT_N2), lambda b, c, g: (g, 0, 0)),
                  pl.BlockSpec((kg, 2 * DFT_N2, DFT_N2), lambda b, c, g: (g, 0, 0)),
                  pl.BlockSpec((kg, DFT_N2, wl), lambda b, c, g: (g, 0, kf_blk0 + c)),
                  pl.BlockSpec((1, slabs, 2 * kg), lambda b, c, g: (g, 0, 0)),
                  pl.BlockSpec((1, wl), lambda b, c, g: (0, c))],
        out_specs=pl.BlockSpec((1, seq_len, wl), lambda b, c, g: (b, 0, c)),
        out_shape=jax.ShapeDtypeStruct((bsz, seq_len, ch), BF16),
        scratch_shapes=[pltpu.VMEM((nl, slabs * p_x, LANES), F32),
                        pltpu.VMEM((nl, DFT_N2 * p_a, LANES), F32),
                        pltpu.VMEM((nl, kg * p_z, LANES), F32),
                        pltpu.VMEM((nl, DFT_N2 * p_y, LANES), F32)],
        compiler_params=_cparams(("parallel", "parallel", "arbitrary")),
        name="hyena_long_conv",
    )(u, gate, tables["f1_half"], tables["g"], tables["h"], kf, tables["fb"],
      skip.astype(F32).reshape(1, ch))


def _hyena(hy, tables, kf, skip, order, ch):
    nblk = ch // (tables["nl"] * LANES)
    z, z_blk0 = hy, 0
    for o in range(order):
        z = _long_conv_gate(z, z_blk0, hy, (o + 1) * nblk, tables, kf, o * nblk, skip[o], ch)
    return z


def _rope(x, cos, sin):
    half = x.shape[-1] // 2
    x1, x2 = x[:, :half], x[:, half:]
    return jnp.concatenate([x1 * cos - x2 * sin, x1 * sin + x2 * cos], axis=-1)


def _ret_kernel(lg_ref, q_ref, k_ref, v_ref, g_ref, cos_ref, sin_ref, o_ref,
                sb_all, kr_all, kz_all, sf_ref, sb_ref, *, n_super, chunks_per_super, k_scale):
    h = pl.program_id(1)
    s = pl.program_id(2)
    cc = RET_CHUNK
    dh = q_ref.shape[-1]
    lgf = lg_ref[0, h]
    lgb = lg_ref[1, h]
    row = lax.broadcasted_iota(jnp.int32, (cc, dh), 0).astype(F32)
    zeta_f = jnp.exp(lgf * (cc - 1.0 - row))

    @pl.when(s < n_super)
    def _backward_sweep():
        @pl.when(s == 0)
        def _():
            sb_ref[...] = jnp.zeros_like(sb_ref)

        sup = n_super - 1 - s
        zeta_b = jnp.exp(lgb * row)
        chunk_decay = jnp.exp(lgb * jnp.full((1, dh), float(cc), F32))

        def body(t, carry):
            c = chunks_per_super - 1 - t
            c0 = pl.multiple_of(c * cc, cc)
            n = sup * chunks_per_super + c
            sb_all[n] = sb_ref[...].astype(BF16)
            k = _rope(k_ref[0, pl.ds(c0, cc), :].astype(F32), cos_ref[pl.ds(c0, cc), :],
                      sin_ref[pl.ds(c0, cc), :]) * k_scale
            kr_all[n] = k.astype(BF16)
            kz_all[n] = (k * zeta_f).astype(BF16)
            kz = (k * zeta_b).astype(BF16)
            v = v_ref[0, pl.ds(c0, cc), :]
            upd = lax.dot_general(kz, v, (((0,), (0,)), ((), ())), preferred_element_type=F32)
            sb_ref[...] = sb_ref[...] * chunk_decay + upd
            return carry

        lax.fori_loop(0, chunks_per_super, body, 0, unroll=4)

    @pl.when(s >= n_super)
    def _forward_sweep():
        @pl.when(s == n_super)
        def _():
            sf_ref[...] = jnp.zeros_like(sf_ref)

        sup = s - n_super
        col = lax.broadcasted_iota(jnp.int32, (cc, cc), 1).astype(F32)
        rowc = lax.broadcasted_iota(jnp.int32, (cc, cc), 0).astype(F32)
        diff = rowc - col
        inner_decay = jnp.where(diff >= 0, jnp.exp(lgf * jnp.maximum(diff, 0.0)),
                                jnp.exp(lgb * jnp.maximum(-diff, 0.0)))
        xi_f = jnp.exp(lgf * (row + 1.0))
        xi_b = jnp.exp(lgb * (cc - row))
        chunk_decay = jnp.exp(lgf * jnp.full((1, dh), float(cc), F32))

        def body(c, carry):
            c0 = pl.multiple_of(c * cc, cc)
            n = sup * chunks_per_super + c
            q = _rope(q_ref[0, pl.ds(c0, cc), :].astype(F32), cos_ref[pl.ds(c0, cc), :],
                      sin_ref[pl.ds(c0, cc), :]).astype(BF16)
            v = v_ref[0, pl.ds(c0, cc), :]
            sc = lax.dot_general(q, kr_all[n], (((1,), (1,)), ((), ())),
                                 preferred_element_type=F32) * inner_decay
            ret = jnp.dot(sc.astype(BF16), v, preferred_element_type=F32)
            ret = ret + xi_f * jnp.dot(q, sf_ref[...].astype(BF16), preferred_element_type=F32)
            ret = ret + xi_b * jnp.dot(q, sb_all[n], preferred_element_type=F32)
            upd = lax.dot_general(kz_all[n], v, (((0,), (0,)), ((), ())), preferred_element_type=F32)
            sf_ref[...] = sf_ref[...] * chunk_decay + upd
            ret = ret * lax.rsqrt(jnp.mean(ret * ret, -1, keepdims=True) + RMS_EPS)
            gate = g_ref[0, pl.ds(c0, cc), :].astype(F32)
            o_ref[0, pl.ds(c0, cc), :] = (ret * (gate / (1.0 + jnp.exp(-gate)))).astype(o_ref.dtype)
            return carry

        lax.fori_loop(0, chunks_per_super, body, 0, unroll=4)


def _retention(proj3, b0, bsz, seq_len, col0, nh, dh, log_gamma, cos, sin):
    sup_len = min(RET_SUPER, seq_len)
    n_super = seq_len // sup_len
    cps = sup_len // RET_CHUNK
    cb0 = col0 // dh

    def fwd_idx(s):
        return jnp.maximum(s - n_super, 0)

    def kv_idx(s):
        return jnp.where(s < n_super, n_super - 1 - s, s - n_super)

    def k_idx(s):
        return jnp.maximum(n_super - 1 - s, 0)

    n_chunks = seq_len // RET_CHUNK
    return pl.pallas_call(
        functools.partial(_ret_kernel, n_super=n_super, chunks_per_super=cps, k_scale=dh ** -0.5),
        grid=(bsz, nh, 2 * n_super),
        in_specs=[pl.BlockSpec(memory_space=pltpu.SMEM),
                  pl.BlockSpec((1, sup_len, dh), lambda b, h, s: (b0 + b, fwd_idx(s), cb0 + h)),
                  pl.BlockSpec((1, sup_len, dh), lambda b, h, s: (b0 + b, k_idx(s), cb0 + nh + h)),
                  pl.BlockSpec((1, sup_len, dh), lambda b, h, s: (b0 + b, kv_idx(s), cb0 + 2 * nh + h)),
                  pl.BlockSpec((1, sup_len, dh), lambda b, h, s: (b0 + b, fwd_idx(s), cb0 + 3 * nh + h)),
                  pl.BlockSpec((sup_len, dh // 2), lambda b, h, s: (kv_idx(s), 0)),
                  pl.BlockSpec((sup_len, dh // 2), lambda b, h, s: (kv_idx(s), 0))],
        out_specs=pl.BlockSpec((1, sup_len, dh), lambda b, h, s: (b, fwd_idx(s), h)),
        out_shape=jax.ShapeDtypeStruct((bsz, seq_len, nh * dh), BF16),
        scratch_shapes=[pltpu.VMEM((n_chunks, dh, dh), BF16),
                        pltpu.VMEM((n_chunks, RET_CHUNK, dh), BF16),
                        pltpu.VMEM((n_chunks, RET_CHUNK, dh), BF16),
                        pltpu.VMEM((dh, dh), F32),
                        pltpu.VMEM((dh, dh), F32)],
        compiler_params=_cparams(("parallel", "parallel", "arbitrary")),
        name="retention",
    )(log_gamma, proj3, proj3, proj3, proj3, cos, sin)


def _assemble_kernel(a1_ref, a2_ref, z1_ref, z2_ref, c1_ref, c2_ref, ga_ref, gb_ref, o_ref, *, nb1):
    first = pl.program_id(0) < nb1
    wa = a1_ref.shape[-1]
    wb = z1_ref.shape[-1]
    a = jnp.where(first, a1_ref[...], a2_ref[...]).astype(F32)
    a = a * lax.rsqrt(jnp.mean(a * a, -1, keepdims=True) + RMS_EPS) * ga_ref[...]
    z = jnp.where(first, z1_ref[...], z2_ref[...]).astype(F32)
    z = z * lax.rsqrt(jnp.mean(z * z, -1, keepdims=True) + RMS_EPS) * gb_ref[...]
    o_ref[:, :wa] = a.astype(o_ref.dtype)
    o_ref[:, wa:wa + wb] = z.astype(o_ref.dtype)
    o_ref[:, wa + wb:] = jnp.where(first, c1_ref[...], c2_ref[...])


def _assemble(a_parts, z_parts, c_parts, gain_a, gain_b, bm=1024):
    m1, wa = a_parts[0].shape
    m = m1 + a_parts[1].shape[0]
    assert m1 % bm == 0 and m % bm == 0
    wb, wc = z_parts[0].shape[1], c_parts[0].shape[1]
    nb1 = m1 // bm
    part_specs = []
    for w in (wa, wb, wc):
        part_specs += [pl.BlockSpec((bm, w), _first_rows(nb1)), pl.BlockSpec((bm, w), _second_rows(nb1))]
    return pl.pallas_call(
        functools.partial(_assemble_kernel, nb1=nb1),
        grid=(m // bm,),
        in_specs=part_specs + [pl.BlockSpec((1, wa), lambda i: (0, 0)),
                               pl.BlockSpec((1, wb), lambda i: (0, 0))],
        out_specs=pl.BlockSpec((bm, wa + wb + wc), lambda i: (i, 0)),
        out_shape=jax.ShapeDtypeStruct((m, wa + wb + wc), BF16),
        compiler_params=_cparams(("arbitrary",)),
        name="assemble_mixers",
    )(*a_parts, *z_parts, *c_parts, gain_a.astype(F32).reshape(1, wa), gain_b.astype(F32).reshape(1, wb))


def _token_mixers(proj, requests, p, l, dims):
    nh_a, hd_a, ch, order, nh_r, dh_r = dims
    in_cols = proj.shape[1]
    total = proj.shape[0]
    wa = nh_a * hd_a
    hy0 = 3 * wa
    ret0 = hy0 + (order + 1) * ch
    log_gamma = jnp.log1p(-jnp.exp2(-p["ret_decay_exp"][l].astype(F32)))
    a_parts, z_parts, c_parts = [], [], []
    bias = _na_bias_tables(p["na_rpb"][l])
    for (bsz, seq_len, tok0) in requests:
        assert tok0 % seq_len == 0 and total % seq_len == 0
        proj3 = proj.reshape(total // seq_len, seq_len, in_cols)
        b0 = tok0 // seq_len
        a = _neighbourhood_attention(proj3, b0, bsz, seq_len, bias, nh_a, hd_a)
        a_parts.append(a.reshape(bsz * seq_len, wa))
        tables = _dft_tables(seq_len)
        f = _hyena_filter_signal(seq_len, p["hy_f_w1"][l], p["hy_f_b1"][l], p["hy_f_freq"][l],
                                 p["hy_f_w2"][l], p["hy_f_b2"][l], p["hy_f_w3"][l], p["hy_f_b3"][l],
                                 order, ch)
        kf = _filter_spectrum(f, tables)
        hy = _short_conv3(proj3, b0, bsz, seq_len, hy0, p["hy_conv_w"][l].astype(F32),
                          p["hy_conv_b"][l].astype(F32))
        z = _hyena(hy, tables, kf, p["hy_skip"][l], order, ch)
        z_parts.append(z.reshape(bsz * seq_len, ch))
        half = dh_r // 2
        inv_freq = 1.0 / (ROPE_BASE ** jnp.linspace(0.0, 1.0, half, dtype=F32))
        ang = jnp.arange(seq_len, dtype=F32)[:, None] * inv_freq[None, :]
        c = _retention(proj3, b0, bsz, seq_len, ret0, nh_r, dh_r, log_gamma, jnp.cos(ang), jnp.sin(ang))
        c_parts.append(c.reshape(bsz * seq_len, nh_r * dh_r))
    return _assemble(a_parts, z_parts, c_parts, p["grp_gain_a"][l], p["grp_gain_b"][l])


def _pick(n, candidates):
    for c in candidates:
        if n % c == 0:
            return c
    raise ValueError(f"no block size for {n}")


def kernel(x_prompt, x_sample, ln_in_g, ln_in_b, w_in, na_rpb, hy_conv_w, hy_conv_b, hy_f_w1, hy_f_b1,
           hy_f_freq, hy_f_w2, hy_f_b2, hy_f_w3, hy_f_b3, hy_skip, ret_decay_exp, grp_gain_a, grp_gain_b,
           w_out, ln1_g, ln1_b, w_ffn_in, w_ffn_out, ln2_g, ln2_b):
    p = dict(na_rpb=na_rpb, hy_conv_w=hy_conv_w, hy_conv_b=hy_conv_b, hy_f_w1=hy_f_w1, hy_f_b1=hy_f_b1,
             hy_f_freq=hy_f_freq, hy_f_w2=hy_f_w2, hy_f_b2=hy_f_b2, hy_f_w3=hy_f_w3, hy_f_b3=hy_f_b3,
             hy_skip=hy_skip, ret_decay_exp=ret_decay_exp, grp_gain_a=grp_gain_a, grp_gain_b=grp_gain_b)
    depth, d_model, in_cols = w_in.shape
    nh_a = na_rpb.shape[1]
    wa = grp_gain_a.shape[1]
    ch = grp_gain_b.shape[1]
    order = hy_skip.shape[1]
    nh_r = ret_decay_exp.shape[2]
    wr = (in_cols - 3 * wa - (order + 1) * ch) // 4
    dims = (nh_a, wa // nh_a, ch, order, nh_r, wr // nh_r)
    d_ff = w_ffn_out.shape[1]
    alpha = (2.0 * depth) ** 0.25

    bp, lp, _ = x_prompt.shape
    bs, ls, _ = x_sample.shape
    requests = [(bp, lp, 0), (bs, ls, bp * lp)]
    m1 = bp * lp
    m = m1 + bs * ls
    bm = _pick(m, (1024, 512, 256))
    bm_ln = 512
    assert m1 % bm_ln == 0 and m % bm_ln == 0

    xf, xb = _layernorm_join(x_prompt.reshape(m1, d_model), x_sample.reshape(bs * ls, d_model),
                             ln_in_g.astype(F32), ln_in_b.astype(F32), bm_ln // 2)
    w_in_b = w_in[0:1].astype(BF16)
    resid = (xf,)
    for l in range(depth):
        bn_out = _pick(d_model, (1024, 512, 256) if len(resid) == 1 else (512, 256))
        cast = [(w_out, l, bn_out), (w_ffn_in, l, _pick(d_ff, (512, 256, 128))),
                (w_ffn_out, l, _pick(d_model, (512, 256)))]
        if l + 1 < depth:
            cast.append((w_in, l + 1, in_cols))
        proj, cast_w = _matmul(xb, w_in_b, 0, bm, _pick(in_cols, (1024, 512, 256)), cast)
        w_out_b, w_ffn_in_b, w_ffn_out_b = cast_w[:3]
        if l + 1 < depth:
            w_in_b = cast_w[3]
        mixed = _token_mixers(proj, requests, p, l, dims)
        y = _matmul_residual(mixed, w_out_b, resid, alpha, bm)
        g1, b1 = ln1_g[l].astype(F32), ln1_b[l].astype(F32)
        xb, stats = _layernorm_stats(y, g1, b1, bm_ln)
        hmid = _matmul_swiglu(xb, w_ffn_in_b, _pick(m, (2048, 1024, 512, 256)))
        y = _matmul_residual(hmid, w_ffn_out_b, (y, stats, g1, b1), alpha, _pick(m, (512, 256)))
        if l + 1 < depth:
            g2, b2 = ln2_g[l].astype(F32), ln2_b[l].astype(F32)
            xb, stats = _layernorm_stats(y, g2, b2, bm_ln)
            resid = (y, stats, g2, b2)
    o1, o2 = _layernorm_split(y, ln2_g[depth - 1].astype(F32), ln2_b[depth - 1].astype(F32), m1, bm_ln // 2)
    return (o1.reshape(bp, lp, d_model), o2.reshape(bs, ls, d_model))
```

```python
import functools
import math

import numpy as np
import jax
import jax.numpy as jnp
from jax import lax
from jax.experimental import pallas as pl
from jax.experimental.pallas import tpu as pltpu

F32 = jnp.float32
BF16 = jnp.bfloat16

GRID_W = 64
WIN_R = 8
WIN_C = 16
ROPE_BASE = 10000.0
LN_EPS = 1e-5
RMS_EPS = 1e-6
NEG_INF = -1e30
HY_TARGET = 1e-2
HY_FAST_DECAY = 0.3
HY_SLOW_DECAY = 1.5
HY_MIN_DECAY = math.log(HY_TARGET) / HY_SLOW_DECAY
HY_MAX_DECAY = math.log(HY_TARGET) / HY_FAST_DECAY

V7X_VMEM_LIMIT = 56 * 1024 * 1024
LANES = 128
SUBLANES = 8
DFT_N2 = 128
DFT_GROUP = 32
NA_ROWS_PER_STEP = 16
RET_CHUNK = 256
RET_SUPER = 2048


def _cparams(sem, vmem=V7X_VMEM_LIMIT):
    return pltpu.CompilerParams(dimension_semantics=sem, vmem_limit_bytes=vmem)


def _pitch(rows):
    p = -(-rows // SUBLANES)
    return SUBLANES * (p + 1 - p % 2)


def _ln(x, g, b):
    mu = jnp.mean(x, -1, keepdims=True)
    xc = x - mu
    var = jnp.mean(xc * xc, -1, keepdims=True)
    return xc * lax.rsqrt(var + LN_EPS) * g + b


def _ln_stats_kernel(x_ref, g_ref, b_ref, ob_ref, st_ref):
    x = x_ref[...]
    mu = jnp.mean(x, -1, keepdims=True)
    xc = x - mu
    rstd = lax.rsqrt(jnp.mean(xc * xc, -1, keepdims=True) + LN_EPS)
    ob_ref[...] = (xc * rstd * g_ref[...] + b_ref[...]).astype(BF16)
    st_ref[:, :LANES] = jnp.broadcast_to(mu, (x.shape[0], LANES))
    st_ref[:, LANES:] = jnp.broadcast_to(rstd, (x.shape[0], LANES))


def _ln_join_kernel(x1_ref, x2_ref, g_ref, b_ref, of_ref, ob_ref, *, nb1):
    x = jnp.where(pl.program_id(0) < nb1, x1_ref[...], x2_ref[...])
    y = _ln(x, g_ref[...], b_ref[...])
    of_ref[...] = y
    ob_ref[...] = y.astype(BF16)


def _ln_split_kernel(x_ref, g_ref, b_ref, o1_ref, o2_ref, *, nb1):
    y = _ln(x_ref[...], g_ref[...], b_ref[...])

    @pl.when(pl.program_id(0) < nb1)
    def _():
        o1_ref[...] = y

    @pl.when(pl.program_id(0) >= nb1)
    def _():
        o2_ref[...] = y


def _first_rows(nb1):
    return lambda i: (jnp.minimum(i, nb1 - 1), 0)


def _second_rows(nb1):
    return lambda i: (jnp.maximum(i - nb1, 0), 0)


def _layernorm_join(x1, x2, g, b, bm=256):
    m1, d = x1.shape
    m = m1 + x2.shape[0]
    nb1 = m1 // bm
    return pl.pallas_call(
        functools.partial(_ln_join_kernel, nb1=nb1),
        grid=(m // bm,),
        in_specs=[pl.BlockSpec((bm, d), _first_rows(nb1)),
                  pl.BlockSpec((bm, d), _second_rows(nb1)),
                  pl.BlockSpec((1, d), lambda i: (0, 0)),
                  pl.BlockSpec((1, d), lambda i: (0, 0))],
        out_specs=[pl.BlockSpec((bm, d), lambda i: (i, 0)),
                   pl.BlockSpec((bm, d), lambda i: (i, 0))],
        out_shape=[jax.ShapeDtypeStruct((m, d), F32), jax.ShapeDtypeStruct((m, d), BF16)],
        compiler_params=_cparams(("arbitrary",)),
        name="layernorm_join",
    )(x1, x2, g.reshape(1, d), b.reshape(1, d))


def _layernorm_split(x, g, b, m1, bm=256):
    m, d = x.shape
    nb1 = m1 // bm
    return pl.pallas_call(
        functools.partial(_ln_split_kernel, nb1=nb1),
        grid=(m // bm,),
        in_specs=[pl.BlockSpec((bm, d), lambda i: (i, 0)),
                  pl.BlockSpec((1, d), lambda i: (0, 0)),
                  pl.BlockSpec((1, d), lambda i: (0, 0))],
        out_specs=[pl.BlockSpec((bm, d), _first_rows(nb1)),
                   pl.BlockSpec((bm, d), _second_rows(nb1))],
        out_shape=[jax.ShapeDtypeStruct((m1, d), F32), jax.ShapeDtypeStruct((m - m1, d), F32)],
        compiler_params=_cparams(("arbitrary",)),
        name="layernorm_split",
    )(x, g.reshape(1, d), b.reshape(1, d))


def _layernorm_stats(x, g, b, bm=256):
    m, d = x.shape
    return pl.pallas_call(
        _ln_stats_kernel,
        grid=(m // bm,),
        in_specs=[pl.BlockSpec((bm, d), lambda i: (i, 0)),
                  pl.BlockSpec((1, d), lambda i: (0, 0)),
                  pl.BlockSpec((1, d), lambda i: (0, 0))],
        out_specs=[pl.BlockSpec((bm, d), lambda i: (i, 0)),
                   pl.BlockSpec((bm, 2 * LANES), lambda i: (i, 0))],
        out_shape=[jax.ShapeDtypeStruct((m, d), BF16), jax.ShapeDtypeStruct((m, 2 * LANES), F32)],
        compiler_params=_cparams(("parallel",)),
        name="layernorm_stats",
    )(x, g.reshape(1, d), b.reshape(1, d))


def _mm_res_kernel(x_ref, w_ref, r_ref, o_ref, *, alpha):
    o_ref[...] = alpha * r_ref[...] + jnp.dot(x_ref[...], w_ref[...], preferred_element_type=F32)


def _mm_res_ln_kernel(x_ref, w_ref, y_ref, st_ref, g_ref, b_ref, o_ref, *, alpha):
    acc = jnp.dot(x_ref[...], w_ref[...], preferred_element_type=F32)
    mu = st_ref[:, :LANES]
    rstd = st_ref[:, LANES:]
    for c in range(y_ref.shape[1] // LANES):
        cols = slice(c * LANES, (c + 1) * LANES)
        r = (y_ref[:, cols] - mu) * rstd * g_ref[:, cols] + b_ref[:, cols]
        o_ref[:, cols] = alpha * r + acc[:, cols]


def _mm_swiglu_kernel(x_ref, wg_ref, wu_ref, o_ref):
    x = x_ref[...]
    g = jnp.dot(x, wg_ref[...], preferred_element_type=F32)
    u = jnp.dot(x, wu_ref[...], preferred_element_type=F32)
    o_ref[...] = (g / (1.0 + jnp.exp(-g)) * u).astype(o_ref.dtype)


def _mm_cast_kernel(x_ref, w_ref, *refs):
    n_cast = (len(refs) - 1) // 2
    o_ref = refs[n_cast]
    o_ref[...] = jnp.dot(x_ref[...], w_ref[...], preferred_element_type=F32).astype(o_ref.dtype)
    for src, dst in zip(refs[:n_cast], refs[n_cast + 1:]):
        tile = dst.shape[-1]
        for t in range(dst.shape[0]):
            dst[t] = src[:, t * tile:(t + 1) * tile].astype(dst.dtype)


def _cast_rows(rows, n_steps):
    rb = 16
    while rows % rb or rows // rb > n_steps:
        rb += 16
        assert rb <= rows
    return rb


def _matmul(x, w, layer, bm, bn, cast=()):
    m, k = x.shape
    n = w.shape[2]
    nj = n // bn
    n_steps = (m // bm) * nj
    cast_in, cast_out, cast_shapes, cast_args = [], [], [], []
    for stack, lyr, tile in cast:
        _, rows, cols = stack.shape
        rb = _cast_rows(rows, n_steps)
        last = rows // rb - 1
        cast_in.append(pl.BlockSpec((None, rb, cols),
                                    lambda i, j, lyr=lyr, last=last: (lyr, jnp.minimum(i * nj + j, last), 0)))
        cast_out.append(pl.BlockSpec((cols // tile, rb, tile),
                                     lambda i, j, last=last: (0, jnp.minimum(i * nj + j, last), 0)))
        cast_shapes.append(jax.ShapeDtypeStruct((cols // tile, rows, tile), BF16))
        cast_args.append(stack)
    outs = pl.pallas_call(
        _mm_cast_kernel,
        grid=(m // bm, nj),
        in_specs=[pl.BlockSpec((bm, k), lambda i, j: (i, 0)),
                  pl.BlockSpec((None, k, bn), lambda i, j: (layer, 0, j))] + cast_in,
        out_specs=[pl.BlockSpec((bm, bn), lambda i, j: (i, j))] + cast_out,
        out_shape=[jax.ShapeDtypeStruct((m, n), BF16)] + cast_shapes,
        compiler_params=_cparams(("arbitrary", "arbitrary")),
        name="matmul",
    )(x, w, *cast_args)
    return outs[0], outs[1:]


def _matmul_residual(x, w, resid, alpha, bm):
    m, k = x.shape
    bn = w.shape[2]
    n = w.shape[0] * bn
    tile = pl.BlockSpec((bm, bn), lambda i, j: (i, j))
    if len(resid) == 1:
        body, resid_specs = _mm_res_kernel, [tile]
    else:
        body = _mm_res_ln_kernel
        resid = (resid[0], resid[1], resid[2].reshape(1, n), resid[3].reshape(1, n))
        resid_specs = [tile, pl.BlockSpec((bm, 2 * LANES), lambda i, j: (i, 0)),
                       pl.BlockSpec((1, bn), lambda i, j: (0, j)), pl.BlockSpec((1, bn), lambda i, j: (0, j))]
    return pl.pallas_call(
        functools.partial(body, alpha=alpha),
        grid=(m // bm, n // bn),
        in_specs=[pl.BlockSpec((bm, k), lambda i, j: (i, 0)),
                  pl.BlockSpec((None, k, bn), lambda i, j: (j, 0, 0))] + resid_specs,
        out_specs=tile,
        out_shape=jax.ShapeDtypeStruct((m, n), F32),
        compiler_params=_cparams(("parallel", "arbitrary")),
        name="matmul_residual",
    )(x, w, *resid)


def _matmul_swiglu(x, w, bm):
    m, k = x.shape
    bn = w.shape[2]
    nb = w.shape[0] // 2
    f = nb * bn
    return pl.pallas_call(
        _mm_swiglu_kernel,
        grid=(m // bm, nb),
        in_specs=[pl.BlockSpec((bm, k), lambda i, j: (i, 0)),
                  pl.BlockSpec((None, k, bn), lambda i, j: (j, 0, 0)),
                  pl.BlockSpec((None, k, bn), lambda i, j: (j + nb, 0, 0))],
        out_specs=pl.BlockSpec((bm, bn), lambda i, j: (i, j)),
        out_shape=jax.ShapeDtypeStruct((m, f), BF16),
        compiler_params=_cparams(("parallel", "arbitrary")),
        name="matmul_swiglu",
    )(x, w, w)


def _na_bias_tables(rpb):
    nh = rpb.shape[0]
    qc = np.arange(GRID_W)[:, None]
    kc = np.arange(GRID_W)[None, :]
    win_start = np.clip(qc - WIN_C // 2, 0, GRID_W - WIN_C)
    col_ok = (kc >= win_start) & (kc < win_start + WIN_C)
    dc_idx = np.clip(kc - qc + (WIN_C - 1), 0, 2 * WIN_C - 2)
    onehot = (dc_idx.reshape(-1)[None, :] == np.arange(2 * WIN_C - 1)[:, None]).astype(np.float32)
    tiles = jnp.einsum("hdc,cq->hdq", rpb.astype(F32), jnp.asarray(onehot),
                       precision=lax.Precision.HIGHEST).reshape(nh, 2 * WIN_R - 1, GRID_W, GRID_W)
    tiles = jnp.where(jnp.asarray(col_ok)[None, None], tiles, NEG_INF)
    dr = np.arange(WIN_R)[:, None] + np.arange(WIN_R)[None, :]
    t = tiles[:, dr]
    return t.transpose(0, 1, 3, 2, 4).reshape(nh, WIN_R, GRID_W, WIN_R * GRID_W)


def _na_kernel(q_ref, k_ref, v_ref, bias_ref, o_ref, *, rows, scale):
    nk = WIN_R * GRID_W
    nq = NA_ROWS_PER_STEP * GRID_W
    hd = q_ref.shape[-1]

    def group(g, carry):
        ks, vs, bias = [], [], []
        for i in range(NA_ROWS_PER_STEP):
            r = g * NA_ROWS_PER_STEP + i
            row_start = jnp.clip(r - WIN_R // 2, 0, rows - WIN_R)
            k0 = pl.multiple_of(row_start * GRID_W, GRID_W)
            ks.append(k_ref[0, pl.ds(k0, nk), :])
            vs.append(v_ref[0, pl.ds(k0, nk), :])
            bias.append(bias_ref[row_start - r + (WIN_R - 1)])
        q_rows = pl.ds(pl.multiple_of(g * nq, nq), nq)
        q = q_ref[0, q_rows, :].reshape(NA_ROWS_PER_STEP, GRID_W, hd)
        s = jnp.einsum("rqd,rkd->rqk", q, jnp.stack(ks), preferred_element_type=F32)
        s = s * scale + jnp.stack(bias)
        m = jnp.max(s, -1, keepdims=True)
        p = jnp.exp(s - m)
        l = jnp.sum(p, -1, keepdims=True)
        o = jnp.einsum("rqk,rkd->rqd", p.astype(BF16), jnp.stack(vs), preferred_element_type=F32) / l
        o_ref[0, q_rows, :] = o.reshape(nq, hd).astype(o_ref.dtype)
        return carry

    lax.fori_loop(0, rows // NA_ROWS_PER_STEP, group, 0, unroll=2)


def _neighbourhood_attention(proj3, b0, bsz, seq_len, bias, nh, hd):
    rows = seq_len // GRID_W
    assert rows % NA_ROWS_PER_STEP == 0 and rows >= WIN_R
    return pl.pallas_call(
        functools.partial(_na_kernel, rows=rows, scale=hd ** -0.5),
        grid=(nh, bsz),
        in_specs=[pl.BlockSpec((1, seq_len, hd), lambda h, b: (b0 + b, 0, h)),
                  pl.BlockSpec((1, seq_len, hd), lambda h, b: (b0 + b, 0, nh + h)),
                  pl.BlockSpec((1, seq_len, hd), lambda h, b: (b0 + b, 0, 2 * nh + h)),
                  pl.BlockSpec((None, WIN_R, GRID_W, WIN_R * GRID_W), lambda h, b: (h, 0, 0, 0))],
        out_specs=pl.BlockSpec((1, seq_len, hd), lambda h, b: (b, 0, h)),
        out_shape=jax.ShapeDtypeStruct((bsz, seq_len, nh * hd), BF16),
        compiler_params=_cparams(("parallel", "parallel")),
        name="neighbourhood_attention",
    )(proj3, proj3, proj3, bias)


def _conv3_kernel(x_ref, w_ref, b_ref, o_ref, *, seq_len, rb):
    w = w_ref[...]
    bias = b_ref[...]
    n_chunks = seq_len // rb

    def body(c, carry):
        r0 = pl.multiple_of(c * rb, rb)
        x = x_ref[0, pl.ds(r0, rb), :].astype(F32)
        row = lax.broadcasted_iota(jnp.int32, x.shape, 0)
        p0 = pl.multiple_of(jnp.maximum(r0 - 16, 0), 16)
        n0 = pl.multiple_of(jnp.minimum(r0 + rb, seq_len - 16), 16)
        prev_row = x_ref[0, pl.ds(p0, 16), :].astype(F32)[15:16]
        next_row = x_ref[0, pl.ds(n0, 16), :].astype(F32)[0:1]
        prev_row = jnp.where(c == 0, 0.0, prev_row)
        next_row = jnp.where(c == n_chunks - 1, 0.0, next_row)
        up = jnp.where(row == 0, prev_row, pltpu.roll(x, 1, 0))
        down = jnp.where(row == rb - 1, next_row, pltpu.roll(x, rb - 1, 0))
        y = up * w[0:1] + x * w[1:2] + down * w[2:3] + bias
        o_ref[0, pl.ds(r0, rb), :] = y.astype(o_ref.dtype)
        return carry

    lax.fori_loop(0, n_chunks, body, 0)


def _short_conv3(proj3, b0, bsz, seq_len, col0, w, b, cb=256, rb=512):
    width = w.shape[1]
    rb = min(rb, seq_len)
    c0 = col0 // cb
    return pl.pallas_call(
        functools.partial(_conv3_kernel, seq_len=seq_len, rb=rb),
        grid=(bsz, width // cb),
        in_specs=[pl.BlockSpec((1, seq_len, cb), lambda bi, c: (b0 + bi, 0, c0 + c)),
                  pl.BlockSpec((3, cb), lambda bi, c: (0, c)),
                  pl.BlockSpec((1, cb), lambda bi, c: (0, c))],
        out_specs=pl.BlockSpec((1, seq_len, cb), lambda bi, c: (bi, 0, c)),
        out_shape=jax.ShapeDtypeStruct((bsz, seq_len, width), BF16),
        compiler_params=_cparams(("parallel", "parallel")),
        name="hyena_short_conv",
    )(proj3, w, b.reshape(1, width))


def _filter_kernel(z_ref, w1_ref, b1_ref, fr_ref, w2_ref, b2_ref, w3_ref, b3_ref, dl_ref, o_ref, *,
                   seq_len, rb):
    i = pl.program_id(0)
    hi = lax.Precision.HIGHEST
    z = z_ref[...]
    h = jnp.sin(fr_ref[0:1] * (jnp.dot(z, w1_ref[...], precision=hi, preferred_element_type=F32)
                               + b1_ref[...]))
    h = jnp.sin(fr_ref[1:2] * (jnp.dot(h, w2_ref[...], precision=hi, preferred_element_type=F32)
                               + b2_ref[...]))
    ch = dl_ref.shape[-1]
    n = i * rb + lax.broadcasted_iota(jnp.int32, (rb, ch), 0)
    sign = jnp.where(n < seq_len, 1.0, jnp.where(n == seq_len, 0.0, -1.0))
    window = jnp.exp(-z[:, 0:1] * dl_ref[...]) * sign
    hb = h.astype(BF16)
    for o in range(w3_ref.shape[0]):
        f = jnp.dot(hb, w3_ref[o].astype(BF16), preferred_element_type=F32) + b3_ref[o]
        o_ref[:, o * ch:(o + 1) * ch] = (f * window).astype(o_ref.dtype)


def _hyena_filter_signal(seq_len, w1, b1, freq, w2, b2, w3, b3, order, ch, rb=512):
    n = 2 * seq_len
    emb, hid = w1.shape
    pad = LANES
    t = jnp.linspace(0.0, 1.0, seq_len, dtype=F32)[:, None]
    bands = (emb - 1) // 2
    fr = jnp.linspace(1e-4, bands - 1, bands, dtype=F32)[None, :]
    wpos = 2.0 * math.pi * jnp.arange(seq_len, dtype=F32)[:, None] / seq_len
    z = jnp.concatenate([t, jnp.cos(fr * wpos), -jnp.sin(fr * wpos)], axis=-1)
    pos = np.arange(n)
    src = np.clip(np.where(pos < seq_len, pos, n - pos), 0, seq_len - 1)
    z2 = jnp.pad(z[src], ((0, 0), (0, pad - emb)))
    w1p = jnp.pad(w1.astype(F32), ((0, pad - emb), (0, pad - hid)))
    b1p = jnp.pad(b1.astype(F32), (0, pad - hid)).reshape(1, pad)
    frp = jnp.pad(freq.astype(F32), ((0, 0), (0, pad - hid)), constant_values=1.0)
    w2p = jnp.pad(w2.astype(F32), ((0, pad - hid), (0, pad - hid)))
    b2p = jnp.pad(b2.astype(F32), (0, pad - hid)).reshape(1, pad)
    w3p = jnp.pad(w3.astype(F32), ((0, pad - hid), (0, 0))).reshape(pad, order, 2, ch).transpose(2, 1, 0, 3)
    b3p = b3.astype(F32).reshape(order, 2, 1, ch).transpose(1, 0, 2, 3)
    deltas = jnp.abs(jnp.linspace(HY_MIN_DECAY, HY_MAX_DECAY, ch, dtype=F32)).reshape(1, ch)
    nb = n // rb
    half = nb // 2
    full = lambda i: (0, 0)
    return pl.pallas_call(
        functools.partial(_filter_kernel, seq_len=seq_len, rb=rb),
        grid=(nb,),
        in_specs=[pl.BlockSpec((rb, pad), lambda i: (i, 0)),
                  pl.BlockSpec((pad, pad), full), pl.BlockSpec((1, pad), full),
                  pl.BlockSpec((2, pad), full),
                  pl.BlockSpec((pad, pad), full), pl.BlockSpec((1, pad), full),
                  pl.BlockSpec((None, order, pad, ch), lambda i: (i // half, 0, 0, 0)),
                  pl.BlockSpec((None, order, 1, ch), lambda i: (i // half, 0, 0, 0)),
                  pl.BlockSpec((1, ch), full)],
        out_specs=pl.BlockSpec((rb, order * ch), lambda i: (i, 0)),
        out_shape=jax.ShapeDtypeStruct((n, order * ch), BF16),
        compiler_params=_cparams(("parallel",)),
        name="hyena_filter_mlp",
    )(z2, w1p, b1p, frp, w2p, b2p, w3p, b3p, deltas)


def _dft_tables(seq_len):
    n = 2 * seq_len
    n2 = DFT_N2
    n1 = n // n2
    hk = n2 // 2
    kg = min(n1, DFT_GROUP)
    ng = n1 // kg
    pi = math.pi
    lows = np.arange(n1 // 2).reshape(ng, kg // 2)
    k1 = jnp.asarray(np.concatenate([lows, n1 - 1 - lows], axis=1).reshape(-1), jnp.int32)
    mirror = jnp.asarray(np.tile(np.arange(kg) >= kg // 2, ng))
    slab = jnp.arange(n1, dtype=jnp.int32)
    ph1 = ((2 * k1[:, None] + 1) * slab[None, :]) % (2 * n1)
    th1 = ph1.astype(F32) * (pi / n1)
    c1 = jnp.cos(th1).reshape(ng, kg, n1)
    s1 = jnp.sin(th1).reshape(ng, kg, n1)
    f1 = jnp.concatenate([c1[:, : kg // 2], -s1[:, : kg // 2]], axis=1)
    fb = jnp.concatenate([c1, -s1], axis=1).transpose(0, 2, 1)[:, : n1 // 2] * (2.0 / n)
    k = k1[:, None, None] + n1 * jnp.arange(hk, dtype=jnp.int32)[None, :, None]
    ph2 = ((2 * k + 1) * jnp.arange(n2, dtype=jnp.int32)[None, None, :]) % (2 * n)
    th2 = ph2.astype(F32) * (pi / n)
    c2, s2 = jnp.cos(th2), jnp.sin(th2)
    sgn = jnp.where(mirror, -1.0, 1.0)[:, None, None]
    g = jnp.concatenate([jnp.concatenate([c2, sgn * s2], axis=2),
                         jnp.concatenate([-s2, sgn * c2], axis=2)], axis=1)
    c2t, s2t = c2.transpose(0, 2, 1), s2.transpose(0, 2, 1)
    h = jnp.concatenate([jnp.concatenate([c2t, -s2t], axis=2),
                         jnp.concatenate([s2t, c2t], axis=2)], axis=1)
    nl = 2 if n1 <= DFT_GROUP else 1
    return dict(n1=n1, kg=kg, ng=ng, nl=nl, f1_full=f1.astype(BF16), f1_half=f1[:, :, : n1 // 2].astype(BF16),
                fb=fb.astype(BF16), g=g.astype(BF16), h=h.astype(BF16))


def _load_lanes(ref, rows):
    parts = [ref[s, rows, :] for s in range(ref.shape[0])]
    return parts[0] if len(parts) == 1 else jnp.concatenate(parts, axis=1)


def _store_lanes(ref, rows, val):
    for s in range(ref.shape[0]):
        ref[s, rows, :] = val[:, s * LANES:(s + 1) * LANES]


def _fill_slabs(x_ref, xs, slabs, pitch):
    def body(n1, carry):
        src = pl.ds(pl.multiple_of(n1 * DFT_N2, DFT_N2), DFT_N2)
        _store_lanes(xs, pl.ds(pl.multiple_of(n1 * pitch, SUBLANES), DFT_N2), x_ref[src, :].astype(F32))
        return carry
    lax.fori_loop(0, slabs, body, 0, unroll=4)


def _slab_stage(xs, a_s, f1, slabs, p_x, p_a):
    rows = f1.shape[0]

    def body(i, carry):
        xn = _load_lanes(xs, pl.ds(i, slabs, stride=p_x)).astype(BF16)
        _store_lanes(a_s, pl.ds(pl.multiple_of(i * p_a, SUBLANES), rows),
                     jnp.dot(f1, xn, preferred_element_type=F32))
        return carry
    lax.fori_loop(0, DFT_N2, body, 0, unroll=128)


def _load_slab_freq(a_s, low, kg, p_a):
    are = _load_lanes(a_s, pl.ds(low, DFT_N2, stride=p_a))
    aim = _load_lanes(a_s, pl.ds(kg // 2 + low, DFT_N2, stride=p_a))
    return jnp.concatenate([are, aim], axis=0).astype(BF16)


def _for_each_slot(kg, body):
    for first in (0, kg // 2):
        def step(low, carry, first=first):
            body(first + low, low)
            return carry
        lax.fori_loop(0, kg // 2, step, 0, unroll=16)


def _filter_spec_kernel(x_ref, f1_ref, g_ref, o_ref, xs, a_s, *, slabs, kg):
    grp = pl.program_id(1)
    p_x, p_a = _pitch(DFT_N2), _pitch(kg)

    @pl.when(grp == 0)
    def _():
        _fill_slabs(x_ref, xs, slabs, p_x)

    _slab_stage(xs, a_s, f1_ref[0], slabs, p_x, p_a)

    def body(j, low):
        o_ref[j] = jnp.dot(g_ref[j], _load_slab_freq(a_s, low, kg, p_a), preferred_element_type=F32)
    _for_each_slot(kg, body)


def _filter_spectrum(f, tables):
    n, cols = f.shape
    n1, kg, ng, nl = tables["n1"], tables["kg"], tables["ng"], tables["nl"]
    p_x, p_a = _pitch(DFT_N2), _pitch(kg)
    wl = nl * LANES
    return pl.pallas_call(
        functools.partial(_filter_spec_kernel, slabs=n1, kg=kg),
        grid=(cols // wl, ng),
        in_specs=[pl.BlockSpec((n, wl), lambda c, g: (0, c)),
                  pl.BlockSpec((1, kg, n1), lambda c, g: (g, 0, 0)),
                  pl.BlockSpec((kg, DFT_N2, 2 * DFT_N2), lambda c, g: (g, 0, 0))],
        out_specs=pl.BlockSpec((kg, DFT_N2, wl), lambda c, g: (g, 0, c)),
        out_shape=jax.ShapeDtypeStruct((n1, DFT_N2, cols), F32),
        scratch_shapes=[pltpu.VMEM((nl, n1 * p_x, LANES), F32),
                        pltpu.VMEM((nl, DFT_N2 * p_a, LANES), F32)],
        compiler_params=_cparams(("parallel", "arbitrary")),
        name="hyena_filter_spectrum",
    )(f, tables["f1_full"], tables["g"])


def _long_conv_kernel(x_ref, gate_ref, f1_ref, g_ref, h_ref, kf_ref, fb_ref, skip_ref, o_ref,
                      xs, a_s, z_s, y_s, *, slabs, kg, ng):
    grp = pl.program_id(2)
    n2 = DFT_N2
    hk = n2 // 2
    p_x, p_a, p_z, p_y = _pitch(n2), _pitch(kg), _pitch(2 * n2), _pitch(slabs)

    @pl.when(grp == 0)
    def _():
        _fill_slabs(x_ref.at[0], xs, slabs, p_x)
        y_s[...] = jnp.zeros_like(y_s)

    _slab_stage(xs, a_s, f1_ref[0], slabs, p_x, p_a)

    def freq_body(j, low):
        x = jnp.dot(g_ref[j], _load_slab_freq(a_s, low, kg, p_a), preferred_element_type=F32)
        kf = kf_ref[j]
        xr, xi = x[:hk], x[hk:]
        kr, ki = kf[:hk], kf[hk:]
        y = jnp.concatenate([xr * kr - xi * ki, xr * ki + xi * kr], axis=0)
        _store_lanes(z_s, pl.ds(pl.multiple_of(j * p_z, SUBLANES), n2), y)
    _for_each_slot(kg, freq_body)

    def inv_freq_body(j, carry):
        y = _load_lanes(z_s, pl.ds(pl.multiple_of(j * p_z, SUBLANES), n2)).astype(BF16)
        _store_lanes(z_s, pl.ds(pl.multiple_of(j * p_z, SUBLANES), 2 * n2),
                     jnp.dot(h_ref[j], y, preferred_element_type=F32))
        return carry
    lax.fori_loop(0, kg, inv_freq_body, 0, unroll=16)

    fb = fb_ref[0]

    def inv_body(t, carry):
        zre = _load_lanes(z_s, pl.ds(t, kg, stride=p_z))
        zim = _load_lanes(z_s, pl.ds(n2 + t, kg, stride=p_z))
        zz = jnp.concatenate([zre, zim], axis=0).astype(BF16)
        dst = pl.ds(pl.multiple_of(t * p_y, SUBLANES), slabs)
        _store_lanes(y_s, dst, _load_lanes(y_s, dst) + jnp.dot(fb, zz, preferred_element_type=F32))
        return carry
    lax.fori_loop(0, n2, inv_body, 0, unroll=128)

    @pl.when(grp == ng - 1)
    def _():
        skip = skip_ref[...]

        def out_body(t1, carry):
            y = _load_lanes(y_s, pl.ds(t1, n2, stride=p_y))
            u = _load_lanes(xs, pl.ds(pl.multiple_of(t1 * p_x, SUBLANES), n2))
            rows = pl.ds(pl.multiple_of(t1 * n2, n2), n2)
            gate = gate_ref[0, rows, :].astype(F32)
            o_ref[0, rows, :] = ((y + skip * u) * gate).astype(o_ref.dtype)
            return carry
        lax.fori_loop(0, slabs, out_body, 0, unroll=4)


def _long_conv_gate(u, u_blk0, gate, gate_blk0, tables, kf, kf_blk0, skip, ch):
    bsz, seq_len, _ = u.shape
    n1, kg, ng, nl = tables["n1"], tables["kg"], tables["ng"], tables["nl"]
    slabs = n1 // 2
    p_x, p_a, p_z, p_y = _pitch(DFT_N2), _pitch(kg), _pitch(2 * DFT_N2), _pitch(slabs)
    wl = nl * LANES
    return pl.pallas_call(
        functools.partial(_long_conv_kernel, slabs=slabs, kg=kg, ng=ng),
        grid=(bsz, ch // wl, ng),
        in_specs=[pl.BlockSpec((1, seq_len, wl), lambda b, c, g: (b, 0, u_blk0 + c)),
                  pl.BlockSpec((1, seq_len, wl), lambda b, c, g: (b, 0, gate_blk0 + c)),
                  pl.BlockSpec((1, kg, slabs), lambda b, c, g: (g, 0, 0)),
                  pl.BlockSpec((kg, DFT_N2, 2 * DFT_N2), lambda b, c, g: (g, 0, 0)),
                  pl.BlockSpec((kg, 2 * DFT_N2, DFT_N2), lambda b, c, g: (g, 0, 0)),
                  pl.BlockSpec((kg, DFT_N2, wl), lambda b, c, g: (g, 0, kf_blk0 + c)),
                  pl.BlockSpec((1, slabs, 2 * kg), lambda b, c, g: (g, 0, 0)),
                  pl.BlockSpec((1, wl), lambda b, c, g: (0, c))],
        out_specs=pl.BlockSpec((1, seq_len, wl), lambda b, c, g: (b, 0, c)),
        out_shape=jax.ShapeDtypeStruct((bsz, seq_len, ch), BF16),
        scratch_shapes=[pltpu.VMEM((nl, slabs * p_x, LANES), F32),
                        pltpu.VMEM((nl, DFT_N2 * p_a, LANES), F32),
                        pltpu.VMEM((nl, kg * p_z, LANES), F32),
                        pltpu.VMEM((nl, DFT_N2 * p_y, LANES), F32)],
        compiler_params=_cparams(("parallel", "parallel", "arbitrary")),
        name="hyena_long_conv",
    )(u, gate, tables["f1_half"], tables["g"], tables["h"], kf, tables["fb"],
      skip.astype(F32).reshape(1, ch))


def _hyena(hy, tables, kf, skip, order, ch):
    nblk = ch // (tables["nl"] * LANES)
    z, z_blk0 = hy, 0
    for o in range(order):
        z = _long_conv_gate(z, z_blk0, hy, (o + 1) * nblk, tables, kf, o * nblk, skip[o], ch)
    return z


def _rope(x, cos, sin):
    half = x.shape[-1] // 2
    x1, x2 = x[:, :half], x[:, half:]
    return jnp.concatenate([x1 * cos - x2 * sin, x1 * sin + x2 * cos], axis=-1)


def _ret_kernel(lg_ref, q_ref, k_ref, v_ref, g_ref, cos_ref, sin_ref, o_ref,
                sb_all, kr_all, kz_all, sf_ref, sb_ref, *, n_super, chunks_per_super, k_scale):
    h = pl.program_id(1)
    s = pl.program_id(2)
    cc = RET_CHUNK
    dh = q_ref.shape[-1]
    lgf = lg_ref[0, h]
    lgb = lg_ref[1, h]
    row = lax.broadcasted_iota(jnp.int32, (cc, dh), 0).astype(F32)
    zeta_f = jnp.exp(lgf * (cc - 1.0 - row))

    @pl.when(s < n_super)
    def _backward_sweep():
        @pl.when(s == 0)
        def _():
            sb_ref[...] = jnp.zeros_like(sb_ref)

        sup = n_super - 1 - s
        zeta_b = jnp.exp(lgb * row)
        chunk_decay = jnp.exp(lgb * jnp.full((1, dh), float(cc), F32))

        def body(t, carry):
            c = chunks_per_super - 1 - t
            c0 = pl.multiple_of(c * cc, cc)
            n = sup * chunks_per_super + c
            sb_all[n] = sb_ref[...].astype(BF16)
            k = _rope(k_ref[0, pl.ds(c0, cc), :].astype(F32), cos_ref[pl.ds(c0, cc), :],
                      sin_ref[pl.ds(c0, cc), :]) * k_scale
            kr_all[n] = k.astype(BF16)
            kz_all[n] = (k * zeta_f).astype(BF16)
            kz = (k * zeta_b).astype(BF16)
            v = v_ref[0, pl.ds(c0, cc), :]
            upd = lax.dot_general(kz, v, (((0,), (0,)), ((), ())), preferred_element_type=F32)
            sb_ref[...] = sb_ref[...] * chunk_decay + upd
            return carry

        lax.fori_loop(0, chunks_per_super, body, 0, unroll=4)

    @pl.when(s >= n_super)
    def _forward_sweep():
        @pl.when(s == n_super)
        def _():
            sf_ref[...] = jnp.zeros_like(sf_ref)

        sup = s - n_super
        col = lax.broadcasted_iota(jnp.int32, (cc, cc), 1).astype(F32)
        rowc = lax.broadcasted_iota(jnp.int32, (cc, cc), 0).astype(F32)
        diff = rowc - col
        inner_decay = jnp.where(diff >= 0, jnp.exp(lgf * jnp.maximum(diff, 0.0)),
                                jnp.exp(lgb * jnp.maximum(-diff, 0.0)))
        xi_f = jnp.exp(lgf * (row + 1.0))
        xi_b = jnp.exp(lgb * (cc - row))
        chunk_decay = jnp.exp(lgf * jnp.full((1, dh), float(cc), F32))

        def body(c, carry):
            c0 = pl.multiple_of(c * cc, cc)
            n = sup * chunks_per_super + c
            q = _rope(q_ref[0, pl.ds(c0, cc), :].astype(F32), cos_ref[pl.ds(c0, cc), :],
                      sin_ref[pl.ds(c0, cc), :]).astype(BF16)
            v = v_ref[0, pl.ds(c0, cc), :]
            sc = lax.dot_general(q, kr_all[n], (((1,), (1,)), ((), ())),
                                 preferred_element_type=F32) * inner_decay
            ret = jnp.dot(sc.astype(BF16), v, preferred_element_type=F32)
            ret = ret + xi_f * jnp.dot(q, sf_ref[...].astype(BF16), preferred_element_type=F32)
            ret = ret + xi_b * jnp.dot(q, sb_all[n], preferred_element_type=F32)
            upd = lax.dot_general(kz_all[n], v, (((0,), (0,)), ((), ())), preferred_element_type=F32)
            sf_ref[...] = sf_ref[...] * chunk_decay + upd
            ret = ret * lax.rsqrt(jnp.mean(ret * ret, -1, keepdims=True) + RMS_EPS)
            gate = g_ref[0, pl.ds(c0, cc), :].astype(F32)
            o_ref[0, pl.ds(c0, cc), :] = (ret * (gate / (1.0 + jnp.exp(-gate)))).astype(o_ref.dtype)
            return carry

        lax.fori_loop(0, chunks_per_super, body, 0, unroll=4)


def _retention(proj3, b0, bsz, seq_len, col0, nh, dh, log_gamma, cos, sin):
    sup_len = min(RET_SUPER, seq_len)
    n_super = seq_len // sup_len
    cps = sup_len // RET_CHUNK
    cb0 = col0 // dh

    def fwd_idx(s):
        return jnp.maximum(s - n_super, 0)

    def kv_idx(s):
        return jnp.where(s < n_super, n_super - 1 - s, s - n_super)

    def k_idx(s):
        return jnp.maximum(n_super - 1 - s, 0)

    n_chunks = seq_len // RET_CHUNK
    return pl.pallas_call(
        functools.partial(_ret_kernel, n_super=n_super, chunks_per_super=cps, k_scale=dh ** -0.5),
        grid=(bsz, nh, 2 * n_super),
        in_specs=[pl.BlockSpec(memory_space=pltpu.SMEM),
                  pl.BlockSpec((1, sup_len, dh), lambda b, h, s: (b0 + b, fwd_idx(s), cb0 + h)),
                  pl.BlockSpec((1, sup_len, dh), lambda b, h, s: (b0 + b, k_idx(s), cb0 + nh + h)),
                  pl.BlockSpec((1, sup_len, dh), lambda b, h, s: (b0 + b, kv_idx(s), cb0 + 2 * nh + h)),
                  pl.BlockSpec((1, sup_len, dh), lambda b, h, s: (b0 + b, fwd_idx(s), cb0 + 3 * nh + h)),
                  pl.BlockSpec((sup_len, dh // 2), lambda b, h, s: (kv_idx(s), 0)),
                  pl.BlockSpec((sup_len, dh // 2), lambda b, h, s: (kv_idx(s), 0))],
        out_specs=pl.BlockSpec((1, sup_len, dh), lambda b, h, s: (b, fwd_idx(s), h)),
        out_shape=jax.ShapeDtypeStruct((bsz, seq_len, nh * dh), BF16),
        scratch_shapes=[pltpu.VMEM((n_chunks, dh, dh), BF16),
                        pltpu.VMEM((n_chunks, RET_CHUNK, dh), BF16),
                        pltpu.VMEM((n_chunks, RET_CHUNK, dh), BF16),
                        pltpu.VMEM((dh, dh), F32),
                        pltpu.VMEM((dh, dh), F32)],
        compiler_params=_cparams(("parallel", "parallel", "arbitrary")),
        name="retention",
    )(log_gamma, proj3, proj3, proj3, proj3, cos, sin)


def _assemble_kernel(a1_ref, a2_ref, z1_ref, z2_ref, c1_ref, c2_ref, ga_ref, gb_ref, o_ref, *, nb1):
    first = pl.program_id(0) < nb1
    wa = a1_ref.shape[-1]
    wb = z1_ref.shape[-1]
    a = jnp.where(first, a1_ref[...], a2_ref[...]).astype(F32)
    a = a * lax.rsqrt(jnp.mean(a * a, -1, keepdims=True) + RMS_EPS) * ga_ref[...]
    z = jnp.where(first, z1_ref[...], z2_ref[...]).astype(F32)
    z = z * lax.rsqrt(jnp.mean(z * z, -1, keepdims=True) + RMS_EPS) * gb_ref[...]
    o_ref[:, :wa] = a.astype(o_ref.dtype)
    o_ref[:, wa:wa + wb] = z.astype(o_ref.dtype)
    o_ref[:, wa + wb:] = jnp.where(first, c1_ref[...], c2_ref[...])


def _assemble(a_parts, z_parts, c_parts, gain_a, gain_b, bm=1024):
    m1, wa = a_parts[0].shape
    m = m1 + a_parts[1].shape[0]
    assert m1 % bm == 0 and m % bm == 0
    wb, wc = z_parts[0].shape[1], c_parts[0].shape[1]
    nb1 = m1 // bm
    part_specs = []
    for w in (wa, wb, wc):
        part_specs += [pl.BlockSpec((bm, w), _first_rows(nb1)), pl.BlockSpec((bm, w), _second_rows(nb1))]
    return pl.pallas_call(
        functools.partial(_assemble_kernel, nb1=nb1),
        grid=(m // bm,),
        in_specs=part_specs + [pl.BlockSpec((1, wa), lambda i: (0, 0)),
                               pl.BlockSpec((1, wb), lambda i: (0, 0))],
        out_specs=pl.BlockSpec((bm, wa + wb + wc), lambda i: (i, 0)),
        out_shape=jax.ShapeDtypeStruct((m, wa + wb + wc), BF16),
        compiler_params=_cparams(("arbitrary",)),
        name="assemble_mixers",
    )(*a_parts, *z_parts, *c_parts, gain_a.astype(F32).reshape(1, wa), gain_b.astype(F32).reshape(1, wb))


def _token_mixers(proj, requests, p, l, dims):
    nh_a, hd_a, ch, order, nh_r, dh_r = dims
    in_cols = proj.shape[1]
    total = proj.shape[0]
    wa = nh_a * hd_a
    hy0 = 3 * wa
    ret0 = hy0 + (order + 1) * ch
    log_gamma = jnp.log1p(-jnp.exp2(-p["ret_decay_exp"][l].astype(F32)))
    a_parts, z_parts, c_parts = [], [], []
    bias = _na_bias_tables(p["na_rpb"][l])
    for (bsz, seq_len, tok0) in requests:
        assert tok0 % seq_len == 0 and total % seq_len == 0
        proj3 = proj.reshape(total // seq_len, seq_len, in_cols)
        b0 = tok0 // seq_len
        a = _neighbourhood_attention(proj3, b0, bsz, seq_len, bias, nh_a, hd_a)
        a_parts.append(a.reshape(bsz * seq_len, wa))
        tables = _dft_tables(seq_len)
        f = _hyena_filter_signal(seq_len, p["hy_f_w1"][l], p["hy_f_b1"][l], p["hy_f_freq"][l],
                                 p["hy_f_w2"][l], p["hy_f_b2"][l], p["hy_f_w3"][l], p["hy_f_b3"][l],
                                 order, ch)
        kf = _filter_spectrum(f, tables)
        hy = _short_conv3(proj3, b0, bsz, seq_len, hy0, p["hy_conv_w"][l].astype(F32),
                          p["hy_conv_b"][l].astype(F32))
        z = _hyena(hy, tables, kf, p["hy_skip"][l], order, ch)
        z_parts.append(z.reshape(bsz * seq_len, ch))
        half = dh_r // 2
        inv_freq = 1.0 / (ROPE_BASE ** jnp.linspace(0.0, 1.0, half, dtype=F32))
        ang = jnp.arange(seq_len, dtype=F32)[:, None] * inv_freq[None, :]
        c = _retention(proj3, b0, bsz, seq_len, ret0, nh_r, dh_r, log_gamma, jnp.cos(ang), jnp.sin(ang))
        c_parts.append(c.reshape(bsz * seq_len, nh_r * dh_r))
    return _assemble(a_parts, z_parts, c_parts, p["grp_gain_a"][l], p["grp_gain_b"][l])


def _pick(n, candidates):
    for c in candidates:
        if n % c == 0:
            return c
    raise ValueError(f"no block size for {n}")


def kernel(x_prompt, x_sample, ln_in_g, ln_in_b, w_in, na_rpb, hy_conv_w, hy_conv_b, hy_f_w1, hy_f_b1,
           hy_f_freq, hy_f_w2, hy_f_b2, hy_f_w3, hy_f_b3, hy_skip, ret_decay_exp, grp_gain_a, grp_gain_b,
           w_out, ln1_g, ln1_b, w_ffn_in, w_ffn_out, ln2_g, ln2_b):
    p = dict(na_rpb=na_rpb, hy_conv_w=hy_conv_w, hy_conv_b=hy_conv_b, hy_f_w1=hy_f_w1, hy_f_b1=hy_f_b1,
             hy_f_freq=hy_f_freq, hy_f_w2=hy_f_w2, hy_f_b2=hy_f_b2, hy_f_w3=hy_f_w3, hy_f_b3=hy_f_b3,
             hy_skip=hy_skip, ret_decay_exp=ret_decay_exp, grp_gain_a=grp_gain_a, grp_gain_b=grp_gain_b)
    depth, d_model, in_cols = w_in.shape
    nh_a = na_rpb.shape[1]
    wa = grp_gain_a.shape[1]
    ch = grp_gain_b.shape[1]
    order = hy_skip.shape[1]
    nh_r = ret_decay_exp.shape[2]
    wr = (in_cols - 3 * wa - (order + 1) * ch) // 4
    dims = (nh_a, wa // nh_a, ch, order, nh_r, wr // nh_r)
    d_ff = w_ffn_out.shape[1]
    alpha = (2.0 * depth) ** 0.25

    bp, lp, _ = x_prompt.shape
    bs, ls, _ = x_sample.shape
    requests = [(bp, lp, 0), (bs, ls, bp * lp)]
    m1 = bp * lp
    m = m1 + bs * ls
    bm = _pick(m, (1024, 512, 256))
    bm_ln = 512
    assert m1 % bm_ln == 0 and m % bm_ln == 0

    xf, xb = _layernorm_join(x_prompt.reshape(m1, d_model), x_sample.reshape(bs * ls, d_model),
                             ln_in_g.astype(F32), ln_in_b.astype(F32), bm_ln // 2)
    w_in_b = w_in[0:1].astype(BF16)
    resid = (xf,)
    for l in range(depth):
        bn_out = _pick(d_model, (1024, 512, 256) if len(resid) == 1 else (512, 256))
        cast = [(w_out, l, bn_out), (w_ffn_in, l, _pick(d_ff, (512, 256, 128))),
                (w_ffn_out, l, _pick(d_model, (512, 256)))]
        if l + 1 < depth:
            cast.append((w_in, l + 1, in_cols))
        proj, cast_w = _matmul(xb, w_in_b, 0, bm, _pick(in_cols, (1024, 512, 256)), cast)
        w_out_b, w_ffn_in_b, w_ffn_out_b = cast_w[:3]
        if l + 1 < depth:
            w_in_b = cast_w[3]
        mixed = _token_mixers(proj, requests, p, l, dims)
        y = _matmul_residual(mixed, w_out_b, resid, alpha, bm)
        g1, b1 = ln1_g[l].astype(F32), ln1_b[l].astype(F32)
        xb, stats = _layernorm_stats(y, g1, b1, bm_ln)
        hmid = _matmul_swiglu(xb, w_ffn_in_b, _pick(m, (2048, 1024, 512, 256)))
        y = _matmul_residual(hmid, w_ffn_out_b, (y, stats, g1, b1), alpha, _pick(m, (512, 256)))
        if l + 1 < depth:
            g2, b2 = ln2_g[l].astype(F32), ln2_b[l].astype(F32)
            xb, stats = _layernorm_stats(y, g2, b2, bm_ln)
            resid = (y, stats, g2, b2)
    o1, o2 = _layernorm_split(y, ln2_g[depth - 1].astype(F32), ln2_b[depth - 1].astype(F32), m1, bm_ln // 2)
    return (o1.reshape(bp, lp, d_model), o2.reshape(bs, ls, d_model))
```
